```python
import math
import jax, jax.numpy as jnp
from jax import lax
import numpy as np

D_MODEL = 1024
BATCH = 4
SEQ = 4096
DEPTH = 4
DEC_BATCH = 128
DEC_SEQ = 1
PAST_LEN = 8192
PAGE_SIZE = 128

N_A_LAYERS = DEPTH // 2
N_B_LAYERS = DEPTH - N_A_LAYERS
A_HEADS = 8
A_DK = D_MODEL // A_HEADS
A_DV = D_MODEL // A_HEADS
CONV_W = 4
CHUNK = 64
A_QKV = 2 * A_HEADS * A_DK + A_HEADS * A_DV
A_IN = A_QKV + A_HEADS * A_DV + 2 * A_HEADS
B_HEADS = 16
B_KV_HEADS = 4
B_HD = D_MODEL // B_HEADS
B_GROUP = B_HEADS // B_KV_HEADS
WINDOW = 128
BLOCK = 128
ROT_DIM = B_HD // 4
ROPE_THETA = 500000.0
D_FF = 2816
N_EXPERTS = 8
TOP_K = 2
D_FF_EXPERT = 1024
N_DENSE = (DEPTH + 1) // 2
N_MOE = DEPTH // 2
EPS = 1e-6

kernel_name = "yoco_gdn_swa_sink_moe_step"


def rms_norm(x, g):
    xf = x.astype(jnp.float32)
    y = xf * lax.rsqrt(jnp.mean(xf * xf, axis=-1, keepdims=True) + EPS)
    return (y * g.astype(jnp.float32)).astype(x.dtype)


def l2_norm(x):
    xf = x.astype(jnp.float32)
    return xf * lax.rsqrt(jnp.sum(xf * xf, axis=-1, keepdims=True) + EPS)


def causal_conv(u, buf, w):
    T = u.shape[1]
    full = jnp.concatenate([buf.astype(u.dtype), u], axis=1)
    out = full[:, 0:T] * w[0]
    for j in range(1, CONV_W):
        out = out + full[:, j:j + T] * w[j]
    return jax.nn.silu(out), full[:, -(CONV_W - 1):]


def gated_delta_chunked(q, k, v, g, beta, S0):
    Bn, T, H, DK = q.shape
    DV = v.shape[-1]
    N = T // CHUNK

    def to_chunks(a):
        a = a.reshape((Bn, N, CHUNK, H) + a.shape[3:])
        return jnp.moveaxis(a, (1, 3), (0, 2))

    qc, kc, vc, gc, bc = (to_chunks(a) for a in (q, k, v, g, beta))
    gcum = jnp.cumsum(gc, axis=-1)
    idx = jnp.arange(CHUNK)
    tril = idx[:, None] >= idx[None, :]
    strict = idx[:, None] > idx[None, :]
    eye = jnp.eye(CHUNK, dtype=jnp.float32)

    def step(S, inp):
        qi, ki, vi, gi, bi = inp
        diff = gi[..., :, None] - gi[..., None, :]
        decay = jnp.where(tril, jnp.exp(jnp.where(tril, diff, 0.0)), 0.0)
        kb = ki * bi[..., None]
        L = jnp.where(strict, jnp.einsum('bhid,bhjd->bhij', kb, ki) * decay, 0.0)
        rhs = jnp.concatenate([vi * bi[..., None], kb * jnp.exp(gi)[..., None]], axis=-1)
        sol = lax.linalg.triangular_solve(L + eye, rhs, left_side=True, lower=True, unit_diagonal=True)
        u, wk = sol[..., :DV], sol[..., DV:]
        v_new = u - jnp.einsum('bhck,bhkv->bhcv', wk, S)
        attn = jnp.where(tril, jnp.einsum('bhid,bhjd->bhij', qi, ki) * decay, 0.0)
        o = (jnp.einsum('bhck,bhkv->bhcv', qi * jnp.exp(gi)[..., None], S)
             + jnp.einsum('bhij,bhjv->bhiv', attn, v_new))
        g_last = gi[..., -1]
        S = (S * jnp.exp(g_last)[..., None, None]
             + jnp.einsum('bhck,bhcv->bhkv', ki * jnp.exp(g_last[..., None] - gi)[..., None], v_new))
        return S, o

    S, o = lax.scan(step, S0, (qc, kc, vc, gcum, bc))
    o = jnp.moveaxis(o, (0, 2), (1, 3)).reshape(Bn, T, H, DV)
    return o, S


def gated_delta_recurrent(q, k, v, g, beta, S0):
    def step(S, inp):
        qt, kt, vt, gt, bt = inp
        S = S * jnp.exp(gt)[..., None, None]
        kv_mem = jnp.einsum('bhk,bhkv->bhv', kt, S)
        delta = (vt - kv_mem) * bt[..., None]
        S = S + jnp.einsum('bhk,bhv->bhkv', kt, delta)
        return S, jnp.einsum('bhk,bhkv->bhv', qt, S)

    xs = tuple(jnp.moveaxis(a, 1, 0) for a in (q, k, v, g, beta))
    S, o = lax.scan(step, S0, xs)
    return jnp.moveaxis(o, 0, 1), S


def deltanet_mixer(hn, conv_buf, S0, w_in, conv_w, A_log, dt_bias, o_norm, w_out, chunked):
    Bn, T, _ = hn.shape
    proj = hn @ w_in
    qkv, z, a, b = jnp.split(proj, [A_QKV, A_QKV + A_HEADS * A_DV, A_QKV + A_HEADS * A_DV + A_HEADS], axis=-1)
    qkv, new_buf = causal_conv(qkv, conv_buf, conv_w)
    q, k, v = jnp.split(qkv, [A_HEADS * A_DK, 2 * A_HEADS * A_DK], axis=-1)
    q = l2_norm(q.reshape(Bn, T, A_HEADS, A_DK)) * (A_DK ** -0.5)
    k = l2_norm(k.reshape(Bn, T, A_HEADS, A_DK))
    v = v.reshape(Bn, T, A_HEADS, A_DV).astype(jnp.float32)
    g = -jnp.exp(A_log.astype(jnp.float32)) * jax.nn.softplus(a.astype(jnp.float32) + dt_bias.astype(jnp.float32))
    beta = jax.nn.sigmoid(b.astype(jnp.float32))
    S0 = S0.astype(jnp.float32)
    if chunked:
        o, S = gated_delta_chunked(q, k, v, g, beta, S0)
    else:
        o, S = gated_delta_recurrent(q, k, v, g, beta, S0)
    zf = z.reshape(Bn, T, A_HEADS, A_DV).astype(jnp.float32)
    o = rms_norm(o, o_norm) * jax.nn.silu(zf)
    o = o.reshape(Bn, T, A_HEADS * A_DV).astype(hn.dtype) @ w_out
    return o, S, new_buf


def partial_rope(x, pos):
    inv = ROPE_THETA ** (-jnp.arange(0, ROT_DIM, 2, dtype=jnp.float32) / ROT_DIM)
    ang = pos.astype(jnp.float32)[:, None] * inv[None, :]
    cos = jnp.cos(ang)[None, :, None, :]
    sin = jnp.sin(ang)[None, :, None, :]
    xf = x.astype(jnp.float32)
    x1 = xf[..., :ROT_DIM // 2]
    x2 = xf[..., ROT_DIM // 2:ROT_DIM]
    out = jnp.concatenate([x1 * cos - x2 * sin, x2 * cos + x1 * sin, xf[..., ROT_DIM:]], axis=-1)
    return out.astype(x.dtype)


def shared_kv(h, kv_norm, w_kv, k_norm, pos):
    Bn, T, _ = h.shape
    kv = rms_norm(h, kv_norm) @ w_kv
    k, v = jnp.split(kv, 2, axis=-1)
    k = partial_rope(rms_norm(k.reshape(Bn, T, B_KV_HEADS, B_HD), k_norm), pos)
    v = v.reshape(Bn, T, B_KV_HEADS, B_HD)
    return k, v


def swa_query(hn, w_q, q_norm, pos):
    Bn, T, _ = hn.shape
    q = rms_norm((hn @ w_q).reshape(Bn, T, B_HEADS, B_HD), q_norm)
    return partial_rope(q, pos)


def sink_attend(s, sinks, v, spec):
    sink = sinks.astype(jnp.float32).reshape(B_KV_HEADS, B_GROUP)[:, :, None, None]
    m = jnp.maximum(jnp.max(s, axis=-1, keepdims=True), sink)
    p = jnp.exp(s - m)
    denom = jnp.sum(p, axis=-1, keepdims=True) + jnp.exp(sink - m)
    return jnp.einsum(spec, (p / denom).astype(v.dtype), v)


def swa_prompt(q, k, v, sinks):
    Bn, T = q.shape[:2]
    nb = T // BLOCK
    qb = q.reshape(Bn, nb, BLOCK, B_KV_HEADS, B_GROUP, B_HD)
    kb = k.reshape(Bn, nb, BLOCK, B_KV_HEADS, B_HD)
    vb = v.reshape(Bn, nb, BLOCK, B_KV_HEADS, B_HD)
    kk = jnp.concatenate([jnp.concatenate([jnp.zeros_like(kb[:, :1]), kb[:, :-1]], axis=1), kb], axis=2)
    vv = jnp.concatenate([jnp.concatenate([jnp.zeros_like(vb[:, :1]), vb[:, :-1]], axis=1), vb], axis=2)
    s = jnp.einsum('bnqhgd,bnkhd->bnhgqk', qb, kk).astype(jnp.float32) * (B_HD ** -0.5)
    blk = jnp.arange(nb)[:, None, None]
    qpos = blk * BLOCK + jnp.arange(BLOCK)[None, :, None]
    kpos = (blk - 1) * BLOCK + jnp.arange(2 * BLOCK)[None, None, :]
    d = qpos - kpos
    mask = (d >= 0) & (d <= WINDOW) & (kpos >= 0)
    s = jnp.where(mask[None, :, None, None], s, -jnp.inf)
    o = sink_attend(s, sinks, vv, 'bnhgqk,bnkhd->bnqhgd')
    return o.reshape(Bn, T, B_HEADS * B_HD)


def swa_sample(q, k_all, v_all, sinks, pos, w_cache):
    Bn, T = q.shape[:2]
    kpos = jnp.concatenate([PAST_LEN - w_cache + jnp.arange(w_cache, dtype=jnp.int32), pos])
    qb = q.reshape(Bn, T, B_KV_HEADS, B_GROUP, B_HD)
    s = jnp.einsum('bqhgd,bkhd->bhgqk', qb, k_all).astype(jnp.float32) * (B_HD ** -0.5)
    d = pos[:, None] - kpos[None, :]
    mask = (d >= 0) & (d <= WINDOW)
    s = jnp.where(mask, s, -jnp.inf)
    o = sink_attend(s, sinks, v_all, 'bhgqk,bkhd->bqhgd')
    return o.reshape(Bn, T, B_HEADS * B_HD)


def swiglu(hn, w_gu, w_down):
    gte, up = jnp.split(hn @ w_gu, 2, axis=-1)
    return (jax.nn.silu(gte) * up) @ w_down


def moe_swiglu(hn, router_w, router_b, w_gu, w_down):
    logits = (hn @ router_w + router_b).astype(jnp.float32)
    top_v, top_i = lax.top_k(logits, TOP_K)
    gates = jax.nn.softmax(top_v, axis=-1)
    combine = jnp.sum(jax.nn.one_hot(top_i, N_EXPERTS, dtype=jnp.float32) * gates[..., None], axis=-2)
    out = jnp.zeros_like(hn)
    for e in range(N_EXPERTS):
        out = out + combine[..., e:e + 1].astype(hn.dtype) * swiglu(hn, w_gu[e], w_down[e])
    return out


def run_group(x, pos, conv_state, delta_state, k_win, v_win, is_prompt, p):
    h = x
    new_conv, new_delta = [], []
    k_sh = v_sh = k_all = v_all = new_k_win = new_v_win = None
    for l in range(DEPTH):
        if l < N_A_LAYERS:
            o, s_new, c_new = deltanet_mixer(
                rms_norm(h, p['a_norm'][l]), conv_state[l], delta_state[l], p['a_w_in'][l], p['a_conv_w'][l],
                p['a_A_log'][l], p['a_dt_bias'][l], p['a_o_norm'][l], p['a_w_out'][l], is_prompt)
            new_conv.append(c_new.astype(conv_state.dtype))
            new_delta.append(s_new.astype(delta_state.dtype))
        else:
            j = l - N_A_LAYERS
            if j == 0:
                k_sh, v_sh = shared_kv(h, p['kv_norm'], p['w_kv'], p['k_norm'], pos)
                if is_prompt:
                    w_p = min(WINDOW, k_sh.shape[1])
                    new_k_win, new_v_win = k_sh[:, -w_p:], v_sh[:, -w_p:]
                else:
                    k_all = jnp.concatenate([k_win.astype(k_sh.dtype), k_sh], axis=1)
                    v_all = jnp.concatenate([v_win.astype(v_sh.dtype), v_sh], axis=1)
                    new_k_win = k_all[:, -k_win.shape[1]:].astype(k_win.dtype)
                    new_v_win = v_all[:, -v_win.shape[1]:].astype(v_win.dtype)
            q = swa_query(rms_norm(h, p['b_norm'][j]), p['b_w_q'][j], p['b_q_norm'][j], pos)
            if is_prompt:
                o = swa_prompt(q, k_sh, v_sh, p['b_sinks'][j])
            else:
                o = swa_sample(q, k_all, v_all, p['b_sinks'][j], pos, k_win.shape[1])
            o = o @ p['b_w_o'][j]
        h = h + o
        hn = rms_norm(h, p['ffn_norm'][l])
        if l % 2 == 0:
            f = swiglu(hn, p['dense_w_gu'][l // 2], p['dense_w_down'][l // 2])
        else:
            f = moe_swiglu(hn, p['moe_router'][l // 2], p['moe_router_b'][l // 2],
                           p['moe_w_gu'][l // 2], p['moe_w_down'][l // 2])
        h = h + f
    return h, jnp.stack(new_conv), jnp.stack(new_delta), new_k_win, new_v_win


def setup_inputs(seed: int = 0) -> dict:
    key = jax.random.key(seed)
    ks = iter(jax.random.split(key, 40))

    def nrm(shape, scale):
        return scale * jax.random.normal(next(ks), shape, jnp.float32)

    def gain(shape):
        return 1.0 + 0.05 * jax.random.normal(next(ks), shape, jnp.float32)

    w_cache = min(WINDOW, PAST_LEN)
    dt = jax.random.uniform(next(ks), (N_A_LAYERS, A_HEADS), jnp.float32, minval=1e-3, maxval=0.1)
    return {
        'x_prompt': nrm((BATCH, SEQ, D_MODEL), 1.0),
        'x_sample': nrm((DEC_BATCH, DEC_SEQ, D_MODEL), 1.0),
        'state_conv': nrm((N_A_LAYERS, DEC_BATCH, CONV_W - 1, A_QKV), 1.0),
        'state_delta': nrm((N_A_LAYERS, DEC_BATCH, A_HEADS, A_DK, A_DV), A_DK ** -0.5),
        'cache_k_win': nrm((DEC_BATCH, w_cache, B_KV_HEADS, B_HD), 1.0),
        'cache_v_win': nrm((DEC_BATCH, w_cache, B_KV_HEADS, B_HD), 1.0),
        'a_norm': gain((N_A_LAYERS, D_MODEL)),
        'a_w_in': nrm((N_A_LAYERS, D_MODEL, A_IN), D_MODEL ** -0.5),
        'a_conv_w': nrm((N_A_LAYERS, CONV_W, A_QKV), CONV_W ** -0.5),
        'a_A_log': jnp.log(jax.random.uniform(next(ks), (N_A_LAYERS, A_HEADS), jnp.float32, minval=1.0, maxval=16.0)),
        'a_dt_bias': dt + jnp.log(-jnp.expm1(-dt)),
        'a_o_norm': gain((N_A_LAYERS, A_DV)),
        'a_w_out': nrm((N_A_LAYERS, A_HEADS * A_DV, D_MODEL), (A_HEADS * A_DV) ** -0.5),
        'kv_norm': gain((D_MODEL,)),
        'w_kv': nrm((D_MODEL, 2 * B_KV_HEADS * B_HD), D_MODEL ** -0.5),
        'k_norm': gain((B_HD,)),
        'b_norm': gain((N_B_LAYERS, D_MODEL)),
        'b_w_q': nrm((N_B_LAYERS, D_MODEL, B_HEADS * B_HD), D_MODEL ** -0.5),
        'b_q_norm': gain((N_B_LAYERS, B_HD)),
        'b_sinks': nrm((N_B_LAYERS, B_HEADS), 0.5),
        'b_w_o': nrm((N_B_LAYERS, B_HEADS * B_HD, D_MODEL), (B_HEADS * B_HD) ** -0.5),
        'ffn_norm': gain((DEPTH, D_MODEL)),
        'dense_w_gu': nrm((N_DENSE, D_MODEL, 2 * D_FF), D_MODEL ** -0.5),
        'dense_w_down': nrm((N_DENSE, D_FF, D_MODEL), D_FF ** -0.5),
        'moe_router': nrm((N_MOE, D_MODEL, N_EXPERTS), D_MODEL ** -0.5),
        'moe_router_b': nrm((N_MOE, N_EXPERTS), 0.01),
        'moe_w_gu': nrm((N_MOE, N_EXPERTS, D_MODEL, 2 * D_FF_EXPERT), D_MODEL ** -0.5),
        'moe_w_down': nrm((N_MOE, N_EXPERTS, D_FF_EXPERT, D_MODEL), D_FF_EXPERT ** -0.5),
    }


def reference(x_prompt, x_sample, state_conv, state_delta, cache_k_win, cache_v_win,
              a_norm, a_w_in, a_conv_w, a_A_log, a_dt_bias, a_o_norm, a_w_out,
              kv_norm, w_kv, k_norm,
              b_norm, b_w_q, b_q_norm, b_sinks, b_w_o,
              ffn_norm, dense_w_gu, dense_w_down, moe_router, moe_router_b, moe_w_gu, moe_w_down):
    p = {
        'a_norm': a_norm, 'a_w_in': a_w_in, 'a_conv_w': a_conv_w, 'a_A_log': a_A_log,
        'a_dt_bias': a_dt_bias, 'a_o_norm': a_o_norm, 'a_w_out': a_w_out,
        'kv_norm': kv_norm, 'w_kv': w_kv, 'k_norm': k_norm,
        'b_norm': b_norm, 'b_w_q': b_w_q, 'b_q_norm': b_q_norm, 'b_sinks': b_sinks, 'b_w_o': b_w_o,
        'ffn_norm': ffn_norm, 'dense_w_gu': dense_w_gu, 'dense_w_down': dense_w_down,
        'moe_router': moe_router, 'moe_router_b': moe_router_b, 'moe_w_gu': moe_w_gu, 'moe_w_down': moe_w_down,
    }
    n_p = x_prompt.shape[0]
    pos_p = jnp.arange(x_prompt.shape[1], dtype=jnp.int32)
    conv0 = jnp.zeros((N_A_LAYERS, n_p, CONV_W - 1, A_QKV), state_conv.dtype)
    delta0 = jnp.zeros((N_A_LAYERS, n_p, A_HEADS, A_DK, A_DV), state_delta.dtype)
    y_prompt, conv_p, delta_p, kw_p, vw_p = run_group(x_prompt, pos_p, conv0, delta0, None, None, True, p)
    pos_s = PAST_LEN + jnp.arange(x_sample.shape[1], dtype=jnp.int32)
    y_sample, conv_s, delta_s, kw_s, vw_s = run_group(
        x_sample, pos_s, state_conv, state_delta, cache_k_win, cache_v_win, False, p)
    return (y_prompt, y_sample, conv_p, conv_s, delta_p, delta_s, kw_p, kw_s, vw_p, vw_s)
```

```python
import functools

import jax
import jax.numpy as jnp
from jax import lax
from jax.experimental import pallas as pl
from jax.experimental.pallas import tpu as pltpu

F32, BF16, I32 = jnp.float32, jnp.bfloat16, jnp.int32
HI = lax.Precision.HIGHEST

D_MODEL = 1024
N_GDN = 2
GDN_HEADS = 8
GDN_DK = 128
CONV_TAPS = 4
CHUNK = 64
QKV_W = 3 * GDN_HEADS * GDN_DK
PROJ_W = QKV_W + D_MODEL + 128
ATT_HEADS = 16
KV_HEADS = 4
HEAD_DIM = 64
ATT_GROUP = ATT_HEADS // KV_HEADS
WINDOW = 128
ROT = HEAD_DIM // 4
ROPE_THETA = 500000.0
PAST_LEN = 8192
N_EXPERTS = 8
EPS = 1e-6

LANES = 128
VMEM_LIMIT = 56 * 1024 * 1024
NEG = -1e30


def _params(sem):
    return pltpu.CompilerParams(dimension_semantics=sem, vmem_limit_bytes=VMEM_LIMIT)


def _rms(x, g):
    return x * lax.rsqrt(jnp.mean(x * x, axis=-1, keepdims=True) + EPS) * g


def _silu(x):
    return x * jax.nn.sigmoid(x)


def _mm_body(*refs, has_norm, has_res):
    it = iter(refs)
    x_ref = next(it)
    g_ref = next(it) if has_norm else None
    w_ref = next(it)
    r_ref = next(it) if has_res else None
    o_ref = next(it)
    xn_ref = next(it)

    @pl.when(pl.program_id(1) == 0)
    def _():
        x = x_ref[...].astype(F32)
        if has_norm:
            x = _rms(x, g_ref[...])
        xn_ref[...] = x.astype(BF16)

    acc = jnp.dot(xn_ref[...], w_ref[...], preferred_element_type=F32)
    if has_res:
        acc = acc + r_ref[...]
    o_ref[...] = acc.astype(o_ref.dtype)


def _mm(x, w, gain=None, res=None, tm=512, tn=1024, out_dtype=F32):
    m, k = x.shape
    n = w.shape[1]
    tm = min(tm, m)
    tn = min(tn, n)
    assert m % tm == 0 and n % tn == 0
    ins = [x]
    specs = [pl.BlockSpec((tm, k), lambda i, j: (i, 0))]
    if gain is not None:
        ins.append(gain.reshape(1, k).astype(F32))
        specs.append(pl.BlockSpec((1, k), lambda i, j: (0, 0)))
    ins.append(w)
    specs.append(pl.BlockSpec((k, tn), lambda i, j: (0, j)))
    if res is not None:
        ins.append(res)
        specs.append(pl.BlockSpec((tm, tn), lambda i, j: (i, j)))
    return pl.pallas_call(
        functools.partial(_mm_body, has_norm=gain is not None, has_res=res is not None),
        grid=(m // tm, n // tn),
        in_specs=specs,
        out_specs=pl.BlockSpec((tm, tn), lambda i, j: (i, j)),
        out_shape=jax.ShapeDtypeStruct((m, n), out_dtype),
        scratch_shapes=[pltpu.VMEM((tm, k), BF16)],
        compiler_params=_params(("parallel", "arbitrary")),
        name="mm",
    )(*ins)


def _ffn_dense_body(x_ref, g_ref, wg_ref, wu_ref, wd_ref, o_ref, xn_ref, acc_ref):
    f = pl.program_id(1)

    @pl.when(f == 0)
    def _():
        xn_ref[...] = _rms(x_ref[...], g_ref[...]).astype(BF16)
        acc_ref[...] = x_ref[...]

    xn = xn_ref[...]
    gate = jnp.dot(xn, wg_ref[...], preferred_element_type=F32)
    up = jnp.dot(xn, wu_ref[...], preferred_element_type=F32)
    act = (_silu(gate) * up).astype(BF16)
    acc_ref[...] += jnp.dot(act, wd_ref[...], preferred_element_type=F32)

    @pl.when(f == pl.num_programs(1) - 1)
    def _():
        o_ref[...] = acc_ref[...]


def _ffn_dense(x, gain, w_gu, w_down, tm=512):
    m, k = x.shape
    ff = w_down.shape[0]
    tm = min(tm, m)
    tf = ff // 2 if (ff // 2) % LANES == 0 else ff
    nf = ff // tf
    return pl.pallas_call(
        _ffn_dense_body,
        grid=(m // tm, nf),
        in_specs=[
            pl.BlockSpec((tm, k), lambda i, f: (i, 0)),
            pl.BlockSpec((1, k), lambda i, f: (0, 0)),
            pl.BlockSpec((k, tf), lambda i, f: (0, f)),
            pl.BlockSpec((k, tf), lambda i, f: (0, nf + f)),
            pl.BlockSpec((tf, k), lambda i, f: (f, 0)),
        ],
        out_specs=pl.BlockSpec((tm, k), lambda i, f: (i, 0)),
        out_shape=jax.ShapeDtypeStruct((m, k), F32),
        scratch_shapes=[pltpu.VMEM((tm, k), BF16), pltpu.VMEM((tm, k), F32)],
        compiler_params=_params(("parallel", "arbitrary")),
        name="ffn_dense",
    )(x, gain.reshape(1, k).astype(F32), w_gu, w_gu, w_down)


def _router_body(x_ref, g_ref, wr_ref, br_ref, idx_ref, gate_ref, cnt_ref, carry_ref):
    i = pl.program_id(0)

    @pl.when(i == 0)
    def _():
        carry_ref[...] = jnp.zeros_like(carry_ref)

    tm = x_ref.shape[0]
    xn = _rms(x_ref[...], g_ref[...])
    logits = jnp.dot(xn, wr_ref[...], precision=HI, preferred_element_type=F32) + br_ref[...]
    lane = lax.broadcasted_iota(I32, logits.shape, 1)
    m1 = jnp.max(logits, axis=-1, keepdims=True)
    i1 = jnp.min(jnp.where(logits == m1, lane, LANES), axis=-1, keepdims=True)
    rest = jnp.where(lane == i1, NEG, logits)
    m2 = jnp.max(rest, axis=-1, keepdims=True)
    i2 = jnp.min(jnp.where(rest == m2, lane, LANES), axis=-1, keepdims=True)
    e2 = jnp.exp(m2 - m1)
    g1 = 1.0 / (1.0 + e2)
    g2 = e2 * g1
    oh1 = lane == i1
    oh2 = lane == i2
    oh = jnp.where(oh1 | oh2, 1.0, 0.0).astype(BF16)
    r = lax.broadcasted_iota(I32, (tm, tm), 0)
    c = lax.broadcasted_iota(I32, (tm, tm), 1)
    below = jnp.where(c < r, 1.0, 0.0).astype(BF16)
    before = jnp.dot(below, oh, preferred_element_type=F32) + carry_ref[0:1, :]
    rk1 = jnp.sum(jnp.where(oh1, before, 0.0), axis=-1, keepdims=True).astype(I32)
    rk2 = jnp.sum(jnp.where(oh2, before, 0.0), axis=-1, keepdims=True).astype(I32)
    idx_ref[...] = jnp.where(lane == 0, i1, jnp.where(lane == 1, i2, jnp.where(lane == 2, rk1, rk2)))
    gate_ref[...] = jnp.where(lane == 0, g1, g2)
    total = carry_ref[0:1, :] + jnp.sum(oh.astype(F32), axis=0, keepdims=True)
    carry_ref[...] = jnp.broadcast_to(total, carry_ref.shape)
    cnt_ref[...] = carry_ref[...]


def _router(x, gain, w_router, b_router, tm=512):
    m, k = x.shape
    tm = min(tm, m)
    wr = jnp.zeros((k, LANES), F32).at[:, :N_EXPERTS].set(w_router)
    br = jnp.full((1, LANES), NEG, F32).at[0, :N_EXPERTS].set(b_router)
    return pl.pallas_call(
        _router_body,
        grid=(m // tm,),
        in_specs=[
            pl.BlockSpec((tm, k), lambda i: (i, 0)),
            pl.BlockSpec((1, k), lambda i: (0, 0)),
            pl.BlockSpec((k, LANES), lambda i: (0, 0)),
            pl.BlockSpec((1, LANES), lambda i: (0, 0)),
        ],
        out_specs=[
            pl.BlockSpec((tm, LANES), lambda i: (i, 0)),
            pl.BlockSpec((tm, LANES), lambda i: (i, 0)),
            pl.BlockSpec((8, LANES), lambda i: (0, 0)),
        ],
        out_shape=[
            jax.ShapeDtypeStruct((m, LANES), I32),
            jax.ShapeDtypeStruct((m, LANES), F32),
            jax.ShapeDtypeStruct((8, LANES), F32),
        ],
        scratch_shapes=[pltpu.VMEM((8, LANES), F32)],
        compiler_params=_params(("arbitrary",)),
        name="moe_router",
    )(x, gain.reshape(1, k).astype(F32), wr, br)


def _row_copy(src, dst, src_row, dst_row, sem):
    return pltpu.make_async_copy(src.at[pl.ds(src_row, 1)], dst.at[pl.ds(dst_row, 1)], sem)


def _scatter_body(d_ref, x_ref, g_ref, xs_in_ref, xs_ref, buf_ref, sem):
    del xs_in_ref
    tm = buf_ref.shape[0]
    buf_ref[...] = _rms(x_ref[...], g_ref[...])

    def issue(t, carry):
        for kk in range(2):
            _row_copy(buf_ref, xs_ref, t, d_ref[0, 0, 2 * t + kk], sem).start()
        return carry

    lax.fori_loop(0, tm, issue, 0)

    def drain(t, carry):
        for kk in range(2):
            _row_copy(buf_ref, xs_ref, 0, 0, sem).wait()
        return carry

    lax.fori_loop(0, tm, drain, 0)


def _moe_scatter(x, gain, dest_tiles, rows, tm):
    m, k = x.shape
    return pl.pallas_call(
        _scatter_body,
        grid=(m // tm,),
        in_specs=[
            pl.BlockSpec((1, 1, 2 * tm), lambda i: (i, 0, 0), memory_space=pltpu.SMEM),
            pl.BlockSpec((tm, k), lambda i: (i, 0)),
            pl.BlockSpec((1, k), lambda i: (0, 0)),
            pl.BlockSpec(memory_space=pl.ANY),
        ],
        out_specs=pl.BlockSpec(memory_space=pl.ANY),
        out_shape=jax.ShapeDtypeStruct((rows, k), F32),
        scratch_shapes=[pltpu.VMEM((tm, k), F32), pltpu.SemaphoreType.DMA(())],
        input_output_aliases={3: 0},
        compiler_params=_params(("arbitrary",)),
        name="moe_scatter",
    )(dest_tiles, x, gain.reshape(1, k).astype(F32), jnp.zeros((rows, k), F32))


def _ffn_group_body(te_ref, nu_ref, x_ref, wg_ref, wu_ref, wd_ref, o_ref):
    del te_ref
    used = pl.program_id(0) < nu_ref[0]

    @pl.when(used)
    def _():
        xn = x_ref[...].astype(BF16)
        gate = jnp.dot(xn, wg_ref[...], preferred_element_type=F32)
        up = jnp.dot(xn, wu_ref[...], preferred_element_type=F32)
        act = (_silu(gate) * up).astype(BF16)
        o_ref[...] = jnp.dot(act, wd_ref[...], preferred_element_type=F32)

    @pl.when(jnp.logical_not(used))
    def _():
        o_ref[...] = jnp.zeros_like(o_ref)


def _ffn_grouped(xs, tile_expert, n_used, w_gu, w_down, tm):
    rows, k = xs.shape
    ff = w_down.shape[1]
    nt = rows // tm

    def row_map(i, te, nu):
        return (jnp.minimum(i, nu[0] - 1), 0)

    grid_spec = pltpu.PrefetchScalarGridSpec(
        num_scalar_prefetch=2,
        grid=(nt,),
        in_specs=[
            pl.BlockSpec((tm, k), row_map),
            pl.BlockSpec((None, k, ff), lambda i, te, nu: (te[i], 0, 0)),
            pl.BlockSpec((None, k, ff), lambda i, te, nu: (te[i], 0, 1)),
            pl.BlockSpec((None, ff, k), lambda i, te, nu: (te[i], 0, 0)),
        ],
        out_specs=pl.BlockSpec((tm, k), lambda i, te, nu: (i, 0)),
    )
    return pl.pallas_call(
        _ffn_group_body,
        grid_spec=grid_spec,
        out_shape=jax.ShapeDtypeStruct((rows, k), F32),
        compiler_params=_params(("arbitrary",)),
        name="moe_experts",
    )(tile_expert, n_used, xs, w_gu, w_gu, w_down)


def _combine_body(d_ref, x_ref, gate_ref, ys_ref, o_ref, buf_ref, sem):
    tm = x_ref.shape[0]

    def issue(t, carry):
        for kk in range(2):
            _row_copy(ys_ref, buf_ref.at[kk], d_ref[0, 0, 2 * t + kk], t, sem).start()
        return carry

    lax.fori_loop(0, tm, issue, 0)

    def drain(t, carry):
        for kk in range(2):
            _row_copy(ys_ref, buf_ref.at[kk], 0, 0, sem).wait()
        return carry

    lax.fori_loop(0, tm, drain, 0)
    g = gate_ref[...]
    o_ref[...] = x_ref[...] + g[:, 0:1] * buf_ref[0] + g[:, 1:2] * buf_ref[1]


def _moe_combine(x, gates, dest_tiles, ys, tm):
    m, k = x.shape
    return pl.pallas_call(
        _combine_body,
        grid=(m // tm,),
        in_specs=[
            pl.BlockSpec((1, 1, 2 * tm), lambda i: (i, 0, 0), memory_space=pltpu.SMEM),
            pl.BlockSpec((tm, k), lambda i: (i, 0)),
            pl.BlockSpec((tm, LANES), lambda i: (i, 0)),
            pl.BlockSpec(memory_space=pl.ANY),
        ],
        out_specs=pl.BlockSpec((tm, k), lambda i: (i, 0)),
        out_shape=jax.ShapeDtypeStruct((m, k), F32),
        scratch_shapes=[pltpu.VMEM((2, tm, k), F32), pltpu.SemaphoreType.DMA(())],
        compiler_params=_params(("arbitrary",)),
        name="moe_combine",
    )(dest_tiles, x, gates, ys)


def _moe(x, gain, w_router, b_router, w_gu, w_down):
    m, k = x.shape
    tm_tok = min(256, m)
    tm_grp = 512 if m >= 4096 else 128
    idx, gates, counts = _router(x, gain, w_router, b_router)
    counts = counts[0, :N_EXPERTS].astype(I32)
    padded = ((counts + tm_grp - 1) // tm_grp) * tm_grp
    ends = jnp.cumsum(padded)
    starts = ends - padded
    dest = starts[idx[:, 0:2]] + idx[:, 2:4]
    dest_tiles = dest.reshape(m // tm_tok, 1, 2 * tm_tok)
    rows = ((2 * m + N_EXPERTS * (tm_grp - 1)) // tm_grp) * tm_grp
    nt = rows // tm_grp
    n_used = (ends[-1] // tm_grp).astype(I32)
    tile_start = jnp.minimum(jnp.arange(nt, dtype=I32), n_used - 1) * tm_grp
    tile_expert = jnp.sum(tile_start[:, None] >= ends[None, :], axis=1).astype(I32)
    xs = _moe_scatter(x, gain, dest_tiles, rows, tm_tok)
    ys = _ffn_grouped(xs, tile_expert, n_used.reshape(1), w_gu, w_down, tm_grp)
    return _moe_combine(x, gates, dest_tiles, ys, tm_tok)


def _head_cols(x, h, width=GDN_DK):
    return x[:, h * width:(h + 1) * width]


def _gdn_prep_body(x_ref, halo_ref, ab_ref, cw_ref, alog_ref, dtb_ref,
                   q_ref, qg_ref, k_ref, kb_ref, kbg_ref, kd_ref, vb_ref,
                   gc_ref, gct_ref, egl_ref, ext_ref):
    i = pl.program_id(1)
    tt = x_ref.shape[0]
    ext_ref[0:8, :] = jnp.where(i == 0, 0.0, halo_ref[...])
    ext_ref[8:, :] = x_ref[...]
    base = 8 - (CONV_TAPS - 1)
    acc = ext_ref[pl.ds(base, tt), :] * cw_ref[0:1, :]
    for j in range(1, CONV_TAPS):
        acc = acc + ext_ref[pl.ds(base + j, tt), :] * cw_ref[j:j + 1, :]
    y = _silu(acc)

    ab = ab_ref[...]
    g = -jnp.exp(alog_ref[...]) * jax.nn.softplus(ab + dtb_ref[...])
    beta = jax.nn.sigmoid(pltpu.roll(ab, LANES - GDN_HEADS, axis=1))
    r = lax.broadcasted_iota(I32, (tt, tt), 0)
    c = lax.broadcasted_iota(I32, (tt, tt), 1)
    same = (r // CHUNK) == (c // CHUNK)
    tril = jnp.where(same & (c <= r), 1.0, 0.0)
    gc = jnp.dot(tril, g, precision=HI, preferred_element_type=F32)
    gl = jnp.dot(jnp.where(same, 1.0, 0.0), g, precision=HI, preferred_element_type=F32)
    e_gc = jnp.exp(gc)
    e_gd = jnp.exp(gl - gc)
    gc_ref[...] = gc
    gct = gc.T
    sub = lax.broadcasted_iota(I32, (8, LANES), 0)
    lan = lax.broadcasted_iota(I32, (8, LANES), 1)
    for cc in range(tt // CHUNK):
        gct_ref[cc] = gct[0:8, cc * CHUNK:(cc + 1) * CHUNK]
        row = jnp.broadcast_to(gl[cc * CHUNK:cc * CHUNK + 1, :], (8, LANES))
        val = jnp.sum(jnp.where(sub == lan, row, 0.0), axis=-1, keepdims=True)
        egl_ref[cc] = jnp.broadcast_to(jnp.exp(val), (8, LANES))

    nq = GDN_HEADS * GDN_DK
    for h in range(GDN_HEADS):
        qh = _head_cols(y, h)
        kh = _head_cols(y, GDN_HEADS + h)
        vh = _head_cols(y, 2 * GDN_HEADS + h)
        qn = qh * lax.rsqrt(jnp.sum(qh * qh, axis=-1, keepdims=True) + EPS) * (GDN_DK ** -0.5)
        kn = kh * lax.rsqrt(jnp.sum(kh * kh, axis=-1, keepdims=True) + EPS)
        b_h = beta[:, h:h + 1]
        eg_h = e_gc[:, h:h + 1]
        sl = slice(h * GDN_DK, (h + 1) * GDN_DK)
        kb = kn * b_h
        q_ref[:, sl] = qn.astype(BF16)
        qg_ref[:, sl] = (qn * eg_h).astype(BF16)
        k_ref[:, sl] = kn.astype(BF16)
        kb_ref[:, sl] = kb.astype(BF16)
        kbg_ref[:, sl] = (kb * eg_h).astype(BF16)
        kd_ref[:, sl] = (kn * e_gd[:, h:h + 1]).astype(BF16)
        vb_ref[:, sl] = (vh * b_h).astype(BF16)
    del nq


def _gdn_prep(proj, conv_w, a_log, dt_bias, tt=256):
    b, t, _ = proj.shape
    nc = t // CHUNK
    cpt = tt // CHUNK
    row = lambda v: jnp.zeros((1, LANES), F32).at[0, :GDN_HEADS].set(v)
    tok = pl.BlockSpec((None, tt, D_MODEL), lambda bi, i: (bi, i, 0))
    out_shapes = [jax.ShapeDtypeStruct((b, t, D_MODEL), BF16)] * 7 + [
        jax.ShapeDtypeStruct((b, t, LANES), F32),
        jax.ShapeDtypeStruct((b, nc, 8, CHUNK), F32),
        jax.ShapeDtypeStruct((b, nc, 8, LANES), F32),
    ]
    out_specs = [tok] * 7 + [
        pl.BlockSpec((None, tt, LANES), lambda bi, i: (bi, i, 0)),
        pl.BlockSpec((None, cpt, 8, CHUNK), lambda bi, i: (bi, i, 0, 0)),
        pl.BlockSpec((None, cpt, 8, LANES), lambda bi, i: (bi, i, 0, 0)),
    ]
    return pl.pallas_call(
        _gdn_prep_body,
        grid=(b, t // tt),
        in_specs=[
            pl.BlockSpec((None, tt, QKV_W), lambda bi, i: (bi, i, 0)),
            pl.BlockSpec((None, 8, QKV_W), lambda bi, i: (bi, jnp.maximum(i * (tt // 8) - 1, 0), 0)),
            pl.BlockSpec((None, tt, LANES), lambda bi, i: (bi, i, (QKV_W + D_MODEL) // LANES)),
            pl.BlockSpec((CONV_TAPS, QKV_W), lambda bi, i: (0, 0)),
            pl.BlockSpec((1, LANES), lambda bi, i: (0, 0)),
            pl.BlockSpec((1, LANES), lambda bi, i: (0, 0)),
        ],
        out_specs=out_specs,
        out_shape=out_shapes,
        scratch_shapes=[pltpu.VMEM((tt + 8, QKV_W), F32)],
        compiler_params=_params(("parallel", "arbitrary")),
        name="gdn_prep",
    )(proj, proj, proj, conv_w.astype(F32), row(a_log), row(dt_bias))


def _decay(gc_col, gc_row, keep):
    return jnp.where(keep, jnp.exp(jnp.where(keep, gc_col - gc_row, 0.0)), 0.0)


def _gdn_l_body(kb_ref, k_ref, gc_ref, gct_ref, l_ref):
    r = lax.broadcasted_iota(I32, (CHUNK, CHUNK), 0)
    c = lax.broadcasted_iota(I32, (CHUNK, CHUNK), 1)
    strict = c < r
    for cc in range(l_ref.shape[0]):
        rows = slice(cc * CHUNK, (cc + 1) * CHUNK)
        for h in range(GDN_HEADS):
            cols = slice(h * GDN_DK, (h + 1) * GDN_DK)
            kk = lax.dot_general(kb_ref[rows, cols], k_ref[rows, cols], (((1,), (1,)), ((), ())),
                                 preferred_element_type=F32)
            dec = _decay(gc_ref[rows, h:h + 1], gct_ref[cc, h:h + 1, :], strict)
            l_ref[cc, h] = kk * dec


def _gdn_l(kb, k, gc, gct, cpt=4):
    b, t, _ = k.shape
    nc = t // CHUNK
    tt = cpt * CHUNK
    tok = pl.BlockSpec((None, tt, D_MODEL), lambda bi, i: (bi, i, 0))
    return pl.pallas_call(
        _gdn_l_body,
        grid=(b, nc // cpt),
        in_specs=[tok, tok,
                  pl.BlockSpec((None, tt, LANES), lambda bi, i: (bi, i, 0)),
                  pl.BlockSpec((None, cpt, 8, CHUNK), lambda bi, i: (bi, i, 0, 0))],
        out_specs=pl.BlockSpec((None, cpt, GDN_HEADS, CHUNK, CHUNK), lambda bi, i: (bi, i, 0, 0, 0)),
        out_shape=jax.ShapeDtypeStruct((b, nc, GDN_HEADS, CHUNK, CHUNK), F32),
        compiler_params=_params(("parallel", "parallel")),
        name="gdn_l",
    )(kb, k, gc, gct)


def _tri_inv_body(l_ref, t_ref):
    sub = lax.broadcasted_iota(I32, (8, LANES), 0)
    for i in range(CHUNK):
        nq = i // 8 + 1
        acc = [jnp.where(sub == i % 8, 1.0, 0.0) if q == i // 8 else jnp.zeros((8, LANES), F32)
               for q in range(nq)]
        for j in range(i):
            lij = l_ref[i * CHUNK + j:i * CHUNK + j + 1, :]
            for q in range(j // 8 + 1):
                acc[q] = acc[q] - lij * t_ref[j * CHUNK + 8 * q:j * CHUNK + 8 * q + 8, :]
        for q in range(CHUNK // 8):
            t_ref[i * CHUNK + 8 * q:i * CHUNK + 8 * q + 8, :] = (
                acc[q] if q < nq else jnp.zeros((8, LANES), F32))


def _tri_inv(lt):
    n, p = lt.shape
    spec = pl.BlockSpec((n, LANES), lambda i: (0, i))
    return pl.pallas_call(
        _tri_inv_body,
        grid=(p // LANES,),
        in_specs=[spec],
        out_specs=spec,
        out_shape=jax.ShapeDtypeStruct((n, p), F32),
        compiler_params=_params(("parallel",)),
        name="gdn_tri_inv",
    )(lt)


def _gdn_chunk_body(q_ref, qg_ref, k_ref, kbg_ref, kd_ref, vb_ref, t_ref, gc_ref, gct_ref, egl_ref,
                    o_ref, s_out_ref, s_ref):
    c = pl.program_id(1)

    @pl.when(c == 0)
    def _():
        s_ref[...] = jnp.zeros_like(s_ref)

    r = lax.broadcasted_iota(I32, (CHUNK, CHUNK), 0)
    cidx = lax.broadcasted_iota(I32, (CHUNK, CHUNK), 1)
    tril = cidx <= r
    for h in range(GDN_HEADS):
        cols = slice(h * GDN_DK, (h + 1) * GDN_DK)
        s_h = s_ref[h]
        s_b = s_h.astype(BF16)
        rhs = jnp.concatenate([vb_ref[:, cols], kbg_ref[:, cols]], axis=1)
        sol = jnp.dot(t_ref[h].astype(BF16), rhs, preferred_element_type=F32)
        u = sol[:, :GDN_DK]
        wk = sol[:, GDN_DK:]
        v_new = u - jnp.dot(wk.astype(BF16), s_b, preferred_element_type=F32)
        v_nb = v_new.astype(BF16)
        qk = lax.dot_general(q_ref[:, cols], k_ref[:, cols], (((1,), (1,)), ((), ())),
                             preferred_element_type=F32)
        attn = qk * _decay(gc_ref[:, h:h + 1], gct_ref[h:h + 1, :], tril)
        o_ref[:, cols] = (jnp.dot(qg_ref[:, cols], s_b, preferred_element_type=F32)
                          + jnp.dot(attn.astype(BF16), v_nb, preferred_element_type=F32))
        s_ref[h] = s_h * egl_ref[h:h + 1, :] + lax.dot_general(
            kd_ref[:, cols], v_nb, (((0,), (0,)), ((), ())), preferred_element_type=F32)

    @pl.when(c == pl.num_programs(1) - 1)
    def _():
        s_out_ref[...] = s_ref[...]


def _gdn_chunks(q, qg, k, kbg, kd, vb, tinv, gc, gct, egl):
    b, t, _ = q.shape
    nc = t // CHUNK
    tok = pl.BlockSpec((None, CHUNK, D_MODEL), lambda bi, c: (bi, c, 0))
    return pl.pallas_call(
        _gdn_chunk_body,
        grid=(b, nc),
        in_specs=[tok] * 6 + [
            pl.BlockSpec((None, None, GDN_HEADS, CHUNK, CHUNK), lambda bi, c: (bi, c, 0, 0, 0)),
            pl.BlockSpec((None, CHUNK, LANES), lambda bi, c: (bi, c, 0)),
            pl.BlockSpec((None, None, 8, CHUNK), lambda bi, c: (bi, c, 0, 0)),
            pl.BlockSpec((None, None, 8, LANES), lambda bi, c: (bi, c, 0, 0)),
        ],
        out_specs=[
            pl.BlockSpec((None, CHUNK, D_MODEL), lambda bi, c: (bi, c, 0)),
            pl.BlockSpec((None, GDN_HEADS, GDN_DK, GDN_DK), lambda bi, c: (bi, 0, 0, 0)),
        ],
        out_shape=[
            jax.ShapeDtypeStruct((b, t, D_MODEL), F32),
            jax.ShapeDtypeStruct((b, GDN_HEADS, GDN_DK, GDN_DK), F32),
        ],
        scratch_shapes=[pltpu.VMEM((GDN_HEADS, GDN_DK, GDN_DK), F32)],
        compiler_params=_params(("parallel", "arbitrary")),
        name="gdn_chunks",
    )(q, qg, k, kbg, kd, vb, tinv, gc, gct, egl)


def _gdn_out_body(o_ref, z_ref, on_ref, w_ref, r_ref, out_ref):
    parts = []
    for h in range(GDN_HEADS):
        oh = _head_cols(o_ref[...], h)
        zh = _head_cols(z_ref[...], h)
        parts.append((_rms(oh, on_ref[...]) * _silu(zh)).astype(BF16))
    gated = jnp.concatenate(parts, axis=1)
    out_ref[...] = r_ref[...] + jnp.dot(gated, w_ref[...], preferred_element_type=F32)


def _gdn_out(o, proj, o_norm, w_out, res, tm=512):
    m, k = o.shape
    tm = min(tm, m)
    return pl.pallas_call(
        _gdn_out_body,
        grid=(m // tm,),
        in_specs=[
            pl.BlockSpec((tm, k), lambda i: (i, 0)),
            pl.BlockSpec((tm, k), lambda i: (i, QKV_W // D_MODEL)),
            pl.BlockSpec((1, GDN_DK), lambda i: (0, 0)),
            pl.BlockSpec((k, k), lambda i: (0, 0)),
            pl.BlockSpec((tm, k), lambda i: (i, 0)),
        ],
        out_specs=pl.BlockSpec((tm, k), lambda i: (i, 0)),
        out_shape=jax.ShapeDtypeStruct((m, k), F32),
        compiler_params=_params(("parallel",)),
        name="gdn_out",
    )(o, proj, o_norm.reshape(1, GDN_DK).astype(F32), w_out, res)


def _gdn_prompt_layer(h, l, p, w):
    b, t, _ = h.shape
    hf = h.reshape(b * t, D_MODEL)
    proj = _mm(hf, w['a_w_in'][l], gain=p['a_norm'][l], tn=PROJ_W // 3)
    proj3 = proj.reshape(b, t, PROJ_W)
    q, qg, k, kb, kbg, kd, vb, gc, gct, egl = _gdn_prep(
        proj3, p['a_conv_w'][l], p['a_A_log'][l], p['a_dt_bias'][l])
    lmat = _gdn_l(kb, k, gc, gct)
    nprob = b * (t // CHUNK) * GDN_HEADS
    tt = _tri_inv(lmat.reshape(nprob, CHUNK * CHUNK).T)
    tinv = tt.T.reshape(b, t // CHUNK, GDN_HEADS, CHUNK, CHUNK)
    o, s_new = _gdn_chunks(q, qg, k, kbg, kd, vb, tinv, gc, gct, egl)
    h_new = _gdn_out(o.reshape(b * t, D_MODEL), proj, p['a_o_norm'][l], w['a_w_out'][l], hf)
    conv_new = proj3[:, t - (CONV_TAPS - 1):, :QKV_W]
    return h_new.reshape(b, t, D_MODEL), s_new, conv_new


def _gdn_prep1_body(x_ref, st_ref, ab_ref, cw_ref, alog_ref, dtb_ref,
                    q_ref, k_ref, v_ref, eg_ref, beta_ref):
    acc = st_ref[0] * cw_ref[0:1, :]
    for j in range(1, CONV_TAPS - 1):
        acc = acc + st_ref[j] * cw_ref[j:j + 1, :]
    acc = acc + x_ref[...] * cw_ref[CONV_TAPS - 1:CONV_TAPS, :]
    y = _silu(acc)
    ab = ab_ref[...]
    eg_ref[...] = jnp.exp(-jnp.exp(alog_ref[...]) * jax.nn.softplus(ab + dtb_ref[...]))
    beta_ref[...] = jax.nn.sigmoid(pltpu.roll(ab, LANES - GDN_HEADS, axis=1))
    for h in range(GDN_HEADS):
        qh = _head_cols(y, h)
        kh = _head_cols(y, GDN_HEADS + h)
        sl = slice(h * GDN_DK, (h + 1) * GDN_DK)
        q_ref[:, sl] = qh * lax.rsqrt(jnp.sum(qh * qh, axis=-1, keepdims=True) + EPS) * (GDN_DK ** -0.5)
        k_ref[:, sl] = kh * lax.rsqrt(jnp.sum(kh * kh, axis=-1, keepdims=True) + EPS)
    v_ref[...] = y[:, 2 * GDN_HEADS * GDN_DK:]


def _gdn_prep1(proj, conv_state, conv_w, a_log, dt_bias):
    n = proj.shape[0]
    row = lambda v: jnp.zeros((1, LANES), F32).at[0, :GDN_HEADS].set(v)
    tok = pl.BlockSpec((n, D_MODEL), lambda i: (0, 0))
    small = pl.BlockSpec((n, LANES), lambda i: (0, 0))
    return pl.pallas_call(
        _gdn_prep1_body,
        grid=(1,),
        in_specs=[
            pl.BlockSpec((n, QKV_W), lambda i: (0, 0)),
            pl.BlockSpec((CONV_TAPS - 1, n, QKV_W), lambda i: (0, 0, 0)),
            pl.BlockSpec((n, LANES), lambda i: (0, (QKV_W + D_MODEL) // LANES)),
            pl.BlockSpec((CONV_TAPS, QKV_W), lambda i: (0, 0)),
            pl.BlockSpec((1, LANES), lambda i: (0, 0)),
            pl.BlockSpec((1, LANES), lambda i: (0, 0)),
        ],
        out_specs=[tok, tok, tok, small, small],
        out_shape=[jax.ShapeDtypeStruct((n, D_MODEL), F32)] * 3 + [jax.ShapeDtypeStruct((n, LANES), F32)] * 2,
        compiler_params=_params(("arbitrary",)),
        name="gdn_prep1",
    )(proj, conv_state, proj, conv_w.astype(F32), row(a_log), row(dt_bias))


def _gdn_step_body(qt_ref, kt_ref, v_ref, eg_ref, beta_ref, s_ref, o_ref, s_out_ref):
    for h in range(GDN_HEADS):
        k_col = kt_ref[:, h:h + 1]
        q_col = qt_ref[:, h:h + 1]
        s = s_ref[h] * eg_ref[h:h + 1, :]
        kv_mem = jnp.sum(k_col * s, axis=0, keepdims=True)
        delta = (v_ref[h:h + 1, :] - kv_mem) * beta_ref[h:h + 1, :]
        s = s + k_col * delta
        s_out_ref[h] = s
        o_ref[h:h + 1, :] = jnp.sum(q_col * s, axis=0, keepdims=True)


def _gdn_step(qt, kt, v, eg, beta, state):
    n = v.shape[0]
    col = pl.BlockSpec((None, GDN_DK, GDN_HEADS), lambda i: (i, 0, 0))
    row = pl.BlockSpec((None, GDN_HEADS, GDN_DK), lambda i: (i, 0, 0))
    st = pl.BlockSpec((None, GDN_HEADS, GDN_DK, GDN_DK), lambda i: (i, 0, 0, 0))
    return pl.pallas_call(
        _gdn_step_body,
        grid=(n,),
        in_specs=[col, col, row, row, row, st],
        out_specs=[row, st],
        out_shape=[jax.ShapeDtypeStruct((n, GDN_HEADS, GDN_DK), F32),
                   jax.ShapeDtypeStruct(state.shape, F32)],
        compiler_params=_params(("parallel",)),
        name="gdn_step",
    )(qt, kt, v, eg, beta, state)


def _gdn_sample_layer(h, l, p, w, conv_state, delta_state):
    n = h.shape[0]
    proj = _mm(h, w['a_w_in'][l], gain=p['a_norm'][l], tn=PROJ_W // 3)
    conv_t = jnp.transpose(conv_state, (1, 0, 2))
    q, k, v, eg, beta = _gdn_prep1(proj, conv_t, p['a_conv_w'][l], p['a_A_log'][l], p['a_dt_bias'][l])
    heads = lambda a: a.reshape(n, GDN_HEADS, GDN_DK)
    cols = lambda a: jnp.transpose(heads(a), (0, 2, 1))
    lanes = lambda a: jnp.broadcast_to(a[:, :GDN_HEADS, None], (n, GDN_HEADS, GDN_DK))
    o, s_new = _gdn_step(cols(q), cols(k), heads(v), lanes(eg), lanes(beta), delta_state)
    h_new = _gdn_out(o.reshape(n, D_MODEL), proj, p['a_o_norm'][l], w['a_w_out'][l], h)
    conv_new = jnp.concatenate([conv_state[:, 1:], proj[:, None, :QKV_W]], axis=1)
    return h_new, s_new, conv_new


def _rope_tables(pos):
    half = ROT // 2
    inv = ROPE_THETA ** (-jnp.arange(0, ROT, 2, dtype=F32) / ROT)
    ang = pos.astype(F32)[:, None] * inv[None, :]
    cos, sin = jnp.cos(ang), jnp.sin(ang)
    ones = jnp.ones((pos.shape[0], HEAD_DIM - ROT), F32)
    zeros = jnp.zeros_like(ones)
    z8 = jnp.zeros_like(sin)
    c = jnp.concatenate([cos, cos, ones], axis=1)
    sm = jnp.concatenate([-sin, z8, zeros], axis=1)
    sp = jnp.concatenate([z8, sin, zeros], axis=1)
    del half
    return tuple(jnp.tile(a, (1, LANES // HEAD_DIM)) for a in (c, sm, sp))


def _headnorm_rope_body(x_ref, g_ref, seg_ref, segt_ref, c_ref, sm_ref, sp_ref, o_ref):
    x = x_ref[...]
    w = x.shape[1]
    ms = jnp.dot(x * x, seg_ref[...], precision=HI, preferred_element_type=F32) * (1.0 / HEAD_DIM)
    scale = jnp.dot(lax.rsqrt(ms + EPS), segt_ref[...], precision=HI, preferred_element_type=F32)
    y = x * scale * g_ref[...]
    reps = w // LANES
    tile = lambda a: jnp.concatenate([a] * reps, axis=1) if reps > 1 else a
    half = ROT // 2
    out = (y * tile(c_ref[...]) + pltpu.roll(y, w - half, axis=1) * tile(sm_ref[...])
           + pltpu.roll(y, half, axis=1) * tile(sp_ref[...]))
    o_ref[...] = out.astype(o_ref.dtype)


def _headnorm_rope(x, col_block, width, gain, tables, rows_per_seq, out_dtype, tm=512):
    m = x.shape[0]
    tm = min(tm, m, rows_per_seq)
    nheads = width // HEAD_DIM
    seg = (jnp.arange(width)[:, None] // HEAD_DIM == jnp.arange(LANES)[None, :]).astype(F32)
    per_seq = rows_per_seq // tm
    tab = pl.BlockSpec((tm, LANES), lambda i: (i % per_seq, 0))
    del nheads
    return pl.pallas_call(
        _headnorm_rope_body,
        grid=(m // tm,),
        in_specs=[
            pl.BlockSpec((tm, width), lambda i: (i, col_block)),
            pl.BlockSpec((1, width), lambda i: (0, 0)),
            pl.BlockSpec((width, LANES), lambda i: (0, 0)),
            pl.BlockSpec((LANES, width), lambda i: (0, 0)),
            tab, tab, tab,
        ],
        out_specs=pl.BlockSpec((tm, width), lambda i: (i, 0)),
        out_shape=jax.ShapeDtypeStruct((m, width), out_dtype),
        compiler_params=_params(("parallel",)),
        name="headnorm_rope",
    )(x, jnp.tile(gain.astype(F32), width // HEAD_DIM).reshape(1, width), seg, seg.T, *tables)


def _softmax_sink(s, sink):
    m = jnp.maximum(jnp.max(s, axis=-1, keepdims=True), sink)
    p = jnp.exp(s - m)
    return p, jnp.sum(p, axis=-1, keepdims=True) + jnp.exp(sink - m)


def _swa_prompt_body(sink_ref, q_ref, kp_ref, kc_ref, vp_ref, vc_ref, o_ref):
    i = pl.program_id(1)
    blk = q_ref.shape[0]
    r = lax.broadcasted_iota(I32, (blk, 2 * blk), 0)
    c = lax.broadcasted_iota(I32, (blk, 2 * blk), 1)
    valid = (c >= r) & (c <= r + WINDOW) & ((c >= blk) | (i > 0))
    for kvh in range(KV_HEADS):
        cols = slice(kvh * HEAD_DIM, (kvh + 1) * HEAD_DIM)
        kk = jnp.concatenate([kp_ref[:, cols], kc_ref[:, cols]], axis=0).astype(BF16)
        vv = jnp.concatenate([vp_ref[:, cols], vc_ref[:, cols]], axis=0).astype(BF16)
        for g in range(ATT_GROUP):
            h = kvh * ATT_GROUP + g
            hc = slice(h * HEAD_DIM, (h + 1) * HEAD_DIM)
            s = lax.dot_general(q_ref[:, hc], kk, (((1,), (1,)), ((), ())),
                                preferred_element_type=F32) * (HEAD_DIM ** -0.5)
            s = jnp.where(valid, s, -jnp.inf)
            p, den = _softmax_sink(s, sink_ref[h])
            o_ref[:, hc] = jnp.dot((p / den).astype(BF16), vv, preferred_element_type=F32).astype(o_ref.dtype)


def _swa_prompt(q, k, kv, sinks):
    b, t, _ = q.shape
    blk = WINDOW
    kvw = KV_HEADS * HEAD_DIM
    prev = lambda bi, i, s: (bi, jnp.maximum(i - 1, 0), 0)
    cur = lambda bi, i, s: (bi, i, 0)
    prev_v = lambda bi, i, s: (bi, jnp.maximum(i - 1, 0), 1)
    cur_v = lambda bi, i, s: (bi, i, 1)
    grid_spec = pltpu.PrefetchScalarGridSpec(
        num_scalar_prefetch=1,
        grid=(b, t // blk),
        in_specs=[
            pl.BlockSpec((None, blk, D_MODEL), cur),
            pl.BlockSpec((None, blk, kvw), prev),
            pl.BlockSpec((None, blk, kvw), cur),
            pl.BlockSpec((None, blk, kvw), prev_v),
            pl.BlockSpec((None, blk, kvw), cur_v),
        ],
        out_specs=pl.BlockSpec((None, blk, D_MODEL), cur),
    )
    return pl.pallas_call(
        _swa_prompt_body,
        grid_spec=grid_spec,
        out_shape=jax.ShapeDtypeStruct((b, t, D_MODEL), BF16),
        compiler_params=_params(("parallel", "parallel")),
        name="swa_prompt",
    )(sinks.astype(F32), q, k, k, kv, kv)


def _swa_sample_body(sink_ref, q_ref, kc_ref, kn_ref, vc_ref, vn_ref, o_ref):
    sub = lax.broadcasted_iota(I32, (ATT_GROUP, 1), 0)
    for b in range(q_ref.shape[0]):
        for kvh in range(KV_HEADS):
            cols = slice(kvh * HEAD_DIM, (kvh + 1) * HEAD_DIM)
            heads = slice(kvh * ATT_GROUP, (kvh + 1) * ATT_GROUP)
            sink = jnp.zeros((ATT_GROUP, 1), F32)
            for g in range(ATT_GROUP):
                sink = jnp.where(sub == g, sink_ref[kvh * ATT_GROUP + g], sink)
            qg = q_ref[b, heads, :]
            scale = HEAD_DIM ** -0.5
            s_c = lax.dot_general(qg.astype(BF16), kc_ref[b, :, cols].astype(BF16), (((1,), (1,)), ((), ())),
                                  preferred_element_type=F32) * scale
            s_n = jnp.sum(qg * kn_ref[b, :, cols], axis=-1, keepdims=True) * scale
            m = jnp.maximum(jnp.maximum(jnp.max(s_c, axis=-1, keepdims=True), s_n), sink)
            p_c = jnp.exp(s_c - m)
            p_n = jnp.exp(s_n - m)
            den = jnp.sum(p_c, axis=-1, keepdims=True) + p_n + jnp.exp(sink - m)
            o = jnp.dot((p_c / den).astype(BF16), vc_ref[b, :, cols].astype(BF16), preferred_element_type=F32)
            o_ref[b, heads, :] = o + (p_n / den) * vn_ref[b, :, cols]


def _swa_sample(q, k_cache, k_new, v_cache, v_new, sinks, sb=8):
    n = q.shape[0]
    wc = k_cache.shape[1]
    kvw = KV_HEADS * HEAD_DIM
    blk = lambda shape: pl.BlockSpec(shape, lambda i, s: (i, 0, 0))
    grid_spec = pltpu.PrefetchScalarGridSpec(
        num_scalar_prefetch=1,
        grid=(n // sb,),
        in_specs=[blk((sb, ATT_HEADS, HEAD_DIM)), blk((sb, wc, kvw)), blk((sb, 1, kvw)),
                  blk((sb, wc, kvw)), blk((sb, 1, kvw))],
        out_specs=blk((sb, ATT_HEADS, HEAD_DIM)),
    )
    return pl.pallas_call(
        _swa_sample_body,
        grid_spec=grid_spec,
        out_shape=jax.ShapeDtypeStruct((n, ATT_HEADS, HEAD_DIM), F32),
        compiler_params=_params(("parallel",)),
        name="swa_sample",
    )(sinks.astype(F32), q, k_cache, k_new, v_cache, v_new)


def _channel_mixer(hf, l, p, w):
    if l % 2 == 0:
        return _ffn_dense(hf, p['ffn_norm'][l], w['dense_w_gu'][l // 2], w['dense_w_down'][l // 2])
    return _moe(hf, p['ffn_norm'][l], p['moe_router'][l // 2], p['moe_router_b'][l // 2],
                w['moe_w_gu'][l // 2], w['moe_w_down'][l // 2])


def _run_prompt(x, p, w):
    b, t, _ = x.shape
    n = b * t
    h = x
    convs, deltas = [], []
    for l in range(N_GDN):
        h, s_new, c_new = _gdn_prompt_layer(h, l, p, w)
        convs.append(c_new)
        deltas.append(s_new)
        h = _channel_mixer(h.reshape(n, D_MODEL), l, p, w).reshape(b, t, D_MODEL)
    hf = h.reshape(n, D_MODEL)
    tables = _rope_tables(jnp.arange(t, dtype=I32))
    kvw = KV_HEADS * HEAD_DIM
    kv = _mm(hf, w['w_kv'], gain=p['kv_norm'], tn=2 * kvw)
    k_sh = _headnorm_rope(kv, 0, kvw, p['k_norm'], tables, t, F32)
    k3 = k_sh.reshape(b, t, kvw)
    kv3 = kv.reshape(b, t, 2 * kvw)
    for j in range(2):
        l = N_GDN + j
        qp = _mm(hf, w['b_w_q'][j], gain=p['b_norm'][j])
        q = _headnorm_rope(qp, 0, D_MODEL, p['b_q_norm'][j], tables, t, BF16)
        o = _swa_prompt(q.reshape(b, t, D_MODEL), k3, kv3, p['b_sinks'][j])
        hf = _mm(o.reshape(n, D_MODEL), w['b_w_o'][j], res=hf)
        hf = _channel_mixer(hf, l, p, w)
    k_win = k3[:, t - WINDOW:].reshape(b, WINDOW, KV_HEADS, HEAD_DIM)
    v_win = kv3[:, t - WINDOW:, kvw:].reshape(b, WINDOW, KV_HEADS, HEAD_DIM)
    return hf.reshape(b, t, D_MODEL), jnp.stack(convs), jnp.stack(deltas), k_win, v_win


def _run_sample(x, conv_state, delta_state, k_cache, v_cache, p, w):
    n = x.shape[0]
    h = x.reshape(n, D_MODEL)
    convs, deltas = [], []
    for l in range(N_GDN):
        h, s_new, c_new = _gdn_sample_layer(h, l, p, w, conv_state[l], delta_state[l])
        convs.append(c_new)
        deltas.append(s_new)
        h = _channel_mixer(h, l, p, w)
    tables = _rope_tables(jnp.full((n,), PAST_LEN, I32))
    kvw = KV_HEADS * HEAD_DIM
    wc = k_cache.shape[1]
    kv = _mm(h, w['w_kv'], gain=p['kv_norm'], tn=2 * kvw)
    k_new = _headnorm_rope(kv, 0, kvw, p['k_norm'], tables, n, F32)
    v_new = kv[:, kvw:]
    kc = k_cache.reshape(n, wc, kvw)
    vc = v_cache.reshape(n, wc, kvw)
    for j in range(2):
        l = N_GDN + j
        qp = _mm(h, w['b_w_q'][j], gain=p['b_norm'][j])
        q = _headnorm_rope(qp, 0, D_MODEL, p['b_q_norm'][j], tables, n, F32)
        o = _swa_sample(q.reshape(n, ATT_HEADS, HEAD_DIM), kc, k_new[:, None, :], vc, v_new[:, None, :],
                        p['b_sinks'][j])
        h = _mm(o.reshape(n, D_MODEL), w['b_w_o'][j], res=h)
        h = _channel_mixer(h, l, p, w)
    k_win = jnp.concatenate([kc, k_new[:, None, :]], axis=1)[:, -wc:].reshape(n, wc, KV_HEADS, HEAD_DIM)
    v_win = jnp.concatenate([vc, v_new[:, None, :]], axis=1)[:, -wc:].reshape(n, wc, KV_HEADS, HEAD_DIM)
    return h.reshape(n, 1, D_MODEL), jnp.stack(convs), jnp.stack(deltas), k_win, v_win


def kernel(x_prompt, x_sample, state_conv, state_delta, cache_k_win, cache_v_win, a_norm, a_w_in, a_conv_w,
           a_A_log, a_dt_bias, a_o_norm, a_w_out, kv_norm, w_kv, k_norm, b_norm, b_w_q, b_q_norm, b_sinks,
           b_w_o, ffn_norm, dense_w_gu, dense_w_down, moe_router, moe_router_b, moe_w_gu, moe_w_down):
    p = dict(a_norm=a_norm, a_conv_w=a_conv_w, a_A_log=a_A_log, a_dt_bias=a_dt_bias, a_o_norm=a_o_norm,
             kv_norm=kv_norm, k_norm=k_norm, b_norm=b_norm, b_q_norm=b_q_norm, b_sinks=b_sinks,
             ffn_norm=ffn_norm, moe_router=moe_router, moe_router_b=moe_router_b)
    pad = PROJ_W - a_w_in.shape[-1]
    w = dict(
        a_w_in=jnp.pad(a_w_in, ((0, 0), (0, 0), (0, pad))).astype(BF16),
        a_w_out=a_w_out.astype(BF16), w_kv=w_kv.astype(BF16), b_w_q=b_w_q.astype(BF16),
        b_w_o=b_w_o.astype(BF16), dense_w_gu=dense_w_gu.astype(BF16), dense_w_down=dense_w_down.astype(BF16),
        moe_w_gu=moe_w_gu.astype(BF16), moe_w_down=moe_w_down.astype(BF16))
    y_p, conv_p, delta_p, kw_p, vw_p = _run_prompt(x_prompt, p, w)
    y_s, conv_s, delta_s, kw_s, vw_s = _run_sample(
        x_sample, state_conv, state_delta, cache_k_win, cache_v_win, p, w)
    return (y_p, y_s, conv_p, conv_s, delta_p, delta_s, kw_p, kw_s, vw_p, vw_s)
```

```python
import functools

import jax
import jax.numpy as jnp
from jax import lax
from jax.experimental import pallas as pl
from jax.experimental.pallas import tpu as pltpu

F32, BF16, I32 = jnp.float32, jnp.bfloat16, jnp.int32
HI = lax.Precision.HIGHEST

D_MODEL = 1024
N_GDN = 2
GDN_HEADS = 8
GDN_DK = 128
CONV_TAPS = 4
CHUNK = 64
QKV_W = 3 * GDN_HEADS * GDN_DK
PROJ_W = QKV_W + D_MODEL + 128
ATT_HEADS = 16
KV_HEADS = 4
HEAD_DIM = 64
ATT_GROUP = ATT_HEADS // KV_HEADS
WINDOW = 128
ROT = HEAD_DIM // 4
ROPE_THETA = 500000.0
PAST_LEN = 8192
N_EXPERTS = 8
EPS = 1e-6

LANES = 128
VMEM_LIMIT = 56 * 1024 * 1024
NEG = -1e30


def _params(sem):
    return pltpu.CompilerParams(dimension_semantics=sem, vmem_limit_bytes=VMEM_LIMIT)


def _rms(x, g):
    return x * lax.rsqrt(jnp.mean(x * x, axis=-1, keepdims=True) + EPS) * g


def _silu(x):
    return x * jax.nn.sigmoid(x)


def _split(x, barrier=False):
    hi = x.astype(BF16)
    if barrier:
        hi = lax.optimization_barrier(hi)
    return hi, (x - hi.astype(F32)).astype(BF16)


def _dot3(xh, xl, wh, wl):
    return (jnp.dot(xh, wh, preferred_element_type=F32)
            + (jnp.dot(xl, wh, preferred_element_type=F32) + jnp.dot(xh, wl, preferred_element_type=F32)))


def _hi_lo(w):
    return (w, None) if not isinstance(w, tuple) else w


def _mm_body(*refs, has_norm, has_res, precise):
    it = iter(refs)
    x_ref = next(it)
    g_ref = next(it) if has_norm else None
    w_ref = next(it)
    wl_ref = next(it) if precise else None
    r_ref = next(it) if has_res else None
    o_ref = next(it)
    xn_ref = next(it)

    @pl.when(pl.program_id(1) == 0)
    def _():
        x = x_ref[...].astype(F32)
        if has_norm:
            x = _rms(x, g_ref[...])
        if precise:
            xn_ref[0], xn_ref[1] = _split(x)
        else:
            xn_ref[0] = x.astype(BF16)

    if precise:
        acc = _dot3(xn_ref[0], xn_ref[1], w_ref[...], wl_ref[...])
    else:
        acc = jnp.dot(xn_ref[0], w_ref[...], preferred_element_type=F32)
    if has_res:
        acc = acc + r_ref[...]
    o_ref[...] = acc.astype(o_ref.dtype)


def _mm(x, w, gain=None, res=None, tm=1024, tn=1024, out_dtype=F32):
    w, w_lo = _hi_lo(w)
    precise = w_lo is not None
    m, k = x.shape
    n = w.shape[1]
    tm = min(tm, m)
    tn = min(tn, n)
    assert m % tm == 0 and n % tn == 0
    ins = [x]
    specs = [pl.BlockSpec((tm, k), lambda i, j: (i, 0))]
    if gain is not None:
        ins.append(gain.reshape(1, k).astype(F32))
        specs.append(pl.BlockSpec((1, k), lambda i, j: (0, 0)))
    for wi in (w, w_lo)[:1 + precise]:
        ins.append(wi)
        specs.append(pl.BlockSpec((k, tn), lambda i, j: (0, j)))
    if res is not None:
        ins.append(res)
        specs.append(pl.BlockSpec((tm, tn), lambda i, j: (i, j)))
    return pl.pallas_call(
        functools.partial(_mm_body, has_norm=gain is not None, has_res=res is not None, precise=precise),
        grid=(m // tm, n // tn),
        in_specs=specs,
        out_specs=pl.BlockSpec((tm, tn), lambda i, j: (i, j)),
        out_shape=jax.ShapeDtypeStruct((m, n), out_dtype),
        scratch_shapes=[pltpu.VMEM((1 + precise, tm, k), BF16)],
        compiler_params=_params(("parallel", "arbitrary")),
        name="mm",
    )(*ins)


def _swiglu(xh, xl, wg, wu, wd, wgl, wul, wdl):
    if wgl is None:
        gate = jnp.dot(xh, wg, preferred_element_type=F32)
        up = jnp.dot(xh, wu, preferred_element_type=F32)
        return jnp.dot((_silu(gate) * up).astype(BF16), wd, preferred_element_type=F32)
    gate = _dot3(xh, xl, wg, wgl)
    up = _dot3(xh, xl, wu, wul)
    ah, al = _split(_silu(gate) * up)
    return _dot3(ah, al, wd, wdl)


def _ffn_dense_body(*refs, precise):
    x_ref, g_ref, wg_ref, wu_ref, wd_ref = refs[:5]
    wgl_ref, wul_ref, wdl_ref = refs[5:8] if precise else (None,) * 3
    o_ref, xn_ref, acc_ref = refs[-3:]
    f = pl.program_id(1)

    @pl.when(f == 0)
    def _():
        xn = _rms(x_ref[...], g_ref[...])
        if precise:
            xn_ref[0], xn_ref[1] = _split(xn)
        else:
            xn_ref[0] = xn.astype(BF16)
        acc_ref[...] = x_ref[...]

    val = lambda r: None if r is None else r[...]
    acc_ref[...] += _swiglu(xn_ref[0], xn_ref[1] if precise else None, wg_ref[...], wu_ref[...], wd_ref[...],
                            val(wgl_ref), val(wul_ref), val(wdl_ref))

    @pl.when(f == pl.num_programs(1) - 1)
    def _():
        o_ref[...] = acc_ref[...]


def _ffn_dense(x, gain, w_gu, w_down, tm=512):
    w_gu, w_gu_lo = _hi_lo(w_gu)
    w_down, w_down_lo = _hi_lo(w_down)
    precise = w_gu_lo is not None
    m, k = x.shape
    ff = w_down.shape[0]
    tm = min(tm, m)
    tf = ff // 2 if (ff // 2) % LANES == 0 else ff
    nf = ff // tf
    w_specs = [
        pl.BlockSpec((k, tf), lambda i, f: (0, f)),
        pl.BlockSpec((k, tf), lambda i, f: (0, nf + f)),
        pl.BlockSpec((tf, k), lambda i, f: (f, 0)),
    ]
    lo = [w_gu_lo, w_gu_lo, w_down_lo] if precise else []
    return pl.pallas_call(
        functools.partial(_ffn_dense_body, precise=precise),
        grid=(m // tm, nf),
        in_specs=[
            pl.BlockSpec((tm, k), lambda i, f: (i, 0)),
            pl.BlockSpec((1, k), lambda i, f: (0, 0)),
        ] + w_specs + (w_specs if precise else []),
        out_specs=pl.BlockSpec((tm, k), lambda i, f: (i, 0)),
        out_shape=jax.ShapeDtypeStruct((m, k), F32),
        scratch_shapes=[pltpu.VMEM((1 + precise, tm, k), BF16), pltpu.VMEM((tm, k), F32)],
        compiler_params=_params(("parallel", "arbitrary")),
        name="ffn_dense",
    )(x, gain.reshape(1, k).astype(F32), w_gu, w_gu, w_down, *lo)


def _router_body(x_ref, g_ref, wr_ref, br_ref, idx_ref, gate_ref, cnt_ref, carry_ref):
    i = pl.program_id(0)

    @pl.when(i == 0)
    def _():
        carry_ref[...] = jnp.zeros_like(carry_ref)

    tm = x_ref.shape[0]
    xn = _rms(x_ref[...], g_ref[...])
    logits = jnp.dot(xn, wr_ref[...], precision=HI, preferred_element_type=F32) + br_ref[...]
    lane = lax.broadcasted_iota(I32, logits.shape, 1)
    m1 = jnp.max(logits, axis=-1, keepdims=True)
    i1 = jnp.min(jnp.where(logits == m1, lane, LANES), axis=-1, keepdims=True)
    rest = jnp.where(lane == i1, NEG, logits)
    m2 = jnp.max(rest, axis=-1, keepdims=True)
    i2 = jnp.min(jnp.where(rest == m2, lane, LANES), axis=-1, keepdims=True)
    e2 = jnp.exp(m2 - m1)
    g1 = 1.0 / (1.0 + e2)
    g2 = e2 * g1
    oh1 = lane == i1
    oh2 = lane == i2
    oh = jnp.where(oh1 | oh2, 1.0, 0.0).astype(BF16)
    r = lax.broadcasted_iota(I32, (tm, tm), 0)
    c = lax.broadcasted_iota(I32, (tm, tm), 1)
    below = jnp.where(c < r, 1.0, 0.0).astype(BF16)
    before = jnp.dot(below, oh, preferred_element_type=F32) + carry_ref[0:1, :]
    rk1 = jnp.sum(jnp.where(oh1, before, 0.0), axis=-1, keepdims=True).astype(I32)
    rk2 = jnp.sum(jnp.where(oh2, before, 0.0), axis=-1, keepdims=True).astype(I32)
    idx_ref[...] = jnp.where(lane == 0, i1, jnp.where(lane == 1, i2, jnp.where(lane == 2, rk1, rk2)))
    gate_ref[...] = jnp.where(lane == 0, g1, g2)
    total = carry_ref[0:1, :] + jnp.sum(oh.astype(F32), axis=0, keepdims=True)
    carry_ref[...] = jnp.broadcast_to(total, carry_ref.shape)
    cnt_ref[...] = carry_ref[...]


def _router(x, gain, w_router, b_router, tm=512):
    m, k = x.shape
    tm = min(tm, m)
    wr = jnp.zeros((k, LANES), F32).at[:, :N_EXPERTS].set(w_router)
    br = jnp.full((1, LANES), NEG, F32).at[0, :N_EXPERTS].set(b_router)
    return pl.pallas_call(
        _router_body,
        grid=(m // tm,),
        in_specs=[
            pl.BlockSpec((tm, k), lambda i: (i, 0)),
            pl.BlockSpec((1, k), lambda i: (0, 0)),
            pl.BlockSpec((k, LANES), lambda i: (0, 0)),
            pl.BlockSpec((1, LANES), lambda i: (0, 0)),
        ],
        out_specs=[
            pl.BlockSpec((tm, LANES), lambda i: (i, 0)),
            pl.BlockSpec((tm, LANES), lambda i: (i, 0)),
            pl.BlockSpec((8, LANES), lambda i: (0, 0)),
        ],
        out_shape=[
            jax.ShapeDtypeStruct((m, LANES), I32),
            jax.ShapeDtypeStruct((m, LANES), F32),
            jax.ShapeDtypeStruct((8, LANES), F32),
        ],
        scratch_shapes=[pltpu.VMEM((8, LANES), F32)],
        compiler_params=_params(("arbitrary",)),
        name="moe_router",
    )(x, gain.reshape(1, k).astype(F32), wr, br)


def _row_copy(src, dst, src_row, dst_row, sem):
    return pltpu.make_async_copy(src.at[pl.ds(src_row, 1)], dst.at[pl.ds(dst_row, 1)], sem)


def _scatter_body(d_ref, x_ref, g_ref, xs_in_ref, xs_ref, buf_ref, sem):
    del xs_in_ref
    tm = buf_ref.shape[0]
    buf_ref[...] = _rms(x_ref[...], g_ref[...])

    def issue(t, carry):
        for kk in range(2):
            _row_copy(buf_ref, xs_ref, t, d_ref[0, 0, 2 * t + kk], sem).start()
        return carry

    lax.fori_loop(0, tm, issue, 0, unroll=8)
    for kk in range(2):
        pltpu.make_async_copy(buf_ref, xs_ref.at[pl.ds(0, tm)], sem).wait()


def _moe_scatter(x, gain, dest_tiles, rows, tm):
    m, k = x.shape
    return pl.pallas_call(
        _scatter_body,
        grid=(m // tm,),
        in_specs=[
            pl.BlockSpec((1, 1, 2 * tm), lambda i: (i, 0, 0), memory_space=pltpu.SMEM),
            pl.BlockSpec((tm, k), lambda i: (i, 0)),
            pl.BlockSpec((1, k), lambda i: (0, 0)),
            pl.BlockSpec(memory_space=pl.ANY),
        ],
        out_specs=pl.BlockSpec(memory_space=pl.ANY),
        out_shape=jax.ShapeDtypeStruct((rows, k), F32),
        scratch_shapes=[pltpu.VMEM((tm, k), F32), pltpu.SemaphoreType.DMA(())],
        input_output_aliases={3: 0},
        compiler_params=_params(("arbitrary",)),
        name="moe_scatter",
    )(dest_tiles, x, gain.reshape(1, k).astype(F32), jnp.zeros((rows, k), F32))


def _ffn_group_body(*refs, precise):
    te_ref, nu_ref, x_ref, wg_ref, wu_ref, wd_ref = refs[:6]
    wgl_ref, wul_ref, wdl_ref = refs[6:9] if precise else (None,) * 3
    o_ref = refs[-1]
    del te_ref
    used = pl.program_id(0) < nu_ref[0]

    @pl.when(used)
    def _():
        val = lambda r: None if r is None else r[...]
        xh, xl = _split(x_ref[...]) if precise else (x_ref[...].astype(BF16), None)
        o_ref[...] = _swiglu(xh, xl, wg_ref[...], wu_ref[...], wd_ref[...],
                             val(wgl_ref), val(wul_ref), val(wdl_ref))

    @pl.when(jnp.logical_not(used))
    def _():
        o_ref[...] = jnp.zeros_like(o_ref)


def _ffn_grouped(xs, tile_expert, n_used, w_gu, w_down, tm):
    w_gu, w_gu_lo = _hi_lo(w_gu)
    w_down, w_down_lo = _hi_lo(w_down)
    precise = w_gu_lo is not None
    rows, k = xs.shape
    ff = w_down.shape[1]
    nt = rows // tm

    def row_map(i, te, nu):
        return (jnp.minimum(i, nu[0] - 1), 0)

    w_specs = [
        pl.BlockSpec((None, k, ff), lambda i, te, nu: (te[i], 0, 0)),
        pl.BlockSpec((None, k, ff), lambda i, te, nu: (te[i], 0, 1)),
        pl.BlockSpec((None, ff, k), lambda i, te, nu: (te[i], 0, 0)),
    ]
    lo = [w_gu_lo, w_gu_lo, w_down_lo] if precise else []
    grid_spec = pltpu.PrefetchScalarGridSpec(
        num_scalar_prefetch=2,
        grid=(nt,),
        in_specs=[pl.BlockSpec((tm, k), row_map)] + w_specs + (w_specs if precise else []),
        out_specs=pl.BlockSpec((tm, k), lambda i, te, nu: (i, 0)),
    )
    return pl.pallas_call(
        functools.partial(_ffn_group_body, precise=precise),
        grid_spec=grid_spec,
        out_shape=jax.ShapeDtypeStruct((rows, k), F32),
        compiler_params=_params(("arbitrary",)),
        name="moe_experts",
    )(tile_expert, n_used, xs, w_gu, w_gu, w_down, *lo)


def _combine_body(d_ref, x_ref, gate_ref, ys_ref, o_ref, buf_ref, sem):
    tm = x_ref.shape[0]

    def issue(t, carry):
        for kk in range(2):
            _row_copy(ys_ref, buf_ref.at[kk], d_ref[0, 0, 2 * t + kk], t, sem).start()
        return carry

    lax.fori_loop(0, tm, issue, 0, unroll=8)
    for kk in range(2):
        pltpu.make_async_copy(ys_ref.at[pl.ds(0, tm)], buf_ref.at[kk], sem).wait()
    g = gate_ref[...]
    o_ref[...] = x_ref[...] + g[:, 0:1] * buf_ref[0] + g[:, 1:2] * buf_ref[1]


def _moe_combine(x, gates, dest_tiles, ys, tm):
    m, k = x.shape
    return pl.pallas_call(
        _combine_body,
        grid=(m // tm,),
        in_specs=[
            pl.BlockSpec((1, 1, 2 * tm), lambda i: (i, 0, 0), memory_space=pltpu.SMEM),
            pl.BlockSpec((tm, k), lambda i: (i, 0)),
            pl.BlockSpec((tm, LANES), lambda i: (i, 0)),
            pl.BlockSpec(memory_space=pl.ANY),
        ],
        out_specs=pl.BlockSpec((tm, k), lambda i: (i, 0)),
        out_shape=jax.ShapeDtypeStruct((m, k), F32),
        scratch_shapes=[pltpu.VMEM((2, tm, k), F32), pltpu.SemaphoreType.DMA(())],
        compiler_params=_params(("arbitrary",)),
        name="moe_combine",
    )(dest_tiles, x, gates, ys)


def _moe(x, gain, w_router, b_router, w_gu, w_down):
    m, k = x.shape
    tm_tok = min(256, m)
    tm_grp = 512 if m >= 4096 else 128
    idx, gates, counts = _router(x, gain, w_router, b_router)
    counts = counts[0, :N_EXPERTS].astype(I32)
    padded = ((counts + tm_grp - 1) // tm_grp) * tm_grp
    ends = jnp.cumsum(padded)
    starts = ends - padded
    dest = starts[idx[:, 0:2]] + idx[:, 2:4]
    dest_tiles = dest.reshape(m // tm_tok, 1, 2 * tm_tok)
    rows = ((2 * m + N_EXPERTS * (tm_grp - 1)) // tm_grp) * tm_grp
    nt = rows // tm_grp
    n_used = (ends[-1] // tm_grp).astype(I32)
    tile_start = jnp.minimum(jnp.arange(nt, dtype=I32), n_used - 1) * tm_grp
    tile_expert = jnp.sum(tile_start[:, None] >= ends[None, :], axis=1).astype(I32)
    xs = _moe_scatter(x, gain, dest_tiles, rows, tm_tok)
    ys = _ffn_grouped(xs, tile_expert, n_used.reshape(1), w_gu, w_down, tm_grp)
    return _moe_combine(x, gates, dest_tiles, ys, tm_tok)


def _head_cols(x, h, width=GDN_DK):
    return x[:, h * width:(h + 1) * width]


def _gdn_prep_body(x_ref, halo_ref, ab_ref, cw_ref, alog_ref, dtb_ref,
                   q_ref, qg_ref, k_ref, kb_ref, kbg_ref, kd_ref, vb_ref,
                   gc_ref, gct_ref, egl_ref, ext_ref):
    i = pl.program_id(1)
    tt = x_ref.shape[0]
    ext_ref[0:8, :] = jnp.where(i == 0, 0.0, halo_ref[...])
    ext_ref[8:, :] = x_ref[...]
    ext = ext_ref[...]
    acc = pltpu.roll(ext, CONV_TAPS - 1, axis=0)[8:] * cw_ref[0:1, :]
    for j in range(1, CONV_TAPS - 1):
        acc = acc + pltpu.roll(ext, CONV_TAPS - 1 - j, axis=0)[8:] * cw_ref[j:j + 1, :]
    acc = acc + ext[8:] * cw_ref[CONV_TAPS - 1:CONV_TAPS, :]
    y = _silu(acc)

    ab = ab_ref[...]
    g = -jnp.exp(alog_ref[...]) * jax.nn.softplus(ab + dtb_ref[...])
    beta = jax.nn.sigmoid(pltpu.roll(ab, LANES - GDN_HEADS, axis=1))
    r = lax.broadcasted_iota(I32, (tt, tt), 0)
    c = lax.broadcasted_iota(I32, (tt, tt), 1)
    same = (r // CHUNK) == (c // CHUNK)
    tril = jnp.where(same & (c <= r), 1.0, 0.0)
    gc = jnp.dot(tril, g, precision=HI, preferred_element_type=F32)
    gl = jnp.dot(jnp.where(same, 1.0, 0.0), g, precision=HI, preferred_element_type=F32)
    e_gc = jnp.exp(gc)
    e_gd = jnp.exp(gl - gc)
    gc_ref[...] = gc
    gct = gc.T
    sub = lax.broadcasted_iota(I32, (8, LANES), 0)
    lan = lax.broadcasted_iota(I32, (8, LANES), 1)
    for cc in range(tt // CHUNK):
        gct_ref[cc] = gct[0:8, cc * CHUNK:(cc + 1) * CHUNK]
        row = jnp.broadcast_to(gl[cc * CHUNK:cc * CHUNK + 1, :], (8, LANES))
        val = jnp.sum(jnp.where(sub == lan, row, 0.0), axis=-1, keepdims=True)
        egl_ref[cc] = jnp.broadcast_to(jnp.exp(val), (8, LANES))

    nq = GDN_HEADS * GDN_DK
    for h in range(GDN_HEADS):
        qh = _head_cols(y, h)
        kh = _head_cols(y, GDN_HEADS + h)
        vh = _head_cols(y, 2 * GDN_HEADS + h)
        qn = qh * lax.rsqrt(jnp.sum(qh * qh, axis=-1, keepdims=True) + EPS) * (GDN_DK ** -0.5)
        kn = kh * lax.rsqrt(jnp.sum(kh * kh, axis=-1, keepdims=True) + EPS)
        b_h = beta[:, h:h + 1]
        eg_h = e_gc[:, h:h + 1]
        sl = slice(h * GDN_DK, (h + 1) * GDN_DK)
        kb = kn * b_h
        q_ref[:, sl] = qn.astype(BF16)
        qg_ref[:, sl] = (qn * eg_h).astype(BF16)
        k_ref[:, sl] = kn.astype(BF16)
        kb_ref[:, sl] = kb.astype(BF16)
        kbg_ref[:, sl] = (kb * eg_h).astype(BF16)
        kd_ref[:, sl] = (kn * e_gd[:, h:h + 1]).astype(BF16)
        vb_ref[:, sl] = (vh * b_h).astype(BF16)
    del nq


def _gdn_prep(proj, conv_w, a_log, dt_bias, tt=256):
    b, t, _ = proj.shape
    nc = t // CHUNK
    cpt = tt // CHUNK
    row = lambda v: jnp.zeros((1, LANES), F32).at[0, :GDN_HEADS].set(v)
    tok = pl.BlockSpec((None, tt, D_MODEL), lambda bi, i: (bi, i, 0))
    out_shapes = [jax.ShapeDtypeStruct((b, t, D_MODEL), BF16)] * 7 + [
        jax.ShapeDtypeStruct((b, t, LANES), F32),
        jax.ShapeDtypeStruct((b, nc, 8, CHUNK), F32),
        jax.ShapeDtypeStruct((b, nc, 8, LANES), F32),
    ]
    out_specs = [tok] * 7 + [
        pl.BlockSpec((None, tt, LANES), lambda bi, i: (bi, i, 0)),
        pl.BlockSpec((None, cpt, 8, CHUNK), lambda bi, i: (bi, i, 0, 0)),
        pl.BlockSpec((None, cpt, 8, LANES), lambda bi, i: (bi, i, 0, 0)),
    ]
    return pl.pallas_call(
        _gdn_prep_body,
        grid=(b, t // tt),
        in_specs=[
            pl.BlockSpec((None, tt, QKV_W), lambda bi, i: (bi, i, 0)),
            pl.BlockSpec((None, 8, QKV_W), lambda bi, i: (bi, jnp.maximum(i * (tt // 8) - 1, 0), 0)),
            pl.BlockSpec((None, tt, LANES), lambda bi, i: (bi, i, (QKV_W + D_MODEL) // LANES)),
            pl.BlockSpec((CONV_TAPS, QKV_W), lambda bi, i: (0, 0)),
            pl.BlockSpec((1, LANES), lambda bi, i: (0, 0)),
            pl.BlockSpec((1, LANES), lambda bi, i: (0, 0)),
        ],
        out_specs=out_specs,
        out_shape=out_shapes,
        scratch_shapes=[pltpu.VMEM((tt + 8, QKV_W), F32)],
        compiler_params=_params(("parallel", "arbitrary")),
        name="gdn_prep",
    )(proj, proj, proj, conv_w.astype(F32), row(a_log), row(dt_bias))


def _decay(gc_col, gc_row, keep):
    return jnp.where(keep, jnp.exp(jnp.where(keep, gc_col - gc_row, 0.0)), 0.0)


def _gdn_l_body(kb_ref, k_ref, gc_ref, gct_ref, l_ref):
    r = lax.broadcasted_iota(I32, (CHUNK, CHUNK), 0)
    c = lax.broadcasted_iota(I32, (CHUNK, CHUNK), 1)
    strict = c < r
    for cc in range(l_ref.shape[0]):
        rows = slice(cc * CHUNK, (cc + 1) * CHUNK)
        for h in range(GDN_HEADS):
            cols = slice(h * GDN_DK, (h + 1) * GDN_DK)
            kk = lax.dot_general(kb_ref[rows, cols], k_ref[rows, cols], (((1,), (1,)), ((), ())),
                                 preferred_element_type=F32)
            dec = _decay(gc_ref[rows, h:h + 1], gct_ref[cc, h:h + 1, :], strict)
            l_ref[cc, :, h * CHUNK:(h + 1) * CHUNK] = kk * dec


def _gdn_l(kb, k, gc, gct, cpt=4):
    b, t, _ = k.shape
    nc = t // CHUNK
    tt = cpt * CHUNK
    tok = pl.BlockSpec((None, tt, D_MODEL), lambda bi, i: (bi, i, 0))
    return pl.pallas_call(
        _gdn_l_body,
        grid=(b, nc // cpt),
        in_specs=[tok, tok,
                  pl.BlockSpec((None, tt, LANES), lambda bi, i: (bi, i, 0)),
                  pl.BlockSpec((None, cpt, 8, CHUNK), lambda bi, i: (bi, i, 0, 0))],
        out_specs=pl.BlockSpec((None, cpt, CHUNK, GDN_HEADS * CHUNK), lambda bi, i: (bi, i, 0, 0)),
        out_shape=jax.ShapeDtypeStruct((b, nc, CHUNK, GDN_HEADS * CHUNK), F32),
        compiler_params=_params(("parallel", "parallel")),
        name="gdn_l",
    )(kb, k, gc, gct)


def _tri_inv_body(l_ref, t_ref):
    sub = lax.broadcasted_iota(I32, (8, LANES), 0)
    for i in range(CHUNK):
        nq = i // 8 + 1
        acc = [jnp.where(sub == i % 8, 1.0, 0.0) if q == i // 8 else jnp.zeros((8, LANES), F32)
               for q in range(nq)]
        for j in range(i):
            lij = l_ref[i, j:j + 1, :]
            for q in range(j // 8 + 1):
                acc[q] = acc[q] - lij * t_ref[j, 8 * q:8 * q + 8, :]
        for q in range(CHUNK // 8):
            t_ref[i, 8 * q:8 * q + 8, :] = acc[q] if q < nq else jnp.zeros((8, LANES), F32)


def _tri_inv(lt):
    _, nh, _, p = lt.shape
    spec = pl.BlockSpec((CHUNK, None, CHUNK, LANES), lambda h, i: (0, h, 0, i))
    return pl.pallas_call(
        _tri_inv_body,
        grid=(nh, p // LANES),
        in_specs=[spec],
        out_specs=spec,
        out_shape=jax.ShapeDtypeStruct(lt.shape, F32),
        compiler_params=_params(("parallel", "parallel")),
        name="gdn_tri_inv",
    )(lt)


def _gdn_wy_body(t_ref, vb_ref, kbg_ref, q_ref, k_ref, qg_ref, gc_ref, gct_ref, u_ref, wq_ref, attn_ref):
    r = lax.broadcasted_iota(I32, (CHUNK, CHUNK), 0)
    c = lax.broadcasted_iota(I32, (CHUNK, CHUNK), 1)
    tril = c <= r
    for cc in range(t_ref.shape[0]):
        rows = slice(cc * CHUNK, (cc + 1) * CHUNK)
        for h in range(GDN_HEADS):
            cols = slice(h * GDN_DK, (h + 1) * GDN_DK)
            hc = slice(h * CHUNK, (h + 1) * CHUNK)
            rhs = jnp.concatenate([vb_ref[rows, cols], kbg_ref[rows, cols]], axis=1)
            sol = jnp.dot(t_ref[cc, :, hc].astype(BF16), rhs, preferred_element_type=F32)
            u_ref[rows, cols] = sol[:, :GDN_DK]
            wq_ref[cc, 0:CHUNK, cols] = sol[:, GDN_DK:].astype(BF16)
            wq_ref[cc, CHUNK:2 * CHUNK, cols] = qg_ref[rows, cols]
            qk = lax.dot_general(q_ref[rows, cols], k_ref[rows, cols], (((1,), (1,)), ((), ())),
                                 preferred_element_type=F32)
            dec = _decay(gc_ref[rows, h:h + 1], gct_ref[cc, h:h + 1, :], tril)
            attn_ref[cc, :, hc] = (qk * dec).astype(BF16)


def _gdn_wy(tinv, vb, kbg, q, k, qg, gc, gct, cpt=4):
    b, t, _ = k.shape
    nc = t // CHUNK
    tt = cpt * CHUNK
    tok = pl.BlockSpec((None, tt, D_MODEL), lambda bi, i: (bi, i, 0))
    sq = pl.BlockSpec((None, cpt, CHUNK, GDN_HEADS * CHUNK), lambda bi, i: (bi, i, 0, 0))
    return pl.pallas_call(
        _gdn_wy_body,
        grid=(b, nc // cpt),
        in_specs=[sq, tok, tok, tok, tok, tok,
                  pl.BlockSpec((None, tt, LANES), lambda bi, i: (bi, i, 0)),
                  pl.BlockSpec((None, cpt, 8, CHUNK), lambda bi, i: (bi, i, 0, 0))],
        out_specs=[tok,
                   pl.BlockSpec((None, cpt, 2 * CHUNK, D_MODEL), lambda bi, i: (bi, i, 0, 0)),
                   sq],
        out_shape=[jax.ShapeDtypeStruct((b, t, D_MODEL), F32),
                   jax.ShapeDtypeStruct((b, nc, 2 * CHUNK, D_MODEL), BF16),
                   jax.ShapeDtypeStruct((b, nc, CHUNK, GDN_HEADS * CHUNK), BF16)],
        compiler_params=_params(("parallel", "parallel")),
        name="gdn_wy",
    )(tinv, vb, kbg, q, k, qg, gc, gct)


def _gdn_chunk_body(wq_ref, u_ref, attn_ref, kd_ref, egl_ref, o_ref, s_out_ref, sa_ref, sb_ref):
    c = pl.program_id(0)

    @pl.when(c == 0)
    def _():
        sa_ref[...] = jnp.zeros_like(sa_ref)

    def step(src_ref, dst_ref):
        heads = [(b, h) for b in range(src_ref.shape[0]) for h in range(GDN_HEADS)]
        cols = lambda h: slice(h * GDN_DK, (h + 1) * GDN_DK)
        ws = [jnp.dot(wq_ref[b, :, cols(h)], src_ref[b, h].astype(BF16), preferred_element_type=F32)
              for b, h in heads]
        v_nb = [(u_ref[b, :, cols(h)] - w_[:CHUNK]).astype(BF16) for (b, h), w_ in zip(heads, ws)]
        for (b, h), w_, v_ in zip(heads, ws, v_nb):
            o_ref[b, :, cols(h)] = w_[CHUNK:] + jnp.dot(attn_ref[b, :, h * CHUNK:(h + 1) * CHUNK], v_,
                                                        preferred_element_type=F32)
        for (b, h), v_ in zip(heads, v_nb):
            dst_ref[b, h] = src_ref[b, h] * egl_ref[b, h:h + 1, :] + lax.dot_general(
                kd_ref[b, :, cols(h)], v_, (((0,), (0,)), ((), ())), preferred_element_type=F32)

        @pl.when(c == pl.num_programs(0) - 1)
        def _():
            s_out_ref[...] = dst_ref[...]

    @pl.when(c % 2 == 0)
    def _():
        step(sa_ref, sb_ref)

    @pl.when(c % 2 == 1)
    def _():
        step(sb_ref, sa_ref)


def _gdn_chunks(wq, u, attn, kd, egl):
    b, t, _ = u.shape
    nc = t // CHUNK
    tok = pl.BlockSpec((b, CHUNK, D_MODEL), lambda c: (0, c, 0))
    return pl.pallas_call(
        _gdn_chunk_body,
        grid=(nc,),
        in_specs=[
            pl.BlockSpec((b, None, 2 * CHUNK, D_MODEL), lambda c: (0, c, 0, 0)),
            tok,
            pl.BlockSpec((b, None, CHUNK, GDN_HEADS * CHUNK), lambda c: (0, c, 0, 0)),
            tok,
            pl.BlockSpec((b, None, 8, LANES), lambda c: (0, c, 0, 0)),
        ],
        out_specs=[
            tok,
            pl.BlockSpec((b, GDN_HEADS, GDN_DK, GDN_DK), lambda c: (0, 0, 0, 0)),
        ],
        out_shape=[
            jax.ShapeDtypeStruct((b, t, D_MODEL), F32),
            jax.ShapeDtypeStruct((b, GDN_HEADS, GDN_DK, GDN_DK), F32),
        ],
        scratch_shapes=[pltpu.VMEM((b, GDN_HEADS, GDN_DK, GDN_DK), F32)] * 2,
        compiler_params=_params(("arbitrary",)),
        name="gdn_chunks",
    )(wq, u, attn, kd, egl)


def _gdn_out_body(*refs, precise):
    o_ref, z_ref, on_ref, w_ref = refs[:4]
    wl_ref = refs[4] if precise else None
    r_ref, out_ref = refs[-2:]
    parts = []
    for h in range(GDN_HEADS):
        oh = _head_cols(o_ref[...], h)
        zh = _head_cols(z_ref[...], h)
        parts.append(_rms(oh, on_ref[...]) * _silu(zh))
    if precise:
        gh, gl = _split(jnp.concatenate(parts, axis=1))
        out_ref[...] = r_ref[...] + _dot3(gh, gl, w_ref[...], wl_ref[...])
    else:
        gated = jnp.concatenate([a.astype(BF16) for a in parts], axis=1)
        out_ref[...] = r_ref[...] + jnp.dot(gated, w_ref[...], preferred_element_type=F32)


def _gdn_out(o, proj, o_norm, w_out, res, tm=512):
    w_out, w_out_lo = _hi_lo(w_out)
    precise = w_out_lo is not None
    m, k = o.shape
    tm = min(tm, m)
    w_spec = pl.BlockSpec((k, k), lambda i: (0, 0))
    return pl.pallas_call(
        functools.partial(_gdn_out_body, precise=precise),
        grid=(m // tm,),
        in_specs=[
            pl.BlockSpec((tm, k), lambda i: (i, 0)),
            pl.BlockSpec((tm, k), lambda i: (i, QKV_W // D_MODEL)),
            pl.BlockSpec((1, GDN_DK), lambda i: (0, 0)),
        ] + [w_spec] * (1 + precise) + [pl.BlockSpec((tm, k), lambda i: (i, 0))],
        out_specs=pl.BlockSpec((tm, k), lambda i: (i, 0)),
        out_shape=jax.ShapeDtypeStruct((m, k), F32),
        compiler_params=_params(("parallel",)),
        name="gdn_out",
    )(o, proj, o_norm.reshape(1, GDN_DK).astype(F32), w_out, *([w_out_lo] if precise else []), res)


def _gdn_prompt_layer(h, l, p, w):
    b, t, _ = h.shape
    hf = h.reshape(b * t, D_MODEL)
    proj = _mm(hf, w['a_w_in'][l], gain=p['a_norm'][l], tn=PROJ_W // 3)
    proj3 = proj.reshape(b, t, PROJ_W)
    q, qg, k, kb, kbg, kd, vb, gc, gct, egl = _gdn_prep(
        proj3, p['a_conv_w'][l], p['a_A_log'][l], p['a_dt_bias'][l])
    lmat = _gdn_l(kb, k, gc, gct)
    nprob = b * (t // CHUNK)
    ppad = -nprob % LANES
    lt = jnp.transpose(lmat.reshape(nprob, CHUNK, GDN_HEADS * CHUNK), (1, 2, 0))
    lt = jnp.pad(lt, ((0, 0), (0, 0), (0, ppad))).reshape(CHUNK, GDN_HEADS, CHUNK, nprob + ppad)
    tt = _tri_inv(lt).reshape(CHUNK, GDN_HEADS * CHUNK, nprob + ppad)[:, :, :nprob]
    tinv = jnp.transpose(tt, (2, 0, 1)).reshape(b, t // CHUNK, CHUNK, GDN_HEADS * CHUNK)
    u, wq, attn = _gdn_wy(tinv, vb, kbg, q, k, qg, gc, gct)
    o, s_new = _gdn_chunks(wq, u, attn, kd, egl)
    h_new = _gdn_out(o.reshape(b * t, D_MODEL), proj, p['a_o_norm'][l], w['a_w_out'][l], hf)
    conv_new = proj3[:, t - (CONV_TAPS - 1):, :QKV_W]
    return h_new.reshape(b, t, D_MODEL), s_new, conv_new


def _gdn_prep1_body(x_ref, st_ref, ab_ref, cw_ref, alog_ref, dtb_ref,
                    q_ref, k_ref, v_ref, eg_ref, beta_ref):
    acc = st_ref[0] * cw_ref[0:1, :]
    for j in range(1, CONV_TAPS - 1):
        acc = acc + st_ref[j] * cw_ref[j:j + 1, :]
    acc = acc + x_ref[...] * cw_ref[CONV_TAPS - 1:CONV_TAPS, :]
    y = _silu(acc)
    ab = ab_ref[...]
    eg_ref[...] = jnp.exp(-jnp.exp(alog_ref[...]) * jax.nn.softplus(ab + dtb_ref[...]))
    beta_ref[...] = jax.nn.sigmoid(pltpu.roll(ab, LANES - GDN_HEADS, axis=1))
    for h in range(GDN_HEADS):
        qh = _head_cols(y, h)
        kh = _head_cols(y, GDN_HEADS + h)
        sl = slice(h * GDN_DK, (h + 1) * GDN_DK)
        q_ref[:, sl] = qh * lax.rsqrt(jnp.sum(qh * qh, axis=-1, keepdims=True) + EPS) * (GDN_DK ** -0.5)
        k_ref[:, sl] = kh * lax.rsqrt(jnp.sum(kh * kh, axis=-1, keepdims=True) + EPS)
    v_ref[...] = y[:, 2 * GDN_HEADS * GDN_DK:]


def _gdn_prep1(proj, conv_state, conv_w, a_log, dt_bias):
    n = proj.shape[0]
    row = lambda v: jnp.zeros((1, LANES), F32).at[0, :GDN_HEADS].set(v)
    tok = pl.BlockSpec((n, D_MODEL), lambda i: (0, 0))
    small = pl.BlockSpec((n, LANES), lambda i: (0, 0))
    return pl.pallas_call(
        _gdn_prep1_body,
        grid=(1,),
        in_specs=[
            pl.BlockSpec((n, QKV_W), lambda i: (0, 0)),
            pl.BlockSpec((CONV_TAPS - 1, n, QKV_W), lambda i: (0, 0, 0)),
            pl.BlockSpec((n, LANES), lambda i: (0, (QKV_W + D_MODEL) // LANES)),
            pl.BlockSpec((CONV_TAPS, QKV_W), lambda i: (0, 0)),
            pl.BlockSpec((1, LANES), lambda i: (0, 0)),
            pl.BlockSpec((1, LANES), lambda i: (0, 0)),
        ],
        out_specs=[tok, tok, tok, small, small],
        out_shape=[jax.ShapeDtypeStruct((n, D_MODEL), F32)] * 3 + [jax.ShapeDtypeStruct((n, LANES), F32)] * 2,
        compiler_params=_params(("arbitrary",)),
        name="gdn_prep1",
    )(proj, conv_state, proj, conv_w.astype(F32), row(a_log), row(dt_bias))


def _gdn_step_body(qt_ref, kt_ref, v_ref, eg_ref, beta_ref, s_ref, o_ref, s_out_ref):
    for h in range(GDN_HEADS):
        k_col = kt_ref[:, h:h + 1]
        q_col = qt_ref[:, h:h + 1]
        s = s_ref[h] * eg_ref[h:h + 1, :]
        kv_mem = jnp.sum(k_col * s, axis=0, keepdims=True)
        delta = (v_ref[h:h + 1, :] - kv_mem) * beta_ref[h:h + 1, :]
        s = s + k_col * delta
        s_out_ref[h] = s
        o_ref[h:h + 1, :] = jnp.sum(q_col * s, axis=0, keepdims=True)


def _gdn_step(qt, kt, v, eg, beta, state):
    n = v.shape[0]
    col = pl.BlockSpec((None, GDN_DK, GDN_HEADS), lambda i: (i, 0, 0))
    row = pl.BlockSpec((None, GDN_HEADS, GDN_DK), lambda i: (i, 0, 0))
    st = pl.BlockSpec((None, GDN_HEADS, GDN_DK, GDN_DK), lambda i: (i, 0, 0, 0))
    return pl.pallas_call(
        _gdn_step_body,
        grid=(n,),
        in_specs=[col, col, row, row, row, st],
        out_specs=[row, st],
        out_shape=[jax.ShapeDtypeStruct((n, GDN_HEADS, GDN_DK), F32),
                   jax.ShapeDtypeStruct(state.shape, F32)],
        compiler_params=_params(("parallel",)),
        name="gdn_step",
    )(qt, kt, v, eg, beta, state)


def _gdn_sample_layer(h, l, p, w, conv_state, delta_state):
    n = h.shape[0]
    proj = _mm(h, w['a_w_in'][l], gain=p['a_norm'][l], tn=PROJ_W // 3)
    conv_t = jnp.transpose(conv_state, (1, 0, 2))
    q, k, v, eg, beta = _gdn_prep1(proj, conv_t, p['a_conv_w'][l], p['a_A_log'][l], p['a_dt_bias'][l])
    heads = lambda a: a.reshape(n, GDN_HEADS, GDN_DK)
    cols = lambda a: jnp.transpose(heads(a), (0, 2, 1))
    lanes = lambda a: jnp.broadcast_to(a[:, :GDN_HEADS, None], (n, GDN_HEADS, GDN_DK))
    o, s_new = _gdn_step(cols(q), cols(k), heads(v), lanes(eg), lanes(beta), delta_state)
    h_new = _gdn_out(o.reshape(n, D_MODEL), proj, p['a_o_norm'][l], w['a_w_out'][l], h)
    conv_new = jnp.concatenate([conv_state[:, 1:], proj[:, None, :QKV_W]], axis=1)
    return h_new, s_new, conv_new


def _rope_tables(pos):
    half = ROT // 2
    inv = ROPE_THETA ** (-jnp.arange(0, ROT, 2, dtype=F32) / ROT)
    ang = pos.astype(F32)[:, None] * inv[None, :]
    cos, sin = jnp.cos(ang), jnp.sin(ang)
    ones = jnp.ones((pos.shape[0], HEAD_DIM - ROT), F32)
    zeros = jnp.zeros_like(ones)
    z8 = jnp.zeros_like(sin)
    c = jnp.concatenate([cos, cos, ones], axis=1)
    sm = jnp.concatenate([-sin, z8, zeros], axis=1)
    sp = jnp.concatenate([z8, sin, zeros], axis=1)
    del half
    return tuple(jnp.tile(a, (1, LANES // HEAD_DIM)) for a in (c, sm, sp))


def _dot_split(a, sel):
    hi = a.astype(BF16)
    lo = (a - hi.astype(F32)).astype(BF16)
    return (jnp.dot(hi, sel, preferred_element_type=F32) + jnp.dot(lo, sel, preferred_element_type=F32))


def _headnorm_rope_body(x_ref, g_ref, seg_ref, segt_ref, c_ref, sm_ref, sp_ref, o_ref):
    x = x_ref[...]
    w = x.shape[1]
    ms = _dot_split(x * x, seg_ref[...]) * (1.0 / HEAD_DIM)
    scale = _dot_split(lax.rsqrt(ms + EPS), segt_ref[...])
    y = x * scale * g_ref[...]
    reps = w // LANES
    tile = lambda a: jnp.concatenate([a] * reps, axis=1) if reps > 1 else a
    half = ROT // 2
    out = (y * tile(c_ref[...]) + pltpu.roll(y, w - half, axis=1) * tile(sm_ref[...])
           + pltpu.roll(y, half, axis=1) * tile(sp_ref[...]))
    o_ref[...] = out.astype(o_ref.dtype)


def _headnorm_rope(x, col_block, width, gain, tables, rows_per_seq, out_dtype, tm=512):
    m = x.shape[0]
    tm = min(tm, m, rows_per_seq)
    nheads = width // HEAD_DIM
    seg = (jnp.arange(width)[:, None] // HEAD_DIM == jnp.arange(LANES)[None, :]).astype(BF16)
    per_seq = rows_per_seq // tm
    tab = pl.BlockSpec((tm, LANES), lambda i: (i % per_seq, 0))
    del nheads
    return pl.pallas_call(
        _headnorm_rope_body,
        grid=(m // tm,),
        in_specs=[
            pl.BlockSpec((tm, width), lambda i: (i, col_block)),
            pl.BlockSpec((1, width), lambda i: (0, 0)),
            pl.BlockSpec((width, LANES), lambda i: (0, 0)),
            pl.BlockSpec((LANES, width), lambda i: (0, 0)),
            tab, tab, tab,
        ],
        out_specs=pl.BlockSpec((tm, width), lambda i: (i, 0)),
        out_shape=jax.ShapeDtypeStruct((m, width), out_dtype),
        compiler_params=_params(("parallel",)),
        name="headnorm_rope",
    )(x, jnp.tile(gain.astype(F32), width // HEAD_DIM).reshape(1, width), seg, seg.T, *tables)


def _softmax_sink(s, sink):
    m = jnp.maximum(jnp.max(s, axis=-1, keepdims=True), sink)
    p = jnp.exp(s - m)
    return p, jnp.sum(p, axis=-1, keepdims=True) + jnp.exp(sink - m)


def _swa_prompt_body(sink_ref, q_ref, kp_ref, kc_ref, vp_ref, vc_ref, o_ref):
    i = pl.program_id(1)
    blk = q_ref.shape[0]
    r = lax.broadcasted_iota(I32, (blk, 2 * blk), 0)
    c = lax.broadcasted_iota(I32, (blk, 2 * blk), 1)
    valid = (c >= r) & (c <= r + WINDOW) & ((c >= blk) | (i > 0))
    for kvh in range(KV_HEADS):
        cols = slice(kvh * HEAD_DIM, (kvh + 1) * HEAD_DIM)
        kk = jnp.concatenate([kp_ref[:, cols], kc_ref[:, cols]], axis=0).astype(BF16)
        vv = jnp.concatenate([vp_ref[:, cols], vc_ref[:, cols]], axis=0).astype(BF16)
        for g in range(ATT_GROUP):
            h = kvh * ATT_GROUP + g
            hc = slice(h * HEAD_DIM, (h + 1) * HEAD_DIM)
            s = lax.dot_general(q_ref[:, hc], kk, (((1,), (1,)), ((), ())),
                                preferred_element_type=F32) * (HEAD_DIM ** -0.5)
            s = jnp.where(valid, s, -jnp.inf)
            p, den = _softmax_sink(s, sink_ref[h])
            o_ref[:, hc] = jnp.dot((p / den).astype(BF16), vv, preferred_element_type=F32).astype(o_ref.dtype)


def _swa_prompt(q, k, kv, sinks):
    b, t, _ = q.shape
    blk = WINDOW
    kvw = KV_HEADS * HEAD_DIM
    prev = lambda bi, i, s: (bi, jnp.maximum(i - 1, 0), 0)
    cur = lambda bi, i, s: (bi, i, 0)
    prev_v = lambda bi, i, s: (bi, jnp.maximum(i - 1, 0), 1)
    cur_v = lambda bi, i, s: (bi, i, 1)
    grid_spec = pltpu.PrefetchScalarGridSpec(
        num_scalar_prefetch=1,
        grid=(b, t // blk),
        in_specs=[
            pl.BlockSpec((None, blk, D_MODEL), cur),
            pl.BlockSpec((None, blk, kvw), prev),
            pl.BlockSpec((None, blk, kvw), cur),
            pl.BlockSpec((None, blk, kvw), prev_v),
            pl.BlockSpec((None, blk, kvw), cur_v),
        ],
        out_specs=pl.BlockSpec((None, blk, D_MODEL), cur),
    )
    return pl.pallas_call(
        _swa_prompt_body,
        grid_spec=grid_spec,
        out_shape=jax.ShapeDtypeStruct((b, t, D_MODEL), BF16),
        compiler_params=_params(("parallel", "parallel")),
        name="swa_prompt",
    )(sinks.astype(F32), q, k, k, kv, kv)


def _swa_sample_body(sink_ref, q_ref, kc_ref, kn_ref, vc_ref, vn_ref, o_ref):
    wc = kc_ref.shape[1]
    kvw = kc_ref.shape[2]
    nkeys = 2 * wc
    fill = jnp.zeros((nkeys - wc - 8, kvw), F32)
    key_idx = lax.broadcasted_iota(I32, (ATT_HEADS, nkeys), 1)
    sink = sink_ref[:, 0:1]
    seqs = range(q_ref.shape[0])
    scores = []
    for b in seqs:
        kk = jnp.concatenate([kc_ref[b], kn_ref[b], fill], axis=0)
        scores.append(lax.dot_general(q_ref[b], kk, (((1,), (1,)), ((), ())), precision=HI,
                                      preferred_element_type=F32))
    probs = []
    for s in scores:
        s = jnp.where(key_idx <= wc, s * (HEAD_DIM ** -0.5), -jnp.inf)
        p, den = _softmax_sink(s, sink)
        probs.append(p / den)
    outs = []
    for b, p in zip(seqs, probs):
        vv = jnp.concatenate([vc_ref[b], vn_ref[b], fill], axis=0)
        outs.append(jnp.dot(p, vv, precision=HI, preferred_element_type=F32))
    for b, o in zip(seqs, outs):
        for kvh in range(KV_HEADS):
            heads = slice(kvh * ATT_GROUP, (kvh + 1) * ATT_GROUP)
            o_ref[b, heads, :] = o[heads, kvh * HEAD_DIM:(kvh + 1) * HEAD_DIM]


def _swa_sample(q, k_cache, k_new, v_cache, v_new, sinks, sb=8):
    n = q.shape[0]
    wc = k_cache.shape[1]
    kvw = KV_HEADS * HEAD_DIM
    own = (jnp.arange(ATT_HEADS)[:, None] // ATT_GROUP == jnp.arange(KV_HEADS)[None, :])
    q_rows = (q[:, :, None, :] * own[None, :, :, None].astype(q.dtype)).reshape(n, ATT_HEADS, kvw)
    pad8 = lambda a: jnp.pad(a[:, None, :], ((0, 0), (0, 7), (0, 0)))
    blk = lambda shape: pl.BlockSpec(shape, lambda i: (i, 0, 0))
    return pl.pallas_call(
        _swa_sample_body,
        grid=(n // sb,),
        in_specs=[pl.BlockSpec((ATT_HEADS, LANES), lambda i: (0, 0)),
                  blk((sb, ATT_HEADS, kvw)), blk((sb, wc, kvw)), blk((sb, 8, kvw)),
                  blk((sb, wc, kvw)), blk((sb, 8, kvw))],
        out_specs=blk((sb, ATT_HEADS, HEAD_DIM)),
        out_shape=jax.ShapeDtypeStruct((n, ATT_HEADS, HEAD_DIM), F32),
        compiler_params=_params(("parallel",)),
        name="swa_sample",
    )(jnp.broadcast_to(sinks.astype(F32)[:, None], (ATT_HEADS, LANES)), q_rows, k_cache, pad8(k_new),
      v_cache, pad8(v_new))


def _channel_mixer(hf, l, p, w):
    if l % 2 == 0:
        return _ffn_dense(hf, p['ffn_norm'][l], w['dense_w_gu'][l // 2], w['dense_w_down'][l // 2])
    return _moe(hf, p['ffn_norm'][l], p['moe_router'][l // 2], p['moe_router_b'][l // 2],
                w['moe_w_gu'][l // 2], w['moe_w_down'][l // 2])


def _run_prompt(x, p, w):
    b, t, _ = x.shape
    n = b * t
    h = x
    convs, deltas = [], []
    for l in range(N_GDN):
        h, s_new, c_new = _gdn_prompt_layer(h, l, p, w)
        convs.append(c_new)
        deltas.append(s_new)
        h = _channel_mixer(h.reshape(n, D_MODEL), l, p, w).reshape(b, t, D_MODEL)
    hf = h.reshape(n, D_MODEL)
    tables = _rope_tables(jnp.arange(t, dtype=I32))
    kvw = KV_HEADS * HEAD_DIM
    kv = _mm(hf, w['w_kv'], gain=p['kv_norm'], tn=2 * kvw)
    k_sh = _headnorm_rope(kv, 0, kvw, p['k_norm'], tables, t, F32)
    k3 = k_sh.reshape(b, t, kvw)
    kv3 = kv.reshape(b, t, 2 * kvw)
    for j in range(2):
        l = N_GDN + j
        qp = _mm(hf, w['b_w_q'][j], gain=p['b_norm'][j])
        q = _headnorm_rope(qp, 0, D_MODEL, p['b_q_norm'][j], tables, t, BF16)
        o = _swa_prompt(q.reshape(b, t, D_MODEL), k3, kv3, p['b_sinks'][j])
        hf = _mm(o.reshape(n, D_MODEL), w['b_w_o'][j], res=hf)
        hf = _channel_mixer(hf, l, p, w)
    k_win = k3[:, t - WINDOW:].reshape(b, WINDOW, KV_HEADS, HEAD_DIM)
    v_win = kv3[:, t - WINDOW:, kvw:].reshape(b, WINDOW, KV_HEADS, HEAD_DIM)
    return hf.reshape(b, t, D_MODEL), jnp.stack(convs), jnp.stack(deltas), k_win, v_win


def _run_sample(x, conv_state, delta_state, k_cache, v_cache, p, w):
    n = x.shape[0]
    h = x.reshape(n, D_MODEL)
    convs, deltas = [], []
    for l in range(N_GDN):
        h, s_new, c_new = _gdn_sample_layer(h, l, p, w, conv_state[l], delta_state[l])
        convs.append(c_new)
        deltas.append(s_new)
        h = _channel_mixer(h, l, p, w)
    tables = _rope_tables(jnp.full((n,), PAST_LEN, I32))
    kvw = KV_HEADS * HEAD_DIM
    wc = k_cache.shape[1]
    kv = _mm(h, w['w_kv'], gain=p['kv_norm'], tn=2 * kvw)
    k_new = _headnorm_rope(kv, 0, kvw, p['k_norm'], tables, n, F32)
    v_new = kv[:, kvw:]
    kc = k_cache.reshape(n, wc, kvw)
    vc = v_cache.reshape(n, wc, kvw)
    for j in range(2):
        l = N_GDN + j
        qp = _mm(h, w['b_w_q'][j], gain=p['b_norm'][j])
        q = _headnorm_rope(qp, 0, D_MODEL, p['b_q_norm'][j], tables, n, F32)
        o = _swa_sample(q.reshape(n, ATT_HEADS, HEAD_DIM), kc, k_new, vc, v_new, p['b_sinks'][j])
        h = _mm(o.reshape(n, D_MODEL), w['b_w_o'][j], res=h)
        h = _channel_mixer(h, l, p, w)
    k_win = jnp.concatenate([kc, k_new[:, None, :]], axis=1)[:, -wc:].reshape(n, wc, KV_HEADS, HEAD_DIM)
    v_win = jnp.concatenate([vc, v_new[:, None, :]], axis=1)[:, -wc:].reshape(n, wc, KV_HEADS, HEAD_DIM)
    return h.reshape(n, 1, D_MODEL), jnp.stack(convs), jnp.stack(deltas), k_win, v_win


def kernel(x_prompt, x_sample, state_conv, state_delta, cache_k_win, cache_v_win, a_norm, a_w_in, a_conv_w,
           a_A_log, a_dt_bias, a_o_norm, a_w_out, kv_norm, w_kv, k_norm, b_norm, b_w_q, b_q_norm, b_sinks,
           b_w_o, ffn_norm, dense_w_gu, dense_w_down, moe_router, moe_router_b, moe_w_gu, moe_w_down):
    p = dict(a_norm=a_norm, a_conv_w=a_conv_w, a_A_log=a_A_log, a_dt_bias=a_dt_bias, a_o_norm=a_o_norm,
             kv_norm=kv_norm, k_norm=k_norm, b_norm=b_norm, b_q_norm=b_q_norm, b_sinks=b_sinks,
             ffn_norm=ffn_norm, moe_router=moe_router, moe_router_b=moe_router_b)
    pad = PROJ_W - a_w_in.shape[-1]
    raw = dict(a_w_in=jnp.pad(a_w_in, ((0, 0), (0, 0), (0, pad))), a_w_out=a_w_out, w_kv=w_kv[None],
               b_w_q=b_w_q, b_w_o=b_w_o, dense_w_gu=dense_w_gu, dense_w_down=dense_w_down,
               moe_w_gu=moe_w_gu, moe_w_down=moe_w_down)
    w_fast, w_precise = {}, {}
    for name, val in raw.items():
        hi, lo = _split(val, barrier=True)
        w_fast[name] = [hi[l] for l in range(val.shape[0])]
        w_precise[name] = [(hi[l], lo[l]) for l in range(val.shape[0])]
    for ws in (w_fast, w_precise):
        ws['w_kv'] = ws['w_kv'][0]
    y_p, conv_p, delta_p, kw_p, vw_p = _run_prompt(x_prompt, p, w_fast)
    y_s, conv_s, delta_s, kw_s, vw_s = _run_sample(
        x_sample, state_conv, state_delta, cache_k_win, cache_v_win, p, w_precise)
    return (y_p, y_s, conv_p, conv_s, delta_p, delta_s, kw_p, kw_s, vw_p, vw_s)
```

```python
import functools

import jax
import jax.numpy as jnp
from jax import lax
from jax.experimental import pallas as pl
from jax.experimental.pallas import tpu as pltpu

F32, BF16, I32 = jnp.float32, jnp.bfloat16, jnp.int32
HI = lax.Precision.HIGHEST

D_MODEL = 1024
N_GDN = 2
GDN_HEADS = 8
GDN_DK = 128
CONV_TAPS = 4
CHUNK = 64
QKV_W = 3 * GDN_HEADS * GDN_DK
PROJ_W = QKV_W + D_MODEL
ATT_HEADS = 16
KV_HEADS = 4
HEAD_DIM = 64
ATT_GROUP = ATT_HEADS // KV_HEADS
WINDOW = 128
ROT = HEAD_DIM // 4
ROPE_THETA = 500000.0
PAST_LEN = 8192
N_EXPERTS = 8
EPS = 1e-6

LANES = 128
VMEM_LIMIT = 56 * 1024 * 1024
NEG = -1e30


def _params(sem):
    return pltpu.CompilerParams(dimension_semantics=sem, vmem_limit_bytes=VMEM_LIMIT)


def _rms(x, g):
    return x * lax.rsqrt(jnp.mean(x * x, axis=-1, keepdims=True) + EPS) * g


def _silu(x):
    return x * jax.nn.sigmoid(x)


def _split(x):
    hi = x.astype(BF16)
    return hi, (x - hi.astype(F32)).astype(BF16)


def _dot3(xh, xl, w):
    wh, wl = _split(w)
    return (jnp.dot(xh, wh, preferred_element_type=F32)
            + (jnp.dot(xl, wh, preferred_element_type=F32) + jnp.dot(xh, wl, preferred_element_type=F32)))


def _is_precise(w):
    assert w.dtype in (BF16, F32)
    return w.dtype == F32


def _mm_body(*refs, has_norm, has_aux, has_res, precise):
    it = iter(refs)
    x_ref = next(it)
    g_ref = next(it) if has_norm else None
    w_ref = next(it)
    wa_ref = next(it) if has_aux else None
    r_ref = next(it) if has_res else None
    o_ref = next(it)
    oa_ref = next(it) if has_aux else None
    xn_ref = next(it)

    def product(w):
        if precise:
            return _dot3(xn_ref[0], xn_ref[1], w)
        return jnp.dot(xn_ref[0], w, preferred_element_type=F32)

    @pl.when(pl.program_id(1) == 0)
    def _():
        x = x_ref[...].astype(F32)
        if has_norm:
            x = _rms(x, g_ref[...])
        if precise:
            xn_ref[0], xn_ref[1] = _split(x)
        else:
            xn_ref[0] = x.astype(BF16)
        if has_aux:
            oa_ref[...] = product(wa_ref[...])

    acc = product(w_ref[...])
    if has_res:
        acc = acc + r_ref[...]
    o_ref[...] = acc.astype(o_ref.dtype)


def _mm(x, w, gain=None, res=None, w_aux=None, tm=1024, tn=1024, out_dtype=F32):
    precise = _is_precise(w)
    m, k = x.shape
    n = w.shape[1]
    tm = min(tm, m)
    tn = min(tn, n)
    assert m % tm == 0 and n % tn == 0
    ins = [x]
    specs = [pl.BlockSpec((tm, k), lambda i, j: (i, 0))]
    if gain is not None:
        ins.append(gain.reshape(1, k).astype(F32))
        specs.append(pl.BlockSpec((1, k), lambda i, j: (0, 0)))
    ins.append(w)
    specs.append(pl.BlockSpec((k, tn), lambda i, j: (0, j)))
    out_specs = [pl.BlockSpec((tm, tn), lambda i, j: (i, j))]
    out_shape = [jax.ShapeDtypeStruct((m, n), out_dtype)]
    if w_aux is not None:
        na = w_aux.shape[1]
        ins.append(w_aux)
        specs.append(pl.BlockSpec((k, na), lambda i, j: (0, 0)))
        out_specs.append(pl.BlockSpec((tm, na), lambda i, j: (i, 0)))
        out_shape.append(jax.ShapeDtypeStruct((m, na), F32))
    if res is not None:
        ins.append(res)
        specs.append(pl.BlockSpec((tm, tn), lambda i, j: (i, j)))
    out = pl.pallas_call(
        functools.partial(_mm_body, has_norm=gain is not None, has_aux=w_aux is not None,
                          has_res=res is not None, precise=precise),
        grid=(m // tm, n // tn),
        in_specs=specs,
        out_specs=out_specs,
        out_shape=out_shape,
        scratch_shapes=[pltpu.VMEM((1 + precise, tm, k), BF16)],
        compiler_params=_params(("parallel", "arbitrary")),
        name="mm",
    )(*ins)
    return out if w_aux is not None else out[0]


def _swiglu(xh, xl, wg, wu, wd):
    if xl is None:
        gate = jnp.dot(xh, wg, preferred_element_type=F32)
        up = jnp.dot(xh, wu, preferred_element_type=F32)
        return jnp.dot((_silu(gate) * up).astype(BF16), wd, preferred_element_type=F32)
    gate = _dot3(xh, xl, wg)
    up = _dot3(xh, xl, wu)
    ah, al = _split(_silu(gate) * up)
    return _dot3(ah, al, wd)


def _ffn_dense_body(x_ref, g_ref, wg_ref, wu_ref, wd_ref, o_ref, xn_ref, acc_ref, *, precise):
    f = pl.program_id(1)

    @pl.when(f == 0)
    def _():
        xn = _rms(x_ref[...], g_ref[...])
        if precise:
            xn_ref[0], xn_ref[1] = _split(xn)
        else:
            xn_ref[0] = xn.astype(BF16)
        acc_ref[...] = x_ref[...]

    acc_ref[...] += _swiglu(xn_ref[0], xn_ref[1] if precise else None, wg_ref[...], wu_ref[...], wd_ref[...])

    @pl.when(f == pl.num_programs(1) - 1)
    def _():
        o_ref[...] = acc_ref[...]


def _ffn_dense(x, gain, w_gu, w_down, tm=512):
    precise = _is_precise(w_gu)
    m, k = x.shape
    ff = w_down.shape[0]
    tm = min(tm, m)
    nf = 2 if not precise else ff // (2 * LANES)
    tf = ff // nf
    assert tf % LANES == 0 and tf * nf == ff
    return pl.pallas_call(
        functools.partial(_ffn_dense_body, precise=precise),
        grid=(m // tm, nf),
        in_specs=[
            pl.BlockSpec((tm, k), lambda i, f: (i, 0)),
            pl.BlockSpec((1, k), lambda i, f: (0, 0)),
            pl.BlockSpec((k, tf), lambda i, f: (0, f)),
            pl.BlockSpec((k, tf), lambda i, f: (0, nf + f)),
            pl.BlockSpec((tf, k), lambda i, f: (f, 0)),
        ],
        out_specs=pl.BlockSpec((tm, k), lambda i, f: (i, 0)),
        out_shape=jax.ShapeDtypeStruct((m, k), F32),
        scratch_shapes=[pltpu.VMEM((1 + precise, tm, k), BF16), pltpu.VMEM((tm, k), F32)],
        compiler_params=_params(("parallel", "arbitrary")),
        name="ffn_dense",
    )(x, gain.reshape(1, k).astype(F32), w_gu, w_gu, w_down)


def _router_body(x_ref, g_ref, wr_ref, br_ref, idx_ref, gate_ref, cnt_ref, carry_ref):
    i = pl.program_id(0)

    @pl.when(i == 0)
    def _():
        carry_ref[...] = jnp.zeros_like(carry_ref)

    tm = x_ref.shape[0]
    xn = _rms(x_ref[...], g_ref[...])
    logits = _dot3(*_split(xn), wr_ref[...]) + br_ref[...]
    lane = lax.broadcasted_iota(I32, logits.shape, 1)
    m1 = jnp.max(logits, axis=-1, keepdims=True)
    i1 = jnp.min(jnp.where(logits == m1, lane, LANES), axis=-1, keepdims=True)
    rest = jnp.where(lane == i1, NEG, logits)
    m2 = jnp.max(rest, axis=-1, keepdims=True)
    i2 = jnp.min(jnp.where(rest == m2, lane, LANES), axis=-1, keepdims=True)
    e2 = jnp.exp(m2 - m1)
    g1 = 1.0 / (1.0 + e2)
    g2 = e2 * g1
    oh1 = lane == i1
    oh2 = lane == i2
    oh = jnp.where(oh1 | oh2, 1.0, 0.0).astype(BF16)
    r = lax.broadcasted_iota(I32, (tm, tm), 0)
    c = lax.broadcasted_iota(I32, (tm, tm), 1)
    below = jnp.where(c < r, 1.0, 0.0).astype(BF16)
    before = jnp.dot(below, oh, preferred_element_type=F32) + carry_ref[0:1, :]
    rk1 = jnp.sum(jnp.where(oh1, before, 0.0), axis=-1, keepdims=True).astype(I32)
    rk2 = jnp.sum(jnp.where(oh2, before, 0.0), axis=-1, keepdims=True).astype(I32)
    idx_ref[...] = jnp.where(lane == 0, i1, jnp.where(lane == 1, i2, jnp.where(lane == 2, rk1, rk2)))
    gate_ref[...] = jnp.where(lane == 0, g1, g2)
    total = carry_ref[0:1, :] + jnp.sum(oh.astype(F32), axis=0, keepdims=True)
    carry_ref[...] = jnp.broadcast_to(total, carry_ref.shape)
    cnt_ref[...] = carry_ref[...]


def _router(x, gain, w_router, b_router, tm=512):
    m, k = x.shape
    tm = min(tm, m)
    wr = jnp.zeros((k, LANES), F32).at[:, :N_EXPERTS].set(w_router)
    br = jnp.full((1, LANES), NEG, F32).at[0, :N_EXPERTS].set(b_router)
    return pl.pallas_call(
        _router_body,
        grid=(m // tm,),
        in_specs=[
            pl.BlockSpec((tm, k), lambda i: (i, 0)),
            pl.BlockSpec((1, k), lambda i: (0, 0)),
            pl.BlockSpec((k, LANES), lambda i: (0, 0)),
            pl.BlockSpec((1, LANES), lambda i: (0, 0)),
        ],
        out_specs=[
            pl.BlockSpec((tm, LANES), lambda i: (i, 0)),
            pl.BlockSpec((tm, LANES), lambda i: (i, 0)),
            pl.BlockSpec((8, LANES), lambda i: (0, 0)),
        ],
        out_shape=[
            jax.ShapeDtypeStruct((m, LANES), I32),
            jax.ShapeDtypeStruct((m, LANES), F32),
            jax.ShapeDtypeStruct((8, LANES), F32),
        ],
        scratch_shapes=[pltpu.VMEM((8, LANES), F32)],
        compiler_params=_params(("arbitrary",)),
        name="moe_router",
    )(x, gain.reshape(1, k).astype(F32), wr, br)


def _row_copy(src, dst, src_row, dst_row, sem):
    return pltpu.make_async_copy(src.at[pl.ds(src_row, 1)], dst.at[pl.ds(dst_row, 1)], sem)


def _scatter_body(d_ref, x_ref, g_ref, xs_in_ref, xs_ref, buf_ref, sem):
    del xs_in_ref
    tm = buf_ref.shape[0]
    buf_ref[...] = _rms(x_ref[...], g_ref[...])

    def issue(t, carry):
        for kk in range(2):
            _row_copy(buf_ref, xs_ref, t, d_ref[0, 0, 2 * t + kk], sem).start()
        return carry

    lax.fori_loop(0, tm, issue, 0, unroll=8)
    for kk in range(2):
        pltpu.make_async_copy(buf_ref, xs_ref.at[pl.ds(0, tm)], sem).wait()


def _moe_scatter(x, gain, dest_tiles, rows, tm):
    m, k = x.shape
    return pl.pallas_call(
        _scatter_body,
        grid=(m // tm,),
        in_specs=[
            pl.BlockSpec((1, 1, 2 * tm), lambda i: (i, 0, 0), memory_space=pltpu.SMEM),
            pl.BlockSpec((tm, k), lambda i: (i, 0)),
            pl.BlockSpec((1, k), lambda i: (0, 0)),
            pl.BlockSpec(memory_space=pl.ANY),
        ],
        out_specs=pl.BlockSpec(memory_space=pl.ANY),
        out_shape=jax.ShapeDtypeStruct((rows, k), F32),
        scratch_shapes=[pltpu.VMEM((tm, k), F32), pltpu.SemaphoreType.DMA(())],
        input_output_aliases={3: 0},
        compiler_params=_params(("arbitrary",)),
        name="moe_scatter",
    )(dest_tiles, x, gain.reshape(1, k).astype(F32), jnp.zeros((rows, k), F32))


def _ffn_group_body(te_ref, nu_ref, x_ref, wg_ref, wu_ref, wd_ref, o_ref, *, precise):
    del te_ref
    used = pl.program_id(0) < nu_ref[0]

    @pl.when(used)
    def _():
        xh, xl = _split(x_ref[...]) if precise else (x_ref[...].astype(BF16), None)
        o_ref[...] = _swiglu(xh, xl, wg_ref[...], wu_ref[...], wd_ref[...])

    @pl.when(jnp.logical_not(used))
    def _():
        o_ref[...] = jnp.zeros_like(o_ref)


def _ffn_grouped(xs, tile_expert, n_used, w_gu, w_down, tm):
    precise = _is_precise(w_gu)
    rows, k = xs.shape
    ff = w_down.shape[1]
    nt = rows // tm

    def row_map(i, te, nu):
        return (jnp.minimum(i, nu[0] - 1), 0)

    grid_spec = pltpu.PrefetchScalarGridSpec(
        num_scalar_prefetch=2,
        grid=(nt,),
        in_specs=[
            pl.BlockSpec((tm, k), row_map),
            pl.BlockSpec((None, k, ff), lambda i, te, nu: (te[i], 0, 0)),
            pl.BlockSpec((None, k, ff), lambda i, te, nu: (te[i], 0, 1)),
            pl.BlockSpec((None, ff, k), lambda i, te, nu: (te[i], 0, 0)),
        ],
        out_specs=pl.BlockSpec((tm, k), lambda i, te, nu: (i, 0)),
    )
    return pl.pallas_call(
        functools.partial(_ffn_group_body, precise=precise),
        grid_spec=grid_spec,
        out_shape=jax.ShapeDtypeStruct((rows, k), F32),
        compiler_params=_params(("arbitrary",)),
        name="moe_experts",
    )(tile_expert, n_used, xs, w_gu, w_gu, w_down)


def _combine_body(d_ref, x_ref, gate_ref, ys_ref, o_ref, buf_ref, sem):
    tm = x_ref.shape[0]

    def issue(t, carry):
        for kk in range(2):
            _row_copy(ys_ref, buf_ref.at[kk], d_ref[0, 0, 2 * t + kk], t, sem).start()
        return carry

    lax.fori_loop(0, tm, issue, 0, unroll=8)
    for kk in range(2):
        pltpu.make_async_copy(ys_ref.at[pl.ds(0, tm)], buf_ref.at[kk], sem).wait()
    g = gate_ref[...]
    o_ref[...] = x_ref[...] + g[:, 0:1] * buf_ref[0] + g[:, 1:2] * buf_ref[1]


def _moe_combine(x, gates, dest_tiles, ys, tm):
    m, k = x.shape
    return pl.pallas_call(
        _combine_body,
        grid=(m // tm,),
        in_specs=[
            pl.BlockSpec((1, 1, 2 * tm), lambda i: (i, 0, 0), memory_space=pltpu.SMEM),
            pl.BlockSpec((tm, k), lambda i: (i, 0)),
            pl.BlockSpec((tm, LANES), lambda i: (i, 0)),
            pl.BlockSpec(memory_space=pl.ANY),
        ],
        out_specs=pl.BlockSpec((tm, k), lambda i: (i, 0)),
        out_shape=jax.ShapeDtypeStruct((m, k), F32),
        scratch_shapes=[pltpu.VMEM((2, tm, k), F32), pltpu.SemaphoreType.DMA(())],
        compiler_params=_params(("arbitrary",)),
        name="moe_combine",
    )(dest_tiles, x, gates, ys)


def _moe(x, gain, w_router, b_router, w_gu, w_down):
    m, k = x.shape
    tm_tok = min(256, m)
    tm_grp = 512 if m >= 4096 else 128
    idx, gates, counts = _router(x, gain, w_router, b_router)
    counts = counts[0, :N_EXPERTS].astype(I32)
    padded = ((counts + tm_grp - 1) // tm_grp) * tm_grp
    ends = jnp.cumsum(padded)
    starts = ends - padded
    dest = starts[idx[:, 0:2]] + idx[:, 2:4]
    dest_tiles = dest.reshape(m // tm_tok, 1, 2 * tm_tok)
    rows = ((2 * m + N_EXPERTS * (tm_grp - 1)) // tm_grp) * tm_grp
    nt = rows // tm_grp
    n_used = (ends[-1] // tm_grp).astype(I32)
    tile_start = jnp.minimum(jnp.arange(nt, dtype=I32), n_used - 1) * tm_grp
    tile_expert = jnp.sum(tile_start[:, None] >= ends[None, :], axis=1).astype(I32)
    xs = _moe_scatter(x, gain, dest_tiles, rows, tm_tok)
    ys = _ffn_grouped(xs, tile_expert, n_used.reshape(1), w_gu, w_down, tm_grp)
    return _moe_combine(x, gates, dest_tiles, ys, tm_tok)


def _head_cols(x, h, width=GDN_DK):
    return x[:, h * width:(h + 1) * width]


def _gdn_prep_body(x_ref, halo_ref, ab_ref, cw_ref, alog_ref, dtb_ref,
                   q_ref, qg_ref, k_ref, kb_ref, kbg_ref, kd_ref, vb_ref,
                   gc_ref, gct_ref, egl_ref, ext_ref):
    i = pl.program_id(1)
    tt = x_ref.shape[0]
    ext_ref[0:8, :] = jnp.where(i == 0, 0.0, halo_ref[...])
    ext_ref[8:, :] = x_ref[...]
    ext = ext_ref[...]
    acc = pltpu.roll(ext, CONV_TAPS - 1, axis=0)[8:] * cw_ref[0:1, :]
    for j in range(1, CONV_TAPS - 1):
        acc = acc + pltpu.roll(ext, CONV_TAPS - 1 - j, axis=0)[8:] * cw_ref[j:j + 1, :]
    acc = acc + ext[8:] * cw_ref[CONV_TAPS - 1:CONV_TAPS, :]
    y = _silu(acc)

    ab = ab_ref[...]
    g = -jnp.exp(alog_ref[...]) * jax.nn.softplus(ab + dtb_ref[...])
    beta = jax.nn.sigmoid(pltpu.roll(ab, LANES - GDN_HEADS, axis=1))
    r = lax.broadcasted_iota(I32, (tt, tt), 0)
    c = lax.broadcasted_iota(I32, (tt, tt), 1)
    same = (r // CHUNK) == (c // CHUNK)
    tril = jnp.where(same & (c <= r), 1.0, 0.0)
    gc = jnp.dot(tril, g, precision=HI, preferred_element_type=F32)
    gl = jnp.dot(jnp.where(same, 1.0, 0.0), g, precision=HI, preferred_element_type=F32)
    e_gc = jnp.exp(gc)
    e_gd = jnp.exp(gl - gc)
    gc_ref[...] = gc
    gct = gc.T
    sub = lax.broadcasted_iota(I32, (8, LANES), 0)
    lan = lax.broadcasted_iota(I32, (8, LANES), 1)
    for cc in range(tt // CHUNK):
        gct_ref[cc] = gct[0:8, cc * CHUNK:(cc + 1) * CHUNK]
        row = jnp.broadcast_to(gl[cc * CHUNK:cc * CHUNK + 1, :], (8, LANES))
        val = jnp.sum(jnp.where(sub == lan, row, 0.0), axis=-1, keepdims=True)
        egl_ref[cc] = jnp.broadcast_to(jnp.exp(val), (8, LANES))

    nq = GDN_HEADS * GDN_DK
    for h in range(GDN_HEADS):
        qh = _head_cols(y, h)
        kh = _head_cols(y, GDN_HEADS + h)
        vh = _head_cols(y, 2 * GDN_HEADS + h)
        qn = qh * lax.rsqrt(jnp.sum(qh * qh, axis=-1, keepdims=True) + EPS) * (GDN_DK ** -0.5)
        kn = kh * lax.rsqrt(jnp.sum(kh * kh, axis=-1, keepdims=True) + EPS)
        b_h = beta[:, h:h + 1]
        eg_h = e_gc[:, h:h + 1]
        sl = slice(h * GDN_DK, (h + 1) * GDN_DK)
        kb = kn * b_h
        q_ref[:, sl] = qn.astype(BF16)
        qg_ref[:, sl] = (qn * eg_h).astype(BF16)
        k_ref[:, sl] = kn.astype(BF16)
        kb_ref[:, sl] = kb.astype(BF16)
        kbg_ref[:, sl] = (kb * eg_h).astype(BF16)
        kd_ref[:, sl] = (kn * e_gd[:, h:h + 1]).astype(BF16)
        vb_ref[:, sl] = (vh * b_h).astype(BF16)
    del nq


def _gdn_prep(proj, ab, conv_w, a_log, dt_bias, tt=256):
    b, t, _ = proj.shape
    nc = t // CHUNK
    cpt = tt // CHUNK
    row = lambda v: jnp.zeros((1, LANES), F32).at[0, :GDN_HEADS].set(v)
    tok = pl.BlockSpec((None, tt, D_MODEL), lambda bi, i: (bi, i, 0))
    out_shapes = [jax.ShapeDtypeStruct((b, t, D_MODEL), BF16)] * 7 + [
        jax.ShapeDtypeStruct((b, t, LANES), F32),
        jax.ShapeDtypeStruct((b, nc, 8, CHUNK), F32),
        jax.ShapeDtypeStruct((b, nc, 8, LANES), F32),
    ]
    out_specs = [tok] * 7 + [
        pl.BlockSpec((None, tt, LANES), lambda bi, i: (bi, i, 0)),
        pl.BlockSpec((None, cpt, 8, CHUNK), lambda bi, i: (bi, i, 0, 0)),
        pl.BlockSpec((None, cpt, 8, LANES), lambda bi, i: (bi, i, 0, 0)),
    ]
    return pl.pallas_call(
        _gdn_prep_body,
        grid=(b, t // tt),
        in_specs=[
            pl.BlockSpec((None, tt, QKV_W), lambda bi, i: (bi, i, 0)),
            pl.BlockSpec((None, 8, QKV_W), lambda bi, i: (bi, jnp.maximum(i * (tt // 8) - 1, 0), 0)),
            pl.BlockSpec((None, tt, LANES), lambda bi, i: (bi, i, 0)),
            pl.BlockSpec((CONV_TAPS, QKV_W), lambda bi, i: (0, 0)),
            pl.BlockSpec((1, LANES), lambda bi, i: (0, 0)),
            pl.BlockSpec((1, LANES), lambda bi, i: (0, 0)),
        ],
        out_specs=out_specs,
        out_shape=out_shapes,
        scratch_shapes=[pltpu.VMEM((tt + 8, QKV_W), F32)],
        compiler_params=_params(("parallel", "arbitrary")),
        name="gdn_prep",
    )(proj, proj, ab, conv_w.astype(F32), row(a_log), row(dt_bias))


def _decay(gc_col, gc_row, keep):
    return jnp.where(keep, jnp.exp(jnp.where(keep, gc_col - gc_row, 0.0)), 0.0)


def _gdn_l_body(kb_ref, k_ref, gc_ref, gct_ref, l_ref):
    r = lax.broadcasted_iota(I32, (CHUNK, CHUNK), 0)
    c = lax.broadcasted_iota(I32, (CHUNK, CHUNK), 1)
    strict = c < r
    for cc in range(l_ref.shape[0]):
        rows = slice(cc * CHUNK, (cc + 1) * CHUNK)
        for h in range(GDN_HEADS):
            cols = slice(h * GDN_DK, (h + 1) * GDN_DK)
            kk = lax.dot_general(kb_ref[rows, cols], k_ref[rows, cols], (((1,), (1,)), ((), ())),
                                 preferred_element_type=F32)
            dec = _decay(gc_ref[rows, h:h + 1], gct_ref[cc, h:h + 1, :], strict)
            l_ref[cc, :, h * CHUNK:(h + 1) * CHUNK] = kk * dec


def _gdn_l(kb, k, gc, gct, cpt=4):
    b, t, _ = k.shape
    nc = t // CHUNK
    tt = cpt * CHUNK
    tok = pl.BlockSpec((None, tt, D_MODEL), lambda bi, i: (bi, i, 0))
    return pl.pallas_call(
        _gdn_l_body,
        grid=(b, nc // cpt),
        in_specs=[tok, tok,
                  pl.BlockSpec((None, tt, LANES), lambda bi, i: (bi, i, 0)),
                  pl.BlockSpec((None, cpt, 8, CHUNK), lambda bi, i: (bi, i, 0, 0))],
        out_specs=pl.BlockSpec((None, cpt, CHUNK, GDN_HEADS * CHUNK), lambda bi, i: (bi, i, 0, 0)),
        out_shape=jax.ShapeDtypeStruct((b, nc, CHUNK, GDN_HEADS * CHUNK), F32),
        compiler_params=_params(("parallel", "parallel")),
        name="gdn_l",
    )(kb, k, gc, gct)


def _tri_inv_body(l_ref, t_ref):
    sub = lax.broadcasted_iota(I32, (8, LANES), 0)
    for i in range(CHUNK):
        nq = i // 8 + 1
        acc = [jnp.where(sub == i % 8, 1.0, 0.0) if q == i // 8 else jnp.zeros((8, LANES), F32)
               for q in range(nq)]
        for j in range(i):
            lij = l_ref[i, j:j + 1, :]
            for q in range(j // 8 + 1):
                acc[q] = acc[q] - lij * t_ref[j, 8 * q:8 * q + 8, :]
        for q in range(CHUNK // 8):
            t_ref[i, 8 * q:8 * q + 8, :] = acc[q] if q < nq else jnp.zeros((8, LANES), F32)


def _tri_inv(lt):
    _, nh, _, p = lt.shape
    spec = pl.BlockSpec((CHUNK, None, CHUNK, LANES), lambda h, i: (0, h, 0, i))
    return pl.pallas_call(
        _tri_inv_body,
        grid=(nh, p // LANES),
        in_specs=[spec],
        out_specs=spec,
        out_shape=jax.ShapeDtypeStruct(lt.shape, F32),
        compiler_params=_params(("parallel", "parallel")),
        name="gdn_tri_inv",
    )(lt)


def _gdn_wy_body(t_ref, vb_ref, kbg_ref, q_ref, k_ref, qg_ref, gc_ref, gct_ref, u_ref, wq_ref, attn_ref):
    r = lax.broadcasted_iota(I32, (CHUNK, CHUNK), 0)
    c = lax.broadcasted_iota(I32, (CHUNK, CHUNK), 1)
    tril = c <= r
    for cc in range(t_ref.shape[0]):
        rows = slice(cc * CHUNK, (cc + 1) * CHUNK)
        for h in range(GDN_HEADS):
            cols = slice(h * GDN_DK, (h + 1) * GDN_DK)
            hc = slice(h * CHUNK, (h + 1) * CHUNK)
            rhs = jnp.concatenate([vb_ref[rows, cols], kbg_ref[rows, cols]], axis=1)
            sol = jnp.dot(t_ref[cc, :, hc].astype(BF16), rhs, preferred_element_type=F32)
            u_ref[rows, cols] = sol[:, :GDN_DK]
            wq_ref[cc, 0:CHUNK, cols] = sol[:, GDN_DK:].astype(BF16)
            wq_ref[cc, CHUNK:2 * CHUNK, cols] = qg_ref[rows, cols]
            qk = lax.dot_general(q_ref[rows, cols], k_ref[rows, cols], (((1,), (1,)), ((), ())),
                                 preferred_element_type=F32)
            dec = _decay(gc_ref[rows, h:h + 1], gct_ref[cc, h:h + 1, :], tril)
            attn_ref[cc, :, hc] = (qk * dec).astype(BF16)


def _gdn_wy(tinv, vb, kbg, q, k, qg, gc, gct, cpt=4):
    b, t, _ = k.shape
    nc = t // CHUNK
    tt = cpt * CHUNK
    tok = pl.BlockSpec((None, tt, D_MODEL), lambda bi, i: (bi, i, 0))
    sq = pl.BlockSpec((None, cpt, CHUNK, GDN_HEADS * CHUNK), lambda bi, i: (bi, i, 0, 0))
    return pl.pallas_call(
        _gdn_wy_body,
        grid=(b, nc // cpt),
        in_specs=[sq, tok, tok, tok, tok, tok,
                  pl.BlockSpec((None, tt, LANES), lambda bi, i: (bi, i, 0)),
                  pl.BlockSpec((None, cpt, 8, CHUNK), lambda bi, i: (bi, i, 0, 0))],
        out_specs=[tok,
                   pl.BlockSpec((None, cpt, 2 * CHUNK, D_MODEL), lambda bi, i: (bi, i, 0, 0)),
                   sq],
        out_shape=[jax.ShapeDtypeStruct((b, t, D_MODEL), F32),
                   jax.ShapeDtypeStruct((b, nc, 2 * CHUNK, D_MODEL), BF16),
                   jax.ShapeDtypeStruct((b, nc, CHUNK, GDN_HEADS * CHUNK), BF16)],
        compiler_params=_params(("parallel", "parallel")),
        name="gdn_wy",
    )(tinv, vb, kbg, q, k, qg, gc, gct)


def _gdn_chunk_body(wq_ref, u_ref, attn_ref, kd_ref, egl_ref, o_ref, s_out_ref, sa_ref, sb_ref):
    c = pl.program_id(0)

    @pl.when(c == 0)
    def _():
        sa_ref[...] = jnp.zeros_like(sa_ref)

    def step(src_ref, dst_ref):
        heads = [(b, h) for b in range(src_ref.shape[0]) for h in range(GDN_HEADS)]
        cols = lambda h: slice(h * GDN_DK, (h + 1) * GDN_DK)
        ws = [jnp.dot(wq_ref[b, :, cols(h)], src_ref[b, h].astype(BF16), preferred_element_type=F32)
              for b, h in heads]
        v_nb = [(u_ref[b, :, cols(h)] - w_[:CHUNK]).astype(BF16) for (b, h), w_ in zip(heads, ws)]
        for (b, h), w_, v_ in zip(heads, ws, v_nb):
            o_ref[b, :, cols(h)] = w_[CHUNK:] + jnp.dot(attn_ref[b, :, h * CHUNK:(h + 1) * CHUNK], v_,
                                                        preferred_element_type=F32)
        for (b, h), v_ in zip(heads, v_nb):
            dst_ref[b, h] = src_ref[b, h] * egl_ref[b, h:h + 1, :] + lax.dot_general(
                kd_ref[b, :, cols(h)], v_, (((0,), (0,)), ((), ())), preferred_element_type=F32)

        @pl.when(c == pl.num_programs(0) - 1)
        def _():
            s_out_ref[...] = dst_ref[...]

    @pl.when(c % 2 == 0)
    def _():
        step(sa_ref, sb_ref)

    @pl.when(c % 2 == 1)
    def _():
        step(sb_ref, sa_ref)


def _gdn_chunks(wq, u, attn, kd, egl):
    b, t, _ = u.shape
    nc = t // CHUNK
    tok = pl.BlockSpec((b, CHUNK, D_MODEL), lambda c: (0, c, 0))
    return pl.pallas_call(
        _gdn_chunk_body,
        grid=(nc,),
        in_specs=[
            pl.BlockSpec((b, None, 2 * CHUNK, D_MODEL), lambda c: (0, c, 0, 0)),
            tok,
            pl.BlockSpec((b, None, CHUNK, GDN_HEADS * CHUNK), lambda c: (0, c, 0, 0)),
            tok,
            pl.BlockSpec((b, None, 8, LANES), lambda c: (0, c, 0, 0)),
        ],
        out_specs=[
            tok,
            pl.BlockSpec((b, GDN_HEADS, GDN_DK, GDN_DK), lambda c: (0, 0, 0, 0)),
        ],
        out_shape=[
            jax.ShapeDtypeStruct((b, t, D_MODEL), F32),
            jax.ShapeDtypeStruct((b, GDN_HEADS, GDN_DK, GDN_DK), F32),
        ],
        scratch_shapes=[pltpu.VMEM((b, GDN_HEADS, GDN_DK, GDN_DK), F32)] * 2,
        compiler_params=_params(("arbitrary",)),
        name="gdn_chunks",
    )(wq, u, attn, kd, egl)


def _gdn_out_body(o_ref, z_ref, on_ref, w_ref, r_ref, out_ref, *, precise):
    parts = []
    for h in range(GDN_HEADS):
        oh = _head_cols(o_ref[...], h)
        zh = _head_cols(z_ref[...], h)
        parts.append(_rms(oh, on_ref[...]) * _silu(zh))
    if precise:
        gh, gl = _split(jnp.concatenate(parts, axis=1))
        out_ref[...] = r_ref[...] + _dot3(gh, gl, w_ref[...])
    else:
        gated = jnp.concatenate([a.astype(BF16) for a in parts], axis=1)
        out_ref[...] = r_ref[...] + jnp.dot(gated, w_ref[...], preferred_element_type=F32)


def _gdn_out(o, proj, o_norm, w_out, res, tm=512):
    m, k = o.shape
    tm = min(tm, m)
    return pl.pallas_call(
        functools.partial(_gdn_out_body, precise=_is_precise(w_out)),
        grid=(m // tm,),
        in_specs=[
            pl.BlockSpec((tm, k), lambda i: (i, 0)),
            pl.BlockSpec((tm, k), lambda i: (i, QKV_W // D_MODEL)),
            pl.BlockSpec((1, GDN_DK), lambda i: (0, 0)),
            pl.BlockSpec((k, k), lambda i: (0, 0)),
            pl.BlockSpec((tm, k), lambda i: (i, 0)),
        ],
        out_specs=pl.BlockSpec((tm, k), lambda i: (i, 0)),
        out_shape=jax.ShapeDtypeStruct((m, k), F32),
        compiler_params=_params(("parallel",)),
        name="gdn_out",
    )(o, proj, o_norm.reshape(1, GDN_DK).astype(F32), w_out, res)


def _gdn_prompt_layer(h, l, p, w):
    b, t, _ = h.shape
    hf = h.reshape(b * t, D_MODEL)
    proj, ab = _mm(hf, w['a_w_in'][l], gain=p['a_norm'][l], w_aux=w['a_w_ab'][l], tn=2 * D_MODEL)
    proj3 = proj.reshape(b, t, PROJ_W)
    q, qg, k, kb, kbg, kd, vb, gc, gct, egl = _gdn_prep(
        proj3, ab.reshape(b, t, LANES), p['a_conv_w'][l], p['a_A_log'][l], p['a_dt_bias'][l])
    lmat = _gdn_l(kb, k, gc, gct)
    nprob = b * (t // CHUNK)
    ppad = -nprob % LANES
    lt = jnp.transpose(lmat.reshape(nprob, CHUNK, GDN_HEADS * CHUNK), (1, 2, 0))
    lt = jnp.pad(lt, ((0, 0), (0, 0), (0, ppad))).reshape(CHUNK, GDN_HEADS, CHUNK, nprob + ppad)
    tt = _tri_inv(lt).reshape(CHUNK, GDN_HEADS * CHUNK, nprob + ppad)[:, :, :nprob]
    tinv = jnp.transpose(tt, (2, 0, 1)).reshape(b, t // CHUNK, CHUNK, GDN_HEADS * CHUNK)
    u, wq, attn = _gdn_wy(tinv, vb, kbg, q, k, qg, gc, gct)
    o, s_new = _gdn_chunks(wq, u, attn, kd, egl)
    h_new = _gdn_out(o.reshape(b * t, D_MODEL), proj, p['a_o_norm'][l], w['a_w_out'][l], hf)
    conv_new = proj3[:, t - (CONV_TAPS - 1):, :QKV_W]
    return h_new.reshape(b, t, D_MODEL), s_new, conv_new


def _gdn_prep1_body(x_ref, st_ref, ab_ref, cw_ref, alog_ref, dtb_ref,
                    q_ref, k_ref, v_ref, eg_ref, beta_ref):
    acc = st_ref[0] * cw_ref[0:1, :]
    for j in range(1, CONV_TAPS - 1):
        acc = acc + st_ref[j] * cw_ref[j:j + 1, :]
    acc = acc + x_ref[...] * cw_ref[CONV_TAPS - 1:CONV_TAPS, :]
    y = _silu(acc)
    ab = ab_ref[...]
    eg_ref[...] = jnp.exp(-jnp.exp(alog_ref[...]) * jax.nn.softplus(ab + dtb_ref[...]))
    beta_ref[...] = jax.nn.sigmoid(pltpu.roll(ab, LANES - GDN_HEADS, axis=1))
    for h in range(GDN_HEADS):
        qh = _head_cols(y, h)
        kh = _head_cols(y, GDN_HEADS + h)
        sl = slice(h * GDN_DK, (h + 1) * GDN_DK)
        q_ref[:, sl] = qh * lax.rsqrt(jnp.sum(qh * qh, axis=-1, keepdims=True) + EPS) * (GDN_DK ** -0.5)
        k_ref[:, sl] = kh * lax.rsqrt(jnp.sum(kh * kh, axis=-1, keepdims=True) + EPS)
    v_ref[...] = y[:, 2 * GDN_HEADS * GDN_DK:]


def _gdn_prep1(proj, ab, conv_state, conv_w, a_log, dt_bias):
    n = proj.shape[0]
    row = lambda v: jnp.zeros((1, LANES), F32).at[0, :GDN_HEADS].set(v)
    tok = pl.BlockSpec((n, D_MODEL), lambda i: (0, 0))
    small = pl.BlockSpec((n, LANES), lambda i: (0, 0))
    return pl.pallas_call(
        _gdn_prep1_body,
        grid=(1,),
        in_specs=[
            pl.BlockSpec((n, QKV_W), lambda i: (0, 0)),
            pl.BlockSpec((CONV_TAPS - 1, n, QKV_W), lambda i: (0, 0, 0)),
            pl.BlockSpec((n, LANES), lambda i: (0, 0)),
            pl.BlockSpec((CONV_TAPS, QKV_W), lambda i: (0, 0)),
            pl.BlockSpec((1, LANES), lambda i: (0, 0)),
            pl.BlockSpec((1, LANES), lambda i: (0, 0)),
        ],
        out_specs=[tok, tok, tok, small, small],
        out_shape=[jax.ShapeDtypeStruct((n, D_MODEL), F32)] * 3 + [jax.ShapeDtypeStruct((n, LANES), F32)] * 2,
        compiler_params=_params(("arbitrary",)),
        name="gdn_prep1",
    )(proj, conv_state, ab, conv_w.astype(F32), row(a_log), row(dt_bias))


def _gdn_step_body(qt_ref, kt_ref, v_ref, eg_ref, beta_ref, s_ref, o_ref, s_out_ref):
    for b in range(s_ref.shape[0]):
        for h in range(GDN_HEADS):
            k_col = kt_ref[b, :, h:h + 1]
            q_col = qt_ref[b, :, h:h + 1]
            s = s_ref[b, h] * eg_ref[b, h:h + 1, :]
            kv_mem = jnp.sum(k_col * s, axis=0, keepdims=True)
            delta = (v_ref[b, h:h + 1, :] - kv_mem) * beta_ref[b, h:h + 1, :]
            s = s + k_col * delta
            s_out_ref[b, h] = s
            o_ref[b, h:h + 1, :] = jnp.sum(q_col * s, axis=0, keepdims=True)


def _gdn_step(qt, kt, v, eg, beta, state, sb=4):
    n = v.shape[0]
    col = pl.BlockSpec((sb, GDN_DK, GDN_HEADS), lambda i: (i, 0, 0))
    row = pl.BlockSpec((sb, GDN_HEADS, GDN_DK), lambda i: (i, 0, 0))
    st = pl.BlockSpec((sb, GDN_HEADS, GDN_DK, GDN_DK), lambda i: (i, 0, 0, 0))
    return pl.pallas_call(
        _gdn_step_body,
        grid=(n // sb,),
        in_specs=[col, col, row, row, row, st],
        out_specs=[row, st],
        out_shape=[jax.ShapeDtypeStruct((n, GDN_HEADS, GDN_DK), F32),
                   jax.ShapeDtypeStruct(state.shape, F32)],
        compiler_params=_params(("parallel",)),
        name="gdn_step",
    )(qt, kt, v, eg, beta, state)


def _gdn_sample_layer(h, l, p, w, conv_state, delta_state):
    n = h.shape[0]
    proj, ab = _mm(h, w['a_w_in'][l], gain=p['a_norm'][l], w_aux=w['a_w_ab'][l])
    conv_t = jnp.transpose(conv_state, (1, 0, 2))
    q, k, v, eg, beta = _gdn_prep1(proj, ab, conv_t, p['a_conv_w'][l], p['a_A_log'][l], p['a_dt_bias'][l])
    heads = lambda a: a.reshape(n, GDN_HEADS, GDN_DK)
    cols = lambda a: jnp.transpose(heads(a), (0, 2, 1))
    lanes = lambda a: jnp.broadcast_to(a[:, :GDN_HEADS, None], (n, GDN_HEADS, GDN_DK))
    o, s_new = _gdn_step(cols(q), cols(k), heads(v), lanes(eg), lanes(beta), delta_state)
    h_new = _gdn_out(o.reshape(n, D_MODEL), proj, p['a_o_norm'][l], w['a_w_out'][l], h)
    conv_new = jnp.concatenate([conv_state[:, 1:], proj[:, None, :QKV_W]], axis=1)
    return h_new, s_new, conv_new


def _rope_tables(pos):
    half = ROT // 2
    inv = ROPE_THETA ** (-jnp.arange(0, ROT, 2, dtype=F32) / ROT)
    ang = pos.astype(F32)[:, None] * inv[None, :]
    cos, sin = jnp.cos(ang), jnp.sin(ang)
    ones = jnp.ones((pos.shape[0], HEAD_DIM - ROT), F32)
    zeros = jnp.zeros_like(ones)
    z8 = jnp.zeros_like(sin)
    c = jnp.concatenate([cos, cos, ones], axis=1)
    sm = jnp.concatenate([-sin, z8, zeros], axis=1)
    sp = jnp.concatenate([z8, sin, zeros], axis=1)
    del half
    return tuple(jnp.tile(a, (1, LANES // HEAD_DIM)) for a in (c, sm, sp))


def _dot_split(a, sel):
    hi = a.astype(BF16)
    lo = (a - hi.astype(F32)).astype(BF16)
    return (jnp.dot(hi, sel, preferred_element_type=F32) + jnp.dot(lo, sel, preferred_element_type=F32))


def _headnorm_rope_body(x_ref, g_ref, seg_ref, segt_ref, c_ref, sm_ref, sp_ref, o_ref):
    x = x_ref[...]
    w = x.shape[1]
    ms = _dot_split(x * x, seg_ref[...]) * (1.0 / HEAD_DIM)
    scale = _dot_split(lax.rsqrt(ms + EPS), segt_ref[...])
    y = x * scale * g_ref[...]
    reps = w // LANES
    tile = lambda a: jnp.concatenate([a] * reps, axis=1) if reps > 1 else a
    half = ROT // 2
    out = (y * tile(c_ref[...]) + pltpu.roll(y, w - half, axis=1) * tile(sm_ref[...])
           + pltpu.roll(y, half, axis=1) * tile(sp_ref[...]))
    o_ref[...] = out.astype(o_ref.dtype)


def _headnorm_rope(x, col_block, width, gain, tables, rows_per_seq, out_dtype, tm=512):
    m = x.shape[0]
    tm = min(tm, m, rows_per_seq)
    nheads = width // HEAD_DIM
    seg = (jnp.arange(width)[:, None] // HEAD_DIM == jnp.arange(LANES)[None, :]).astype(BF16)
    per_seq = rows_per_seq // tm
    tab = pl.BlockSpec((tm, LANES), lambda i: (i % per_seq, 0))
    del nheads
    return pl.pallas_call(
        _headnorm_rope_body,
        grid=(m // tm,),
        in_specs=[
            pl.BlockSpec((tm, width), lambda i: (i, col_block)),
            pl.BlockSpec((1, width), lambda i: (0, 0)),
            pl.BlockSpec((width, LANES), lambda i: (0, 0)),
            pl.BlockSpec((LANES, width), lambda i: (0, 0)),
            tab, tab, tab,
        ],
        out_specs=pl.BlockSpec((tm, width), lambda i: (i, 0)),
        out_shape=jax.ShapeDtypeStruct((m, width), out_dtype),
        compiler_params=_params(("parallel",)),
        name="headnorm_rope",
    )(x, jnp.tile(gain.astype(F32), width // HEAD_DIM).reshape(1, width), seg, seg.T, *tables)


def _softmax_sink(s, sink):
    m = jnp.maximum(jnp.max(s, axis=-1, keepdims=True), sink)
    p = jnp.exp(s - m)
    return p, jnp.sum(p, axis=-1, keepdims=True) + jnp.exp(sink - m)


def _swa_prompt_body(sink_ref, q_ref, kp_ref, kc_ref, vp_ref, vc_ref, o_ref):
    i = pl.program_id(1)
    blk = q_ref.shape[0]
    r = lax.broadcasted_iota(I32, (blk, 2 * blk), 0)
    c = lax.broadcasted_iota(I32, (blk, 2 * blk), 1)
    valid = (c >= r) & (c <= r + WINDOW) & ((c >= blk) | (i > 0))
    for kvh in range(KV_HEADS):
        cols = slice(kvh * HEAD_DIM, (kvh + 1) * HEAD_DIM)
        kk = jnp.concatenate([kp_ref[:, cols], kc_ref[:, cols]], axis=0).astype(BF16)
        vv = jnp.concatenate([vp_ref[:, cols], vc_ref[:, cols]], axis=0).astype(BF16)
        for g in range(ATT_GROUP):
            h = kvh * ATT_GROUP + g
            hc = slice(h * HEAD_DIM, (h + 1) * HEAD_DIM)
            s = lax.dot_general(q_ref[:, hc], kk, (((1,), (1,)), ((), ())),
                                preferred_element_type=F32) * (HEAD_DIM ** -0.5)
            s = jnp.where(valid, s, -jnp.inf)
            p, den = _softmax_sink(s, sink_ref[h])
            o_ref[:, hc] = jnp.dot((p / den).astype(BF16), vv, preferred_element_type=F32).astype(o_ref.dtype)


def _swa_prompt(q, k, kv, sinks):
    b, t, _ = q.shape
    blk = WINDOW
    kvw = KV_HEADS * HEAD_DIM
    prev = lambda bi, i, s: (bi, jnp.maximum(i - 1, 0), 0)
    cur = lambda bi, i, s: (bi, i, 0)
    prev_v = lambda bi, i, s: (bi, jnp.maximum(i - 1, 0), 1)
    cur_v = lambda bi, i, s: (bi, i, 1)
    grid_spec = pltpu.PrefetchScalarGridSpec(
        num_scalar_prefetch=1,
        grid=(b, t // blk),
        in_specs=[
            pl.BlockSpec((None, blk, D_MODEL), cur),
            pl.BlockSpec((None, blk, kvw), prev),
            pl.BlockSpec((None, blk, kvw), cur),
            pl.BlockSpec((None, blk, kvw), prev_v),
            pl.BlockSpec((None, blk, kvw), cur_v),
        ],
        out_specs=pl.BlockSpec((None, blk, D_MODEL), cur),
    )
    return pl.pallas_call(
        _swa_prompt_body,
        grid_spec=grid_spec,
        out_shape=jax.ShapeDtypeStruct((b, t, D_MODEL), BF16),
        compiler_params=_params(("parallel", "parallel")),
        name="swa_prompt",
    )(sinks.astype(F32), q, k, k, kv, kv)


def _swa_sample_body(sink_ref, q_ref, kc_ref, kn_ref, vc_ref, vn_ref, o_ref):
    wc = kc_ref.shape[1]
    kvw = kc_ref.shape[2]
    nkeys = 2 * wc
    fill = jnp.zeros((nkeys - wc - 8, kvw), F32)
    key_idx = lax.broadcasted_iota(I32, (ATT_HEADS, nkeys), 1)
    sink = sink_ref[:, 0:1]
    seqs = range(q_ref.shape[0])
    scores = []
    for b in seqs:
        kk = jnp.concatenate([kc_ref[b], kn_ref[b], fill], axis=0)
        scores.append(lax.dot_general(q_ref[b], kk, (((1,), (1,)), ((), ())), precision=HI,
                                      preferred_element_type=F32))
    probs = []
    for s in scores:
        s = jnp.where(key_idx <= wc, s * (HEAD_DIM ** -0.5), -jnp.inf)
        p, den = _softmax_sink(s, sink)
        probs.append(p / den)
    outs = []
    for b, p in zip(seqs, probs):
        vv = jnp.concatenate([vc_ref[b], vn_ref[b], fill], axis=0)
        outs.append(jnp.dot(p, vv, precision=HI, preferred_element_type=F32))
    for b, o in zip(seqs, outs):
        for kvh in range(KV_HEADS):
            heads = slice(kvh * ATT_GROUP, (kvh + 1) * ATT_GROUP)
            o_ref[b, heads, :] = o[heads, kvh * HEAD_DIM:(kvh + 1) * HEAD_DIM]


def _swa_sample(q, k_cache, k_new, v_cache, v_new, sinks, sb=8):
    n = q.shape[0]
    wc = k_cache.shape[1]
    kvw = KV_HEADS * HEAD_DIM
    own = (jnp.arange(ATT_HEADS)[:, None] // ATT_GROUP == jnp.arange(KV_HEADS)[None, :])
    q_rows = (q[:, :, None, :] * own[None, :, :, None].astype(q.dtype)).reshape(n, ATT_HEADS, kvw)
    pad8 = lambda a: jnp.pad(a[:, None, :], ((0, 0), (0, 7), (0, 0)))
    blk = lambda shape: pl.BlockSpec(shape, lambda i: (i, 0, 0))
    return pl.pallas_call(
        _swa_sample_body,
        grid=(n // sb,),
        in_specs=[pl.BlockSpec((ATT_HEADS, LANES), lambda i: (0, 0)),
                  blk((sb, ATT_HEADS, kvw)), blk((sb, wc, kvw)), blk((sb, 8, kvw)),
                  blk((sb, wc, kvw)), blk((sb, 8, kvw))],
        out_specs=blk((sb, ATT_HEADS, HEAD_DIM)),
        out_shape=jax.ShapeDtypeStruct((n, ATT_HEADS, HEAD_DIM), F32),
        compiler_params=_params(("parallel",)),
        name="swa_sample",
    )(jnp.broadcast_to(sinks.astype(F32)[:, None], (ATT_HEADS, LANES)), q_rows, k_cache, pad8(k_new),
      v_cache, pad8(v_new))


def _channel_mixer(hf, l, p, w):
    if l % 2 == 0:
        return _ffn_dense(hf, p['ffn_norm'][l], w['dense_w_gu'][l // 2], w['dense_w_down'][l // 2])
    return _moe(hf, p['ffn_norm'][l], p['moe_router'][l // 2], p['moe_router_b'][l // 2],
                w['moe_w_gu'][l // 2], w['moe_w_down'][l // 2])


def _run_prompt(x, p, w):
    b, t, _ = x.shape
    n = b * t
    h = x
    convs, deltas = [], []
    for l in range(N_GDN):
        h, s_new, c_new = _gdn_prompt_layer(h, l, p, w)
        convs.append(c_new)
        deltas.append(s_new)
        h = _channel_mixer(h.reshape(n, D_MODEL), l, p, w).reshape(b, t, D_MODEL)
    hf = h.reshape(n, D_MODEL)
    tables = _rope_tables(jnp.arange(t, dtype=I32))
    kvw = KV_HEADS * HEAD_DIM
    kv = _mm(hf, w['w_kv'], gain=p['kv_norm'], tn=2 * kvw)
    k_sh = _headnorm_rope(kv, 0, kvw, p['k_norm'], tables, t, F32)
    k3 = k_sh.reshape(b, t, kvw)
    kv3 = kv.reshape(b, t, 2 * kvw)
    for j in range(2):
        l = N_GDN + j
        qp = _mm(hf, w['b_w_q'][j], gain=p['b_norm'][j])
        q = _headnorm_rope(qp, 0, D_MODEL, p['b_q_norm'][j], tables, t, BF16)
        o = _swa_prompt(q.reshape(b, t, D_MODEL), k3, kv3, p['b_sinks'][j])
        hf = _mm(o.reshape(n, D_MODEL), w['b_w_o'][j], res=hf)
        hf = _channel_mixer(hf, l, p, w)
    k_win = k3[:, t - WINDOW:].reshape(b, WINDOW, KV_HEADS, HEAD_DIM)
    v_win = kv3[:, t - WINDOW:, kvw:].reshape(b, WINDOW, KV_HEADS, HEAD_DIM)
    return hf.reshape(b, t, D_MODEL), jnp.stack(convs), jnp.stack(deltas), k_win, v_win


def _run_sample(x, conv_state, delta_state, k_cache, v_cache, p, w):
    n = x.shape[0]
    h = x.reshape(n, D_MODEL)
    convs, deltas = [], []
    for l in range(N_GDN):
        h, s_new, c_new = _gdn_sample_layer(h, l, p, w, conv_state[l], delta_state[l])
        convs.append(c_new)
        deltas.append(s_new)
        h = _channel_mixer(h, l, p, w)
    tables = _rope_tables(jnp.full((n,), PAST_LEN, I32))
    kvw = KV_HEADS * HEAD_DIM
    wc = k_cache.shape[1]
    kv = _mm(h, w['w_kv'], gain=p['kv_norm'], tn=2 * kvw)
    k_new = _headnorm_rope(kv, 0, kvw, p['k_norm'], tables, n, F32)
    v_new = kv[:, kvw:]
    kc = k_cache.reshape(n, wc, kvw)
    vc = v_cache.reshape(n, wc, kvw)
    for j in range(2):
        l = N_GDN + j
        qp = _mm(h, w['b_w_q'][j], gain=p['b_norm'][j])
        q = _headnorm_rope(qp, 0, D_MODEL, p['b_q_norm'][j], tables, n, F32)
        o = _swa_sample(q.reshape(n, ATT_HEADS, HEAD_DIM), kc, k_new, vc, v_new, p['b_sinks'][j])
        h = _mm(o.reshape(n, D_MODEL), w['b_w_o'][j], res=h)
        h = _channel_mixer(h, l, p, w)
    k_win = jnp.concatenate([kc, k_new[:, None, :]], axis=1)[:, -wc:].reshape(n, wc, KV_HEADS, HEAD_DIM)
    v_win = jnp.concatenate([vc, v_new[:, None, :]], axis=1)[:, -wc:].reshape(n, wc, KV_HEADS, HEAD_DIM)
    return h.reshape(n, 1, D_MODEL), jnp.stack(convs), jnp.stack(deltas), k_win, v_win


def kernel(x_prompt, x_sample, state_conv, state_delta, cache_k_win, cache_v_win, a_norm, a_w_in, a_conv_w,
           a_A_log, a_dt_bias, a_o_norm, a_w_out, kv_norm, w_kv, k_norm, b_norm, b_w_q, b_q_norm, b_sinks,
           b_w_o, ffn_norm, dense_w_gu, dense_w_down, moe_router, moe_router_b, moe_w_gu, moe_w_down):
    p = dict(a_norm=a_norm, a_conv_w=a_conv_w, a_A_log=a_A_log, a_dt_bias=a_dt_bias, a_o_norm=a_o_norm,
             kv_norm=kv_norm, k_norm=k_norm, b_norm=b_norm, b_q_norm=b_q_norm, b_sinks=b_sinks,
             ffn_norm=ffn_norm, moe_router=moe_router, moe_router_b=moe_router_b)
    n_proj = PROJ_W
    gate_cols = jnp.pad(a_w_in[:, :, n_proj:], ((0, 0), (0, 0), (0, LANES - (a_w_in.shape[-1] - n_proj))))
    raw = dict(a_w_in=a_w_in[:, :, :n_proj], a_w_ab=gate_cols, a_w_out=a_w_out, w_kv=w_kv[None],
               b_w_q=b_w_q, b_w_o=b_w_o, dense_w_gu=dense_w_gu, dense_w_down=dense_w_down,
               moe_w_gu=moe_w_gu, moe_w_down=moe_w_down)
    w_precise = {name: [val[l] for l in range(val.shape[0])] for name, val in raw.items()}
    w_fast = {name: [wl.astype(BF16) for wl in vals] for name, vals in w_precise.items()}
    for ws in (w_fast, w_precise):
        ws['w_kv'] = ws['w_kv'][0]
    y_p, conv_p, delta_p, kw_p, vw_p = _run_prompt(x_prompt, p, w_fast)
    y_s, conv_s, delta_s, kw_s, vw_s = _run_sample(
        x_sample, state_conv, state_delta, cache_k_win, cache_v_win, p, w_precise)
    return (y_p, y_s, conv_p, conv_s, delta_p, delta_s, kw_p, kw_s, vw_p, vw_s)
```

```python
import functools

import jax
import jax.numpy as jnp
from jax import lax
from jax.experimental import pallas as pl
from jax.experimental.pallas import tpu as pltpu

F32, BF16, I32 = jnp.float32, jnp.bfloat16, jnp.int32
HI = lax.Precision.HIGHEST

D_MODEL = 1024
N_GDN = 2
GDN_HEADS = 8
GDN_DK = 128
CONV_TAPS = 4
CHUNK = 64
QKV_W = 3 * GDN_HEADS * GDN_DK
PROJ_W = QKV_W + D_MODEL
ATT_HEADS = 16
KV_HEADS = 4
HEAD_DIM = 64
ATT_GROUP = ATT_HEADS // KV_HEADS
WINDOW = 128
ROT = HEAD_DIM // 4
ROPE_THETA = 500000.0
PAST_LEN = 8192
N_EXPERTS = 8
EPS = 1e-6

LANES = 128
VMEM_LIMIT = 56 * 1024 * 1024
NEG = -1e30


def _params(sem):
    return pltpu.CompilerParams(dimension_semantics=sem, vmem_limit_bytes=VMEM_LIMIT)


def _rms(x, g):
    return x * lax.rsqrt(jnp.mean(x * x, axis=-1, keepdims=True) + EPS) * g


def _silu(x):
    return x * jax.nn.sigmoid(x)


def _split(x):
    hi = x.astype(BF16)
    return hi, (x - hi.astype(F32)).astype(BF16)


def _dot3(xh, xl, w):
    wh, wl = _split(w)
    return (jnp.dot(xh, wh, preferred_element_type=F32)
            + (jnp.dot(xl, wh, preferred_element_type=F32) + jnp.dot(xh, wl, preferred_element_type=F32)))


class _Layer:
    def __init__(self, arr, layer, cols=None):
        self.arr, self.lead, self.dtype = arr, (layer,), arr.dtype
        self.shape = arr.shape[1:] if cols is None else arr.shape[1:-1] + (cols,)


def _warr(w):
    return w.arr if isinstance(w, _Layer) else w


def _wspec(w, block, index_fn):
    lead = w.lead if isinstance(w, _Layer) else ()
    return pl.BlockSpec((None,) * len(lead) + tuple(block), lambda *a: lead + tuple(index_fn(*a)))


def _is_precise(w):
    assert w.dtype in (BF16, F32)
    return w.dtype == F32


def _mm_body(*refs, has_norm, has_aux, has_res, precise):
    it = iter(refs)
    x_ref = next(it)
    g_ref = next(it) if has_norm else None
    w_ref = next(it)
    wa_ref = next(it) if has_aux else None
    r_ref = next(it) if has_res else None
    o_ref = next(it)
    oa_ref = next(it) if has_aux else None
    xn_ref = next(it)

    def product(w):
        if precise:
            return _dot3(xn_ref[0], xn_ref[1], w)
        return jnp.dot(xn_ref[0], w, preferred_element_type=F32)

    @pl.when(pl.program_id(1) == 0)
    def _():
        x = x_ref[...].astype(F32)
        if has_norm:
            x = _rms(x, g_ref[...])
        if precise:
            xn_ref[0], xn_ref[1] = _split(x)
        else:
            xn_ref[0] = x.astype(BF16)
        if has_aux:
            oa_ref[...] = product(wa_ref[...])

    acc = product(w_ref[...])
    if has_res:
        acc = acc + r_ref[...]
    o_ref[...] = acc.astype(o_ref.dtype)


def _mm(x, w, gain=None, res=None, w_aux=None, tm=1024, tn=1024, out_dtype=F32):
    precise = _is_precise(w)
    m, k = x.shape
    n = w.shape[1]
    tm = min(tm, m)
    tn = min(tn, n)
    assert m % tm == 0 and n % tn == 0
    ins = [x]
    specs = [pl.BlockSpec((tm, k), lambda i, j: (i, 0))]
    if gain is not None:
        ins.append(gain.reshape(1, k).astype(F32))
        specs.append(pl.BlockSpec((1, k), lambda i, j: (0, 0)))
    ins.append(_warr(w))
    specs.append(_wspec(w, (k, tn), lambda i, j: (0, j)))
    out_specs = [pl.BlockSpec((tm, tn), lambda i, j: (i, j))]
    out_shape = [jax.ShapeDtypeStruct((m, n), out_dtype)]
    if w_aux is not None:
        na = w_aux.shape[1]
        ins.append(_warr(w_aux))
        specs.append(_wspec(w_aux, (k, na), lambda i, j: (0, 0)))
        out_specs.append(pl.BlockSpec((tm, na), lambda i, j: (i, 0)))
        out_shape.append(jax.ShapeDtypeStruct((m, na), F32))
    if res is not None:
        ins.append(res)
        specs.append(pl.BlockSpec((tm, tn), lambda i, j: (i, j)))
    out = pl.pallas_call(
        functools.partial(_mm_body, has_norm=gain is not None, has_aux=w_aux is not None,
                          has_res=res is not None, precise=precise),
        grid=(m // tm, n // tn),
        in_specs=specs,
        out_specs=out_specs,
        out_shape=out_shape,
        scratch_shapes=[pltpu.VMEM((1 + precise, tm, k), BF16)],
        compiler_params=_params(("parallel", "arbitrary")),
        name="mm",
    )(*ins)
    return out if w_aux is not None else out[0]


def _swiglu(xh, xl, wg, wu, wd):
    if xl is None:
        gate = jnp.dot(xh, wg, preferred_element_type=F32)
        up = jnp.dot(xh, wu, preferred_element_type=F32)
        return jnp.dot((_silu(gate) * up).astype(BF16), wd, preferred_element_type=F32)
    gate = _dot3(xh, xl, wg)
    up = _dot3(xh, xl, wu)
    ah, al = _split(_silu(gate) * up)
    return _dot3(ah, al, wd)


def _ffn_dense_body(x_ref, g_ref, wg_ref, wu_ref, wd_ref, o_ref, xn_ref, acc_ref, *, precise):
    f = pl.program_id(1)

    @pl.when(f == 0)
    def _():
        xn = _rms(x_ref[...], g_ref[...])
        if precise:
            xn_ref[0], xn_ref[1] = _split(xn)
        else:
            xn_ref[0] = xn.astype(BF16)
        acc_ref[...] = x_ref[...]

    acc_ref[...] += _swiglu(xn_ref[0], xn_ref[1] if precise else None, wg_ref[...], wu_ref[...], wd_ref[...])

    @pl.when(f == pl.num_programs(1) - 1)
    def _():
        o_ref[...] = acc_ref[...]


def _ffn_dense(x, gain, w_gu, w_down, tm=512):
    precise = _is_precise(w_gu)
    m, k = x.shape
    ff = w_down.shape[0]
    tm = min(tm, m)
    nf = 2 if not precise else ff // (2 * LANES)
    tf = ff // nf
    assert tf % LANES == 0 and tf * nf == ff
    return pl.pallas_call(
        functools.partial(_ffn_dense_body, precise=precise),
        grid=(m // tm, nf),
        in_specs=[
            pl.BlockSpec((tm, k), lambda i, f: (i, 0)),
            pl.BlockSpec((1, k), lambda i, f: (0, 0)),
            _wspec(w_gu, (k, tf), lambda i, f: (0, f)),
            _wspec(w_gu, (k, tf), lambda i, f: (0, nf + f)),
            _wspec(w_down, (tf, k), lambda i, f: (f, 0)),
        ],
        out_specs=pl.BlockSpec((tm, k), lambda i, f: (i, 0)),
        out_shape=jax.ShapeDtypeStruct((m, k), F32),
        scratch_shapes=[pltpu.VMEM((1 + precise, tm, k), BF16), pltpu.VMEM((tm, k), F32)],
        compiler_params=_params(("parallel", "arbitrary")),
        name="ffn_dense",
    )(x, gain.reshape(1, k).astype(F32), _warr(w_gu), _warr(w_gu), _warr(w_down))


def _router_body(x_ref, g_ref, wr_ref, br_ref, idx_ref, gate_ref, cnt_ref, carry_ref):
    i = pl.program_id(0)

    @pl.when(i == 0)
    def _():
        carry_ref[...] = jnp.zeros_like(carry_ref)

    tm = x_ref.shape[0]
    xn = _rms(x_ref[...], g_ref[...])
    logits = _dot3(*_split(xn), wr_ref[...]) + br_ref[...]
    lane = lax.broadcasted_iota(I32, logits.shape, 1)
    m1 = jnp.max(logits, axis=-1, keepdims=True)
    i1 = jnp.min(jnp.where(logits == m1, lane, LANES), axis=-1, keepdims=True)
    rest = jnp.where(lane == i1, NEG, logits)
    m2 = jnp.max(rest, axis=-1, keepdims=True)
    i2 = jnp.min(jnp.where(rest == m2, lane, LANES), axis=-1, keepdims=True)
    e2 = jnp.exp(m2 - m1)
    g1 = 1.0 / (1.0 + e2)
    g2 = e2 * g1
    oh1 = lane == i1
    oh2 = lane == i2
    oh = jnp.where(oh1 | oh2, 1.0, 0.0).astype(BF16)
    r = lax.broadcasted_iota(I32, (tm, tm), 0)
    c = lax.broadcasted_iota(I32, (tm, tm), 1)
    below = jnp.where(c < r, 1.0, 0.0).astype(BF16)
    before = jnp.dot(below, oh, preferred_element_type=F32) + carry_ref[0:1, :]
    rk1 = jnp.sum(jnp.where(oh1, before, 0.0), axis=-1, keepdims=True).astype(I32)
    rk2 = jnp.sum(jnp.where(oh2, before, 0.0), axis=-1, keepdims=True).astype(I32)
    idx_ref[...] = jnp.where(lane == 0, i1, jnp.where(lane == 1, i2, jnp.where(lane == 2, rk1, rk2)))
    gate_ref[...] = jnp.where(lane == 0, g1, g2)
    total = carry_ref[0:1, :] + jnp.sum(oh.astype(F32), axis=0, keepdims=True)
    carry_ref[...] = jnp.broadcast_to(total, carry_ref.shape)
    cnt_ref[...] = carry_ref[...]


def _router(x, gain, w_router, b_router, tm=512):
    m, k = x.shape
    tm = min(tm, m)
    wr = jnp.zeros((k, LANES), F32).at[:, :N_EXPERTS].set(w_router)
    br = jnp.full((1, LANES), NEG, F32).at[0, :N_EXPERTS].set(b_router)
    return pl.pallas_call(
        _router_body,
        grid=(m // tm,),
        in_specs=[
            pl.BlockSpec((tm, k), lambda i: (i, 0)),
            pl.BlockSpec((1, k), lambda i: (0, 0)),
            pl.BlockSpec((k, LANES), lambda i: (0, 0)),
            pl.BlockSpec((1, LANES), lambda i: (0, 0)),
        ],
        out_specs=[
            pl.BlockSpec((tm, LANES), lambda i: (i, 0)),
            pl.BlockSpec((tm, LANES), lambda i: (i, 0)),
            pl.BlockSpec((8, LANES), lambda i: (0, 0)),
        ],
        out_shape=[
            jax.ShapeDtypeStruct((m, LANES), I32),
            jax.ShapeDtypeStruct((m, LANES), F32),
            jax.ShapeDtypeStruct((8, LANES), F32),
        ],
        scratch_shapes=[pltpu.VMEM((8, LANES), F32)],
        compiler_params=_params(("arbitrary",)),
        name="moe_router",
    )(x, gain.reshape(1, k).astype(F32), wr, br)


def _row_copy(src, dst, src_row, dst_row, sem):
    return pltpu.make_async_copy(src.at[pl.ds(src_row, 1)], dst.at[pl.ds(dst_row, 1)], sem)


def _scatter_body(d_ref, x_ref, g_ref, xs_in_ref, xs_ref, buf_ref, sem):
    del xs_in_ref
    tm = buf_ref.shape[0]
    buf_ref[...] = _rms(x_ref[...], g_ref[...])

    def issue(t, carry):
        for kk in range(2):
            _row_copy(buf_ref, xs_ref, t, d_ref[0, 0, 2 * t + kk], sem).start()
        return carry

    lax.fori_loop(0, tm, issue, 0, unroll=8)
    for kk in range(2):
        pltpu.make_async_copy(buf_ref, xs_ref.at[pl.ds(0, tm)], sem).wait()


def _moe_scatter(x, gain, dest_tiles, rows, tm):
    m, k = x.shape
    return pl.pallas_call(
        _scatter_body,
        grid=(m // tm,),
        in_specs=[
            pl.BlockSpec((1, 1, 2 * tm), lambda i: (i, 0, 0), memory_space=pltpu.SMEM),
            pl.BlockSpec((tm, k), lambda i: (i, 0)),
            pl.BlockSpec((1, k), lambda i: (0, 0)),
            pl.BlockSpec(memory_space=pl.ANY),
        ],
        out_specs=pl.BlockSpec(memory_space=pl.ANY),
        out_shape=jax.ShapeDtypeStruct((rows, k), F32),
        scratch_shapes=[pltpu.VMEM((tm, k), F32), pltpu.SemaphoreType.DMA(())],
        input_output_aliases={3: 0},
        compiler_params=_params(("arbitrary",)),
        name="moe_scatter",
    )(dest_tiles, x, gain.reshape(1, k).astype(F32), jnp.zeros((rows, k), F32))


def _ffn_group_body(te_ref, nu_ref, x_ref, wg_ref, wu_ref, wd_ref, o_ref, *, precise):
    del te_ref
    used = pl.program_id(0) < nu_ref[0]

    @pl.when(used)
    def _():
        xh, xl = _split(x_ref[...]) if precise else (x_ref[...].astype(BF16), None)
        o_ref[...] = _swiglu(xh, xl, wg_ref[...], wu_ref[...], wd_ref[...])

    @pl.when(jnp.logical_not(used))
    def _():
        o_ref[...] = jnp.zeros_like(o_ref)


def _ffn_grouped(xs, tile_expert, n_used, w_gu, w_down, tm):
    precise = _is_precise(w_gu)
    rows, k = xs.shape
    ff = w_down.shape[1]
    nt = rows // tm

    def row_map(i, te, nu):
        return (jnp.minimum(i, nu[0] - 1), 0)

    grid_spec = pltpu.PrefetchScalarGridSpec(
        num_scalar_prefetch=2,
        grid=(nt,),
        in_specs=[
            pl.BlockSpec((tm, k), row_map),
            _wspec(w_gu, (None, k, ff), lambda i, te, nu: (te[i], 0, 0)),
            _wspec(w_gu, (None, k, ff), lambda i, te, nu: (te[i], 0, 1)),
            _wspec(w_down, (None, ff, k), lambda i, te, nu: (te[i], 0, 0)),
        ],
        out_specs=pl.BlockSpec((tm, k), lambda i, te, nu: (i, 0)),
    )
    return pl.pallas_call(
        functools.partial(_ffn_group_body, precise=precise),
        grid_spec=grid_spec,
        out_shape=jax.ShapeDtypeStruct((rows, k), F32),
        compiler_params=_params(("arbitrary",)),
        name="moe_experts",
    )(tile_expert, n_used, xs, _warr(w_gu), _warr(w_gu), _warr(w_down))


def _combine_body(d_ref, x_ref, gate_ref, ys_ref, o_ref, buf_ref, sem):
    tm = x_ref.shape[0]

    def issue(t, carry):
        for kk in range(2):
            _row_copy(ys_ref, buf_ref.at[kk], d_ref[0, 0, 2 * t + kk], t, sem).start()
        return carry

    lax.fori_loop(0, tm, issue, 0, unroll=8)
    for kk in range(2):
        pltpu.make_async_copy(ys_ref.at[pl.ds(0, tm)], buf_ref.at[kk], sem).wait()
    g = gate_ref[...]
    o_ref[...] = x_ref[...] + g[:, 0:1] * buf_ref[0] + g[:, 1:2] * buf_ref[1]


def _moe_combine(x, gates, dest_tiles, ys, tm):
    m, k = x.shape
    return pl.pallas_call(
        _combine_body,
        grid=(m // tm,),
        in_specs=[
            pl.BlockSpec((1, 1, 2 * tm), lambda i: (i, 0, 0), memory_space=pltpu.SMEM),
            pl.BlockSpec((tm, k), lambda i: (i, 0)),
            pl.BlockSpec((tm, LANES), lambda i: (i, 0)),
            pl.BlockSpec(memory_space=pl.ANY),
        ],
        out_specs=pl.BlockSpec((tm, k), lambda i: (i, 0)),
        out_shape=jax.ShapeDtypeStruct((m, k), F32),
        scratch_shapes=[pltpu.VMEM((2, tm, k), F32), pltpu.SemaphoreType.DMA(())],
        compiler_params=_params(("arbitrary",)),
        name="moe_combine",
    )(dest_tiles, x, gates, ys)


SEG_TOK = 256
SEG_ROWS = 640
EXPERT_TILE = 512


def _router_seg_body(x_ref, g_ref, wr_ref, br_ref, idx_ref, gate_ref, tab_ref, cnt_ref, carry_ref):
    i = pl.program_id(0)

    @pl.when(i == 0)
    def _():
        carry_ref[...] = jnp.zeros_like(carry_ref)

    tm = x_ref.shape[0]
    xn = _rms(x_ref[...], g_ref[...])
    logits = _dot3(*_split(xn), wr_ref[...]) + br_ref[...]
    lane = lax.broadcasted_iota(I32, logits.shape, 1)
    m1 = jnp.max(logits, axis=-1, keepdims=True)
    i1 = jnp.min(jnp.where(logits == m1, lane, LANES), axis=-1, keepdims=True)
    rest = jnp.where(lane == i1, NEG, logits)
    m2 = jnp.max(rest, axis=-1, keepdims=True)
    i2 = jnp.min(jnp.where(rest == m2, lane, LANES), axis=-1, keepdims=True)
    e2 = jnp.exp(m2 - m1)
    g1 = 1.0 / (1.0 + e2)
    g2 = e2 * g1
    oh1 = lane == i1
    oh2 = lane == i2
    oh = jnp.where(oh1 | oh2, 1.0, 0.0)
    r = lax.broadcasted_iota(I32, (tm, tm), 0)
    c = lax.broadcasted_iota(I32, (tm, tm), 1)
    below = jnp.where(c < r, 1.0, 0.0).astype(BF16)
    rank = jnp.dot(below, oh.astype(BF16), preferred_element_type=F32)
    count = jnp.sum(oh, axis=0, keepdims=True)
    padded = jnp.floor((count + 7.0) * 0.125) * 8.0
    er = lax.broadcasted_iota(I32, (LANES, LANES), 0)
    ec = lax.broadcasted_iota(I32, (LANES, LANES), 1)
    start = jnp.dot(jnp.broadcast_to(padded, (8, LANES)), jnp.where(er < ec, 1.0, 0.0),
                    precision=HI, preferred_element_type=F32)[0:1]
    slot = rank + start
    s1 = jnp.sum(jnp.where(oh1, slot, 0.0), axis=-1, keepdims=True).astype(I32)
    s2 = jnp.sum(jnp.where(oh2, slot, 0.0), axis=-1, keepdims=True).astype(I32)
    idx_ref[...] = jnp.where(lane == 0, i1, jnp.where(lane == 1, i2, jnp.where(lane == 2, s1, s2)))
    gate_ref[...] = jnp.where(lane == 0, g1, g2)
    sub = lax.broadcasted_iota(I32, (8, LANES), 0)
    rows = jnp.where(sub == 0, padded * 0.125, jnp.where(sub == 1, start, carry_ref[...]))
    tab_ref[...] = jnp.where(sub < 3, rows, 0.0).astype(I32)
    carry_ref[...] = carry_ref[...] + padded
    cnt_ref[...] = carry_ref[...]


def _router_seg(x, gain, w_router, b_router):
    m, k = x.shape
    tm = SEG_TOK
    wr = jnp.zeros((k, LANES), F32).at[:, :N_EXPERTS].set(w_router)
    br = jnp.full((1, LANES), NEG, F32).at[0, :N_EXPERTS].set(b_router)
    return pl.pallas_call(
        _router_seg_body,
        grid=(m // tm,),
        in_specs=[
            pl.BlockSpec((tm, k), lambda i: (i, 0)),
            pl.BlockSpec((1, k), lambda i: (0, 0)),
            pl.BlockSpec((k, LANES), lambda i: (0, 0)),
            pl.BlockSpec((1, LANES), lambda i: (0, 0)),
        ],
        out_specs=[
            pl.BlockSpec((tm, LANES), lambda i: (i, 0)),
            pl.BlockSpec((tm, LANES), lambda i: (i, 0)),
            pl.BlockSpec((None, 8, LANES), lambda i: (i, 0, 0)),
            pl.BlockSpec((8, LANES), lambda i: (0, 0)),
        ],
        out_shape=[
            jax.ShapeDtypeStruct((m, LANES), I32),
            jax.ShapeDtypeStruct((m, LANES), F32),
            jax.ShapeDtypeStruct((m // tm, 8, LANES), I32),
            jax.ShapeDtypeStruct((8, LANES), F32),
        ],
        scratch_shapes=[pltpu.VMEM((8, LANES), F32)],
        compiler_params=_params(("arbitrary",)),
        name="moe_router_seg",
    )(x, gain.reshape(1, k).astype(F32), wr, br)


def _chunk_copy(src, dst, src_row, dst_row, sem):
    rows = lambda r: pl.ds(r if isinstance(r, int) else pl.multiple_of(r, 8), 8)
    return pltpu.make_async_copy(src.at[rows(src_row)], dst.at[rows(dst_row)], sem)


def _segment_copies(tab_ref, base_ref, local_ref, hbm_ref, sem, to_hbm):
    total = 0
    for e in range(N_EXPERTS):
        chunks = tab_ref[0, e]
        local0 = tab_ref[1, e]
        hbm0 = base_ref[e] + tab_ref[2, e]

        def issue(j, carry, local0=local0, hbm0=hbm0):
            if to_hbm:
                _chunk_copy(local_ref, hbm_ref, local0 + 8 * j, hbm0 + 8 * j, sem).start()
            else:
                _chunk_copy(hbm_ref, local_ref, hbm0 + 8 * j, local0 + 8 * j, sem).start()
            return carry

        lax.fori_loop(0, chunks, issue, 0)
        total = total + chunks

    def drain(j, carry):
        if to_hbm:
            _chunk_copy(local_ref, hbm_ref, 0, 0, sem).wait()
        else:
            _chunk_copy(hbm_ref, local_ref, 0, 0, sem).wait()
        return carry

    lax.fori_loop(0, total, drain, 0)


def _slot_onehot(idx_ref, which, width):
    lane = lax.broadcasted_iota(I32, (idx_ref.shape[0], width), 1)
    return jnp.where(lane == idx_ref[:, 2 + which:3 + which], 1.0, 0.0)


def _scatter_seg_body(base_ref, tab_ref, idx_ref, x_ref, g_ref, xs_in_ref, xs_ref, buf_ref, sem):
    del xs_in_ref
    xn = _rms(x_ref[...], g_ref[...]).astype(BF16)
    width = buf_ref.shape[0]
    place = (_slot_onehot(idx_ref, 0, width) + _slot_onehot(idx_ref, 1, width)).astype(BF16)
    buf_ref[...] = lax.dot_general(place, xn, (((0,), (0,)), ((), ())), preferred_element_type=F32)
    _segment_copies(tab_ref, base_ref, buf_ref, xs_ref, sem, to_hbm=True)


def _moe_scatter_seg(x, gain, idx, tab, base, rows):
    m, k = x.shape
    tm = SEG_TOK
    grid_spec = pltpu.PrefetchScalarGridSpec(
        num_scalar_prefetch=1,
        grid=(m // tm,),
        in_specs=[
            pl.BlockSpec((None, 8, LANES), lambda i, b: (i, 0, 0), memory_space=pltpu.SMEM),
            pl.BlockSpec((tm, LANES), lambda i, b: (i, 0)),
            pl.BlockSpec((tm, k), lambda i, b: (i, 0)),
            pl.BlockSpec((1, k), lambda i, b: (0, 0)),
            pl.BlockSpec(memory_space=pl.ANY),
        ],
        out_specs=pl.BlockSpec(memory_space=pl.ANY),
        scratch_shapes=[pltpu.VMEM((SEG_ROWS, k), F32), pltpu.SemaphoreType.DMA(())],
    )
    return pl.pallas_call(
        _scatter_seg_body,
        grid_spec=grid_spec,
        out_shape=jax.ShapeDtypeStruct((rows, k), F32),
        input_output_aliases={5: 0},
        compiler_params=_params(("arbitrary",)),
        name="moe_scatter_seg",
    )(base, tab, idx, x, gain.reshape(1, k).astype(F32), jnp.zeros((rows, k), F32))


def _combine_seg_body(base_ref, tab_ref, idx_ref, x_ref, gate_ref, ys_ref, o_ref, buf_ref, sem):
    @pl.when(pl.program_id(0) == 0)
    def _():
        buf_ref[...] = jnp.zeros_like(buf_ref)

    _segment_copies(tab_ref, base_ref, buf_ref, ys_ref, sem, to_hbm=False)
    width = buf_ref.shape[0]
    y = buf_ref[...].astype(BF16)
    g = gate_ref[...]
    pick = lambda which: jnp.dot(_slot_onehot(idx_ref, which, width).astype(BF16), y, preferred_element_type=F32)
    o_ref[...] = x_ref[...] + g[:, 0:1] * pick(0) + g[:, 1:2] * pick(1)


def _moe_combine_seg(x, gates, idx, tab, base, ys):
    m, k = x.shape
    tm = SEG_TOK
    grid_spec = pltpu.PrefetchScalarGridSpec(
        num_scalar_prefetch=1,
        grid=(m // tm,),
        in_specs=[
            pl.BlockSpec((None, 8, LANES), lambda i, b: (i, 0, 0), memory_space=pltpu.SMEM),
            pl.BlockSpec((tm, LANES), lambda i, b: (i, 0)),
            pl.BlockSpec((tm, k), lambda i, b: (i, 0)),
            pl.BlockSpec((tm, LANES), lambda i, b: (i, 0)),
            pl.BlockSpec(memory_space=pl.ANY),
        ],
        out_specs=pl.BlockSpec((tm, k), lambda i, b: (i, 0)),
        scratch_shapes=[pltpu.VMEM((SEG_ROWS, k), F32), pltpu.SemaphoreType.DMA(())],
    )
    return pl.pallas_call(
        _combine_seg_body,
        grid_spec=grid_spec,
        out_shape=jax.ShapeDtypeStruct((m, k), F32),
        compiler_params=_params(("arbitrary",)),
        name="moe_combine_seg",
    )(base, tab, idx, x, gates, ys)


def _moe_seg(x, gain, w_router, b_router, w_gu, w_down):
    m, k = x.shape
    idx, gates, tab, totals = _router_seg(x, gain, w_router, b_router)
    totals = totals[0, :N_EXPERTS].astype(I32)
    region = ((totals + EXPERT_TILE - 1) // EXPERT_TILE) * EXPERT_TILE
    ends = jnp.cumsum(region)
    base = (ends - region).astype(I32)
    max_rows = 2 * m + N_EXPERTS * (7 * (m // SEG_TOK) + EXPERT_TILE - 1)
    rows = (max_rows // EXPERT_TILE) * EXPERT_TILE
    nt = rows // EXPERT_TILE
    n_used = (ends[-1] // EXPERT_TILE).astype(I32)
    tile_start = jnp.minimum(jnp.arange(nt, dtype=I32), n_used - 1) * EXPERT_TILE
    tile_expert = jnp.sum(tile_start[:, None] >= ends[None, :], axis=1).astype(I32)
    xs = _moe_scatter_seg(x, gain, idx, tab, base, rows)
    ys = _ffn_grouped(xs, tile_expert, n_used.reshape(1), w_gu, w_down, EXPERT_TILE)
    return _moe_combine_seg(x, gates, idx, tab, base, ys)


def _moe(x, gain, w_router, b_router, w_gu, w_down):
    m, k = x.shape
    if m % SEG_TOK == 0 and m >= 8 * SEG_TOK and not _is_precise(w_gu):
        return _moe_seg(x, gain, w_router, b_router, w_gu, w_down)
    tm_tok = min(256, m)
    tm_grp = 512 if m >= 4096 else 128
    idx, gates, counts = _router(x, gain, w_router, b_router)
    counts = counts[0, :N_EXPERTS].astype(I32)
    padded = ((counts + tm_grp - 1) // tm_grp) * tm_grp
    ends = jnp.cumsum(padded)
    starts = ends - padded
    dest = starts[idx[:, 0:2]] + idx[:, 2:4]
    dest_tiles = dest.reshape(m // tm_tok, 1, 2 * tm_tok)
    rows = ((2 * m + N_EXPERTS * (tm_grp - 1)) // tm_grp) * tm_grp
    nt = rows // tm_grp
    n_used = (ends[-1] // tm_grp).astype(I32)
    tile_start = jnp.minimum(jnp.arange(nt, dtype=I32), n_used - 1) * tm_grp
    tile_expert = jnp.sum(tile_start[:, None] >= ends[None, :], axis=1).astype(I32)
    xs = _moe_scatter(x, gain, dest_tiles, rows, tm_tok)
    ys = _ffn_grouped(xs, tile_expert, n_used.reshape(1), w_gu, w_down, tm_grp)
    return _moe_combine(x, gates, dest_tiles, ys, tm_tok)


def _head_cols(x, h, width=GDN_DK):
    return x[:, h * width:(h + 1) * width]


def _gdn_prep_body(x_ref, halo_ref, ab_ref, cw_ref, alog_ref, dtb_ref,
                   q_ref, qg_ref, k_ref, kb_ref, kbg_ref, kd_ref, vb_ref,
                   gc_ref, gct_ref, egl_ref, ext_ref):
    i = pl.program_id(1)
    tt = x_ref.shape[0]
    ext_ref[0:8, :] = jnp.where(i == 0, 0.0, halo_ref[...])
    ext_ref[8:, :] = x_ref[...]
    ext = ext_ref[...]
    acc = pltpu.roll(ext, CONV_TAPS - 1, axis=0)[8:] * cw_ref[0:1, :]
    for j in range(1, CONV_TAPS - 1):
        acc = acc + pltpu.roll(ext, CONV_TAPS - 1 - j, axis=0)[8:] * cw_ref[j:j + 1, :]
    acc = acc + ext[8:] * cw_ref[CONV_TAPS - 1:CONV_TAPS, :]
    y = _silu(acc)

    ab = ab_ref[...]
    g = -jnp.exp(alog_ref[...]) * jax.nn.softplus(ab + dtb_ref[...])
    beta = jax.nn.sigmoid(pltpu.roll(ab, LANES - GDN_HEADS, axis=1))
    r = lax.broadcasted_iota(I32, (tt, tt), 0)
    c = lax.broadcasted_iota(I32, (tt, tt), 1)
    same = (r // CHUNK) == (c // CHUNK)
    tril = jnp.where(same & (c <= r), 1.0, 0.0)
    gc = jnp.dot(tril, g, precision=HI, preferred_element_type=F32)
    gl = jnp.dot(jnp.where(same, 1.0, 0.0), g, precision=HI, preferred_element_type=F32)
    e_gc = jnp.exp(gc)
    e_gd = jnp.exp(gl - gc)
    gc_ref[...] = gc
    gct = gc.T
    sub = lax.broadcasted_iota(I32, (8, LANES), 0)
    lan = lax.broadcasted_iota(I32, (8, LANES), 1)
    for cc in range(tt // CHUNK):
        gct_ref[cc] = gct[0:8, cc * CHUNK:(cc + 1) * CHUNK]
        row = jnp.broadcast_to(gl[cc * CHUNK:cc * CHUNK + 1, :], (8, LANES))
        val = jnp.sum(jnp.where(sub == lan, row, 0.0), axis=-1, keepdims=True)
        egl_ref[cc] = jnp.broadcast_to(jnp.exp(val), (8, LANES))

    nq = GDN_HEADS * GDN_DK
    for h in range(GDN_HEADS):
        qh = _head_cols(y, h)
        kh = _head_cols(y, GDN_HEADS + h)
        vh = _head_cols(y, 2 * GDN_HEADS + h)
        qn = qh * lax.rsqrt(jnp.sum(qh * qh, axis=-1, keepdims=True) + EPS) * (GDN_DK ** -0.5)
        kn = kh * lax.rsqrt(jnp.sum(kh * kh, axis=-1, keepdims=True) + EPS)
        b_h = beta[:, h:h + 1]
        eg_h = e_gc[:, h:h + 1]
        sl = slice(h * GDN_DK, (h + 1) * GDN_DK)
        kb = kn * b_h
        q_ref[:, sl] = qn.astype(BF16)
        qg_ref[:, sl] = (qn * eg_h).astype(BF16)
        k_ref[:, sl] = kn.astype(BF16)
        kb_ref[:, sl] = kb.astype(BF16)
        kbg_ref[:, sl] = (kb * eg_h).astype(BF16)
        kd_ref[:, sl] = (kn * e_gd[:, h:h + 1]).astype(BF16)
        vb_ref[:, sl] = (vh * b_h).astype(BF16)
    del nq


def _gdn_prep(proj, ab, conv_w, a_log, dt_bias, tt=256):
    b, t, _ = proj.shape
    nc = t // CHUNK
    cpt = tt // CHUNK
    row = lambda v: jnp.zeros((1, LANES), F32).at[0, :GDN_HEADS].set(v)
    tok = pl.BlockSpec((None, tt, D_MODEL), lambda bi, i: (bi, i, 0))
    out_shapes = [jax.ShapeDtypeStruct((b, t, D_MODEL), BF16)] * 7 + [
        jax.ShapeDtypeStruct((b, t, LANES), F32),
        jax.ShapeDtypeStruct((b, nc, 8, CHUNK), F32),
        jax.ShapeDtypeStruct((b, nc, 8, LANES), F32),
    ]
    out_specs = [tok] * 7 + [
        pl.BlockSpec((None, tt, LANES), lambda bi, i: (bi, i, 0)),
        pl.BlockSpec((None, cpt, 8, CHUNK), lambda bi, i: (bi, i, 0, 0)),
        pl.BlockSpec((None, cpt, 8, LANES), lambda bi, i: (bi, i, 0, 0)),
    ]
    return pl.pallas_call(
        _gdn_prep_body,
        grid=(b, t // tt),
        in_specs=[
            pl.BlockSpec((None, tt, QKV_W), lambda bi, i: (bi, i, 0)),
            pl.BlockSpec((None, 8, QKV_W), lambda bi, i: (bi, jnp.maximum(i * (tt // 8) - 1, 0), 0)),
            pl.BlockSpec((None, tt, LANES), lambda bi, i: (bi, i, 0)),
            pl.BlockSpec((CONV_TAPS, QKV_W), lambda bi, i: (0, 0)),
            pl.BlockSpec((1, LANES), lambda bi, i: (0, 0)),
            pl.BlockSpec((1, LANES), lambda bi, i: (0, 0)),
        ],
        out_specs=out_specs,
        out_shape=out_shapes,
        scratch_shapes=[pltpu.VMEM((tt + 8, QKV_W), F32)],
        compiler_params=_params(("parallel", "arbitrary")),
        name="gdn_prep",
    )(proj, proj, ab, conv_w.astype(F32), row(a_log), row(dt_bias))


def _decay(gc_col, gc_row, keep):
    return jnp.where(keep, jnp.exp(jnp.where(keep, gc_col - gc_row, 0.0)), 0.0)


def _gdn_l_body(kb_ref, k_ref, gc_ref, gct_ref, l_ref):
    r = lax.broadcasted_iota(I32, (CHUNK, CHUNK), 0)
    c = lax.broadcasted_iota(I32, (CHUNK, CHUNK), 1)
    strict = c < r
    for cc in range(l_ref.shape[0]):
        rows = slice(cc * CHUNK, (cc + 1) * CHUNK)
        for h in range(GDN_HEADS):
            cols = slice(h * GDN_DK, (h + 1) * GDN_DK)
            kk = lax.dot_general(kb_ref[rows, cols], k_ref[rows, cols], (((1,), (1,)), ((), ())),
                                 preferred_element_type=F32)
            dec = _decay(gc_ref[rows, h:h + 1], gct_ref[cc, h:h + 1, :], strict)
            l_ref[cc, :, h * CHUNK:(h + 1) * CHUNK] = kk * dec


def _gdn_l(kb, k, gc, gct, cpt=4):
    b, t, _ = k.shape
    nc = t // CHUNK
    tt = cpt * CHUNK
    tok = pl.BlockSpec((None, tt, D_MODEL), lambda bi, i: (bi, i, 0))
    return pl.pallas_call(
        _gdn_l_body,
        grid=(b, nc // cpt),
        in_specs=[tok, tok,
                  pl.BlockSpec((None, tt, LANES), lambda bi, i: (bi, i, 0)),
                  pl.BlockSpec((None, cpt, 8, CHUNK), lambda bi, i: (bi, i, 0, 0))],
        out_specs=pl.BlockSpec((None, cpt, CHUNK, GDN_HEADS * CHUNK), lambda bi, i: (bi, i, 0, 0)),
        out_shape=jax.ShapeDtypeStruct((b, nc, CHUNK, GDN_HEADS * CHUNK), F32),
        compiler_params=_params(("parallel", "parallel")),
        name="gdn_l",
    )(kb, k, gc, gct)


def _tri_inv_body(l_ref, t_ref):
    sub = lax.broadcasted_iota(I32, (8, LANES), 0)
    for i in range(CHUNK):
        nq = i // 8 + 1
        acc = [jnp.where(sub == i % 8, 1.0, 0.0) if q == i // 8 else jnp.zeros((8, LANES), F32)
               for q in range(nq)]
        for j in range(i):
            lij = l_ref[i, j:j + 1, :]
            for q in range(j // 8 + 1):
                acc[q] = acc[q] - lij * t_ref[j, 8 * q:8 * q + 8, :]
        for q in range(CHUNK // 8):
            t_ref[i, 8 * q:8 * q + 8, :] = acc[q] if q < nq else jnp.zeros((8, LANES), F32)


def _tri_inv(lt):
    _, nh, _, p = lt.shape
    spec = pl.BlockSpec((CHUNK, None, CHUNK, LANES), lambda h, i: (0, h, 0, i))
    return pl.pallas_call(
        _tri_inv_body,
        grid=(nh, p // LANES),
        in_specs=[spec],
        out_specs=spec,
        out_shape=jax.ShapeDtypeStruct(lt.shape, F32),
        compiler_params=_params(("parallel", "parallel")),
        name="gdn_tri_inv",
    )(lt)


def _gdn_wy_body(t_ref, vb_ref, kbg_ref, q_ref, k_ref, qg_ref, gc_ref, gct_ref, u_ref, wq_ref, attn_ref):
    r = lax.broadcasted_iota(I32, (CHUNK, CHUNK), 0)
    c = lax.broadcasted_iota(I32, (CHUNK, CHUNK), 1)
    tril = c <= r
    for cc in range(t_ref.shape[0]):
        rows = slice(cc * CHUNK, (cc + 1) * CHUNK)
        for h in range(GDN_HEADS):
            cols = slice(h * GDN_DK, (h + 1) * GDN_DK)
            hc = slice(h * CHUNK, (h + 1) * CHUNK)
            rhs = jnp.concatenate([vb_ref[rows, cols], kbg_ref[rows, cols]], axis=1)
            sol = jnp.dot(t_ref[cc, :, hc].astype(BF16), rhs, preferred_element_type=F32)
            u_ref[rows, cols] = sol[:, :GDN_DK]
            wq_ref[cc, 0:CHUNK, cols] = sol[:, GDN_DK:].astype(BF16)
            wq_ref[cc, CHUNK:2 * CHUNK, cols] = qg_ref[rows, cols]
            qk = lax.dot_general(q_ref[rows, cols], k_ref[rows, cols], (((1,), (1,)), ((), ())),
                                 preferred_element_type=F32)
            dec = _decay(gc_ref[rows, h:h + 1], gct_ref[cc, h:h + 1, :], tril)
            attn_ref[cc, :, hc] = (qk * dec).astype(BF16)


def _gdn_wy(tinv, vb, kbg, q, k, qg, gc, gct, cpt=4):
    b, t, _ = k.shape
    nc = t // CHUNK
    tt = cpt * CHUNK
    tok = pl.BlockSpec((None, tt, D_MODEL), lambda bi, i: (bi, i, 0))
    sq = pl.BlockSpec((None, cpt, CHUNK, GDN_HEADS * CHUNK), lambda bi, i: (bi, i, 0, 0))
    return pl.pallas_call(
        _gdn_wy_body,
        grid=(b, nc // cpt),
        in_specs=[sq, tok, tok, tok, tok, tok,
                  pl.BlockSpec((None, tt, LANES), lambda bi, i: (bi, i, 0)),
                  pl.BlockSpec((None, cpt, 8, CHUNK), lambda bi, i: (bi, i, 0, 0))],
        out_specs=[tok,
                   pl.BlockSpec((None, cpt, 2 * CHUNK, D_MODEL), lambda bi, i: (bi, i, 0, 0)),
                   sq],
        out_shape=[jax.ShapeDtypeStruct((b, t, D_MODEL), F32),
                   jax.ShapeDtypeStruct((b, nc, 2 * CHUNK, D_MODEL), BF16),
                   jax.ShapeDtypeStruct((b, nc, CHUNK, GDN_HEADS * CHUNK), BF16)],
        compiler_params=_params(("parallel", "parallel")),
        name="gdn_wy",
    )(tinv, vb, kbg, q, k, qg, gc, gct)


def _gdn_chunk_body(wq_ref, u_ref, attn_ref, kd_ref, egl_ref, o_ref, s_out_ref, sa_ref, sb_ref):
    c = pl.program_id(0)

    @pl.when(c == 0)
    def _():
        sa_ref[...] = jnp.zeros_like(sa_ref)

    def step(src_ref, dst_ref):
        heads = [(b, h) for b in range(src_ref.shape[0]) for h in range(GDN_HEADS)]
        cols = lambda h: slice(h * GDN_DK, (h + 1) * GDN_DK)
        ws = [jnp.dot(wq_ref[b, :, cols(h)], src_ref[b, h].astype(BF16), preferred_element_type=F32)
              for b, h in heads]
        v_nb = [(u_ref[b, :, cols(h)] - w_[:CHUNK]).astype(BF16) for (b, h), w_ in zip(heads, ws)]
        for (b, h), w_, v_ in zip(heads, ws, v_nb):
            o_ref[b, :, cols(h)] = w_[CHUNK:] + jnp.dot(attn_ref[b, :, h * CHUNK:(h + 1) * CHUNK], v_,
                                                        preferred_element_type=F32)
        for (b, h), v_ in zip(heads, v_nb):
            dst_ref[b, h] = src_ref[b, h] * egl_ref[b, h:h + 1, :] + lax.dot_general(
                kd_ref[b, :, cols(h)], v_, (((0,), (0,)), ((), ())), preferred_element_type=F32)

        @pl.when(c == pl.num_programs(0) - 1)
        def _():
            s_out_ref[...] = dst_ref[...]

    @pl.when(c % 2 == 0)
    def _():
        step(sa_ref, sb_ref)

    @pl.when(c % 2 == 1)
    def _():
        step(sb_ref, sa_ref)


def _gdn_chunks(wq, u, attn, kd, egl):
    b, t, _ = u.shape
    nc = t // CHUNK
    tok = pl.BlockSpec((b, CHUNK, D_MODEL), lambda c: (0, c, 0))
    return pl.pallas_call(
        _gdn_chunk_body,
        grid=(nc,),
        in_specs=[
            pl.BlockSpec((b, None, 2 * CHUNK, D_MODEL), lambda c: (0, c, 0, 0)),
            tok,
            pl.BlockSpec((b, None, CHUNK, GDN_HEADS * CHUNK), lambda c: (0, c, 0, 0)),
            tok,
            pl.BlockSpec((b, None, 8, LANES), lambda c: (0, c, 0, 0)),
        ],
        out_specs=[
            tok,
            pl.BlockSpec((b, GDN_HEADS, GDN_DK, GDN_DK), lambda c: (0, 0, 0, 0)),
        ],
        out_shape=[
            jax.ShapeDtypeStruct((b, t, D_MODEL), F32),
            jax.ShapeDtypeStruct((b, GDN_HEADS, GDN_DK, GDN_DK), F32),
        ],
        scratch_shapes=[pltpu.VMEM((b, GDN_HEADS, GDN_DK, GDN_DK), F32)] * 2,
        compiler_params=_params(("arbitrary",)),
        name="gdn_chunks",
    )(wq, u, attn, kd, egl)


def _gdn_out_body(o_ref, z_ref, on_ref, w_ref, r_ref, out_ref, *, precise):
    parts = []
    for h in range(GDN_HEADS):
        oh = _head_cols(o_ref[...], h)
        zh = _head_cols(z_ref[...], h)
        parts.append(_rms(oh, on_ref[...]) * _silu(zh))
    if precise:
        gh, gl = _split(jnp.concatenate(parts, axis=1))
        out_ref[...] = r_ref[...] + _dot3(gh, gl, w_ref[...])
    else:
        gated = jnp.concatenate([a.astype(BF16) for a in parts], axis=1)
        out_ref[...] = r_ref[...] + jnp.dot(gated, w_ref[...], preferred_element_type=F32)


def _gdn_out(o, proj, o_norm, w_out, res, tm=512):
    m, k = o.shape
    tm = min(tm, m)
    return pl.pallas_call(
        functools.partial(_gdn_out_body, precise=_is_precise(w_out)),
        grid=(m // tm,),
        in_specs=[
            pl.BlockSpec((tm, k), lambda i: (i, 0)),
            pl.BlockSpec((tm, k), lambda i: (i, QKV_W // D_MODEL)),
            pl.BlockSpec((1, GDN_DK), lambda i: (0, 0)),
            _wspec(w_out, (k, k), lambda i: (0, 0)),
            pl.BlockSpec((tm, k), lambda i: (i, 0)),
        ],
        out_specs=pl.BlockSpec((tm, k), lambda i: (i, 0)),
        out_shape=jax.ShapeDtypeStruct((m, k), F32),
        compiler_params=_params(("parallel",)),
        name="gdn_out",
    )(o, proj, o_norm.reshape(1, GDN_DK).astype(F32), _warr(w_out), res)


def _gdn_prompt_layer(h, l, p, w):
    b, t, _ = h.shape
    hf = h.reshape(b * t, D_MODEL)
    proj, ab = _mm(hf, w['a_w_in'][l], gain=p['a_norm'][l], w_aux=w['a_w_ab'][l], tn=2 * D_MODEL)
    proj3 = proj.reshape(b, t, PROJ_W)
    q, qg, k, kb, kbg, kd, vb, gc, gct, egl = _gdn_prep(
        proj3, ab.reshape(b, t, LANES), p['a_conv_w'][l], p['a_A_log'][l], p['a_dt_bias'][l])
    lmat = _gdn_l(kb, k, gc, gct)
    nprob = b * (t // CHUNK)
    ppad = -nprob % LANES
    lt = jnp.transpose(lmat.reshape(nprob, CHUNK, GDN_HEADS * CHUNK), (1, 2, 0))
    lt = jnp.pad(lt, ((0, 0), (0, 0), (0, ppad))).reshape(CHUNK, GDN_HEADS, CHUNK, nprob + ppad)
    tt = _tri_inv(lt).reshape(CHUNK, GDN_HEADS * CHUNK, nprob + ppad)[:, :, :nprob]
    tinv = jnp.transpose(tt, (2, 0, 1)).reshape(b, t // CHUNK, CHUNK, GDN_HEADS * CHUNK)
    u, wq, attn = _gdn_wy(tinv, vb, kbg, q, k, qg, gc, gct)
    o, s_new = _gdn_chunks(wq, u, attn, kd, egl)
    h_new = _gdn_out(o.reshape(b * t, D_MODEL), proj, p['a_o_norm'][l], w['a_w_out'][l], hf)
    conv_new = proj3[:, t - (CONV_TAPS - 1):, :QKV_W]
    return h_new.reshape(b, t, D_MODEL), s_new, conv_new


def _gdn_prep1_body(x_ref, st_ref, ab_ref, cw_ref, alog_ref, dtb_ref,
                    q_ref, k_ref, v_ref, eg_ref, beta_ref):
    acc = st_ref[0] * cw_ref[0:1, :]
    for j in range(1, CONV_TAPS - 1):
        acc = acc + st_ref[j] * cw_ref[j:j + 1, :]
    acc = acc + x_ref[...] * cw_ref[CONV_TAPS - 1:CONV_TAPS, :]
    y = _silu(acc)
    ab = ab_ref[...]
    eg_ref[...] = jnp.exp(-jnp.exp(alog_ref[...]) * jax.nn.softplus(ab + dtb_ref[...]))
    beta_ref[...] = jax.nn.sigmoid(pltpu.roll(ab, LANES - GDN_HEADS, axis=1))
    for h in range(GDN_HEADS):
        qh = _head_cols(y, h)
        kh = _head_cols(y, GDN_HEADS + h)
        sl = slice(h * GDN_DK, (h + 1) * GDN_DK)
        q_ref[:, sl] = qh * lax.rsqrt(jnp.sum(qh * qh, axis=-1, keepdims=True) + EPS) * (GDN_DK ** -0.5)
        k_ref[:, sl] = kh * lax.rsqrt(jnp.sum(kh * kh, axis=-1, keepdims=True) + EPS)
    v_ref[...] = y[:, 2 * GDN_HEADS * GDN_DK:]


def _gdn_prep1(proj, ab, conv_state, conv_w, a_log, dt_bias):
    n = proj.shape[0]
    row = lambda v: jnp.zeros((1, LANES), F32).at[0, :GDN_HEADS].set(v)
    tok = pl.BlockSpec((n, D_MODEL), lambda i: (0, 0))
    small = pl.BlockSpec((n, LANES), lambda i: (0, 0))
    return pl.pallas_call(
        _gdn_prep1_body,
        grid=(1,),
        in_specs=[
            pl.BlockSpec((n, QKV_W), lambda i: (0, 0)),
            pl.BlockSpec((CONV_TAPS - 1, n, QKV_W), lambda i: (0, 0, 0)),
            pl.BlockSpec((n, LANES), lambda i: (0, 0)),
            pl.BlockSpec((CONV_TAPS, QKV_W), lambda i: (0, 0)),
            pl.BlockSpec((1, LANES), lambda i: (0, 0)),
            pl.BlockSpec((1, LANES), lambda i: (0, 0)),
        ],
        out_specs=[tok, tok, tok, small, small],
        out_shape=[jax.ShapeDtypeStruct((n, D_MODEL), F32)] * 3 + [jax.ShapeDtypeStruct((n, LANES), F32)] * 2,
        compiler_params=_params(("arbitrary",)),
        name="gdn_prep1",
    )(proj, conv_state, ab, conv_w.astype(F32), row(a_log), row(dt_bias))


def _gdn_step_body(qt_ref, kt_ref, v_ref, eg_ref, beta_ref, s_ref, o_ref, s_out_ref):
    for b in range(s_ref.shape[0]):
        for h in range(GDN_HEADS):
            k_col = kt_ref[b, :, h:h + 1]
            q_col = qt_ref[b, :, h:h + 1]
            s = s_ref[b, h] * eg_ref[b, h:h + 1, :]
            kv_mem = jnp.sum(k_col * s, axis=0, keepdims=True)
            delta = (v_ref[b, h:h + 1, :] - kv_mem) * beta_ref[b, h:h + 1, :]
            s = s + k_col * delta
            s_out_ref[b, h] = s
            o_ref[b, h:h + 1, :] = jnp.sum(q_col * s, axis=0, keepdims=True)


def _gdn_step(qt, kt, v, eg, beta, state, sb=4):
    n = v.shape[0]
    col = pl.BlockSpec((sb, GDN_DK, GDN_HEADS), lambda i: (i, 0, 0))
    row = pl.BlockSpec((sb, GDN_HEADS, GDN_DK), lambda i: (i, 0, 0))
    st = pl.BlockSpec((sb, GDN_HEADS, GDN_DK, GDN_DK), lambda i: (i, 0, 0, 0))
    return pl.pallas_call(
        _gdn_step_body,
        grid=(n // sb,),
        in_specs=[col, col, row, row, row, st],
        out_specs=[row, st],
        out_shape=[jax.ShapeDtypeStruct((n, GDN_HEADS, GDN_DK), F32),
                   jax.ShapeDtypeStruct(state.shape, F32)],
        compiler_params=_params(("parallel",)),
        name="gdn_step",
    )(qt, kt, v, eg, beta, state)


def _gdn_sample_layer(h, l, p, w, conv_state, delta_state):
    n = h.shape[0]
    proj, ab = _mm(h, w['a_w_in'][l], gain=p['a_norm'][l], w_aux=w['a_w_ab'][l])
    conv_t = jnp.transpose(conv_state, (1, 0, 2))
    q, k, v, eg, beta = _gdn_prep1(proj, ab, conv_t, p['a_conv_w'][l], p['a_A_log'][l], p['a_dt_bias'][l])
    heads = lambda a: a.reshape(n, GDN_HEADS, GDN_DK)
    cols = lambda a: jnp.transpose(heads(a), (0, 2, 1))
    lanes = lambda a: jnp.broadcast_to(a[:, :GDN_HEADS, None], (n, GDN_HEADS, GDN_DK))
    o, s_new = _gdn_step(cols(q), cols(k), heads(v), lanes(eg), lanes(beta), delta_state)
    h_new = _gdn_out(o.reshape(n, D_MODEL), proj, p['a_o_norm'][l], w['a_w_out'][l], h)
    conv_new = jnp.concatenate([conv_state[:, 1:], proj[:, None, :QKV_W]], axis=1)
    return h_new, s_new, conv_new


def _rope_tables(pos):
    half = ROT // 2
    inv = ROPE_THETA ** (-jnp.arange(0, ROT, 2, dtype=F32) / ROT)
    ang = pos.astype(F32)[:, None] * inv[None, :]
    cos, sin = jnp.cos(ang), jnp.sin(ang)
    ones = jnp.ones((pos.shape[0], HEAD_DIM - ROT), F32)
    zeros = jnp.zeros_like(ones)
    z8 = jnp.zeros_like(sin)
    c = jnp.concatenate([cos, cos, ones], axis=1)
    sm = jnp.concatenate([-sin, z8, zeros], axis=1)
    sp = jnp.concatenate([z8, sin, zeros], axis=1)
    del half
    return tuple(jnp.tile(a, (1, LANES // HEAD_DIM)) for a in (c, sm, sp))


def _dot_split(a, sel):
    hi = a.astype(BF16)
    lo = (a - hi.astype(F32)).astype(BF16)
    return (jnp.dot(hi, sel, preferred_element_type=F32) + jnp.dot(lo, sel, preferred_element_type=F32))


def _headnorm_rope_body(x_ref, g_ref, seg_ref, segt_ref, c_ref, sm_ref, sp_ref, o_ref):
    x = x_ref[...]
    w = x.shape[1]
    ms = _dot_split(x * x, seg_ref[...]) * (1.0 / HEAD_DIM)
    scale = _dot_split(lax.rsqrt(ms + EPS), segt_ref[...])
    y = x * scale * g_ref[...]
    reps = w // LANES
    tile = lambda a: jnp.concatenate([a] * reps, axis=1) if reps > 1 else a
    half = ROT // 2
    out = (y * tile(c_ref[...]) + pltpu.roll(y, w - half, axis=1) * tile(sm_ref[...])
           + pltpu.roll(y, half, axis=1) * tile(sp_ref[...]))
    o_ref[...] = out.astype(o_ref.dtype)


def _headnorm_rope(x, col_block, width, gain, tables, rows_per_seq, out_dtype, tm=512):
    m = x.shape[0]
    tm = min(tm, m, rows_per_seq)
    nheads = width // HEAD_DIM
    seg = (jnp.arange(width)[:, None] // HEAD_DIM == jnp.arange(LANES)[None, :]).astype(BF16)
    per_seq = rows_per_seq // tm
    tab = pl.BlockSpec((tm, LANES), lambda i: (i % per_seq, 0))
    del nheads
    return pl.pallas_call(
        _headnorm_rope_body,
        grid=(m // tm,),
        in_specs=[
            pl.BlockSpec((tm, width), lambda i: (i, col_block)),
            pl.BlockSpec((1, width), lambda i: (0, 0)),
            pl.BlockSpec((width, LANES), lambda i: (0, 0)),
            pl.BlockSpec((LANES, width), lambda i: (0, 0)),
            tab, tab, tab,
        ],
        out_specs=pl.BlockSpec((tm, width), lambda i: (i, 0)),
        out_shape=jax.ShapeDtypeStruct((m, width), out_dtype),
        compiler_params=_params(("parallel",)),
        name="headnorm_rope",
    )(x, jnp.tile(gain.astype(F32), width // HEAD_DIM).reshape(1, width), seg, seg.T, *tables)


def _softmax_sink(s, sink):
    m = jnp.maximum(jnp.max(s, axis=-1, keepdims=True), sink)
    p = jnp.exp(s - m)
    return p, jnp.sum(p, axis=-1, keepdims=True) + jnp.exp(sink - m)


def _swa_prompt_body(sink_ref, q_ref, kp_ref, kc_ref, vp_ref, vc_ref, o_ref):
    i = pl.program_id(1)
    blk = q_ref.shape[0]
    r = lax.broadcasted_iota(I32, (blk, 2 * blk), 0)
    c = lax.broadcasted_iota(I32, (blk, 2 * blk), 1)
    valid = (c >= r) & (c <= r + WINDOW) & ((c >= blk) | (i > 0))
    for kvh in range(KV_HEADS):
        cols = slice(kvh * HEAD_DIM, (kvh + 1) * HEAD_DIM)
        kk = jnp.concatenate([kp_ref[:, cols], kc_ref[:, cols]], axis=0).astype(BF16)
        vv = jnp.concatenate([vp_ref[:, cols], vc_ref[:, cols]], axis=0).astype(BF16)
        for g in range(ATT_GROUP):
            h = kvh * ATT_GROUP + g
            hc = slice(h * HEAD_DIM, (h + 1) * HEAD_DIM)
            s = lax.dot_general(q_ref[:, hc], kk, (((1,), (1,)), ((), ())),
                                preferred_element_type=F32) * (HEAD_DIM ** -0.5)
            s = jnp.where(valid, s, -jnp.inf)
            p, den = _softmax_sink(s, sink_ref[h])
            o_ref[:, hc] = jnp.dot((p / den).astype(BF16), vv, preferred_element_type=F32).astype(o_ref.dtype)


def _swa_prompt(q, k, kv, sinks):
    b, t, _ = q.shape
    blk = WINDOW
    kvw = KV_HEADS * HEAD_DIM
    prev = lambda bi, i, s: (bi, jnp.maximum(i - 1, 0), 0)
    cur = lambda bi, i, s: (bi, i, 0)
    prev_v = lambda bi, i, s: (bi, jnp.maximum(i - 1, 0), 1)
    cur_v = lambda bi, i, s: (bi, i, 1)
    grid_spec = pltpu.PrefetchScalarGridSpec(
        num_scalar_prefetch=1,
        grid=(b, t // blk),
        in_specs=[
            pl.BlockSpec((None, blk, D_MODEL), cur),
            pl.BlockSpec((None, blk, kvw), prev),
            pl.BlockSpec((None, blk, kvw), cur),
            pl.BlockSpec((None, blk, kvw), prev_v),
            pl.BlockSpec((None, blk, kvw), cur_v),
        ],
        out_specs=pl.BlockSpec((None, blk, D_MODEL), cur),
    )
    return pl.pallas_call(
        _swa_prompt_body,
        grid_spec=grid_spec,
        out_shape=jax.ShapeDtypeStruct((b, t, D_MODEL), BF16),
        compiler_params=_params(("parallel", "parallel")),
        name="swa_prompt",
    )(sinks.astype(F32), q, k, k, kv, kv)


def _swa_sample_body(sink_ref, q_ref, kc_ref, kn_ref, vc_ref, vn_ref, o_ref):
    wc = kc_ref.shape[1]
    kvw = kc_ref.shape[2]
    nkeys = 2 * wc
    fill = jnp.zeros((nkeys - wc - 8, kvw), F32)
    key_idx = lax.broadcasted_iota(I32, (ATT_HEADS, nkeys), 1)
    sink = sink_ref[:, 0:1]
    seqs = range(q_ref.shape[0])
    scores = []
    for b in seqs:
        kk = jnp.concatenate([kc_ref[b], kn_ref[b], fill], axis=0)
        scores.append(lax.dot_general(q_ref[b], kk, (((1,), (1,)), ((), ())), precision=HI,
                                      preferred_element_type=F32))
    probs = []
    for s in scores:
        s = jnp.where(key_idx <= wc, s * (HEAD_DIM ** -0.5), -jnp.inf)
        p, den = _softmax_sink(s, sink)
        probs.append(p / den)
    outs = []
    for b, p in zip(seqs, probs):
        vv = jnp.concatenate([vc_ref[b], vn_ref[b], fill], axis=0)
        outs.append(jnp.dot(p, vv, precision=HI, preferred_element_type=F32))
    for b, o in zip(seqs, outs):
        for kvh in range(KV_HEADS):
            heads = slice(kvh * ATT_GROUP, (kvh + 1) * ATT_GROUP)
            o_ref[b, heads, :] = o[heads, kvh * HEAD_DIM:(kvh + 1) * HEAD_DIM]


def _swa_sample(q, k_cache, k_new, v_cache, v_new, sinks, sb=8):
    n = q.shape[0]
    wc = k_cache.shape[1]
    kvw = KV_HEADS * HEAD_DIM
    own = (jnp.arange(ATT_HEADS)[:, None] // ATT_GROUP == jnp.arange(KV_HEADS)[None, :])
    q_rows = (q[:, :, None, :] * own[None, :, :, None].astype(q.dtype)).reshape(n, ATT_HEADS, kvw)
    pad8 = lambda a: jnp.pad(a[:, None, :], ((0, 0), (0, 7), (0, 0)))
    blk = lambda shape: pl.BlockSpec(shape, lambda i: (i, 0, 0))
    return pl.pallas_call(
        _swa_sample_body,
        grid=(n // sb,),
        in_specs=[pl.BlockSpec((ATT_HEADS, LANES), lambda i: (0, 0)),
                  blk((sb, ATT_HEADS, kvw)), blk((sb, wc, kvw)), blk((sb, 8, kvw)),
                  blk((sb, wc, kvw)), blk((sb, 8, kvw))],
        out_specs=blk((sb, ATT_HEADS, HEAD_DIM)),
        out_shape=jax.ShapeDtypeStruct((n, ATT_HEADS, HEAD_DIM), F32),
        compiler_params=_params(("parallel",)),
        name="swa_sample",
    )(jnp.broadcast_to(sinks.astype(F32)[:, None], (ATT_HEADS, LANES)), q_rows, k_cache, pad8(k_new),
      v_cache, pad8(v_new))


def _channel_mixer(hf, l, p, w):
    if l % 2 == 0:
        return _ffn_dense(hf, p['ffn_norm'][l], w['dense_w_gu'][l // 2], w['dense_w_down'][l // 2])
    return _moe(hf, p['ffn_norm'][l], p['moe_router'][l // 2], p['moe_router_b'][l // 2],
                w['moe_w_gu'][l // 2], w['moe_w_down'][l // 2])


def _run_prompt(x, p, w):
    b, t, _ = x.shape
    n = b * t
    h = x
    convs, deltas = [], []
    for l in range(N_GDN):
        h, s_new, c_new = _gdn_prompt_layer(h, l, p, w)
        convs.append(c_new)
        deltas.append(s_new)
        h = _channel_mixer(h.reshape(n, D_MODEL), l, p, w).reshape(b, t, D_MODEL)
    hf = h.reshape(n, D_MODEL)
    tables = _rope_tables(jnp.arange(t, dtype=I32))
    kvw = KV_HEADS * HEAD_DIM
    kv = _mm(hf, w['w_kv'], gain=p['kv_norm'], tn=2 * kvw)
    k_sh = _headnorm_rope(kv, 0, kvw, p['k_norm'], tables, t, F32)
    k3 = k_sh.reshape(b, t, kvw)
    kv3 = kv.reshape(b, t, 2 * kvw)
    for j in range(2):
        l = N_GDN + j
        qp = _mm(hf, w['b_w_q'][j], gain=p['b_norm'][j])
        q = _headnorm_rope(qp, 0, D_MODEL, p['b_q_norm'][j], tables, t, BF16)
        o = _swa_prompt(q.reshape(b, t, D_MODEL), k3, kv3, p['b_sinks'][j])
        hf = _mm(o.reshape(n, D_MODEL), w['b_w_o'][j], res=hf)
        hf = _channel_mixer(hf, l, p, w)
    k_win = k3[:, t - WINDOW:].reshape(b, WINDOW, KV_HEADS, HEAD_DIM)
    v_win = kv3[:, t - WINDOW:, kvw:].reshape(b, WINDOW, KV_HEADS, HEAD_DIM)
    return hf.reshape(b, t, D_MODEL), jnp.stack(convs), jnp.stack(deltas), k_win, v_win


def _run_sample(x, conv_state, delta_state, k_cache, v_cache, p, w):
    n = x.shape[0]
    h = x.reshape(n, D_MODEL)
    convs, deltas = [], []
    for l in range(N_GDN):
        h, s_new, c_new = _gdn_sample_layer(h, l, p, w, conv_state[l], delta_state[l])
        convs.append(c_new)
        deltas.append(s_new)
        h = _channel_mixer(h, l, p, w)
    tables = _rope_tables(jnp.full((n,), PAST_LEN, I32))
    kvw = KV_HEADS * HEAD_DIM
    wc = k_cache.shape[1]
    kv = _mm(h, w['w_kv'], gain=p['kv_norm'], tn=2 * kvw)
    k_new = _headnorm_rope(kv, 0, kvw, p['k_norm'], tables, n, F32)
    v_new = kv[:, kvw:]
    kc = k_cache.reshape(n, wc, kvw)
    vc = v_cache.reshape(n, wc, kvw)
    for j in range(2):
        l = N_GDN + j
        qp = _mm(h, w['b_w_q'][j], gain=p['b_norm'][j])
        q = _headnorm_rope(qp, 0, D_MODEL, p['b_q_norm'][j], tables, n, F32)
        o = _swa_sample(q.reshape(n, ATT_HEADS, HEAD_DIM), kc, k_new, vc, v_new, p['b_sinks'][j])
        h = _mm(o.reshape(n, D_MODEL), w['b_w_o'][j], res=h)
        h = _channel_mixer(h, l, p, w)
    k_win = jnp.concatenate([kc, k_new[:, None, :]], axis=1)[:, -wc:].reshape(n, wc, KV_HEADS, HEAD_DIM)
    v_win = jnp.concatenate([vc, v_new[:, None, :]], axis=1)[:, -wc:].reshape(n, wc, KV_HEADS, HEAD_DIM)
    return h.reshape(n, 1, D_MODEL), jnp.stack(convs), jnp.stack(deltas), k_win, v_win


def kernel(x_prompt, x_sample, state_conv, state_delta, cache_k_win, cache_v_win, a_norm, a_w_in, a_conv_w,
           a_A_log, a_dt_bias, a_o_norm, a_w_out, kv_norm, w_kv, k_norm, b_norm, b_w_q, b_q_norm, b_sinks,
           b_w_o, ffn_norm, dense_w_gu, dense_w_down, moe_router, moe_router_b, moe_w_gu, moe_w_down):
    p = dict(a_norm=a_norm, a_conv_w=a_conv_w, a_A_log=a_A_log, a_dt_bias=a_dt_bias, a_o_norm=a_o_norm,
             kv_norm=kv_norm, k_norm=k_norm, b_norm=b_norm, b_q_norm=b_q_norm, b_sinks=b_sinks,
             ffn_norm=ffn_norm, moe_router=moe_router, moe_router_b=moe_router_b)
    n_proj = PROJ_W
    gate_cols = jnp.pad(a_w_in[:, :, n_proj:], ((0, 0), (0, 0), (0, LANES - (a_w_in.shape[-1] - n_proj))))
    raw = dict(a_w_in=a_w_in[:, :, :n_proj], a_w_ab=gate_cols, a_w_out=a_w_out, w_kv=w_kv[None],
               b_w_q=b_w_q, b_w_o=b_w_o, dense_w_gu=dense_w_gu, dense_w_down=dense_w_down,
               moe_w_gu=moe_w_gu, moe_w_down=moe_w_down)
    w_fast = {name: [val[l].astype(BF16) for l in range(val.shape[0])] for name, val in raw.items()}
    w_precise = {name: [_Layer(val, l) for l in range(val.shape[0])] for name, val in raw.items()}
    w_precise['a_w_in'] = [_Layer(a_w_in, l, cols=n_proj) for l in range(a_w_in.shape[0])]
    w_fast['w_kv'], w_precise['w_kv'] = w_fast['w_kv'][0], w_kv
    y_p, conv_p, delta_p, kw_p, vw_p = _run_prompt(x_prompt, p, w_fast)
    y_s, conv_s, delta_s, kw_s, vw_s = _run_sample(
        x_sample, state_conv, state_delta, cache_k_win, cache_v_win, p, w_precise)
    return (y_p, y_s, conv_p, conv_s, delta_p, delta_s, kw_p, kw_s, vw_p, vw_s)
```

```python
import functools

import jax
import jax.numpy as jnp
from jax import lax
from jax.experimental import pallas as pl
from jax.experimental.pallas import tpu as pltpu

F32, BF16, I32 = jnp.float32, jnp.bfloat16, jnp.int32
HI = lax.Precision.HIGHEST

D_MODEL = 1024
N_GDN = 2
GDN_HEADS = 8
GDN_DK = 128
CONV_TAPS = 4
CHUNK = 64
QKV_W = 3 * GDN_HEADS * GDN_DK
PROJ_W = QKV_W + D_MODEL
ATT_HEADS = 16
KV_HEADS = 4
HEAD_DIM = 64
ATT_GROUP = ATT_HEADS // KV_HEADS
WINDOW = 128
ROT = HEAD_DIM // 4
ROPE_THETA = 500000.0
PAST_LEN = 8192
N_EXPERTS = 8
EPS = 1e-6

LANES = 128
VMEM_LIMIT = 56 * 1024 * 1024
NEG = -1e30


def _params(sem):
    return pltpu.CompilerParams(dimension_semantics=sem, vmem_limit_bytes=VMEM_LIMIT)


def _rms(x, g):
    return x * lax.rsqrt(jnp.mean(x * x, axis=-1, keepdims=True) + EPS) * g


def _silu(x):
    return x * jax.nn.sigmoid(x)


def _split(x):
    hi = x.astype(BF16)
    return hi, (x - hi.astype(F32)).astype(BF16)


def _dot3(xh, xl, w):
    wh, wl = _split(w)
    return (jnp.dot(xh, wh, preferred_element_type=F32)
            + (jnp.dot(xl, wh, preferred_element_type=F32) + jnp.dot(xh, wl, preferred_element_type=F32)))


class _Layer:
    def __init__(self, arr, layer, cols=None):
        self.arr, self.lead, self.dtype = arr, (layer,), arr.dtype
        self.shape = arr.shape[1:] if cols is None else arr.shape[1:-1] + (cols,)


def _warr(w):
    return w.arr if isinstance(w, _Layer) else w


def _wspec(w, block, index_fn):
    lead = w.lead if isinstance(w, _Layer) else ()
    return pl.BlockSpec((None,) * len(lead) + tuple(block), lambda *a: lead + tuple(index_fn(*a)))


def _is_precise(w):
    assert w.dtype in (BF16, F32)
    return w.dtype == F32


def _mm_body(*refs, has_norm, has_aux, has_res, precise):
    it = iter(refs)
    x_ref = next(it)
    g_ref = next(it) if has_norm else None
    w_ref = next(it)
    wa_ref = next(it) if has_aux else None
    r_ref = next(it) if has_res else None
    o_ref = next(it)
    oa_ref = next(it) if has_aux else None
    xn_ref = next(it)

    def product(w):
        if precise:
            return _dot3(xn_ref[0], xn_ref[1], w)
        return jnp.dot(xn_ref[0], w, preferred_element_type=F32)

    @pl.when(pl.program_id(1) == 0)
    def _():
        x = x_ref[...].astype(F32)
        if has_norm:
            x = _rms(x, g_ref[...])
        if precise:
            xn_ref[0], xn_ref[1] = _split(x)
        else:
            xn_ref[0] = x.astype(BF16)
        if has_aux:
            oa_ref[...] = product(wa_ref[...])

    acc = product(w_ref[...])
    if has_res:
        acc = acc + r_ref[...]
    o_ref[...] = acc.astype(o_ref.dtype)


def _mm(x, w, gain=None, res=None, w_aux=None, tm=1024, tn=1024, out_dtype=F32):
    precise = _is_precise(w)
    m, k = x.shape
    n = w.shape[1]
    tm = min(tm, m)
    tn = min(tn, n)
    assert m % tm == 0 and n % tn == 0
    ins = [x]
    specs = [pl.BlockSpec((tm, k), lambda i, j: (i, 0))]
    if gain is not None:
        ins.append(gain.reshape(1, k).astype(F32))
        specs.append(pl.BlockSpec((1, k), lambda i, j: (0, 0)))
    ins.append(_warr(w))
    specs.append(_wspec(w, (k, tn), lambda i, j: (0, j)))
    out_specs = [pl.BlockSpec((tm, tn), lambda i, j: (i, j))]
    out_shape = [jax.ShapeDtypeStruct((m, n), out_dtype)]
    if w_aux is not None:
        na = w_aux.shape[1]
        ins.append(_warr(w_aux))
        specs.append(_wspec(w_aux, (k, na), lambda i, j: (0, 0)))
        out_specs.append(pl.BlockSpec((tm, na), lambda i, j: (i, 0)))
        out_shape.append(jax.ShapeDtypeStruct((m, na), F32))
    if res is not None:
        ins.append(res)
        specs.append(pl.BlockSpec((tm, tn), lambda i, j: (i, j)))
    out = pl.pallas_call(
        functools.partial(_mm_body, has_norm=gain is not None, has_aux=w_aux is not None,
                          has_res=res is not None, precise=precise),
        grid=(m // tm, n // tn),
        in_specs=specs,
        out_specs=out_specs,
        out_shape=out_shape,
        scratch_shapes=[pltpu.VMEM((1 + precise, tm, k), BF16)],
        compiler_params=_params(("parallel", "arbitrary")),
        name="mm",
    )(*ins)
    return out if w_aux is not None else out[0]


def _swiglu(xh, xl, wg, wu, wd):
    if xl is None:
        gate = jnp.dot(xh, wg, preferred_element_type=F32)
        up = jnp.dot(xh, wu, preferred_element_type=F32)
        return jnp.dot((_silu(gate) * up).astype(BF16), wd, preferred_element_type=F32)
    gate = _dot3(xh, xl, wg)
    up = _dot3(xh, xl, wu)
    ah, al = _split(_silu(gate) * up)
    return _dot3(ah, al, wd)


def _ffn_dense_body(x_ref, g_ref, wg_ref, wu_ref, wd_ref, o_ref, xn_ref, acc_ref, *, precise):
    f = pl.program_id(1)

    @pl.when(f == 0)
    def _():
        xn = _rms(x_ref[...], g_ref[...])
        if precise:
            xn_ref[0], xn_ref[1] = _split(xn)
        else:
            xn_ref[0] = xn.astype(BF16)
        acc_ref[...] = x_ref[...]

    acc_ref[...] += _swiglu(xn_ref[0], xn_ref[1] if precise else None, wg_ref[...], wu_ref[...], wd_ref[...])

    @pl.when(f == pl.num_programs(1) - 1)
    def _():
        o_ref[...] = acc_ref[...]


def _ffn_dense(x, gain, w_gu, w_down, tm=512):
    precise = _is_precise(w_gu)
    m, k = x.shape
    ff = w_down.shape[0]
    tm = min(tm, m)
    nf = 2 if not precise else ff // (2 * LANES)
    tf = ff // nf
    assert tf % LANES == 0 and tf * nf == ff
    return pl.pallas_call(
        functools.partial(_ffn_dense_body, precise=precise),
        grid=(m // tm, nf),
        in_specs=[
            pl.BlockSpec((tm, k), lambda i, f: (i, 0)),
            pl.BlockSpec((1, k), lambda i, f: (0, 0)),
            _wspec(w_gu, (k, tf), lambda i, f: (0, f)),
            _wspec(w_gu, (k, tf), lambda i, f: (0, nf + f)),
            _wspec(w_down, (tf, k), lambda i, f: (f, 0)),
        ],
        out_specs=pl.BlockSpec((tm, k), lambda i, f: (i, 0)),
        out_shape=jax.ShapeDtypeStruct((m, k), F32),
        scratch_shapes=[pltpu.VMEM((1 + precise, tm, k), BF16), pltpu.VMEM((tm, k), F32)],
        compiler_params=_params(("parallel", "arbitrary")),
        name="ffn_dense",
    )(x, gain.reshape(1, k).astype(F32), _warr(w_gu), _warr(w_gu), _warr(w_down))


def _router_body(x_ref, g_ref, wr_ref, br_ref, idx_ref, gate_ref, cnt_ref, carry_ref):
    i = pl.program_id(0)

    @pl.when(i == 0)
    def _():
        carry_ref[...] = jnp.zeros_like(carry_ref)

    tm = x_ref.shape[0]
    xn = _rms(x_ref[...], g_ref[...])
    logits = _dot3(*_split(xn), wr_ref[...]) + br_ref[...]
    lane = lax.broadcasted_iota(I32, logits.shape, 1)
    m1 = jnp.max(logits, axis=-1, keepdims=True)
    i1 = jnp.min(jnp.where(logits == m1, lane, LANES), axis=-1, keepdims=True)
    rest = jnp.where(lane == i1, NEG, logits)
    m2 = jnp.max(rest, axis=-1, keepdims=True)
    i2 = jnp.min(jnp.where(rest == m2, lane, LANES), axis=-1, keepdims=True)
    e2 = jnp.exp(m2 - m1)
    g1 = 1.0 / (1.0 + e2)
    g2 = e2 * g1
    oh1 = lane == i1
    oh2 = lane == i2
    oh = jnp.where(oh1 | oh2, 1.0, 0.0).astype(BF16)
    r = lax.broadcasted_iota(I32, (tm, tm), 0)
    c = lax.broadcasted_iota(I32, (tm, tm), 1)
    below = jnp.where(c < r, 1.0, 0.0).astype(BF16)
    before = jnp.dot(below, oh, preferred_element_type=F32) + carry_ref[0:1, :]
    rk1 = jnp.sum(jnp.where(oh1, before, 0.0), axis=-1, keepdims=True).astype(I32)
    rk2 = jnp.sum(jnp.where(oh2, before, 0.0), axis=-1, keepdims=True).astype(I32)
    idx_ref[...] = jnp.where(lane == 0, i1, jnp.where(lane == 1, i2, jnp.where(lane == 2, rk1, rk2)))
    gate_ref[...] = jnp.where(lane == 0, g1, g2)
    total = carry_ref[0:1, :] + jnp.sum(oh.astype(F32), axis=0, keepdims=True)
    carry_ref[...] = jnp.broadcast_to(total, carry_ref.shape)
    cnt_ref[...] = carry_ref[...]


def _router(x, gain, w_router, b_router, tm=512):
    m, k = x.shape
    tm = min(tm, m)
    wr = jnp.zeros((k, LANES), F32).at[:, :N_EXPERTS].set(w_router)
    br = jnp.full((1, LANES), NEG, F32).at[0, :N_EXPERTS].set(b_router)
    return pl.pallas_call(
        _router_body,
        grid=(m // tm,),
        in_specs=[
            pl.BlockSpec((tm, k), lambda i: (i, 0)),
            pl.BlockSpec((1, k), lambda i: (0, 0)),
            pl.BlockSpec((k, LANES), lambda i: (0, 0)),
            pl.BlockSpec((1, LANES), lambda i: (0, 0)),
        ],
        out_specs=[
            pl.BlockSpec((tm, LANES), lambda i: (i, 0)),
            pl.BlockSpec((tm, LANES), lambda i: (i, 0)),
            pl.BlockSpec((8, LANES), lambda i: (0, 0)),
        ],
        out_shape=[
            jax.ShapeDtypeStruct((m, LANES), I32),
            jax.ShapeDtypeStruct((m, LANES), F32),
            jax.ShapeDtypeStruct((8, LANES), F32),
        ],
        scratch_shapes=[pltpu.VMEM((8, LANES), F32)],
        compiler_params=_params(("arbitrary",)),
        name="moe_router",
    )(x, gain.reshape(1, k).astype(F32), wr, br)


def _row_copy(src, dst, src_row, dst_row, sem):
    return pltpu.make_async_copy(src.at[pl.ds(src_row, 1)], dst.at[pl.ds(dst_row, 1)], sem)


def _scatter_body(d_ref, x_ref, g_ref, xs_in_ref, xs_ref, buf_ref, sem):
    del xs_in_ref
    tm = buf_ref.shape[0]
    buf_ref[...] = _rms(x_ref[...], g_ref[...])

    def issue(t, carry):
        for kk in range(2):
            _row_copy(buf_ref, xs_ref, t, d_ref[0, 0, 2 * t + kk], sem).start()
        return carry

    lax.fori_loop(0, tm, issue, 0, unroll=8)
    for kk in range(2):
        pltpu.make_async_copy(buf_ref, xs_ref.at[pl.ds(0, tm)], sem).wait()


def _moe_scatter(x, gain, dest_tiles, rows, tm):
    m, k = x.shape
    return pl.pallas_call(
        _scatter_body,
        grid=(m // tm,),
        in_specs=[
            pl.BlockSpec((1, 1, 2 * tm), lambda i: (i, 0, 0), memory_space=pltpu.SMEM),
            pl.BlockSpec((tm, k), lambda i: (i, 0)),
            pl.BlockSpec((1, k), lambda i: (0, 0)),
            pl.BlockSpec(memory_space=pl.ANY),
        ],
        out_specs=pl.BlockSpec(memory_space=pl.ANY),
        out_shape=jax.ShapeDtypeStruct((rows, k), F32),
        scratch_shapes=[pltpu.VMEM((tm, k), F32), pltpu.SemaphoreType.DMA(())],
        input_output_aliases={3: 0},
        compiler_params=_params(("arbitrary",)),
        name="moe_scatter",
    )(dest_tiles, x, gain.reshape(1, k).astype(F32), jnp.zeros((rows, k), F32))


def _ffn_group_body(te_ref, nu_ref, x_ref, wg_ref, wu_ref, wd_ref, o_ref, *, precise):
    del te_ref
    used = pl.program_id(0) < nu_ref[0]

    @pl.when(used)
    def _():
        xh, xl = _split(x_ref[...]) if precise else (x_ref[...].astype(BF16), None)
        o_ref[...] = _swiglu(xh, xl, wg_ref[...], wu_ref[...], wd_ref[...])

    @pl.when(jnp.logical_not(used))
    def _():
        o_ref[...] = jnp.zeros_like(o_ref)


def _ffn_grouped(xs, tile_expert, n_used, w_gu, w_down, tm):
    precise = _is_precise(w_gu)
    rows, k = xs.shape
    ff = w_down.shape[1]
    nt = rows // tm

    def row_map(i, te, nu):
        return (jnp.minimum(i, nu[0] - 1), 0)

    grid_spec = pltpu.PrefetchScalarGridSpec(
        num_scalar_prefetch=2,
        grid=(nt,),
        in_specs=[
            pl.BlockSpec((tm, k), row_map),
            _wspec(w_gu, (None, k, ff), lambda i, te, nu: (te[i], 0, 0)),
            _wspec(w_gu, (None, k, ff), lambda i, te, nu: (te[i], 0, 1)),
            _wspec(w_down, (None, ff, k), lambda i, te, nu: (te[i], 0, 0)),
        ],
        out_specs=pl.BlockSpec((tm, k), lambda i, te, nu: (i, 0)),
    )
    return pl.pallas_call(
        functools.partial(_ffn_group_body, precise=precise),
        grid_spec=grid_spec,
        out_shape=jax.ShapeDtypeStruct((rows, k), F32),
        compiler_params=_params(("arbitrary",)),
        name="moe_experts",
    )(tile_expert, n_used, xs, _warr(w_gu), _warr(w_gu), _warr(w_down))


def _combine_body(d_ref, x_ref, gate_ref, ys_ref, o_ref, buf_ref, sem):
    tm = x_ref.shape[0]

    def issue(t, carry):
        for kk in range(2):
            _row_copy(ys_ref, buf_ref.at[kk], d_ref[0, 0, 2 * t + kk], t, sem).start()
        return carry

    lax.fori_loop(0, tm, issue, 0, unroll=8)
    for kk in range(2):
        pltpu.make_async_copy(ys_ref.at[pl.ds(0, tm)], buf_ref.at[kk], sem).wait()
    g = gate_ref[...]
    o_ref[...] = x_ref[...] + g[:, 0:1] * buf_ref[0] + g[:, 1:2] * buf_ref[1]


def _moe_combine(x, gates, dest_tiles, ys, tm):
    m, k = x.shape
    return pl.pallas_call(
        _combine_body,
        grid=(m // tm,),
        in_specs=[
            pl.BlockSpec((1, 1, 2 * tm), lambda i: (i, 0, 0), memory_space=pltpu.SMEM),
            pl.BlockSpec((tm, k), lambda i: (i, 0)),
            pl.BlockSpec((tm, LANES), lambda i: (i, 0)),
            pl.BlockSpec(memory_space=pl.ANY),
        ],
        out_specs=pl.BlockSpec((tm, k), lambda i: (i, 0)),
        out_shape=jax.ShapeDtypeStruct((m, k), F32),
        scratch_shapes=[pltpu.VMEM((2, tm, k), F32), pltpu.SemaphoreType.DMA(())],
        compiler_params=_params(("arbitrary",)),
        name="moe_combine",
    )(dest_tiles, x, gates, ys)


SEG_TOK = 256
SEG_ROWS = 640
EXPERT_TILE = 512


def _router_seg_body(x_ref, g_ref, wr_ref, br_ref, idx_ref, gate_ref, tab_ref, cnt_ref, carry_ref):
    i = pl.program_id(0)

    @pl.when(i == 0)
    def _():
        carry_ref[...] = jnp.zeros_like(carry_ref)

    tm = x_ref.shape[0]
    xn = _rms(x_ref[...], g_ref[...])
    logits = _dot3(*_split(xn), wr_ref[...]) + br_ref[...]
    lane = lax.broadcasted_iota(I32, logits.shape, 1)
    m1 = jnp.max(logits, axis=-1, keepdims=True)
    i1 = jnp.min(jnp.where(logits == m1, lane, LANES), axis=-1, keepdims=True)
    rest = jnp.where(lane == i1, NEG, logits)
    m2 = jnp.max(rest, axis=-1, keepdims=True)
    i2 = jnp.min(jnp.where(rest == m2, lane, LANES), axis=-1, keepdims=True)
    e2 = jnp.exp(m2 - m1)
    g1 = 1.0 / (1.0 + e2)
    g2 = e2 * g1
    oh1 = lane == i1
    oh2 = lane == i2
    oh = jnp.where(oh1 | oh2, 1.0, 0.0)
    r = lax.broadcasted_iota(I32, (tm, tm), 0)
    c = lax.broadcasted_iota(I32, (tm, tm), 1)
    below = jnp.where(c < r, 1.0, 0.0).astype(BF16)
    rank = jnp.dot(below, oh.astype(BF16), preferred_element_type=F32)
    count = jnp.sum(oh, axis=0, keepdims=True)
    padded = jnp.floor((count + 7.0) * 0.125) * 8.0
    er = lax.broadcasted_iota(I32, (LANES, LANES), 0)
    ec = lax.broadcasted_iota(I32, (LANES, LANES), 1)
    start = jnp.dot(jnp.broadcast_to(padded, (8, LANES)), jnp.where(er < ec, 1.0, 0.0),
                    precision=HI, preferred_element_type=F32)[0:1]
    slot = rank + start
    s1 = jnp.sum(jnp.where(oh1, slot, 0.0), axis=-1, keepdims=True).astype(I32)
    s2 = jnp.sum(jnp.where(oh2, slot, 0.0), axis=-1, keepdims=True).astype(I32)
    idx_ref[...] = jnp.where(lane == 0, i1, jnp.where(lane == 1, i2, jnp.where(lane == 2, s1, s2)))
    gate_ref[...] = jnp.where(lane == 0, g1, g2)
    sub = lax.broadcasted_iota(I32, (8, LANES), 0)
    rows = jnp.where(sub == 0, padded * 0.125, jnp.where(sub == 1, start, carry_ref[...]))
    tab_ref[...] = jnp.where(sub < 3, rows, 0.0).astype(I32)
    carry_ref[...] = carry_ref[...] + padded
    cnt_ref[...] = carry_ref[...]


def _router_seg(x, gain, w_router, b_router):
    m, k = x.shape
    tm = SEG_TOK
    wr = jnp.zeros((k, LANES), F32).at[:, :N_EXPERTS].set(w_router)
    br = jnp.full((1, LANES), NEG, F32).at[0, :N_EXPERTS].set(b_router)
    return pl.pallas_call(
        _router_seg_body,
        grid=(m // tm,),
        in_specs=[
            pl.BlockSpec((tm, k), lambda i: (i, 0)),
            pl.BlockSpec((1, k), lambda i: (0, 0)),
            pl.BlockSpec((k, LANES), lambda i: (0, 0)),
            pl.BlockSpec((1, LANES), lambda i: (0, 0)),
        ],
        out_specs=[
            pl.BlockSpec((tm, LANES), lambda i: (i, 0)),
            pl.BlockSpec((tm, LANES), lambda i: (i, 0)),
            pl.BlockSpec((None, 8, LANES), lambda i: (i, 0, 0)),
            pl.BlockSpec((8, LANES), lambda i: (0, 0)),
        ],
        out_shape=[
            jax.ShapeDtypeStruct((m, LANES), I32),
            jax.ShapeDtypeStruct((m, LANES), F32),
            jax.ShapeDtypeStruct((m // tm, 8, LANES), I32),
            jax.ShapeDtypeStruct((8, LANES), F32),
        ],
        scratch_shapes=[pltpu.VMEM((8, LANES), F32)],
        compiler_params=_params(("arbitrary",)),
        name="moe_router_seg",
    )(x, gain.reshape(1, k).astype(F32), wr, br)


def _chunk_copy(src, dst, src_row, dst_row, sem):
    rows = lambda r: pl.ds(r if isinstance(r, int) else pl.multiple_of(r, 8), 8)
    return pltpu.make_async_copy(src.at[rows(src_row)], dst.at[rows(dst_row)], sem)


def _segment_start(tab_ref, base_ref, local_ref, hbm_ref, sem, to_hbm):
    for e in range(N_EXPERTS):
        local0 = tab_ref[1, e]
        hbm0 = base_ref[e] + tab_ref[2, e]

        def issue(j, carry, local0=local0, hbm0=hbm0):
            if to_hbm:
                _chunk_copy(local_ref, hbm_ref, local0 + 8 * j, hbm0 + 8 * j, sem).start()
            else:
                _chunk_copy(hbm_ref, local_ref, hbm0 + 8 * j, local0 + 8 * j, sem).start()
            return carry

        lax.fori_loop(0, tab_ref[0, e], issue, 0)


def _segment_wait(tab_ref, local_ref, hbm_ref, sem, to_hbm):
    total = 0
    for e in range(N_EXPERTS):
        total = total + tab_ref[0, e]

    def drain(j, carry):
        if to_hbm:
            _chunk_copy(local_ref, hbm_ref, 0, 0, sem).wait()
        else:
            _chunk_copy(hbm_ref, local_ref, 0, 0, sem).wait()
        return carry

    lax.fori_loop(0, total, drain, 0)


def _slot_onehot(idx_ref, which, width):
    lane = lax.broadcasted_iota(I32, (idx_ref.shape[0], width), 1)
    return jnp.where(lane == idx_ref[:, 2 + which:3 + which], 1.0, 0.0)


def _scatter_seg_body(base_ref, tab_ref, tab_prev_ref, idx_ref, x_ref, g_ref, xs_in_ref, xs_ref, buf_ref, sem):
    del xs_in_ref
    i = pl.program_id(0)
    slot = i % 2
    xn = _rms(x_ref[...], g_ref[...]).astype(BF16)
    width = buf_ref.shape[1]
    place = (_slot_onehot(idx_ref, 0, width) + _slot_onehot(idx_ref, 1, width)).astype(BF16)
    buf_ref[slot] = lax.dot_general(place, xn, (((0,), (0,)), ((), ())), preferred_element_type=F32)
    _segment_start(tab_ref, base_ref, buf_ref.at[slot], xs_ref, sem.at[slot], to_hbm=True)

    @pl.when(i > 0)
    def _():
        _segment_wait(tab_prev_ref, buf_ref.at[1 - slot], xs_ref, sem.at[1 - slot], to_hbm=True)

    @pl.when(i == pl.num_programs(0) - 1)
    def _():
        _segment_wait(tab_ref, buf_ref.at[slot], xs_ref, sem.at[slot], to_hbm=True)


def _tab_spec(shift, nt):
    return pl.BlockSpec((None, 8, LANES), lambda i, b: (jnp.clip(i + shift, 0, nt - 1), 0, 0),
                        memory_space=pltpu.SMEM)


def _moe_scatter_seg(x, gain, idx, tab, base, rows):
    m, k = x.shape
    tm = SEG_TOK
    nt = m // tm
    grid_spec = pltpu.PrefetchScalarGridSpec(
        num_scalar_prefetch=1,
        grid=(nt,),
        in_specs=[
            _tab_spec(0, nt), _tab_spec(-1, nt),
            pl.BlockSpec((tm, LANES), lambda i, b: (i, 0)),
            pl.BlockSpec((tm, k), lambda i, b: (i, 0)),
            pl.BlockSpec((1, k), lambda i, b: (0, 0)),
            pl.BlockSpec(memory_space=pl.ANY),
        ],
        out_specs=pl.BlockSpec(memory_space=pl.ANY),
        scratch_shapes=[pltpu.VMEM((2, SEG_ROWS, k), F32), pltpu.SemaphoreType.DMA((2,))],
    )
    return pl.pallas_call(
        _scatter_seg_body,
        grid_spec=grid_spec,
        out_shape=jax.ShapeDtypeStruct((rows, k), F32),
        input_output_aliases={6: 0},
        compiler_params=_params(("arbitrary",)),
        name="moe_scatter_seg",
    )(base, tab, tab, idx, x, gain.reshape(1, k).astype(F32), jnp.zeros((rows, k), F32))


def _combine_seg_body(base_ref, tab_ref, tab_next_ref, idx_ref, x_ref, gate_ref, ys_ref, o_ref, buf_ref, sem):
    i = pl.program_id(0)
    slot = i % 2

    @pl.when(i == 0)
    def _():
        buf_ref[...] = jnp.zeros_like(buf_ref)
        _segment_start(tab_ref, base_ref, buf_ref.at[0], ys_ref, sem.at[0], to_hbm=False)

    @pl.when(i < pl.num_programs(0) - 1)
    def _():
        _segment_start(tab_next_ref, base_ref, buf_ref.at[1 - slot], ys_ref, sem.at[1 - slot], to_hbm=False)

    _segment_wait(tab_ref, buf_ref.at[slot], ys_ref, sem.at[slot], to_hbm=False)
    width = buf_ref.shape[1]
    y = buf_ref[slot].astype(BF16)
    g = gate_ref[...]
    pick = lambda which: jnp.dot(_slot_onehot(idx_ref, which, width).astype(BF16), y, preferred_element_type=F32)
    o_ref[...] = x_ref[...] + g[:, 0:1] * pick(0) + g[:, 1:2] * pick(1)


def _moe_combine_seg(x, gates, idx, tab, base, ys):
    m, k = x.shape
    tm = SEG_TOK
    nt = m // tm
    grid_spec = pltpu.PrefetchScalarGridSpec(
        num_scalar_prefetch=1,
        grid=(nt,),
        in_specs=[
            _tab_spec(0, nt), _tab_spec(1, nt),
            pl.BlockSpec((tm, LANES), lambda i, b: (i, 0)),
            pl.BlockSpec((tm, k), lambda i, b: (i, 0)),
            pl.BlockSpec((tm, LANES), lambda i, b: (i, 0)),
            pl.BlockSpec(memory_space=pl.ANY),
        ],
        out_specs=pl.BlockSpec((tm, k), lambda i, b: (i, 0)),
        scratch_shapes=[pltpu.VMEM((2, SEG_ROWS, k), F32), pltpu.SemaphoreType.DMA((2,))],
    )
    return pl.pallas_call(
        _combine_seg_body,
        grid_spec=grid_spec,
        out_shape=jax.ShapeDtypeStruct((m, k), F32),
        compiler_params=_params(("arbitrary",)),
        name="moe_combine_seg",
    )(base, tab, tab, idx, x, gates, ys)


def _moe_seg(x, gain, w_router, b_router, w_gu, w_down):
    m, k = x.shape
    idx, gates, tab, totals = _router_seg(x, gain, w_router, b_router)
    totals = totals[0, :N_EXPERTS].astype(I32)
    region = ((totals + EXPERT_TILE - 1) // EXPERT_TILE) * EXPERT_TILE
    ends = jnp.cumsum(region)
    base = (ends - region).astype(I32)
    max_rows = 2 * m + N_EXPERTS * (7 * (m // SEG_TOK) + EXPERT_TILE - 1)
    rows = (max_rows // EXPERT_TILE) * EXPERT_TILE
    nt = rows // EXPERT_TILE
    n_used = (ends[-1] // EXPERT_TILE).astype(I32)
    tile_start = jnp.minimum(jnp.arange(nt, dtype=I32), n_used - 1) * EXPERT_TILE
    tile_expert = jnp.sum(tile_start[:, None] >= ends[None, :], axis=1).astype(I32)
    xs = _moe_scatter_seg(x, gain, idx, tab, base, rows)
    ys = _ffn_grouped(xs, tile_expert, n_used.reshape(1), w_gu, w_down, EXPERT_TILE)
    return _moe_combine_seg(x, gates, idx, tab, base, ys)


def _moe(x, gain, w_router, b_router, w_gu, w_down):
    m, k = x.shape
    if m % SEG_TOK == 0 and m >= 8 * SEG_TOK and not _is_precise(w_gu):
        return _moe_seg(x, gain, w_router, b_router, w_gu, w_down)
    tm_tok = min(256, m)
    tm_grp = 512 if m >= 4096 else 128
    idx, gates, counts = _router(x, gain, w_router, b_router)
    counts = counts[0, :N_EXPERTS].astype(I32)
    padded = ((counts + tm_grp - 1) // tm_grp) * tm_grp
    ends = jnp.cumsum(padded)
    starts = ends - padded
    dest = starts[idx[:, 0:2]] + idx[:, 2:4]
    dest_tiles = dest.reshape(m // tm_tok, 1, 2 * tm_tok)
    rows = ((2 * m + N_EXPERTS * (tm_grp - 1)) // tm_grp) * tm_grp
    nt = rows // tm_grp
    n_used = (ends[-1] // tm_grp).astype(I32)
    tile_start = jnp.minimum(jnp.arange(nt, dtype=I32), n_used - 1) * tm_grp
    tile_expert = jnp.sum(tile_start[:, None] >= ends[None, :], axis=1).astype(I32)
    xs = _moe_scatter(x, gain, dest_tiles, rows, tm_tok)
    ys = _ffn_grouped(xs, tile_expert, n_used.reshape(1), w_gu, w_down, tm_grp)
    return _moe_combine(x, gates, dest_tiles, ys, tm_tok)


def _head_cols(x, h, width=GDN_DK):
    return x[:, h * width:(h + 1) * width]


def _gdn_prep_body(x_ref, halo_ref, ab_ref, cw_ref, alog_ref, dtb_ref,
                   q_ref, qg_ref, k_ref, kb_ref, kbg_ref, kd_ref, vb_ref,
                   gc_ref, gct_ref, egl_ref, ext_ref):
    i = pl.program_id(1)
    tt = x_ref.shape[0]
    ext_ref[0:8, :] = jnp.where(i == 0, 0.0, halo_ref[...])
    ext_ref[8:, :] = x_ref[...]
    ext = ext_ref[...]
    acc = pltpu.roll(ext, CONV_TAPS - 1, axis=0)[8:] * cw_ref[0:1, :]
    for j in range(1, CONV_TAPS - 1):
        acc = acc + pltpu.roll(ext, CONV_TAPS - 1 - j, axis=0)[8:] * cw_ref[j:j + 1, :]
    acc = acc + ext[8:] * cw_ref[CONV_TAPS - 1:CONV_TAPS, :]
    y = _silu(acc)

    ab = ab_ref[...]
    g = -jnp.exp(alog_ref[...]) * jax.nn.softplus(ab + dtb_ref[...])
    beta = jax.nn.sigmoid(pltpu.roll(ab, LANES - GDN_HEADS, axis=1))
    r = lax.broadcasted_iota(I32, (tt, tt), 0)
    c = lax.broadcasted_iota(I32, (tt, tt), 1)
    same = (r // CHUNK) == (c // CHUNK)
    tril = jnp.where(same & (c <= r), 1.0, 0.0)
    gc = jnp.dot(tril, g, precision=HI, preferred_element_type=F32)
    gl = jnp.dot(jnp.where(same, 1.0, 0.0), g, precision=HI, preferred_element_type=F32)
    e_gc = jnp.exp(gc)
    e_gd = jnp.exp(gl - gc)
    gc_ref[...] = gc
    gct = gc.T
    sub = lax.broadcasted_iota(I32, (8, LANES), 0)
    lan = lax.broadcasted_iota(I32, (8, LANES), 1)
    for cc in range(tt // CHUNK):
        gct_ref[cc] = gct[0:8, cc * CHUNK:(cc + 1) * CHUNK]
        row = jnp.broadcast_to(gl[cc * CHUNK:cc * CHUNK + 1, :], (8, LANES))
        val = jnp.sum(jnp.where(sub == lan, row, 0.0), axis=-1, keepdims=True)
        egl_ref[cc] = jnp.broadcast_to(jnp.exp(val), (8, LANES))

    nq = GDN_HEADS * GDN_DK
    for h in range(GDN_HEADS):
        qh = _head_cols(y, h)
        kh = _head_cols(y, GDN_HEADS + h)
        vh = _head_cols(y, 2 * GDN_HEADS + h)
        qn = qh * lax.rsqrt(jnp.sum(qh * qh, axis=-1, keepdims=True) + EPS) * (GDN_DK ** -0.5)
        kn = kh * lax.rsqrt(jnp.sum(kh * kh, axis=-1, keepdims=True) + EPS)
        b_h = beta[:, h:h + 1]
        eg_h = e_gc[:, h:h + 1]
        sl = slice(h * GDN_DK, (h + 1) * GDN_DK)
        kb = kn * b_h
        q_ref[:, sl] = qn.astype(BF16)
        qg_ref[:, sl] = (qn * eg_h).astype(BF16)
        k_ref[:, sl] = kn.astype(BF16)
        kb_ref[:, sl] = kb.astype(BF16)
        kbg_ref[:, sl] = (kb * eg_h).astype(BF16)
        kd_ref[:, sl] = (kn * e_gd[:, h:h + 1]).astype(BF16)
        vb_ref[:, sl] = (vh * b_h).astype(BF16)
    del nq


def _gdn_prep(proj, ab, conv_w, a_log, dt_bias, tt=256):
    b, t, _ = proj.shape
    nc = t // CHUNK
    cpt = tt // CHUNK
    row = lambda v: jnp.zeros((1, LANES), F32).at[0, :GDN_HEADS].set(v)
    tok = pl.BlockSpec((None, tt, D_MODEL), lambda bi, i: (bi, i, 0))
    out_shapes = [jax.ShapeDtypeStruct((b, t, D_MODEL), BF16)] * 7 + [
        jax.ShapeDtypeStruct((b, t, LANES), F32),
        jax.ShapeDtypeStruct((b, nc, 8, CHUNK), F32),
        jax.ShapeDtypeStruct((b, nc, 8, LANES), F32),
    ]
    out_specs = [tok] * 7 + [
        pl.BlockSpec((None, tt, LANES), lambda bi, i: (bi, i, 0)),
        pl.BlockSpec((None, cpt, 8, CHUNK), lambda bi, i: (bi, i, 0, 0)),
        pl.BlockSpec((None, cpt, 8, LANES), lambda bi, i: (bi, i, 0, 0)),
    ]
    return pl.pallas_call(
        _gdn_prep_body,
        grid=(b, t // tt),
        in_specs=[
            pl.BlockSpec((None, tt, QKV_W), lambda bi, i: (bi, i, 0)),
            pl.BlockSpec((None, 8, QKV_W), lambda bi, i: (bi, jnp.maximum(i * (tt // 8) - 1, 0), 0)),
            pl.BlockSpec((None, tt, LANES), lambda bi, i: (bi, i, 0)),
            pl.BlockSpec((CONV_TAPS, QKV_W), lambda bi, i: (0, 0)),
            pl.BlockSpec((1, LANES), lambda bi, i: (0, 0)),
            pl.BlockSpec((1, LANES), lambda bi, i: (0, 0)),
        ],
        out_specs=out_specs,
        out_shape=out_shapes,
        scratch_shapes=[pltpu.VMEM((tt + 8, QKV_W), F32)],
        compiler_params=_params(("parallel", "arbitrary")),
        name="gdn_prep",
    )(proj, proj, ab, conv_w.astype(F32), row(a_log), row(dt_bias))


def _decay(gc_col, gc_row, keep):
    return jnp.where(keep, jnp.exp(jnp.where(keep, gc_col - gc_row, 0.0)), 0.0)


def _gdn_l_body(kb_ref, k_ref, gc_ref, gct_ref, l_ref):
    r = lax.broadcasted_iota(I32, (CHUNK, CHUNK), 0)
    c = lax.broadcasted_iota(I32, (CHUNK, CHUNK), 1)
    strict = c < r
    for cc in range(l_ref.shape[0]):
        rows = slice(cc * CHUNK, (cc + 1) * CHUNK)
        for h in range(GDN_HEADS):
            cols = slice(h * GDN_DK, (h + 1) * GDN_DK)
            kk = lax.dot_general(kb_ref[rows, cols], k_ref[rows, cols], (((1,), (1,)), ((), ())),
                                 preferred_element_type=F32)
            dec = _decay(gc_ref[rows, h:h + 1], gct_ref[cc, h:h + 1, :], strict)
            l_ref[cc, :, h * CHUNK:(h + 1) * CHUNK] = kk * dec


def _gdn_l(kb, k, gc, gct, cpt=4):
    b, t, _ = k.shape
    nc = t // CHUNK
    tt = cpt * CHUNK
    tok = pl.BlockSpec((None, tt, D_MODEL), lambda bi, i: (bi, i, 0))
    return pl.pallas_call(
        _gdn_l_body,
        grid=(b, nc // cpt),
        in_specs=[tok, tok,
                  pl.BlockSpec((None, tt, LANES), lambda bi, i: (bi, i, 0)),
                  pl.BlockSpec((None, cpt, 8, CHUNK), lambda bi, i: (bi, i, 0, 0))],
        out_specs=pl.BlockSpec((None, cpt, CHUNK, GDN_HEADS * CHUNK), lambda bi, i: (bi, i, 0, 0)),
        out_shape=jax.ShapeDtypeStruct((b, nc, CHUNK, GDN_HEADS * CHUNK), F32),
        compiler_params=_params(("parallel", "parallel")),
        name="gdn_l",
    )(kb, k, gc, gct)


def _tri_inv_body(l_ref, t_ref):
    sub = lax.broadcasted_iota(I32, (8, LANES), 0)
    for i in range(CHUNK):
        nq = i // 8 + 1
        acc = [jnp.where(sub == i % 8, 1.0, 0.0) if q == i // 8 else jnp.zeros((8, LANES), F32)
               for q in range(nq)]
        for j in range(i):
            lij = l_ref[i, j:j + 1, :]
            for q in range(j // 8 + 1):
                acc[q] = acc[q] - lij * t_ref[j, 8 * q:8 * q + 8, :]
        for q in range(CHUNK // 8):
            t_ref[i, 8 * q:8 * q + 8, :] = acc[q] if q < nq else jnp.zeros((8, LANES), F32)


def _tri_inv(lt):
    _, nh, _, p = lt.shape
    spec = pl.BlockSpec((CHUNK, None, CHUNK, LANES), lambda h, i: (0, h, 0, i))
    return pl.pallas_call(
        _tri_inv_body,
        grid=(nh, p // LANES),
        in_specs=[spec],
        out_specs=spec,
        out_shape=jax.ShapeDtypeStruct(lt.shape, F32),
        compiler_params=_params(("parallel", "parallel")),
        name="gdn_tri_inv",
    )(lt)


def _gdn_wy_body(t_ref, vb_ref, kbg_ref, q_ref, k_ref, qg_ref, gc_ref, gct_ref, u_ref, wq_ref, attn_ref):
    r = lax.broadcasted_iota(I32, (CHUNK, CHUNK), 0)
    c = lax.broadcasted_iota(I32, (CHUNK, CHUNK), 1)
    tril = c <= r
    for cc in range(t_ref.shape[0]):
        rows = slice(cc * CHUNK, (cc + 1) * CHUNK)
        for h in range(GDN_HEADS):
            cols = slice(h * GDN_DK, (h + 1) * GDN_DK)
            hc = slice(h * CHUNK, (h + 1) * CHUNK)
            rhs = jnp.concatenate([vb_ref[rows, cols], kbg_ref[rows, cols]], axis=1)
            sol = jnp.dot(t_ref[cc, :, hc].astype(BF16), rhs, preferred_element_type=F32)
            u_ref[rows, cols] = sol[:, :GDN_DK]
            wq_ref[cc, 0:CHUNK, cols] = sol[:, GDN_DK:].astype(BF16)
            wq_ref[cc, CHUNK:2 * CHUNK, cols] = qg_ref[rows, cols]
            qk = lax.dot_general(q_ref[rows, cols], k_ref[rows, cols], (((1,), (1,)), ((), ())),
                                 preferred_element_type=F32)
            dec = _decay(gc_ref[rows, h:h + 1], gct_ref[cc, h:h + 1, :], tril)
            attn_ref[cc, :, hc] = (qk * dec).astype(BF16)


def _gdn_wy(tinv, vb, kbg, q, k, qg, gc, gct, cpt=4):
    b, t, _ = k.shape
    nc = t // CHUNK
    tt = cpt * CHUNK
    tok = pl.BlockSpec((None, tt, D_MODEL), lambda bi, i: (bi, i, 0))
    sq = pl.BlockSpec((None, cpt, CHUNK, GDN_HEADS * CHUNK), lambda bi, i: (bi, i, 0, 0))
    return pl.pallas_call(
        _gdn_wy_body,
        grid=(b, nc // cpt),
        in_specs=[sq, tok, tok, tok, tok, tok,
                  pl.BlockSpec((None, tt, LANES), lambda bi, i: (bi, i, 0)),
                  pl.BlockSpec((None, cpt, 8, CHUNK), lambda bi, i: (bi, i, 0, 0))],
        out_specs=[tok,
                   pl.BlockSpec((None, cpt, 2 * CHUNK, D_MODEL), lambda bi, i: (bi, i, 0, 0)),
                   sq],
        out_shape=[jax.ShapeDtypeStruct((b, t, D_MODEL), F32),
                   jax.ShapeDtypeStruct((b, nc, 2 * CHUNK, D_MODEL), BF16),
                   jax.ShapeDtypeStruct((b, nc, CHUNK, GDN_HEADS * CHUNK), BF16)],
        compiler_params=_params(("parallel", "parallel")),
        name="gdn_wy",
    )(tinv, vb, kbg, q, k, qg, gc, gct)


def _gdn_chunk_body(wq_ref, u_ref, attn_ref, kd_ref, egl_ref, o_ref, s_out_ref, sa_ref, sb_ref):
    c = pl.program_id(0)

    @pl.when(c == 0)
    def _():
        sa_ref[...] = jnp.zeros_like(sa_ref)

    def step(src_ref, dst_ref):
        heads = [(b, h) for b in range(src_ref.shape[0]) for h in range(GDN_HEADS)]
        cols = lambda h: slice(h * GDN_DK, (h + 1) * GDN_DK)
        ws = [jnp.dot(wq_ref[b, :, cols(h)], src_ref[b, h].astype(BF16), preferred_element_type=F32)
              for b, h in heads]
        v_nb = [(u_ref[b, :, cols(h)] - w_[:CHUNK]).astype(BF16) for (b, h), w_ in zip(heads, ws)]
        for (b, h), w_, v_ in zip(heads, ws, v_nb):
            o_ref[b, :, cols(h)] = w_[CHUNK:] + jnp.dot(attn_ref[b, :, h * CHUNK:(h + 1) * CHUNK], v_,
                                                        preferred_element_type=F32)
        for (b, h), v_ in zip(heads, v_nb):
            dst_ref[b, h] = src_ref[b, h] * egl_ref[b, h:h + 1, :] + lax.dot_general(
                kd_ref[b, :, cols(h)], v_, (((0,), (0,)), ((), ())), preferred_element_type=F32)

        @pl.when(c == pl.num_programs(0) - 1)
        def _():
            s_out_ref[...] = dst_ref[...]

    @pl.when(c % 2 == 0)
    def _():
        step(sa_ref, sb_ref)

    @pl.when(c % 2 == 1)
    def _():
        step(sb_ref, sa_ref)


def _gdn_chunks(wq, u, attn, kd, egl):
    b, t, _ = u.shape
    nc = t // CHUNK
    tok = pl.BlockSpec((b, CHUNK, D_MODEL), lambda c: (0, c, 0))
    return pl.pallas_call(
        _gdn_chunk_body,
        grid=(nc,),
        in_specs=[
            pl.BlockSpec((b, None, 2 * CHUNK, D_MODEL), lambda c: (0, c, 0, 0)),
            tok,
            pl.BlockSpec((b, None, CHUNK, GDN_HEADS * CHUNK), lambda c: (0, c, 0, 0)),
            tok,
            pl.BlockSpec((b, None, 8, LANES), lambda c: (0, c, 0, 0)),
        ],
        out_specs=[
            tok,
            pl.BlockSpec((b, GDN_HEADS, GDN_DK, GDN_DK), lambda c: (0, 0, 0, 0)),
        ],
        out_shape=[
            jax.ShapeDtypeStruct((b, t, D_MODEL), F32),
            jax.ShapeDtypeStruct((b, GDN_HEADS, GDN_DK, GDN_DK), F32),
        ],
        scratch_shapes=[pltpu.VMEM((b, GDN_HEADS, GDN_DK, GDN_DK), F32)] * 2,
        compiler_params=_params(("arbitrary",)),
        name="gdn_chunks",
    )(wq, u, attn, kd, egl)


def _gdn_out_body(o_ref, z_ref, on_ref, w_ref, r_ref, out_ref, *, precise):
    parts = []
    for h in range(GDN_HEADS):
        oh = _head_cols(o_ref[...], h)
        zh = _head_cols(z_ref[...], h)
        parts.append(_rms(oh, on_ref[...]) * _silu(zh))
    if precise:
        gh, gl = _split(jnp.concatenate(parts, axis=1))
        out_ref[...] = r_ref[...] + _dot3(gh, gl, w_ref[...])
    else:
        gated = jnp.concatenate([a.astype(BF16) for a in parts], axis=1)
        out_ref[...] = r_ref[...] + jnp.dot(gated, w_ref[...], preferred_element_type=F32)


def _gdn_out(o, proj, o_norm, w_out, res, tm=512):
    m, k = o.shape
    tm = min(tm, m)
    return pl.pallas_call(
        functools.partial(_gdn_out_body, precise=_is_precise(w_out)),
        grid=(m // tm,),
        in_specs=[
            pl.BlockSpec((tm, k), lambda i: (i, 0)),
            pl.BlockSpec((tm, k), lambda i: (i, QKV_W // D_MODEL)),
            pl.BlockSpec((1, GDN_DK), lambda i: (0, 0)),
            _wspec(w_out, (k, k), lambda i: (0, 0)),
            pl.BlockSpec((tm, k), lambda i: (i, 0)),
        ],
        out_specs=pl.BlockSpec((tm, k), lambda i: (i, 0)),
        out_shape=jax.ShapeDtypeStruct((m, k), F32),
        compiler_params=_params(("parallel",)),
        name="gdn_out",
    )(o, proj, o_norm.reshape(1, GDN_DK).astype(F32), _warr(w_out), res)


def _gdn_prompt_layer(h, l, p, w):
    b, t, _ = h.shape
    hf = h.reshape(b * t, D_MODEL)
    proj, ab = _mm(hf, w['a_w_in'][l], gain=p['a_norm'][l], w_aux=w['a_w_ab'][l], tn=2 * D_MODEL)
    proj3 = proj.reshape(b, t, PROJ_W)
    q, qg, k, kb, kbg, kd, vb, gc, gct, egl = _gdn_prep(
        proj3, ab.reshape(b, t, LANES), p['a_conv_w'][l], p['a_A_log'][l], p['a_dt_bias'][l])
    lmat = _gdn_l(kb, k, gc, gct)
    nprob = b * (t // CHUNK)
    ppad = -nprob % LANES
    lt = jnp.transpose(lmat.reshape(nprob, CHUNK, GDN_HEADS * CHUNK), (1, 2, 0))
    lt = jnp.pad(lt, ((0, 0), (0, 0), (0, ppad))).reshape(CHUNK, GDN_HEADS, CHUNK, nprob + ppad)
    tt = _tri_inv(lt).reshape(CHUNK, GDN_HEADS * CHUNK, nprob + ppad)[:, :, :nprob]
    tinv = jnp.transpose(tt, (2, 0, 1)).reshape(b, t // CHUNK, CHUNK, GDN_HEADS * CHUNK)
    u, wq, attn = _gdn_wy(tinv, vb, kbg, q, k, qg, gc, gct)
    o, s_new = _gdn_chunks(wq, u, attn, kd, egl)
    h_new = _gdn_out(o.reshape(b * t, D_MODEL), proj, p['a_o_norm'][l], w['a_w_out'][l], hf)
    conv_new = proj3[:, t - (CONV_TAPS - 1):, :QKV_W]
    return h_new.reshape(b, t, D_MODEL), s_new, conv_new


def _gdn_prep1_body(x_ref, st_ref, ab_ref, cw_ref, alog_ref, dtb_ref,
                    q_ref, k_ref, v_ref, eg_ref, beta_ref):
    acc = st_ref[0] * cw_ref[0:1, :]
    for j in range(1, CONV_TAPS - 1):
        acc = acc + st_ref[j] * cw_ref[j:j + 1, :]
    acc = acc + x_ref[...] * cw_ref[CONV_TAPS - 1:CONV_TAPS, :]
    y = _silu(acc)
    ab = ab_ref[...]
    eg_ref[...] = jnp.exp(-jnp.exp(alog_ref[...]) * jax.nn.softplus(ab + dtb_ref[...]))
    beta_ref[...] = jax.nn.sigmoid(pltpu.roll(ab, LANES - GDN_HEADS, axis=1))
    for h in range(GDN_HEADS):
        qh = _head_cols(y, h)
        kh = _head_cols(y, GDN_HEADS + h)
        sl = slice(h * GDN_DK, (h + 1) * GDN_DK)
        q_ref[:, sl] = qh * lax.rsqrt(jnp.sum(qh * qh, axis=-1, keepdims=True) + EPS) * (GDN_DK ** -0.5)
        k_ref[:, sl] = kh * lax.rsqrt(jnp.sum(kh * kh, axis=-1, keepdims=True) + EPS)
    v_ref[...] = y[:, 2 * GDN_HEADS * GDN_DK:]


def _gdn_prep1(proj, ab, conv_state, conv_w, a_log, dt_bias):
    n = proj.shape[0]
    row = lambda v: jnp.zeros((1, LANES), F32).at[0, :GDN_HEADS].set(v)
    tok = pl.BlockSpec((n, D_MODEL), lambda i: (0, 0))
    small = pl.BlockSpec((n, LANES), lambda i: (0, 0))
    return pl.pallas_call(
        _gdn_prep1_body,
        grid=(1,),
        in_specs=[
            pl.BlockSpec((n, QKV_W), lambda i: (0, 0)),
            pl.BlockSpec((CONV_TAPS - 1, n, QKV_W), lambda i: (0, 0, 0)),
            pl.BlockSpec((n, LANES), lambda i: (0, 0)),
            pl.BlockSpec((CONV_TAPS, QKV_W), lambda i: (0, 0)),
            pl.BlockSpec((1, LANES), lambda i: (0, 0)),
            pl.BlockSpec((1, LANES), lambda i: (0, 0)),
        ],
        out_specs=[tok, tok, tok, small, small],
        out_shape=[jax.ShapeDtypeStruct((n, D_MODEL), F32)] * 3 + [jax.ShapeDtypeStruct((n, LANES), F32)] * 2,
        compiler_params=_params(("arbitrary",)),
        name="gdn_prep1",
    )(proj, conv_state, ab, conv_w.astype(F32), row(a_log), row(dt_bias))


def _gdn_step_body(qt_ref, kt_ref, v_ref, eg_ref, beta_ref, s_ref, o_ref, s_out_ref, *, first):
    def compute():
        for b in range(s_ref.shape[0]):
            for h in range(GDN_HEADS):
                k_col = kt_ref[b, :, h:h + 1]
                q_col = qt_ref[b, :, h:h + 1]
                s = s_ref[b, h] * eg_ref[b, h:h + 1, :]
                kv_mem = jnp.sum(k_col * s, axis=0, keepdims=True)
                delta = (v_ref[b, h:h + 1, :] - kv_mem) * beta_ref[b, h:h + 1, :]
                s = s + k_col * delta
                s_out_ref[b, h] = s
                o_ref[b, h:h + 1, :] = jnp.sum(q_col * s, axis=0, keepdims=True)

    if first:
        pl.when(pl.program_id(1) == 0)(compute)

        @pl.when(pl.program_id(1) > 0)
        def _():
            s_out_ref[...] = s_ref[...]
    else:
        compute()


def _gdn_step(qt, kt, v, eg, beta, states, layer, carry, sb=4):
    nl, n = states.shape[:2]
    first = carry is None
    assert first == (layer == 0)
    blk = (None, sb, GDN_HEADS, GDN_DK, GDN_DK)
    if first:
        grid = (n // sb, nl)
        vec = lambda i, j: (i, 0, 0)
        st = pl.BlockSpec(blk, lambda i, j: (j, i, 0, 0, 0))
        sem = ("parallel", "arbitrary")
    else:
        grid = (n // sb,)
        vec = lambda i: (i, 0, 0)
        st = pl.BlockSpec(blk, lambda i: (layer, i, 0, 0, 0))
        sem = ("parallel",)
    col = pl.BlockSpec((sb, GDN_DK, GDN_HEADS), vec)
    row = pl.BlockSpec((sb, GDN_HEADS, GDN_DK), vec)
    return pl.pallas_call(
        functools.partial(_gdn_step_body, first=first),
        grid=grid,
        in_specs=[col, col, row, row, row, st],
        out_specs=[row, st],
        out_shape=[jax.ShapeDtypeStruct((n, GDN_HEADS, GDN_DK), F32),
                   jax.ShapeDtypeStruct(states.shape, F32)],
        input_output_aliases={} if first else {5: 1},
        compiler_params=_params(sem),
        name="gdn_step",
    )(qt, kt, v, eg, beta, states if first else carry)


def _gdn_sample_layer(h, l, p, w, conv_state, delta_states, carry):
    n = h.shape[0]
    proj, ab = _mm(h, w['a_w_in'][l], gain=p['a_norm'][l], w_aux=w['a_w_ab'][l])
    conv_t = jnp.transpose(conv_state, (1, 0, 2))
    q, k, v, eg, beta = _gdn_prep1(proj, ab, conv_t, p['a_conv_w'][l], p['a_A_log'][l], p['a_dt_bias'][l])
    heads = lambda a: a.reshape(n, GDN_HEADS, GDN_DK)
    cols = lambda a: jnp.transpose(heads(a), (0, 2, 1))
    lanes = lambda a: jnp.broadcast_to(a[:, :GDN_HEADS, None], (n, GDN_HEADS, GDN_DK))
    o, s_new = _gdn_step(cols(q), cols(k), heads(v), lanes(eg), lanes(beta), delta_states, l, carry)
    h_new = _gdn_out(o.reshape(n, D_MODEL), proj, p['a_o_norm'][l], w['a_w_out'][l], h)
    conv_new = jnp.concatenate([conv_state[:, 1:], proj[:, None, :QKV_W]], axis=1)
    return h_new, s_new, conv_new


def _rope_tables(pos):
    half = ROT // 2
    inv = ROPE_THETA ** (-jnp.arange(0, ROT, 2, dtype=F32) / ROT)
    ang = pos.astype(F32)[:, None] * inv[None, :]
    cos, sin = jnp.cos(ang), jnp.sin(ang)
    ones = jnp.ones((pos.shape[0], HEAD_DIM - ROT), F32)
    zeros = jnp.zeros_like(ones)
    z8 = jnp.zeros_like(sin)
    c = jnp.concatenate([cos, cos, ones], axis=1)
    sm = jnp.concatenate([-sin, z8, zeros], axis=1)
    sp = jnp.concatenate([z8, sin, zeros], axis=1)
    del half
    return tuple(jnp.tile(a, (1, LANES // HEAD_DIM)) for a in (c, sm, sp))


def _dot_split(a, sel):
    hi = a.astype(BF16)
    lo = (a - hi.astype(F32)).astype(BF16)
    return (jnp.dot(hi, sel, preferred_element_type=F32) + jnp.dot(lo, sel, preferred_element_type=F32))


def _headnorm_rope_body(x_ref, g_ref, seg_ref, segt_ref, c_ref, sm_ref, sp_ref, o_ref):
    x = x_ref[...]
    w = x.shape[1]
    ms = _dot_split(x * x, seg_ref[...]) * (1.0 / HEAD_DIM)
    scale = _dot_split(lax.rsqrt(ms + EPS), segt_ref[...])
    y = x * scale * g_ref[...]
    reps = w // LANES
    tile = lambda a: jnp.concatenate([a] * reps, axis=1) if reps > 1 else a
    half = ROT // 2
    out = (y * tile(c_ref[...]) + pltpu.roll(y, w - half, axis=1) * tile(sm_ref[...])
           + pltpu.roll(y, half, axis=1) * tile(sp_ref[...]))
    o_ref[...] = out.astype(o_ref.dtype)


def _headnorm_rope(x, col_block, width, gain, tables, rows_per_seq, out_dtype, tm=512):
    m = x.shape[0]
    tm = min(tm, m, rows_per_seq)
    nheads = width // HEAD_DIM
    seg = (jnp.arange(width)[:, None] // HEAD_DIM == jnp.arange(LANES)[None, :]).astype(BF16)
    per_seq = rows_per_seq // tm
    tab = pl.BlockSpec((tm, LANES), lambda i: (i % per_seq, 0))
    del nheads
    return pl.pallas_call(
        _headnorm_rope_body,
        grid=(m // tm,),
        in_specs=[
            pl.BlockSpec((tm, width), lambda i: (i, col_block)),
            pl.BlockSpec((1, width), lambda i: (0, 0)),
            pl.BlockSpec((width, LANES), lambda i: (0, 0)),
            pl.BlockSpec((LANES, width), lambda i: (0, 0)),
            tab, tab, tab,
        ],
        out_specs=pl.BlockSpec((tm, width), lambda i: (i, 0)),
        out_shape=jax.ShapeDtypeStruct((m, width), out_dtype),
        compiler_params=_params(("parallel",)),
        name="headnorm_rope",
    )(x, jnp.tile(gain.astype(F32), width // HEAD_DIM).reshape(1, width), seg, seg.T, *tables)


def _softmax_sink(s, sink):
    m = jnp.maximum(jnp.max(s, axis=-1, keepdims=True), sink)
    p = jnp.exp(s - m)
    return p, jnp.sum(p, axis=-1, keepdims=True) + jnp.exp(sink - m)


def _swa_prompt_body(sink_ref, q_ref, kp_ref, kc_ref, vp_ref, vc_ref, o_ref):
    i = pl.program_id(1)
    blk = q_ref.shape[0]
    r = lax.broadcasted_iota(I32, (blk, 2 * blk), 0)
    c = lax.broadcasted_iota(I32, (blk, 2 * blk), 1)
    valid = (c >= r) & (c <= r + WINDOW) & ((c >= blk) | (i > 0))
    for kvh in range(KV_HEADS):
        cols = slice(kvh * HEAD_DIM, (kvh + 1) * HEAD_DIM)
        kk = jnp.concatenate([kp_ref[:, cols], kc_ref[:, cols]], axis=0).astype(BF16)
        vv = jnp.concatenate([vp_ref[:, cols], vc_ref[:, cols]], axis=0).astype(BF16)
        for g in range(ATT_GROUP):
            h = kvh * ATT_GROUP + g
            hc = slice(h * HEAD_DIM, (h + 1) * HEAD_DIM)
            s = lax.dot_general(q_ref[:, hc], kk, (((1,), (1,)), ((), ())),
                                preferred_element_type=F32) * (HEAD_DIM ** -0.5)
            s = jnp.where(valid, s, -jnp.inf)
            p, den = _softmax_sink(s, sink_ref[h])
            o_ref[:, hc] = jnp.dot((p / den).astype(BF16), vv, preferred_element_type=F32).astype(o_ref.dtype)


def _swa_prompt(q, k, kv, sinks):
    b, t, _ = q.shape
    blk = WINDOW
    kvw = KV_HEADS * HEAD_DIM
    prev = lambda bi, i, s: (bi, jnp.maximum(i - 1, 0), 0)
    cur = lambda bi, i, s: (bi, i, 0)
    prev_v = lambda bi, i, s: (bi, jnp.maximum(i - 1, 0), 1)
    cur_v = lambda bi, i, s: (bi, i, 1)
    grid_spec = pltpu.PrefetchScalarGridSpec(
        num_scalar_prefetch=1,
        grid=(b, t // blk),
        in_specs=[
            pl.BlockSpec((None, blk, D_MODEL), cur),
            pl.BlockSpec((None, blk, kvw), prev),
            pl.BlockSpec((None, blk, kvw), cur),
            pl.BlockSpec((None, blk, kvw), prev_v),
            pl.BlockSpec((None, blk, kvw), cur_v),
        ],
        out_specs=pl.BlockSpec((None, blk, D_MODEL), cur),
    )
    return pl.pallas_call(
        _swa_prompt_body,
        grid_spec=grid_spec,
        out_shape=jax.ShapeDtypeStruct((b, t, D_MODEL), BF16),
        compiler_params=_params(("parallel", "parallel")),
        name="swa_prompt",
    )(sinks.astype(F32), q, k, k, kv, kv)


def _swa_sample_body(sink_ref, q_ref, kc_ref, kn_ref, vc_ref, vn_ref, o_ref):
    wc = kc_ref.shape[1]
    kvw = kc_ref.shape[2]
    nkeys = 2 * wc
    fill = jnp.zeros((nkeys - wc - 8, kvw), F32)
    key_idx = lax.broadcasted_iota(I32, (ATT_HEADS, nkeys), 1)
    sink = sink_ref[:, 0:1]
    seqs = range(q_ref.shape[0])
    scores = []
    for b in seqs:
        kk = jnp.concatenate([kc_ref[b], kn_ref[b], fill], axis=0)
        scores.append(lax.dot_general(q_ref[b], kk, (((1,), (1,)), ((), ())), precision=HI,
                                      preferred_element_type=F32))
    probs = []
    for s in scores:
        s = jnp.where(key_idx <= wc, s * (HEAD_DIM ** -0.5), -jnp.inf)
        p, den = _softmax_sink(s, sink)
        probs.append(p / den)
    outs = []
    for b, p in zip(seqs, probs):
        vv = jnp.concatenate([vc_ref[b], vn_ref[b], fill], axis=0)
        outs.append(jnp.dot(p, vv, precision=HI, preferred_element_type=F32))
    for b, o in zip(seqs, outs):
        for kvh in range(KV_HEADS):
            heads = slice(kvh * ATT_GROUP, (kvh + 1) * ATT_GROUP)
            o_ref[b, heads, :] = o[heads, kvh * HEAD_DIM:(kvh + 1) * HEAD_DIM]


def _swa_sample(q, k_cache, k_new, v_cache, v_new, sinks, sb=8):
    n = q.shape[0]
    wc = k_cache.shape[1]
    kvw = KV_HEADS * HEAD_DIM
    own = (jnp.arange(ATT_HEADS)[:, None] // ATT_GROUP == jnp.arange(KV_HEADS)[None, :])
    q_rows = (q[:, :, None, :] * own[None, :, :, None].astype(q.dtype)).reshape(n, ATT_HEADS, kvw)
    pad8 = lambda a: jnp.pad(a[:, None, :], ((0, 0), (0, 7), (0, 0)))
    blk = lambda shape: pl.BlockSpec(shape, lambda i: (i, 0, 0))
    return pl.pallas_call(
        _swa_sample_body,
        grid=(n // sb,),
        in_specs=[pl.BlockSpec((ATT_HEADS, LANES), lambda i: (0, 0)),
                  blk((sb, ATT_HEADS, kvw)), blk((sb, wc, kvw)), blk((sb, 8, kvw)),
                  blk((sb, wc, kvw)), blk((sb, 8, kvw))],
        out_specs=blk((sb, ATT_HEADS, HEAD_DIM)),
        out_shape=jax.ShapeDtypeStruct((n, ATT_HEADS, HEAD_DIM), F32),
        compiler_params=_params(("parallel",)),
        name="swa_sample",
    )(jnp.broadcast_to(sinks.astype(F32)[:, None], (ATT_HEADS, LANES)), q_rows, k_cache, pad8(k_new),
      v_cache, pad8(v_new))


def _channel_mixer(hf, l, p, w):
    if l % 2 == 0:
        return _ffn_dense(hf, p['ffn_norm'][l], w['dense_w_gu'][l // 2], w['dense_w_down'][l // 2])
    return _moe(hf, p['ffn_norm'][l], p['moe_router'][l // 2], p['moe_router_b'][l // 2],
                w['moe_w_gu'][l // 2], w['moe_w_down'][l // 2])


def _run_prompt(x, p, w):
    b, t, _ = x.shape
    n = b * t
    h = x
    convs, deltas = [], []
    for l in range(N_GDN):
        h, s_new, c_new = _gdn_prompt_layer(h, l, p, w)
        convs.append(c_new)
        deltas.append(s_new)
        h = _channel_mixer(h.reshape(n, D_MODEL), l, p, w).reshape(b, t, D_MODEL)
    hf = h.reshape(n, D_MODEL)
    tables = _rope_tables(jnp.arange(t, dtype=I32))
    kvw = KV_HEADS * HEAD_DIM
    kv = _mm(hf, w['w_kv'], gain=p['kv_norm'], tn=2 * kvw)
    k_sh = _headnorm_rope(kv, 0, kvw, p['k_norm'], tables, t, F32)
    k3 = k_sh.reshape(b, t, kvw)
    kv3 = kv.reshape(b, t, 2 * kvw)
    for j in range(2):
        l = N_GDN + j
        qp = _mm(hf, w['b_w_q'][j], gain=p['b_norm'][j])
        q = _headnorm_rope(qp, 0, D_MODEL, p['b_q_norm'][j], tables, t, BF16)
        o = _swa_prompt(q.reshape(b, t, D_MODEL), k3, kv3, p['b_sinks'][j])
        hf = _mm(o.reshape(n, D_MODEL), w['b_w_o'][j], res=hf)
        hf = _channel_mixer(hf, l, p, w)
    k_win = k3[:, t - WINDOW:].reshape(b, WINDOW, KV_HEADS, HEAD_DIM)
    v_win = kv3[:, t - WINDOW:, kvw:].reshape(b, WINDOW, KV_HEADS, HEAD_DIM)
    return hf.reshape(b, t, D_MODEL), jnp.stack(convs), jnp.stack(deltas), k_win, v_win


def _run_sample(x, conv_state, delta_state, k_cache, v_cache, p, w):
    n = x.shape[0]
    h = x.reshape(n, D_MODEL)
    convs, deltas = [], None
    for l in range(N_GDN):
        h, deltas, c_new = _gdn_sample_layer(h, l, p, w, conv_state[l], delta_state, deltas)
        convs.append(c_new)
        h = _channel_mixer(h, l, p, w)
    tables = _rope_tables(jnp.full((n,), PAST_LEN, I32))
    kvw = KV_HEADS * HEAD_DIM
    wc = k_cache.shape[1]
    kv = _mm(h, w['w_kv'], gain=p['kv_norm'], tn=2 * kvw)
    k_new = _headnorm_rope(kv, 0, kvw, p['k_norm'], tables, n, F32)
    v_new = kv[:, kvw:]
    kc = k_cache.reshape(n, wc, kvw)
    vc = v_cache.reshape(n, wc, kvw)
    for j in range(2):
        l = N_GDN + j
        qp = _mm(h, w['b_w_q'][j], gain=p['b_norm'][j])
        q = _headnorm_rope(qp, 0, D_MODEL, p['b_q_norm'][j], tables, n, F32)
        o = _swa_sample(q.reshape(n, ATT_HEADS, HEAD_DIM), kc, k_new, vc, v_new, p['b_sinks'][j])
        h = _mm(o.reshape(n, D_MODEL), w['b_w_o'][j], res=h)
        h = _channel_mixer(h, l, p, w)
    k_win = jnp.concatenate([kc, k_new[:, None, :]], axis=1)[:, -wc:].reshape(n, wc, KV_HEADS, HEAD_DIM)
    v_win = jnp.concatenate([vc, v_new[:, None, :]], axis=1)[:, -wc:].reshape(n, wc, KV_HEADS, HEAD_DIM)
    return h.reshape(n, 1, D_MODEL), jnp.stack(convs), deltas, k_win, v_win


def kernel(x_prompt, x_sample, state_conv, state_delta, cache_k_win, cache_v_win, a_norm, a_w_in, a_conv_w,
           a_A_log, a_dt_bias, a_o_norm, a_w_out, kv_norm, w_kv, k_norm, b_norm, b_w_q, b_q_norm, b_sinks,
           b_w_o, ffn_norm, dense_w_gu, dense_w_down, moe_router, moe_router_b, moe_w_gu, moe_w_down):
    p = dict(a_norm=a_norm, a_conv_w=a_conv_w, a_A_log=a_A_log, a_dt_bias=a_dt_bias, a_o_norm=a_o_norm,
             kv_norm=kv_norm, k_norm=k_norm, b_norm=b_norm, b_q_norm=b_q_norm, b_sinks=b_sinks,
             ffn_norm=ffn_norm, moe_router=moe_router, moe_router_b=moe_router_b)
    n_proj = PROJ_W
    gate_cols = jnp.pad(a_w_in[:, :, n_proj:], ((0, 0), (0, 0), (0, LANES - (a_w_in.shape[-1] - n_proj))))
    raw = dict(a_w_in=a_w_in[:, :, :n_proj], a_w_ab=gate_cols, a_w_out=a_w_out, w_kv=w_kv[None],
               b_w_q=b_w_q, b_w_o=b_w_o, dense_w_gu=dense_w_gu, dense_w_down=dense_w_down,
               moe_w_gu=moe_w_gu, moe_w_down=moe_w_down)
    w_fast = {name: [val[l].astype(BF16) for l in range(val.shape[0])] for name, val in raw.items()}
    w_precise = {name: [_Layer(val, l) for l in range(val.shape[0])] for name, val in raw.items()}
    w_precise['a_w_in'] = [_Layer(a_w_in, l, cols=n_proj) for l in range(a_w_in.shape[0])]
    w_fast['w_kv'], w_precise['w_kv'] = w_fast['w_kv'][0], w_kv
    y_p, conv_p, delta_p, kw_p, vw_p = _run_prompt(x_prompt, p, w_fast)
    y_s, conv_s, delta_s, kw_s, vw_s = _run_sample(
        x_sample, state_conv, state_delta, cache_k_win, cache_v_win, p, w_precise)
    return (y_p, y_s, conv_p, conv_s, delta_p, delta_s, kw_p, kw_s, vw_p, vw_s)
```

```python
import functools

import jax
import jax.numpy as jnp
from jax import lax
from jax.experimental import pallas as pl
from jax.experimental.pallas import tpu as pltpu

F32, BF16, I32 = jnp.float32, jnp.bfloat16, jnp.int32
HI = lax.Precision.HIGHEST

D_MODEL = 1024
N_GDN = 2
GDN_HEADS = 8
GDN_DK = 128
CONV_TAPS = 4
CHUNK = 64
QKV_W = 3 * GDN_HEADS * GDN_DK
PROJ_W = QKV_W + D_MODEL
ATT_HEADS = 16
KV_HEADS = 4
HEAD_DIM = 64
ATT_GROUP = ATT_HEADS // KV_HEADS
WINDOW = 128
ROT = HEAD_DIM // 4
ROPE_THETA = 500000.0
PAST_LEN = 8192
N_EXPERTS = 8
EPS = 1e-6

LANES = 128
VMEM_LIMIT = 56 * 1024 * 1024
NEG = -1e30


def _params(sem):
    return pltpu.CompilerParams(dimension_semantics=sem, vmem_limit_bytes=VMEM_LIMIT)


def _rms(x, g):
    return x * lax.rsqrt(jnp.mean(x * x, axis=-1, keepdims=True) + EPS) * g


def _silu(x):
    return x * jax.nn.sigmoid(x)


def _split(x):
    hi = x.astype(BF16)
    return hi, (x - hi.astype(F32)).astype(BF16)


def _dot3(xh, xl, w):
    wh, wl = _split(w)
    return (jnp.dot(xh, wh, preferred_element_type=F32)
            + (jnp.dot(xl, wh, preferred_element_type=F32) + jnp.dot(xh, wl, preferred_element_type=F32)))


class _Layer:
    def __init__(self, arr, layer, cols=None):
        self.arr, self.lead, self.dtype = arr, (layer,), arr.dtype
        self.shape = arr.shape[1:] if cols is None else arr.shape[1:-1] + (cols,)


def _warr(w):
    return w.arr if isinstance(w, _Layer) else w


def _wspec(w, block, index_fn, **kwargs):
    lead = w.lead if isinstance(w, _Layer) else ()
    return pl.BlockSpec((None,) * len(lead) + tuple(block), lambda *a: lead + tuple(index_fn(*a)), **kwargs)


def _is_precise(w):
    assert w.dtype in (BF16, F32)
    return w.dtype == F32


def _mm_body(*refs, has_norm, has_aux, has_res, precise):
    it = iter(refs)
    x_ref = next(it)
    g_ref = next(it) if has_norm else None
    w_ref = next(it)
    wa_ref = next(it) if has_aux else None
    r_ref = next(it) if has_res else None
    o_ref = next(it)
    oa_ref = next(it) if has_aux else None
    xn_ref = next(it)

    def product(w):
        if precise:
            return _dot3(xn_ref[0], xn_ref[1], w)
        return jnp.dot(xn_ref[0], w, preferred_element_type=F32)

    @pl.when(pl.program_id(1) == 0)
    def _():
        x = x_ref[...].astype(F32)
        if has_norm:
            x = _rms(x, g_ref[...])
        if precise:
            xn_ref[0], xn_ref[1] = _split(x)
        else:
            xn_ref[0] = x.astype(BF16)
        if has_aux:
            oa_ref[...] = product(wa_ref[...])

    acc = product(w_ref[...])
    if has_res:
        acc = acc + r_ref[...]
    o_ref[...] = acc.astype(o_ref.dtype)


def _mm(x, w, gain=None, res=None, w_aux=None, tm=1024, tn=1024, out_dtype=F32):
    precise = _is_precise(w)
    m, k = x.shape
    n = w.shape[1]
    tm = min(tm, m)
    tn = min(tn, n)
    assert m % tm == 0 and n % tn == 0
    ins = [x]
    specs = [pl.BlockSpec((tm, k), lambda i, j: (i, 0))]
    if gain is not None:
        ins.append(gain.reshape(1, k).astype(F32))
        specs.append(pl.BlockSpec((1, k), lambda i, j: (0, 0)))
    ins.append(_warr(w))
    specs.append(_wspec(w, (k, tn), lambda i, j: (0, j)))
    out_specs = [pl.BlockSpec((tm, tn), lambda i, j: (i, j))]
    out_shape = [jax.ShapeDtypeStruct((m, n), out_dtype)]
    if w_aux is not None:
        na = w_aux.shape[1]
        ins.append(_warr(w_aux))
        specs.append(_wspec(w_aux, (k, na), lambda i, j: (0, 0)))
        out_specs.append(pl.BlockSpec((tm, na), lambda i, j: (i, 0)))
        out_shape.append(jax.ShapeDtypeStruct((m, na), F32))
    if res is not None:
        ins.append(res)
        specs.append(pl.BlockSpec((tm, tn), lambda i, j: (i, j)))
    out = pl.pallas_call(
        functools.partial(_mm_body, has_norm=gain is not None, has_aux=w_aux is not None,
                          has_res=res is not None, precise=precise),
        grid=(m // tm, n // tn),
        in_specs=specs,
        out_specs=out_specs,
        out_shape=out_shape,
        scratch_shapes=[pltpu.VMEM((1 + precise, tm, k), BF16)],
        compiler_params=_params(("parallel", "arbitrary")),
        name="mm",
    )(*ins)
    return out if w_aux is not None else out[0]


def _swiglu(xh, xl, wg, wu, wd):
    if xl is None:
        gate = jnp.dot(xh, wg, preferred_element_type=F32)
        up = jnp.dot(xh, wu, preferred_element_type=F32)
        return jnp.dot((_silu(gate) * up).astype(BF16), wd, preferred_element_type=F32)
    gate = _dot3(xh, xl, wg)
    up = _dot3(xh, xl, wu)
    ah, al = _split(_silu(gate) * up)
    return _dot3(ah, al, wd)


def _ffn_dense_body(x_ref, g_ref, wg_ref, wu_ref, wd_ref, o_ref, xn_ref, acc_ref, *, precise):
    f = pl.program_id(1)

    @pl.when(f == 0)
    def _():
        xn = _rms(x_ref[...], g_ref[...])
        if precise:
            xn_ref[0], xn_ref[1] = _split(xn)
        else:
            xn_ref[0] = xn.astype(BF16)
        acc_ref[...] = x_ref[...]

    acc_ref[...] += _swiglu(xn_ref[0], xn_ref[1] if precise else None, wg_ref[...], wu_ref[...], wd_ref[...])

    @pl.when(f == pl.num_programs(1) - 1)
    def _():
        o_ref[...] = acc_ref[...]


def _ffn_dense(x, gain, w_gu, w_down, tm=512):
    precise = _is_precise(w_gu)
    m, k = x.shape
    ff = w_down.shape[0]
    tm = min(tm, m)
    nf = 1 if not precise else ff // (2 * LANES)
    tf = ff // nf
    assert tf % LANES == 0 and tf * nf == ff
    resident = {} if precise else dict(pipeline_mode=pl.Buffered(1))
    return pl.pallas_call(
        functools.partial(_ffn_dense_body, precise=precise),
        grid=(m // tm, nf),
        in_specs=[
            pl.BlockSpec((tm, k), lambda i, f: (i, 0)),
            pl.BlockSpec((1, k), lambda i, f: (0, 0)),
            _wspec(w_gu, (k, tf), lambda i, f: (0, f), **resident),
            _wspec(w_gu, (k, tf), lambda i, f: (0, nf + f), **resident),
            _wspec(w_down, (tf, k), lambda i, f: (f, 0), **resident),
        ],
        out_specs=pl.BlockSpec((tm, k), lambda i, f: (i, 0)),
        out_shape=jax.ShapeDtypeStruct((m, k), F32),
        scratch_shapes=[pltpu.VMEM((1 + precise, tm, k), BF16), pltpu.VMEM((tm, k), F32)],
        compiler_params=_params(("parallel", "arbitrary")),
        name="ffn_dense",
    )(x, gain.reshape(1, k).astype(F32), _warr(w_gu), _warr(w_gu), _warr(w_down))


def _router_body(x_ref, g_ref, wr_ref, br_ref, idx_ref, gate_ref, cnt_ref, carry_ref):
    i = pl.program_id(0)

    @pl.when(i == 0)
    def _():
        carry_ref[...] = jnp.zeros_like(carry_ref)

    tm = x_ref.shape[0]
    xn = _rms(x_ref[...], g_ref[...])
    logits = _dot3(*_split(xn), wr_ref[...]) + br_ref[...]
    lane = lax.broadcasted_iota(I32, logits.shape, 1)
    m1 = jnp.max(logits, axis=-1, keepdims=True)
    i1 = jnp.min(jnp.where(logits == m1, lane, LANES), axis=-1, keepdims=True)
    rest = jnp.where(lane == i1, NEG, logits)
    m2 = jnp.max(rest, axis=-1, keepdims=True)
    i2 = jnp.min(jnp.where(rest == m2, lane, LANES), axis=-1, keepdims=True)
    e2 = jnp.exp(m2 - m1)
    g1 = 1.0 / (1.0 + e2)
    g2 = e2 * g1
    oh1 = lane == i1
    oh2 = lane == i2
    oh = jnp.where(oh1 | oh2, 1.0, 0.0).astype(BF16)
    r = lax.broadcasted_iota(I32, (tm, tm), 0)
    c = lax.broadcasted_iota(I32, (tm, tm), 1)
    below = jnp.where(c < r, 1.0, 0.0).astype(BF16)
    before = jnp.dot(below, oh, preferred_element_type=F32) + carry_ref[0:1, :]
    rk1 = jnp.sum(jnp.where(oh1, before, 0.0), axis=-1, keepdims=True).astype(I32)
    rk2 = jnp.sum(jnp.where(oh2, before, 0.0), axis=-1, keepdims=True).astype(I32)
    idx_ref[...] = jnp.where(lane == 0, i1, jnp.where(lane == 1, i2, jnp.where(lane == 2, rk1, rk2)))
    gate_ref[...] = jnp.where(lane == 0, g1, g2)
    total = carry_ref[0:1, :] + jnp.sum(oh.astype(F32), axis=0, keepdims=True)
    carry_ref[...] = jnp.broadcast_to(total, carry_ref.shape)
    cnt_ref[...] = carry_ref[...]


def _router(x, gain, w_router, b_router, tm=512):
    m, k = x.shape
    tm = min(tm, m)
    wr = jnp.zeros((k, LANES), F32).at[:, :N_EXPERTS].set(w_router)
    br = jnp.full((1, LANES), NEG, F32).at[0, :N_EXPERTS].set(b_router)
    return pl.pallas_call(
        _router_body,
        grid=(m // tm,),
        in_specs=[
            pl.BlockSpec((tm, k), lambda i: (i, 0)),
            pl.BlockSpec((1, k), lambda i: (0, 0)),
            pl.BlockSpec((k, LANES), lambda i: (0, 0)),
            pl.BlockSpec((1, LANES), lambda i: (0, 0)),
        ],
        out_specs=[
            pl.BlockSpec((tm, LANES), lambda i: (i, 0)),
            pl.BlockSpec((tm, LANES), lambda i: (i, 0)),
            pl.BlockSpec((8, LANES), lambda i: (0, 0)),
        ],
        out_shape=[
            jax.ShapeDtypeStruct((m, LANES), I32),
            jax.ShapeDtypeStruct((m, LANES), F32),
            jax.ShapeDtypeStruct((8, LANES), F32),
        ],
        scratch_shapes=[pltpu.VMEM((8, LANES), F32)],
        compiler_params=_params(("arbitrary",)),
        name="moe_router",
    )(x, gain.reshape(1, k).astype(F32), wr, br)


def _row_copy(src, dst, src_row, dst_row, sem):
    return pltpu.make_async_copy(src.at[pl.ds(src_row, 1)], dst.at[pl.ds(dst_row, 1)], sem)


def _scatter_body(d_ref, x_ref, g_ref, xs_in_ref, xs_ref, buf_ref, sem):
    del xs_in_ref
    tm = buf_ref.shape[0]
    buf_ref[...] = _rms(x_ref[...], g_ref[...])

    def issue(t, carry):
        for kk in range(2):
            _row_copy(buf_ref, xs_ref, t, d_ref[0, 0, 2 * t + kk], sem).start()
        return carry

    lax.fori_loop(0, tm, issue, 0, unroll=8)
    for kk in range(2):
        pltpu.make_async_copy(buf_ref, xs_ref.at[pl.ds(0, tm)], sem).wait()


def _moe_scatter(x, gain, dest_tiles, rows, tm):
    m, k = x.shape
    return pl.pallas_call(
        _scatter_body,
        grid=(m // tm,),
        in_specs=[
            pl.BlockSpec((1, 1, 2 * tm), lambda i: (i, 0, 0), memory_space=pltpu.SMEM),
            pl.BlockSpec((tm, k), lambda i: (i, 0)),
            pl.BlockSpec((1, k), lambda i: (0, 0)),
            pl.BlockSpec(memory_space=pl.ANY),
        ],
        out_specs=pl.BlockSpec(memory_space=pl.ANY),
        out_shape=jax.ShapeDtypeStruct((rows, k), F32),
        scratch_shapes=[pltpu.VMEM((tm, k), F32), pltpu.SemaphoreType.DMA(())],
        input_output_aliases={3: 0},
        compiler_params=_params(("arbitrary",)),
        name="moe_scatter",
    )(dest_tiles, x, gain.reshape(1, k).astype(F32), jnp.zeros((rows, k), F32))


def _ffn_group_body(te_ref, nu_ref, x_ref, wg_ref, wu_ref, wd_ref, o_ref, *, precise):
    del te_ref
    used = pl.program_id(0) < nu_ref[0]

    @pl.when(used)
    def _():
        xh, xl = _split(x_ref[...]) if precise else (x_ref[...].astype(BF16), None)
        o_ref[...] = _swiglu(xh, xl, wg_ref[...], wu_ref[...], wd_ref[...])

    @pl.when(jnp.logical_not(used))
    def _():
        o_ref[...] = jnp.zeros_like(o_ref)


def _ffn_grouped(xs, tile_expert, n_used, w_gu, w_down, tm):
    precise = _is_precise(w_gu)
    rows, k = xs.shape
    ff = w_down.shape[1]
    nt = rows // tm

    def row_map(i, te, nu):
        return (jnp.minimum(i, nu[0] - 1), 0)

    grid_spec = pltpu.PrefetchScalarGridSpec(
        num_scalar_prefetch=2,
        grid=(nt,),
        in_specs=[
            pl.BlockSpec((tm, k), row_map),
            _wspec(w_gu, (None, k, ff), lambda i, te, nu: (te[i], 0, 0)),
            _wspec(w_gu, (None, k, ff), lambda i, te, nu: (te[i], 0, 1)),
            _wspec(w_down, (None, ff, k), lambda i, te, nu: (te[i], 0, 0)),
        ],
        out_specs=pl.BlockSpec((tm, k), lambda i, te, nu: (i, 0)),
    )
    return pl.pallas_call(
        functools.partial(_ffn_group_body, precise=precise),
        grid_spec=grid_spec,
        out_shape=jax.ShapeDtypeStruct((rows, k), F32),
        compiler_params=_params(("arbitrary",)),
        name="moe_experts",
    )(tile_expert, n_used, xs, _warr(w_gu), _warr(w_gu), _warr(w_down))


def _combine_body(d_ref, x_ref, gate_ref, ys_ref, o_ref, buf_ref, sem):
    tm = x_ref.shape[0]

    def issue(t, carry):
        for kk in range(2):
            _row_copy(ys_ref, buf_ref.at[kk], d_ref[0, 0, 2 * t + kk], t, sem).start()
        return carry

    lax.fori_loop(0, tm, issue, 0, unroll=8)
    for kk in range(2):
        pltpu.make_async_copy(ys_ref.at[pl.ds(0, tm)], buf_ref.at[kk], sem).wait()
    g = gate_ref[...]
    o_ref[...] = x_ref[...] + g[:, 0:1] * buf_ref[0] + g[:, 1:2] * buf_ref[1]


def _moe_combine(x, gates, dest_tiles, ys, tm):
    m, k = x.shape
    return pl.pallas_call(
        _combine_body,
        grid=(m // tm,),
        in_specs=[
            pl.BlockSpec((1, 1, 2 * tm), lambda i: (i, 0, 0), memory_space=pltpu.SMEM),
            pl.BlockSpec((tm, k), lambda i: (i, 0)),
            pl.BlockSpec((tm, LANES), lambda i: (i, 0)),
            pl.BlockSpec(memory_space=pl.ANY),
        ],
        out_specs=pl.BlockSpec((tm, k), lambda i: (i, 0)),
        out_shape=jax.ShapeDtypeStruct((m, k), F32),
        scratch_shapes=[pltpu.VMEM((2, tm, k), F32), pltpu.SemaphoreType.DMA(())],
        compiler_params=_params(("arbitrary",)),
        name="moe_combine",
    )(dest_tiles, x, gates, ys)


SEG_TOK = 256
SEG_ROWS = 640
EXPERT_TILE = 512


def _router_seg_body(x_ref, g_ref, wr_ref, br_ref, idx_ref, gate_ref, tab_ref, cnt_ref, carry_ref):
    i = pl.program_id(0)

    @pl.when(i == 0)
    def _():
        carry_ref[...] = jnp.zeros_like(carry_ref)

    tm = x_ref.shape[0]
    xn = _rms(x_ref[...], g_ref[...])
    logits = _dot3(*_split(xn), wr_ref[...]) + br_ref[...]
    lane = lax.broadcasted_iota(I32, logits.shape, 1)
    m1 = jnp.max(logits, axis=-1, keepdims=True)
    i1 = jnp.min(jnp.where(logits == m1, lane, LANES), axis=-1, keepdims=True)
    rest = jnp.where(lane == i1, NEG, logits)
    m2 = jnp.max(rest, axis=-1, keepdims=True)
    i2 = jnp.min(jnp.where(rest == m2, lane, LANES), axis=-1, keepdims=True)
    e2 = jnp.exp(m2 - m1)
    g1 = 1.0 / (1.0 + e2)
    g2 = e2 * g1
    oh1 = lane == i1
    oh2 = lane == i2
    oh = jnp.where(oh1 | oh2, 1.0, 0.0)
    r = lax.broadcasted_iota(I32, (tm, tm), 0)
    c = lax.broadcasted_iota(I32, (tm, tm), 1)
    below = jnp.where(c < r, 1.0, 0.0).astype(BF16)
    rank = jnp.dot(below, oh.astype(BF16), preferred_element_type=F32)
    count = jnp.sum(oh, axis=0, keepdims=True)
    padded = jnp.floor((count + 7.0) * 0.125) * 8.0
    er = lax.broadcasted_iota(I32, (LANES, LANES), 0)
    ec = lax.broadcasted_iota(I32, (LANES, LANES), 1)
    start = jnp.dot(jnp.broadcast_to(padded, (8, LANES)), jnp.where(er < ec, 1.0, 0.0),
                    precision=HI, preferred_element_type=F32)[0:1]
    slot = rank + start
    s1 = jnp.sum(jnp.where(oh1, slot, 0.0), axis=-1, keepdims=True).astype(I32)
    s2 = jnp.sum(jnp.where(oh2, slot, 0.0), axis=-1, keepdims=True).astype(I32)
    idx_ref[...] = jnp.where(lane == 0, i1, jnp.where(lane == 1, i2, jnp.where(lane == 2, s1, s2)))
    gate_ref[...] = jnp.where(lane == 0, g1, g2)
    sub = lax.broadcasted_iota(I32, (8, LANES), 0)
    rows = jnp.where(sub == 0, padded * 0.125, jnp.where(sub == 1, start, carry_ref[...]))
    tab_ref[...] = jnp.where(sub < 3, rows, 0.0).astype(I32)
    carry_ref[...] = carry_ref[...] + padded
    cnt_ref[...] = carry_ref[...]


def _router_seg(x, gain, w_router, b_router):
    m, k = x.shape
    tm = SEG_TOK
    wr = jnp.zeros((k, LANES), F32).at[:, :N_EXPERTS].set(w_router)
    br = jnp.full((1, LANES), NEG, F32).at[0, :N_EXPERTS].set(b_router)
    return pl.pallas_call(
        _router_seg_body,
        grid=(m // tm,),
        in_specs=[
            pl.BlockSpec((tm, k), lambda i: (i, 0)),
            pl.BlockSpec((1, k), lambda i: (0, 0)),
            pl.BlockSpec((k, LANES), lambda i: (0, 0)),
            pl.BlockSpec((1, LANES), lambda i: (0, 0)),
        ],
        out_specs=[
            pl.BlockSpec((tm, LANES), lambda i: (i, 0)),
            pl.BlockSpec((tm, LANES), lambda i: (i, 0)),
            pl.BlockSpec((None, 8, LANES), lambda i: (i, 0, 0)),
            pl.BlockSpec((8, LANES), lambda i: (0, 0)),
        ],
        out_shape=[
            jax.ShapeDtypeStruct((m, LANES), I32),
            jax.ShapeDtypeStruct((m, LANES), F32),
            jax.ShapeDtypeStruct((m // tm, 8, LANES), I32),
            jax.ShapeDtypeStruct((8, LANES), F32),
        ],
        scratch_shapes=[pltpu.VMEM((8, LANES), F32)],
        compiler_params=_params(("arbitrary",)),
        name="moe_router_seg",
    )(x, gain.reshape(1, k).astype(F32), wr, br)


def _chunk_copy(src, dst, src_row, dst_row, sem):
    rows = lambda r: pl.ds(r if isinstance(r, int) else pl.multiple_of(r, 8), 8)
    return pltpu.make_async_copy(src.at[rows(src_row)], dst.at[rows(dst_row)], sem)


def _segment_start(tab_ref, base_ref, local_ref, hbm_ref, sem, to_hbm):
    for e in range(N_EXPERTS):
        local0 = tab_ref[1, e]
        hbm0 = base_ref[e] + tab_ref[2, e]

        def issue(j, carry, local0=local0, hbm0=hbm0):
            if to_hbm:
                _chunk_copy(local_ref, hbm_ref, local0 + 8 * j, hbm0 + 8 * j, sem).start()
            else:
                _chunk_copy(hbm_ref, local_ref, hbm0 + 8 * j, local0 + 8 * j, sem).start()
            return carry

        lax.fori_loop(0, tab_ref[0, e], issue, 0)


def _segment_wait(tab_ref, local_ref, hbm_ref, sem, to_hbm):
    total = 0
    for e in range(N_EXPERTS):
        total = total + tab_ref[0, e]

    def drain(j, carry):
        if to_hbm:
            _chunk_copy(local_ref, hbm_ref, 0, 0, sem).wait()
        else:
            _chunk_copy(hbm_ref, local_ref, 0, 0, sem).wait()
        return carry

    lax.fori_loop(0, total, drain, 0)


def _slot_onehot(idx_ref, which, width):
    lane = lax.broadcasted_iota(I32, (idx_ref.shape[0], width), 1)
    return jnp.where(lane == idx_ref[:, 2 + which:3 + which], 1.0, 0.0)


def _scatter_seg_body(base_ref, tab_ref, tab_prev_ref, idx_ref, x_ref, g_ref, xs_ref, buf_ref, zero_ref, sem,
                      fill_sem):
    i = pl.program_id(0)
    slot = i % 2
    xn = _rms(x_ref[...], g_ref[...]).astype(BF16)
    width = buf_ref.shape[1]
    place = (_slot_onehot(idx_ref, 0, width) + _slot_onehot(idx_ref, 1, width)).astype(BF16)
    buf_ref[slot] = lax.dot_general(place, xn, (((0,), (0,)), ((), ())), preferred_element_type=F32)
    _segment_start(tab_ref, base_ref, buf_ref.at[slot], xs_ref, sem.at[slot], to_hbm=True)

    @pl.when(i > 0)
    def _():
        _segment_wait(tab_prev_ref, buf_ref.at[1 - slot], xs_ref, sem.at[1 - slot], to_hbm=True)

    @pl.when(i == pl.num_programs(0) - 1)
    def _():
        _segment_wait(tab_ref, buf_ref.at[slot], xs_ref, sem.at[slot], to_hbm=True)
        zero_ref[...] = jnp.zeros_like(zero_ref)
        tile_rows = zero_ref.shape[0]
        chunks = 0
        for e in range(N_EXPERTS):
            row0 = base_ref[N_EXPERTS + e]

            def fill(j, carry, row0=row0):
                _chunk_copy(zero_ref, xs_ref, 0, row0 + 8 * j, fill_sem).start()
                return carry

            lax.fori_loop(0, base_ref[2 * N_EXPERTS + e], fill, 0)
            chunks = chunks + base_ref[2 * N_EXPERTS + e]

        def tile_copy(j):
            row = pl.multiple_of(base_ref[3 * N_EXPERTS] + tile_rows * j, 8)
            return pltpu.make_async_copy(zero_ref, xs_ref.at[pl.ds(row, tile_rows)], fill_sem)

        def fill_tile(j, carry):
            tile_copy(j).start()
            return carry

        lax.fori_loop(0, base_ref[3 * N_EXPERTS + 1], fill_tile, 0)

        def drain(j, carry):
            _chunk_copy(zero_ref, xs_ref, 0, 0, fill_sem).wait()
            return carry

        lax.fori_loop(0, chunks, drain, 0)

        def drain_tile(j, carry):
            tile_copy(0).wait()
            return carry

        lax.fori_loop(0, base_ref[3 * N_EXPERTS + 1], drain_tile, 0)


def _tab_spec(shift, nt):
    return pl.BlockSpec((None, 8, LANES), lambda i, b: (jnp.clip(i + shift, 0, nt - 1), 0, 0),
                        memory_space=pltpu.SMEM)


def _moe_scatter_seg(x, gain, idx, tab, plan, rows):
    m, k = x.shape
    tm = SEG_TOK
    nt = m // tm
    grid_spec = pltpu.PrefetchScalarGridSpec(
        num_scalar_prefetch=1,
        grid=(nt,),
        in_specs=[
            _tab_spec(0, nt), _tab_spec(-1, nt),
            pl.BlockSpec((tm, LANES), lambda i, b: (i, 0)),
            pl.BlockSpec((tm, k), lambda i, b: (i, 0)),
            pl.BlockSpec((1, k), lambda i, b: (0, 0)),
        ],
        out_specs=pl.BlockSpec(memory_space=pl.ANY),
        scratch_shapes=[pltpu.VMEM((2, SEG_ROWS, k), F32), pltpu.VMEM((EXPERT_TILE, k), F32),
                        pltpu.SemaphoreType.DMA((2,)), pltpu.SemaphoreType.DMA(())],
    )
    return pl.pallas_call(
        _scatter_seg_body,
        grid_spec=grid_spec,
        out_shape=jax.ShapeDtypeStruct((rows, k), F32),
        compiler_params=_params(("arbitrary",)),
        name="moe_scatter_seg",
    )(plan, tab, tab, idx, x, gain.reshape(1, k).astype(F32))


def _combine_seg_body(base_ref, tab_ref, tab_next_ref, idx_ref, x_ref, gate_ref, ys_ref, o_ref, buf_ref, sem):
    i = pl.program_id(0)
    slot = i % 2

    @pl.when(i == 0)
    def _():
        buf_ref[...] = jnp.zeros_like(buf_ref)
        _segment_start(tab_ref, base_ref, buf_ref.at[0], ys_ref, sem.at[0], to_hbm=False)

    @pl.when(i < pl.num_programs(0) - 1)
    def _():
        _segment_start(tab_next_ref, base_ref, buf_ref.at[1 - slot], ys_ref, sem.at[1 - slot], to_hbm=False)

    _segment_wait(tab_ref, buf_ref.at[slot], ys_ref, sem.at[slot], to_hbm=False)
    width = buf_ref.shape[1]
    y = buf_ref[slot].astype(BF16)
    g = gate_ref[...]
    pick = lambda which: jnp.dot(_slot_onehot(idx_ref, which, width).astype(BF16), y, preferred_element_type=F32)
    o_ref[...] = x_ref[...] + g[:, 0:1] * pick(0) + g[:, 1:2] * pick(1)


def _moe_combine_seg(x, gates, idx, tab, base, ys):
    m, k = x.shape
    tm = SEG_TOK
    nt = m // tm
    grid_spec = pltpu.PrefetchScalarGridSpec(
        num_scalar_prefetch=1,
        grid=(nt,),
        in_specs=[
            _tab_spec(0, nt), _tab_spec(1, nt),
            pl.BlockSpec((tm, LANES), lambda i, b: (i, 0)),
            pl.BlockSpec((tm, k), lambda i, b: (i, 0)),
            pl.BlockSpec((tm, LANES), lambda i, b: (i, 0)),
            pl.BlockSpec(memory_space=pl.ANY),
        ],
        out_specs=pl.BlockSpec((tm, k), lambda i, b: (i, 0)),
        scratch_shapes=[pltpu.VMEM((2, SEG_ROWS, k), F32), pltpu.SemaphoreType.DMA((2,))],
    )
    return pl.pallas_call(
        _combine_seg_body,
        grid_spec=grid_spec,
        out_shape=jax.ShapeDtypeStruct((m, k), F32),
        compiler_params=_params(("arbitrary",)),
        name="moe_combine_seg",
    )(base, tab, tab, idx, x, gates, ys)


def _moe_seg(x, gain, w_router, b_router, w_gu, w_down):
    m, k = x.shape
    idx, gates, tab, totals = _router_seg(x, gain, w_router, b_router)
    totals = totals[0, :N_EXPERTS].astype(I32)
    region = ((totals + EXPERT_TILE - 1) // EXPERT_TILE) * EXPERT_TILE
    ends = jnp.cumsum(region)
    base = (ends - region).astype(I32)
    max_rows = 2 * m + N_EXPERTS * (7 * (m // SEG_TOK) + EXPERT_TILE - 1)
    rows = (max_rows // EXPERT_TILE) * EXPERT_TILE
    nt = rows // EXPERT_TILE
    n_used = (ends[-1] // EXPERT_TILE).astype(I32)
    tile_start = jnp.minimum(jnp.arange(nt, dtype=I32), n_used - 1) * EXPERT_TILE
    tile_expert = jnp.sum(tile_start[:, None] >= ends[None, :], axis=1).astype(I32)
    plan = jnp.concatenate([base, base + totals, (region - totals) // 8,
                            jnp.stack([ends[-1], nt * EXPERT_TILE - ends[-1]]) // jnp.array([1, EXPERT_TILE])]).astype(I32)
    xs = _moe_scatter_seg(x, gain, idx, tab, plan, rows)
    ys = _ffn_grouped(xs, tile_expert, n_used.reshape(1), w_gu, w_down, EXPERT_TILE)
    return _moe_combine_seg(x, gates, idx, tab, base, ys)


def _moe(x, gain, w_router, b_router, w_gu, w_down):
    m, k = x.shape
    if m % SEG_TOK == 0 and m >= 8 * SEG_TOK and not _is_precise(w_gu):
        return _moe_seg(x, gain, w_router, b_router, w_gu, w_down)
    tm_tok = min(256, m)
    tm_grp = 512 if m >= 4096 else 128
    idx, gates, counts = _router(x, gain, w_router, b_router)
    counts = counts[0, :N_EXPERTS].astype(I32)
    padded = ((counts + tm_grp - 1) // tm_grp) * tm_grp
    ends = jnp.cumsum(padded)
    starts = ends - padded
    dest = starts[idx[:, 0:2]] + idx[:, 2:4]
    dest_tiles = dest.reshape(m // tm_tok, 1, 2 * tm_tok)
    rows = ((2 * m + N_EXPERTS * (tm_grp - 1)) // tm_grp) * tm_grp
    nt = rows // tm_grp
    n_used = (ends[-1] // tm_grp).astype(I32)
    tile_start = jnp.minimum(jnp.arange(nt, dtype=I32), n_used - 1) * tm_grp
    tile_expert = jnp.sum(tile_start[:, None] >= ends[None, :], axis=1).astype(I32)
    xs = _moe_scatter(x, gain, dest_tiles, rows, tm_tok)
    ys = _ffn_grouped(xs, tile_expert, n_used.reshape(1), w_gu, w_down, tm_grp)
    return _moe_combine(x, gates, dest_tiles, ys, tm_tok)


def _head_cols(x, h, width=GDN_DK):
    return x[:, h * width:(h + 1) * width]


def _gdn_prep_body(x_ref, halo_ref, ab_ref, cw_ref, alog_ref, dtb_ref,
                   q_ref, qg_ref, k_ref, kb_ref, kbg_ref, kd_ref, vb_ref,
                   gc_ref, gct_ref, egl_ref, ext_ref):
    i = pl.program_id(1)
    tt = x_ref.shape[0]
    ext_ref[0:8, :] = jnp.where(i == 0, 0.0, halo_ref[...])
    ext_ref[8:, :] = x_ref[...]
    ext = ext_ref[...]
    acc = pltpu.roll(ext, CONV_TAPS - 1, axis=0)[8:] * cw_ref[0:1, :]
    for j in range(1, CONV_TAPS - 1):
        acc = acc + pltpu.roll(ext, CONV_TAPS - 1 - j, axis=0)[8:] * cw_ref[j:j + 1, :]
    acc = acc + ext[8:] * cw_ref[CONV_TAPS - 1:CONV_TAPS, :]
    y = _silu(acc)

    ab = ab_ref[...]
    g = -jnp.exp(alog_ref[...]) * jax.nn.softplus(ab + dtb_ref[...])
    beta = jax.nn.sigmoid(pltpu.roll(ab, LANES - GDN_HEADS, axis=1))
    r = lax.broadcasted_iota(I32, (tt, tt), 0)
    c = lax.broadcasted_iota(I32, (tt, tt), 1)
    same = (r // CHUNK) == (c // CHUNK)
    tril = jnp.where(same & (c <= r), 1.0, 0.0)
    gc = jnp.dot(tril, g, precision=HI, preferred_element_type=F32)
    gl = jnp.dot(jnp.where(same, 1.0, 0.0), g, precision=HI, preferred_element_type=F32)
    e_gc = jnp.exp(gc)
    e_gd = jnp.exp(gl - gc)
    gc_ref[...] = gc
    gct = gc.T
    sub = lax.broadcasted_iota(I32, (8, LANES), 0)
    lan = lax.broadcasted_iota(I32, (8, LANES), 1)
    for cc in range(tt // CHUNK):
        gct_ref[cc] = gct[0:8, cc * CHUNK:(cc + 1) * CHUNK]
        row = jnp.broadcast_to(gl[cc * CHUNK:cc * CHUNK + 1, :], (8, LANES))
        val = jnp.sum(jnp.where(sub == lan, row, 0.0), axis=-1, keepdims=True)
        egl_ref[cc] = jnp.broadcast_to(jnp.exp(val), (8, LANES))

    nq = GDN_HEADS * GDN_DK
    for h in range(GDN_HEADS):
        qh = _head_cols(y, h)
        kh = _head_cols(y, GDN_HEADS + h)
        vh = _head_cols(y, 2 * GDN_HEADS + h)
        qn = qh * lax.rsqrt(jnp.sum(qh * qh, axis=-1, keepdims=True) + EPS) * (GDN_DK ** -0.5)
        kn = kh * lax.rsqrt(jnp.sum(kh * kh, axis=-1, keepdims=True) + EPS)
        b_h = beta[:, h:h + 1]
        eg_h = e_gc[:, h:h + 1]
        sl = slice(h * GDN_DK, (h + 1) * GDN_DK)
        kb = kn * b_h
        q_ref[:, sl] = qn.astype(BF16)
        qg_ref[:, sl] = (qn * eg_h).astype(BF16)
        k_ref[:, sl] = kn.astype(BF16)
        kb_ref[:, sl] = kb.astype(BF16)
        kbg_ref[:, sl] = (kb * eg_h).astype(BF16)
        kd_ref[:, sl] = (kn * e_gd[:, h:h + 1]).astype(BF16)
        vb_ref[:, sl] = (vh * b_h).astype(BF16)
    del nq


def _gdn_prep(proj, ab, conv_w, a_log, dt_bias, tt=256):
    b, t, _ = proj.shape
    nc = t // CHUNK
    cpt = tt // CHUNK
    row = lambda v: jnp.zeros((1, LANES), F32).at[0, :GDN_HEADS].set(v)
    tok = pl.BlockSpec((None, tt, D_MODEL), lambda bi, i: (bi, i, 0))
    out_shapes = [jax.ShapeDtypeStruct((b, t, D_MODEL), BF16)] * 7 + [
        jax.ShapeDtypeStruct((b, t, LANES), F32),
        jax.ShapeDtypeStruct((b, nc, 8, CHUNK), F32),
        jax.ShapeDtypeStruct((b, nc, 8, LANES), F32),
    ]
    out_specs = [tok] * 7 + [
        pl.BlockSpec((None, tt, LANES), lambda bi, i: (bi, i, 0)),
        pl.BlockSpec((None, cpt, 8, CHUNK), lambda bi, i: (bi, i, 0, 0)),
        pl.BlockSpec((None, cpt, 8, LANES), lambda bi, i: (bi, i, 0, 0)),
    ]
    return pl.pallas_call(
        _gdn_prep_body,
        grid=(b, t // tt),
        in_specs=[
            pl.BlockSpec((None, tt, QKV_W), lambda bi, i: (bi, i, 0)),
            pl.BlockSpec((None, 8, QKV_W), lambda bi, i: (bi, jnp.maximum(i * (tt // 8) - 1, 0), 0)),
            pl.BlockSpec((None, tt, LANES), lambda bi, i: (bi, i, 0)),
            pl.BlockSpec((CONV_TAPS, QKV_W), lambda bi, i: (0, 0)),
            pl.BlockSpec((1, LANES), lambda bi, i: (0, 0)),
            pl.BlockSpec((1, LANES), lambda bi, i: (0, 0)),
        ],
        out_specs=out_specs,
        out_shape=out_shapes,
        scratch_shapes=[pltpu.VMEM((tt + 8, QKV_W), F32)],
        compiler_params=_params(("parallel", "arbitrary")),
        name="gdn_prep",
    )(proj, proj, ab, conv_w.astype(F32), row(a_log), row(dt_bias))


def _decay(gc_col, gc_row, keep):
    return jnp.where(keep, jnp.exp(jnp.where(keep, gc_col - gc_row, 0.0)), 0.0)


def _gdn_l_body(kb_ref, k_ref, gc_ref, gct_ref, l_ref):
    r = lax.broadcasted_iota(I32, (CHUNK, CHUNK), 0)
    c = lax.broadcasted_iota(I32, (CHUNK, CHUNK), 1)
    strict = c < r
    for cc in range(l_ref.shape[0]):
        rows = slice(cc * CHUNK, (cc + 1) * CHUNK)
        for h in range(GDN_HEADS):
            cols = slice(h * GDN_DK, (h + 1) * GDN_DK)
            kk = lax.dot_general(kb_ref[rows, cols], k_ref[rows, cols], (((1,), (1,)), ((), ())),
                                 preferred_element_type=F32)
            dec = _decay(gc_ref[rows, h:h + 1], gct_ref[cc, h:h + 1, :], strict)
            l_ref[cc, :, h * CHUNK:(h + 1) * CHUNK] = kk * dec


def _gdn_l(kb, k, gc, gct, cpt=4):
    b, t, _ = k.shape
    nc = t // CHUNK
    tt = cpt * CHUNK
    tok = pl.BlockSpec((None, tt, D_MODEL), lambda bi, i: (bi, i, 0))
    return pl.pallas_call(
        _gdn_l_body,
        grid=(b, nc // cpt),
        in_specs=[tok, tok,
                  pl.BlockSpec((None, tt, LANES), lambda bi, i: (bi, i, 0)),
                  pl.BlockSpec((None, cpt, 8, CHUNK), lambda bi, i: (bi, i, 0, 0))],
        out_specs=pl.BlockSpec((None, cpt, CHUNK, GDN_HEADS * CHUNK), lambda bi, i: (bi, i, 0, 0)),
        out_shape=jax.ShapeDtypeStruct((b, nc, CHUNK, GDN_HEADS * CHUNK), F32),
        compiler_params=_params(("parallel", "parallel")),
        name="gdn_l",
    )(kb, k, gc, gct)


def _tri_inv_body(l_ref, t_ref):
    sub = lax.broadcasted_iota(I32, (8, LANES), 0)
    for i in range(CHUNK):
        nq = i // 8 + 1
        acc = [jnp.where(sub == i % 8, 1.0, 0.0) if q == i // 8 else jnp.zeros((8, LANES), F32)
               for q in range(nq)]
        for j in range(i):
            lij = l_ref[i, j:j + 1, :]
            for q in range(j // 8 + 1):
                acc[q] = acc[q] - lij * t_ref[j, 8 * q:8 * q + 8, :]
        for q in range(CHUNK // 8):
            t_ref[i, 8 * q:8 * q + 8, :] = acc[q] if q < nq else jnp.zeros((8, LANES), F32)


def _tri_inv(lt):
    _, nh, _, p = lt.shape
    spec = pl.BlockSpec((CHUNK, None, CHUNK, LANES), lambda h, i: (0, h, 0, i))
    return pl.pallas_call(
        _tri_inv_body,
        grid=(nh, p // LANES),
        in_specs=[spec],
        out_specs=spec,
        out_shape=jax.ShapeDtypeStruct(lt.shape, F32),
        compiler_params=_params(("parallel", "parallel")),
        name="gdn_tri_inv",
    )(lt)


def _gdn_wy_body(t_ref, vb_ref, kbg_ref, q_ref, k_ref, qg_ref, gc_ref, gct_ref, u_ref, wq_ref, attn_ref):
    r = lax.broadcasted_iota(I32, (CHUNK, CHUNK), 0)
    c = lax.broadcasted_iota(I32, (CHUNK, CHUNK), 1)
    tril = c <= r
    for cc in range(t_ref.shape[0]):
        rows = slice(cc * CHUNK, (cc + 1) * CHUNK)
        for h in range(GDN_HEADS):
            cols = slice(h * GDN_DK, (h + 1) * GDN_DK)
            hc = slice(h * CHUNK, (h + 1) * CHUNK)
            rhs = jnp.concatenate([vb_ref[rows, cols], kbg_ref[rows, cols]], axis=1)
            sol = jnp.dot(t_ref[cc, :, hc].astype(BF16), rhs, preferred_element_type=F32)
            u_ref[rows, cols] = sol[:, :GDN_DK]
            wq_ref[cc, 0:CHUNK, cols] = sol[:, GDN_DK:].astype(BF16)
            wq_ref[cc, CHUNK:2 * CHUNK, cols] = qg_ref[rows, cols]
            qk = lax.dot_general(q_ref[rows, cols], k_ref[rows, cols], (((1,), (1,)), ((), ())),
                                 preferred_element_type=F32)
            dec = _decay(gc_ref[rows, h:h + 1], gct_ref[cc, h:h + 1, :], tril)
            attn_ref[cc, :, hc] = (qk * dec).astype(BF16)


def _gdn_wy(tinv, vb, kbg, q, k, qg, gc, gct, cpt=4):
    b, t, _ = k.shape
    nc = t // CHUNK
    tt = cpt * CHUNK
    tok = pl.BlockSpec((None, tt, D_MODEL), lambda bi, i: (bi, i, 0))
    sq = pl.BlockSpec((None, cpt, CHUNK, GDN_HEADS * CHUNK), lambda bi, i: (bi, i, 0, 0))
    return pl.pallas_call(
        _gdn_wy_body,
        grid=(b, nc // cpt),
        in_specs=[sq, tok, tok, tok, tok, tok,
                  pl.BlockSpec((None, tt, LANES), lambda bi, i: (bi, i, 0)),
                  pl.BlockSpec((None, cpt, 8, CHUNK), lambda bi, i: (bi, i, 0, 0))],
        out_specs=[tok,
                   pl.BlockSpec((None, cpt, 2 * CHUNK, D_MODEL), lambda bi, i: (bi, i, 0, 0)),
                   sq],
        out_shape=[jax.ShapeDtypeStruct((b, t, D_MODEL), F32),
                   jax.ShapeDtypeStruct((b, nc, 2 * CHUNK, D_MODEL), BF16),
                   jax.ShapeDtypeStruct((b, nc, CHUNK, GDN_HEADS * CHUNK), BF16)],
        compiler_params=_params(("parallel", "parallel")),
        name="gdn_wy",
    )(tinv, vb, kbg, q, k, qg, gc, gct)


def _gdn_chunk_body(wq_ref, u_ref, attn_ref, kd_ref, egl_ref, o_ref, s_out_ref, sa_ref, sb_ref):
    c = pl.program_id(0)

    @pl.when(c == 0)
    def _():
        sa_ref[...] = jnp.zeros_like(sa_ref)

    def step(src_ref, dst_ref):
        heads = [(b, h) for b in range(src_ref.shape[0]) for h in range(GDN_HEADS)]
        cols = lambda h: slice(h * GDN_DK, (h + 1) * GDN_DK)
        ws = [jnp.dot(wq_ref[b, :, cols(h)], src_ref[b, h].astype(BF16), preferred_element_type=F32)
              for b, h in heads]
        v_nb = [(u_ref[b, :, cols(h)] - w_[:CHUNK]).astype(BF16) for (b, h), w_ in zip(heads, ws)]
        for (b, h), w_, v_ in zip(heads, ws, v_nb):
            o_ref[b, :, cols(h)] = w_[CHUNK:] + jnp.dot(attn_ref[b, :, h * CHUNK:(h + 1) * CHUNK], v_,
                                                        preferred_element_type=F32)
        for (b, h), v_ in zip(heads, v_nb):
            dst_ref[b, h] = src_ref[b, h] * egl_ref[b, h:h + 1, :] + lax.dot_general(
                kd_ref[b, :, cols(h)], v_, (((0,), (0,)), ((), ())), preferred_element_type=F32)

        @pl.when(c == pl.num_programs(0) - 1)
        def _():
            s_out_ref[...] = dst_ref[...]

    @pl.when(c % 2 == 0)
    def _():
        step(sa_ref, sb_ref)

    @pl.when(c % 2 == 1)
    def _():
        step(sb_ref, sa_ref)


def _gdn_chunks(wq, u, attn, kd, egl):
    b, t, _ = u.shape
    nc = t // CHUNK
    tok = pl.BlockSpec((b, CHUNK, D_MODEL), lambda c: (0, c, 0))
    return pl.pallas_call(
        _gdn_chunk_body,
        grid=(nc,),
        in_specs=[
            pl.BlockSpec((b, None, 2 * CHUNK, D_MODEL), lambda c: (0, c, 0, 0)),
            tok,
            pl.BlockSpec((b, None, CHUNK, GDN_HEADS * CHUNK), lambda c: (0, c, 0, 0)),
            tok,
            pl.BlockSpec((b, None, 8, LANES), lambda c: (0, c, 0, 0)),
        ],
        out_specs=[
            tok,
            pl.BlockSpec((b, GDN_HEADS, GDN_DK, GDN_DK), lambda c: (0, 0, 0, 0)),
        ],
        out_shape=[
            jax.ShapeDtypeStruct((b, t, D_MODEL), F32),
            jax.ShapeDtypeStruct((b, GDN_HEADS, GDN_DK, GDN_DK), F32),
        ],
        scratch_shapes=[pltpu.VMEM((b, GDN_HEADS, GDN_DK, GDN_DK), F32)] * 2,
        compiler_params=_params(("arbitrary",)),
        name="gdn_chunks",
    )(wq, u, attn, kd, egl)


def _gdn_out_body(o_ref, z_ref, on_ref, w_ref, r_ref, out_ref, *, precise):
    parts = []
    for h in range(GDN_HEADS):
        oh = _head_cols(o_ref[...], h)
        zh = _head_cols(z_ref[...], h)
        parts.append(_rms(oh, on_ref[...]) * _silu(zh))
    if precise:
        gh, gl = _split(jnp.concatenate(parts, axis=1))
        out_ref[...] = r_ref[...] + _dot3(gh, gl, w_ref[...])
    else:
        gated = jnp.concatenate([a.astype(BF16) for a in parts], axis=1)
        out_ref[...] = r_ref[...] + jnp.dot(gated, w_ref[...], preferred_element_type=F32)


def _gdn_out(o, proj, o_norm, w_out, res, tm=512):
    m, k = o.shape
    tm = min(tm, m)
    return pl.pallas_call(
        functools.partial(_gdn_out_body, precise=_is_precise(w_out)),
        grid=(m // tm,),
        in_specs=[
            pl.BlockSpec((tm, k), lambda i: (i, 0)),
            pl.BlockSpec((tm, k), lambda i: (i, QKV_W // D_MODEL)),
            pl.BlockSpec((1, GDN_DK), lambda i: (0, 0)),
            _wspec(w_out, (k, k), lambda i: (0, 0)),
            pl.BlockSpec((tm, k), lambda i: (i, 0)),
        ],
        out_specs=pl.BlockSpec((tm, k), lambda i: (i, 0)),
        out_shape=jax.ShapeDtypeStruct((m, k), F32),
        compiler_params=_params(("parallel",)),
        name="gdn_out",
    )(o, proj, o_norm.reshape(1, GDN_DK).astype(F32), _warr(w_out), res)


def _gdn_prompt_layer(h, l, p, w):
    b, t, _ = h.shape
    hf = h.reshape(b * t, D_MODEL)
    proj, ab = _mm(hf, w['a_w_in'][l], gain=p['a_norm'][l], w_aux=w['a_w_ab'][l], tn=2 * D_MODEL)
    proj3 = proj.reshape(b, t, PROJ_W)
    q, qg, k, kb, kbg, kd, vb, gc, gct, egl = _gdn_prep(
        proj3, ab.reshape(b, t, LANES), p['a_conv_w'][l], p['a_A_log'][l], p['a_dt_bias'][l])
    lmat = _gdn_l(kb, k, gc, gct)
    nprob = b * (t // CHUNK)
    ppad = -nprob % LANES
    lt = jnp.transpose(lmat.reshape(nprob, CHUNK, GDN_HEADS * CHUNK), (1, 2, 0))
    lt = jnp.pad(lt, ((0, 0), (0, 0), (0, ppad))).reshape(CHUNK, GDN_HEADS, CHUNK, nprob + ppad)
    tt = _tri_inv(lt).reshape(CHUNK, GDN_HEADS * CHUNK, nprob + ppad)[:, :, :nprob]
    tinv = jnp.transpose(tt, (2, 0, 1)).reshape(b, t // CHUNK, CHUNK, GDN_HEADS * CHUNK)
    u, wq, attn = _gdn_wy(tinv, vb, kbg, q, k, qg, gc, gct)
    o, s_new = _gdn_chunks(wq, u, attn, kd, egl)
    h_new = _gdn_out(o.reshape(b * t, D_MODEL), proj, p['a_o_norm'][l], w['a_w_out'][l], hf)
    conv_new = proj3[:, t - (CONV_TAPS - 1):, :QKV_W]
    return h_new.reshape(b, t, D_MODEL), s_new, conv_new


def _gdn_prep1_body(x_ref, st_ref, ab_ref, cw_ref, alog_ref, dtb_ref,
                    q_ref, k_ref, v_ref, eg_ref, beta_ref):
    acc = st_ref[0] * cw_ref[0:1, :]
    for j in range(1, CONV_TAPS - 1):
        acc = acc + st_ref[j] * cw_ref[j:j + 1, :]
    acc = acc + x_ref[...] * cw_ref[CONV_TAPS - 1:CONV_TAPS, :]
    y = _silu(acc)
    ab = ab_ref[...]
    eg_ref[...] = jnp.exp(-jnp.exp(alog_ref[...]) * jax.nn.softplus(ab + dtb_ref[...]))
    beta_ref[...] = jax.nn.sigmoid(pltpu.roll(ab, LANES - GDN_HEADS, axis=1))
    for h in range(GDN_HEADS):
        qh = _head_cols(y, h)
        kh = _head_cols(y, GDN_HEADS + h)
        sl = slice(h * GDN_DK, (h + 1) * GDN_DK)
        q_ref[:, sl] = qh * lax.rsqrt(jnp.sum(qh * qh, axis=-1, keepdims=True) + EPS) * (GDN_DK ** -0.5)
        k_ref[:, sl] = kh * lax.rsqrt(jnp.sum(kh * kh, axis=-1, keepdims=True) + EPS)
    v_ref[...] = y[:, 2 * GDN_HEADS * GDN_DK:]


def _gdn_prep1(proj, ab, conv_state, conv_w, a_log, dt_bias):
    n = proj.shape[0]
    row = lambda v: jnp.zeros((1, LANES), F32).at[0, :GDN_HEADS].set(v)
    tok = pl.BlockSpec((n, D_MODEL), lambda i: (0, 0))
    small = pl.BlockSpec((n, LANES), lambda i: (0, 0))
    return pl.pallas_call(
        _gdn_prep1_body,
        grid=(1,),
        in_specs=[
            pl.BlockSpec((n, QKV_W), lambda i: (0, 0)),
            pl.BlockSpec((CONV_TAPS - 1, n, QKV_W), lambda i: (0, 0, 0)),
            pl.BlockSpec((n, LANES), lambda i: (0, 0)),
            pl.BlockSpec((CONV_TAPS, QKV_W), lambda i: (0, 0)),
            pl.BlockSpec((1, LANES), lambda i: (0, 0)),
            pl.BlockSpec((1, LANES), lambda i: (0, 0)),
        ],
        out_specs=[tok, tok, tok, small, small],
        out_shape=[jax.ShapeDtypeStruct((n, D_MODEL), F32)] * 3 + [jax.ShapeDtypeStruct((n, LANES), F32)] * 2,
        compiler_params=_params(("arbitrary",)),
        name="gdn_prep1",
    )(proj, conv_state, ab, conv_w.astype(F32), row(a_log), row(dt_bias))


def _gdn_step_body(qt_ref, kt_ref, v_ref, eg_ref, beta_ref, s_ref, o_ref, s_out_ref, *, first):
    def compute():
        for b in range(s_ref.shape[0]):
            for h in range(GDN_HEADS):
                k_col = kt_ref[b, :, h:h + 1]
                q_col = qt_ref[b, :, h:h + 1]
                s = s_ref[b, h] * eg_ref[b, h:h + 1, :]
                kv_mem = jnp.sum(k_col * s, axis=0, keepdims=True)
                delta = (v_ref[b, h:h + 1, :] - kv_mem) * beta_ref[b, h:h + 1, :]
                s = s + k_col * delta
                s_out_ref[b, h] = s
                o_ref[b, h:h + 1, :] = jnp.sum(q_col * s, axis=0, keepdims=True)

    if first:
        pl.when(pl.program_id(1) == 0)(compute)

        @pl.when(pl.program_id(1) > 0)
        def _():
            s_out_ref[...] = s_ref[...]
    else:
        compute()


def _gdn_step(qt, kt, v, eg, beta, states, layer, carry, sb=4):
    nl, n = states.shape[:2]
    first = carry is None
    assert first == (layer == 0)
    blk = (None, sb, GDN_HEADS, GDN_DK, GDN_DK)
    if first:
        grid = (n // sb, nl)
        vec = lambda i, j: (i, 0, 0)
        st = pl.BlockSpec(blk, lambda i, j: (j, i, 0, 0, 0))
        sem = ("parallel", "arbitrary")
    else:
        grid = (n // sb,)
        vec = lambda i: (i, 0, 0)
        st = pl.BlockSpec(blk, lambda i: (layer, i, 0, 0, 0))
        sem = ("parallel",)
    col = pl.BlockSpec((sb, GDN_DK, GDN_HEADS), vec)
    row = pl.BlockSpec((sb, GDN_HEADS, GDN_DK), vec)
    return pl.pallas_call(
        functools.partial(_gdn_step_body, first=first),
        grid=grid,
        in_specs=[col, col, row, row, row, st],
        out_specs=[row, st],
        out_shape=[jax.ShapeDtypeStruct((n, GDN_HEADS, GDN_DK), F32),
                   jax.ShapeDtypeStruct(states.shape, F32)],
        input_output_aliases={} if first else {5: 1},
        compiler_params=_params(sem),
        name="gdn_step",
    )(qt, kt, v, eg, beta, states if first else carry)


def _gdn_sample_layer(h, l, p, w, conv_state, delta_states, carry):
    n = h.shape[0]
    proj, ab = _mm(h, w['a_w_in'][l], gain=p['a_norm'][l], w_aux=w['a_w_ab'][l])
    conv_t = jnp.transpose(conv_state, (1, 0, 2))
    q, k, v, eg, beta = _gdn_prep1(proj, ab, conv_t, p['a_conv_w'][l], p['a_A_log'][l], p['a_dt_bias'][l])
    heads = lambda a: a.reshape(n, GDN_HEADS, GDN_DK)
    cols = lambda a: jnp.transpose(heads(a), (0, 2, 1))
    lanes = lambda a: jnp.broadcast_to(a[:, :GDN_HEADS, None], (n, GDN_HEADS, GDN_DK))
    o, s_new = _gdn_step(cols(q), cols(k), heads(v), lanes(eg), lanes(beta), delta_states, l, carry)
    h_new = _gdn_out(o.reshape(n, D_MODEL), proj, p['a_o_norm'][l], w['a_w_out'][l], h)
    conv_new = jnp.concatenate([conv_state[:, 1:], proj[:, None, :QKV_W]], axis=1)
    return h_new, s_new, conv_new


def _rope_tables(pos):
    half = ROT // 2
    inv = ROPE_THETA ** (-jnp.arange(0, ROT, 2, dtype=F32) / ROT)
    ang = pos.astype(F32)[:, None] * inv[None, :]
    cos, sin = jnp.cos(ang), jnp.sin(ang)
    ones = jnp.ones((pos.shape[0], HEAD_DIM - ROT), F32)
    zeros = jnp.zeros_like(ones)
    z8 = jnp.zeros_like(sin)
    c = jnp.concatenate([cos, cos, ones], axis=1)
    sm = jnp.concatenate([-sin, z8, zeros], axis=1)
    sp = jnp.concatenate([z8, sin, zeros], axis=1)
    del half
    return tuple(jnp.tile(a, (1, LANES // HEAD_DIM)) for a in (c, sm, sp))


def _dot_split(a, sel):
    hi = a.astype(BF16)
    lo = (a - hi.astype(F32)).astype(BF16)
    return (jnp.dot(hi, sel, preferred_element_type=F32) + jnp.dot(lo, sel, preferred_element_type=F32))


def _headnorm_rope_body(x_ref, g_ref, seg_ref, segt_ref, c_ref, sm_ref, sp_ref, o_ref):
    x = x_ref[...]
    w = x.shape[1]
    ms = _dot_split(x * x, seg_ref[...]) * (1.0 / HEAD_DIM)
    scale = _dot_split(lax.rsqrt(ms + EPS), segt_ref[...])
    y = x * scale * g_ref[...]
    reps = w // LANES
    tile = lambda a: jnp.concatenate([a] * reps, axis=1) if reps > 1 else a
    half = ROT // 2
    out = (y * tile(c_ref[...]) + pltpu.roll(y, w - half, axis=1) * tile(sm_ref[...])
           + pltpu.roll(y, half, axis=1) * tile(sp_ref[...]))
    o_ref[...] = out.astype(o_ref.dtype)


def _headnorm_rope(x, col_block, width, gain, tables, rows_per_seq, out_dtype, tm=512):
    m = x.shape[0]
    tm = min(tm, m, rows_per_seq)
    nheads = width // HEAD_DIM
    seg = (jnp.arange(width)[:, None] // HEAD_DIM == jnp.arange(LANES)[None, :]).astype(BF16)
    per_seq = rows_per_seq // tm
    tab = pl.BlockSpec((tm, LANES), lambda i: (i % per_seq, 0))
    del nheads
    return pl.pallas_call(
        _headnorm_rope_body,
        grid=(m // tm,),
        in_specs=[
            pl.BlockSpec((tm, width), lambda i: (i, col_block)),
            pl.BlockSpec((1, width), lambda i: (0, 0)),
            pl.BlockSpec((width, LANES), lambda i: (0, 0)),
            pl.BlockSpec((LANES, width), lambda i: (0, 0)),
            tab, tab, tab,
        ],
        out_specs=pl.BlockSpec((tm, width), lambda i: (i, 0)),
        out_shape=jax.ShapeDtypeStruct((m, width), out_dtype),
        compiler_params=_params(("parallel",)),
        name="headnorm_rope",
    )(x, jnp.tile(gain.astype(F32), width // HEAD_DIM).reshape(1, width), seg, seg.T, *tables)


def _softmax_sink(s, sink):
    m = jnp.maximum(jnp.max(s, axis=-1, keepdims=True), sink)
    p = jnp.exp(s - m)
    return p, jnp.sum(p, axis=-1, keepdims=True) + jnp.exp(sink - m)


def _swa_prompt_body(sink_ref, q_ref, kp_ref, kc_ref, vp_ref, vc_ref, o_ref):
    i = pl.program_id(1)
    blk = q_ref.shape[0]
    r = lax.broadcasted_iota(I32, (blk, 2 * blk), 0)
    c = lax.broadcasted_iota(I32, (blk, 2 * blk), 1)
    valid = (c >= r) & (c <= r + WINDOW) & ((c >= blk) | (i > 0))
    for kvh in range(KV_HEADS):
        cols = slice(kvh * HEAD_DIM, (kvh + 1) * HEAD_DIM)
        kk = jnp.concatenate([kp_ref[:, cols], kc_ref[:, cols]], axis=0).astype(BF16)
        vv = jnp.concatenate([vp_ref[:, cols], vc_ref[:, cols]], axis=0).astype(BF16)
        for g in range(ATT_GROUP):
            h = kvh * ATT_GROUP + g
            hc = slice(h * HEAD_DIM, (h + 1) * HEAD_DIM)
            s = lax.dot_general(q_ref[:, hc], kk, (((1,), (1,)), ((), ())),
                                preferred_element_type=F32) * (HEAD_DIM ** -0.5)
            s = jnp.where(valid, s, -jnp.inf)
            p, den = _softmax_sink(s, sink_ref[h])
            o_ref[:, hc] = jnp.dot((p / den).astype(BF16), vv, preferred_element_type=F32).astype(o_ref.dtype)


def _swa_prompt(q, k, kv, sinks):
    b, t, _ = q.shape
    blk = WINDOW
    kvw = KV_HEADS * HEAD_DIM
    prev = lambda bi, i, s: (bi, jnp.maximum(i - 1, 0), 0)
    cur = lambda bi, i, s: (bi, i, 0)
    prev_v = lambda bi, i, s: (bi, jnp.maximum(i - 1, 0), 1)
    cur_v = lambda bi, i, s: (bi, i, 1)
    grid_spec = pltpu.PrefetchScalarGridSpec(
        num_scalar_prefetch=1,
        grid=(b, t // blk),
        in_specs=[
            pl.BlockSpec((None, blk, D_MODEL), cur),
            pl.BlockSpec((None, blk, kvw), prev),
            pl.BlockSpec((None, blk, kvw), cur),
            pl.BlockSpec((None, blk, kvw), prev_v),
            pl.BlockSpec((None, blk, kvw), cur_v),
        ],
        out_specs=pl.BlockSpec((None, blk, D_MODEL), cur),
    )
    return pl.pallas_call(
        _swa_prompt_body,
        grid_spec=grid_spec,
        out_shape=jax.ShapeDtypeStruct((b, t, D_MODEL), BF16),
        compiler_params=_params(("parallel", "parallel")),
        name="swa_prompt",
    )(sinks.astype(F32), q, k, k, kv, kv)


def _swa_sample_body(sink_ref, q_ref, kc_ref, kn_ref, vc_ref, vn_ref, o_ref):
    wc = kc_ref.shape[1]
    kvw = kc_ref.shape[2]
    nkeys = 2 * wc
    fill = jnp.zeros((nkeys - wc - 8, kvw), F32)
    key_idx = lax.broadcasted_iota(I32, (ATT_HEADS, nkeys), 1)
    sink = sink_ref[:, 0:1]
    seqs = range(q_ref.shape[0])
    scores = []
    for b in seqs:
        kk = jnp.concatenate([kc_ref[b], kn_ref[b], fill], axis=0)
        scores.append(lax.dot_general(q_ref[b], kk, (((1,), (1,)), ((), ())), precision=HI,
                                      preferred_element_type=F32))
    probs = []
    for s in scores:
        s = jnp.where(key_idx <= wc, s * (HEAD_DIM ** -0.5), -jnp.inf)
        p, den = _softmax_sink(s, sink)
        probs.append(p / den)
    outs = []
    for b, p in zip(seqs, probs):
        vv = jnp.concatenate([vc_ref[b], vn_ref[b], fill], axis=0)
        outs.append(jnp.dot(p, vv, precision=HI, preferred_element_type=F32))
    for b, o in zip(seqs, outs):
        for kvh in range(KV_HEADS):
            heads = slice(kvh * ATT_GROUP, (kvh + 1) * ATT_GROUP)
            o_ref[b, heads, :] = o[heads, kvh * HEAD_DIM:(kvh + 1) * HEAD_DIM]


def _swa_sample(q, k_cache, k_new, v_cache, v_new, sinks, sb=8):
    n = q.shape[0]
    wc = k_cache.shape[1]
    kvw = KV_HEADS * HEAD_DIM
    own = (jnp.arange(ATT_HEADS)[:, None] // ATT_GROUP == jnp.arange(KV_HEADS)[None, :])
    q_rows = (q[:, :, None, :] * own[None, :, :, None].astype(q.dtype)).reshape(n, ATT_HEADS, kvw)
    pad8 = lambda a: jnp.pad(a[:, None, :], ((0, 0), (0, 7), (0, 0)))
    blk = lambda shape: pl.BlockSpec(shape, lambda i: (i, 0, 0))
    return pl.pallas_call(
        _swa_sample_body,
        grid=(n // sb,),
        in_specs=[pl.BlockSpec((ATT_HEADS, LANES), lambda i: (0, 0)),
                  blk((sb, ATT_HEADS, kvw)), blk((sb, wc, kvw)), blk((sb, 8, kvw)),
                  blk((sb, wc, kvw)), blk((sb, 8, kvw))],
        out_specs=blk((sb, ATT_HEADS, HEAD_DIM)),
        out_shape=jax.ShapeDtypeStruct((n, ATT_HEADS, HEAD_DIM), F32),
        compiler_params=_params(("parallel",)),
        name="swa_sample",
    )(jnp.broadcast_to(sinks.astype(F32)[:, None], (ATT_HEADS, LANES)), q_rows, k_cache, pad8(k_new),
      v_cache, pad8(v_new))


def _channel_mixer(hf, l, p, w):
    if l % 2 == 0:
        return _ffn_dense(hf, p['ffn_norm'][l], w['dense_w_gu'][l // 2], w['dense_w_down'][l // 2])
    return _moe(hf, p['ffn_norm'][l], p['moe_router'][l // 2], p['moe_router_b'][l // 2],
                w['moe_w_gu'][l // 2], w['moe_w_down'][l // 2])


def _run_prompt(x, p, w):
    b, t, _ = x.shape
    n = b * t
    h = x
    convs, deltas = [], []
    for l in range(N_GDN):
        h, s_new, c_new = _gdn_prompt_layer(h, l, p, w)
        convs.append(c_new)
        deltas.append(s_new)
        h = _channel_mixer(h.reshape(n, D_MODEL), l, p, w).reshape(b, t, D_MODEL)
    hf = h.reshape(n, D_MODEL)
    tables = _rope_tables(jnp.arange(t, dtype=I32))
    kvw = KV_HEADS * HEAD_DIM
    kv = _mm(hf, w['w_kv'], gain=p['kv_norm'], tn=2 * kvw)
    k_sh = _headnorm_rope(kv, 0, kvw, p['k_norm'], tables, t, F32)
    k3 = k_sh.reshape(b, t, kvw)
    kv3 = kv.reshape(b, t, 2 * kvw)
    for j in range(2):
        l = N_GDN + j
        qp = _mm(hf, w['b_w_q'][j], gain=p['b_norm'][j])
        q = _headnorm_rope(qp, 0, D_MODEL, p['b_q_norm'][j], tables, t, BF16)
        o = _swa_prompt(q.reshape(b, t, D_MODEL), k3, kv3, p['b_sinks'][j])
        hf = _mm(o.reshape(n, D_MODEL), w['b_w_o'][j], res=hf)
        hf = _channel_mixer(hf, l, p, w)
    k_win = k3[:, t - WINDOW:].reshape(b, WINDOW, KV_HEADS, HEAD_DIM)
    v_win = kv3[:, t - WINDOW:, kvw:].reshape(b, WINDOW, KV_HEADS, HEAD_DIM)
    return hf.reshape(b, t, D_MODEL), jnp.stack(convs), jnp.stack(deltas), k_win, v_win


def _run_sample(x, conv_state, delta_state, k_cache, v_cache, p, w):
    n = x.shape[0]
    h = x.reshape(n, D_MODEL)
    convs, deltas = [], None
    for l in range(N_GDN):
        h, deltas, c_new = _gdn_sample_layer(h, l, p, w, conv_state[l], delta_state, deltas)
        convs.append(c_new)
        h = _channel_mixer(h, l, p, w)
    tables = _rope_tables(jnp.full((n,), PAST_LEN, I32))
    kvw = KV_HEADS * HEAD_DIM
    wc = k_cache.shape[1]
    kv = _mm(h, w['w_kv'], gain=p['kv_norm'], tn=2 * kvw)
    k_new = _headnorm_rope(kv, 0, kvw, p['k_norm'], tables, n, F32)
    v_new = kv[:, kvw:]
    kc = k_cache.reshape(n, wc, kvw)
    vc = v_cache.reshape(n, wc, kvw)
    for j in range(2):
        l = N_GDN + j
        qp = _mm(h, w['b_w_q'][j], gain=p['b_norm'][j])
        q = _headnorm_rope(qp, 0, D_MODEL, p['b_q_norm'][j], tables, n, F32)
        o = _swa_sample(q.reshape(n, ATT_HEADS, HEAD_DIM), kc, k_new, vc, v_new, p['b_sinks'][j])
        h = _mm(o.reshape(n, D_MODEL), w['b_w_o'][j], res=h)
        h = _channel_mixer(h, l, p, w)
    k_win = jnp.concatenate([kc, k_new[:, None, :]], axis=1)[:, -wc:].reshape(n, wc, KV_HEADS, HEAD_DIM)
    v_win = jnp.concatenate([vc, v_new[:, None, :]], axis=1)[:, -wc:].reshape(n, wc, KV_HEADS, HEAD_DIM)
    return h.reshape(n, 1, D_MODEL), jnp.stack(convs), deltas, k_win, v_win


def kernel(x_prompt, x_sample, state_conv, state_delta, cache_k_win, cache_v_win, a_norm, a_w_in, a_conv_w,
           a_A_log, a_dt_bias, a_o_norm, a_w_out, kv_norm, w_kv, k_norm, b_norm, b_w_q, b_q_norm, b_sinks,
           b_w_o, ffn_norm, dense_w_gu, dense_w_down, moe_router, moe_router_b, moe_w_gu, moe_w_down):
    p = dict(a_norm=a_norm, a_conv_w=a_conv_w, a_A_log=a_A_log, a_dt_bias=a_dt_bias, a_o_norm=a_o_norm,
             kv_norm=kv_norm, k_norm=k_norm, b_norm=b_norm, b_q_norm=b_q_norm, b_sinks=b_sinks,
             ffn_norm=ffn_norm, moe_router=moe_router, moe_router_b=moe_router_b)
    n_proj = PROJ_W
    gate_cols = jnp.pad(a_w_in[:, :, n_proj:], ((0, 0), (0, 0), (0, LANES - (a_w_in.shape[-1] - n_proj))))
    raw = dict(a_w_in=a_w_in[:, :, :n_proj], a_w_ab=gate_cols, a_w_out=a_w_out, w_kv=w_kv[None],
               b_w_q=b_w_q, b_w_o=b_w_o, dense_w_gu=dense_w_gu, dense_w_down=dense_w_down,
               moe_w_gu=moe_w_gu, moe_w_down=moe_w_down)
    w_fast = {name: [val[l].astype(BF16) for l in range(val.shape[0])] for name, val in raw.items()}
    w_precise = {name: [_Layer(val, l) for l in range(val.shape[0])] for name, val in raw.items()}
    w_precise['a_w_in'] = [_Layer(a_w_in, l, cols=n_proj) for l in range(a_w_in.shape[0])]
    w_fast['w_kv'], w_precise['w_kv'] = w_fast['w_kv'][0], w_kv
    y_p, conv_p, delta_p, kw_p, vw_p = _run_prompt(x_prompt, p, w_fast)
    y_s, conv_s, delta_s, kw_s, vw_s = _run_sample(
        x_sample, state_conv, state_delta, cache_k_win, cache_v_win, p, w_precise)
    return (y_p, y_s, conv_p, conv_s, delta_p, delta_s, kw_p, kw_s, vw_p, vw_s)
```

```python
import functools

import jax
import jax.numpy as jnp
from jax import lax
from jax.experimental import pallas as pl
from jax.experimental.pallas import tpu as pltpu

F32, BF16, I32 = jnp.float32, jnp.bfloat16, jnp.int32
HI = lax.Precision.HIGHEST

D_MODEL = 1024
N_GDN = 2
GDN_HEADS = 8
GDN_DK = 128
CONV_TAPS = 4
CHUNK = 64
QKV_W = 3 * GDN_HEADS * GDN_DK
PROJ_W = QKV_W + D_MODEL
ATT_HEADS = 16
KV_HEADS = 4
HEAD_DIM = 64
ATT_GROUP = ATT_HEADS // KV_HEADS
WINDOW = 128
ROT = HEAD_DIM // 4
ROPE_THETA = 500000.0
PAST_LEN = 8192
N_EXPERTS = 8
EPS = 1e-6

LANES = 128
VMEM_LIMIT = 56 * 1024 * 1024
NEG = -1e30


def _params(sem):
    return pltpu.CompilerParams(dimension_semantics=sem, vmem_limit_bytes=VMEM_LIMIT)


def _rms(x, g):
    return x * lax.rsqrt(jnp.mean(x * x, axis=-1, keepdims=True) + EPS) * g


def _silu(x):
    return x * jax.nn.sigmoid(x)


def _split(x):
    hi = x.astype(BF16)
    return hi, (x - hi.astype(F32)).astype(BF16)


def _dot3(xh, xl, w):
    wh, wl = _split(w)
    return (jnp.dot(xh, wh, preferred_element_type=F32)
            + (jnp.dot(xl, wh, preferred_element_type=F32) + jnp.dot(xh, wl, preferred_element_type=F32)))


class _Layer:
    def __init__(self, arr, layer, cols=None):
        self.arr, self.lead, self.dtype = arr, (layer,), arr.dtype
        self.shape = arr.shape[1:] if cols is None else arr.shape[1:-1] + (cols,)


def _warr(w):
    return w.arr if isinstance(w, _Layer) else w


def _wspec(w, block, index_fn, **kwargs):
    lead = w.lead if isinstance(w, _Layer) else ()
    return pl.BlockSpec((None,) * len(lead) + tuple(block), lambda *a: lead + tuple(index_fn(*a)), **kwargs)


def _is_precise(w):
    assert w.dtype in (BF16, F32)
    return w.dtype == F32


def _mm_body(*refs, has_norm, has_aux, has_res, precise):
    it = iter(refs)
    x_ref = next(it)
    g_ref = next(it) if has_norm else None
    w_ref = next(it)
    wa_ref = next(it) if has_aux else None
    r_ref = next(it) if has_res else None
    o_ref = next(it)
    oa_ref = next(it) if has_aux else None
    xn_ref = next(it)

    def product(w):
        if precise:
            return _dot3(xn_ref[0], xn_ref[1], w)
        return jnp.dot(xn_ref[0], w, preferred_element_type=F32)

    @pl.when(pl.program_id(1) == 0)
    def _():
        x = x_ref[...].astype(F32)
        if has_norm:
            x = _rms(x, g_ref[...])
        if precise:
            xn_ref[0], xn_ref[1] = _split(x)
        else:
            xn_ref[0] = x.astype(BF16)
        if has_aux:
            oa_ref[...] = product(wa_ref[...])

    acc = product(w_ref[...])
    if has_res:
        acc = acc + r_ref[...]
    o_ref[...] = acc.astype(o_ref.dtype)


def _mm(x, w, gain=None, res=None, w_aux=None, tm=1024, tn=1024, out_dtype=F32):
    precise = _is_precise(w)
    m, k = x.shape
    n = w.shape[1]
    tm = min(tm, m)
    tn = min(tn, n)
    assert m % tm == 0 and n % tn == 0
    ins = [x]
    specs = [pl.BlockSpec((tm, k), lambda i, j: (i, 0))]
    if gain is not None:
        ins.append(gain.reshape(1, k).astype(F32))
        specs.append(pl.BlockSpec((1, k), lambda i, j: (0, 0)))
    ins.append(_warr(w))
    resident = dict(pipeline_mode=pl.Buffered(1)) if tn == n else {}
    specs.append(_wspec(w, (k, tn), lambda i, j: (0, j), **resident))
    out_specs = [pl.BlockSpec((tm, tn), lambda i, j: (i, j))]
    out_shape = [jax.ShapeDtypeStruct((m, n), out_dtype)]
    if w_aux is not None:
        na = w_aux.shape[1]
        ins.append(_warr(w_aux))
        specs.append(_wspec(w_aux, (k, na), lambda i, j: (0, 0)))
        out_specs.append(pl.BlockSpec((tm, na), lambda i, j: (i, 0)))
        out_shape.append(jax.ShapeDtypeStruct((m, na), F32))
    if res is not None:
        ins.append(res)
        specs.append(pl.BlockSpec((tm, tn), lambda i, j: (i, j)))
    out = pl.pallas_call(
        functools.partial(_mm_body, has_norm=gain is not None, has_aux=w_aux is not None,
                          has_res=res is not None, precise=precise),
        grid=(m // tm, n // tn),
        in_specs=specs,
        out_specs=out_specs,
        out_shape=out_shape,
        scratch_shapes=[pltpu.VMEM((1 + precise, tm, k), BF16)],
        compiler_params=_params(("parallel", "arbitrary")),
        name="mm",
    )(*ins)
    return out if w_aux is not None else out[0]


def _swiglu(xh, xl, wg, wu, wd):
    if xl is None:
        gate = jnp.dot(xh, wg, preferred_element_type=F32)
        up = jnp.dot(xh, wu, preferred_element_type=F32)
        return jnp.dot((_silu(gate) * up).astype(BF16), wd, preferred_element_type=F32)
    gate = _dot3(xh, xl, wg)
    up = _dot3(xh, xl, wu)
    ah, al = _split(_silu(gate) * up)
    return _dot3(ah, al, wd)


def _ffn_dense_body(x_ref, g_ref, wg_ref, wu_ref, wd_ref, o_ref, xn_ref, acc_ref, *, precise):
    f = pl.program_id(1)

    if not precise:
        xn = _rms(x_ref[...], g_ref[...]).astype(BF16)
        o_ref[...] = x_ref[...] + _swiglu(xn, None, wg_ref[...], wu_ref[...], wd_ref[...])
        return

    @pl.when(f == 0)
    def _():
        xn_ref[0], xn_ref[1] = _split(_rms(x_ref[...], g_ref[...]))
        acc_ref[...] = x_ref[...]

    acc_ref[...] += _swiglu(xn_ref[0], xn_ref[1], wg_ref[...], wu_ref[...], wd_ref[...])

    @pl.when(f == pl.num_programs(1) - 1)
    def _():
        o_ref[...] = acc_ref[...]


def _ffn_dense(x, gain, w_gu, w_down, tm=512):
    precise = _is_precise(w_gu)
    m, k = x.shape
    ff = w_down.shape[0]
    tm = min(tm, m)
    nf = 1 if not precise else ff // (2 * LANES)
    tf = ff // nf
    assert tf % LANES == 0 and tf * nf == ff
    resident = {} if precise else dict(pipeline_mode=pl.Buffered(1))
    return pl.pallas_call(
        functools.partial(_ffn_dense_body, precise=precise),
        grid=(m // tm, nf),
        in_specs=[
            pl.BlockSpec((tm, k), lambda i, f: (i, 0)),
            pl.BlockSpec((1, k), lambda i, f: (0, 0)),
            _wspec(w_gu, (k, tf), lambda i, f: (0, f), **resident),
            _wspec(w_gu, (k, tf), lambda i, f: (0, nf + f), **resident),
            _wspec(w_down, (tf, k), lambda i, f: (f, 0), **resident),
        ],
        out_specs=pl.BlockSpec((tm, k), lambda i, f: (i, 0)),
        out_shape=jax.ShapeDtypeStruct((m, k), F32),
        scratch_shapes=[pltpu.VMEM((1 + precise, tm, k), BF16), pltpu.VMEM((tm, k), F32)],
        compiler_params=_params(("parallel", "arbitrary")),
        name="ffn_dense",
    )(x, gain.reshape(1, k).astype(F32), _warr(w_gu), _warr(w_gu), _warr(w_down))


def _router_body(x_ref, g_ref, wr_ref, br_ref, idx_ref, gate_ref, cnt_ref, carry_ref):
    i = pl.program_id(0)

    @pl.when(i == 0)
    def _():
        carry_ref[...] = jnp.zeros_like(carry_ref)

    tm = x_ref.shape[0]
    xn = _rms(x_ref[...], g_ref[...])
    logits = _dot3(*_split(xn), wr_ref[...]) + br_ref[...]
    lane = lax.broadcasted_iota(I32, logits.shape, 1)
    m1 = jnp.max(logits, axis=-1, keepdims=True)
    i1 = jnp.min(jnp.where(logits == m1, lane, LANES), axis=-1, keepdims=True)
    rest = jnp.where(lane == i1, NEG, logits)
    m2 = jnp.max(rest, axis=-1, keepdims=True)
    i2 = jnp.min(jnp.where(rest == m2, lane, LANES), axis=-1, keepdims=True)
    e2 = jnp.exp(m2 - m1)
    g1 = 1.0 / (1.0 + e2)
    g2 = e2 * g1
    oh1 = lane == i1
    oh2 = lane == i2
    oh = jnp.where(oh1 | oh2, 1.0, 0.0).astype(BF16)
    r = lax.broadcasted_iota(I32, (tm, tm), 0)
    c = lax.broadcasted_iota(I32, (tm, tm), 1)
    below = jnp.where(c < r, 1.0, 0.0).astype(BF16)
    before = jnp.dot(below, oh, preferred_element_type=F32) + carry_ref[0:1, :]
    rk1 = jnp.sum(jnp.where(oh1, before, 0.0), axis=-1, keepdims=True).astype(I32)
    rk2 = jnp.sum(jnp.where(oh2, before, 0.0), axis=-1, keepdims=True).astype(I32)
    idx_ref[...] = jnp.where(lane == 0, i1, jnp.where(lane == 1, i2, jnp.where(lane == 2, rk1, rk2)))
    gate_ref[...] = jnp.where(lane == 0, g1, g2)
    total = carry_ref[0:1, :] + jnp.sum(oh.astype(F32), axis=0, keepdims=True)
    carry_ref[...] = jnp.broadcast_to(total, carry_ref.shape)
    cnt_ref[...] = carry_ref[...]


def _router(x, gain, w_router, b_router, tm=512):
    m, k = x.shape
    tm = min(tm, m)
    wr = jnp.zeros((k, LANES), F32).at[:, :N_EXPERTS].set(w_router)
    br = jnp.full((1, LANES), NEG, F32).at[0, :N_EXPERTS].set(b_router)
    return pl.pallas_call(
        _router_body,
        grid=(m // tm,),
        in_specs=[
            pl.BlockSpec((tm, k), lambda i: (i, 0)),
            pl.BlockSpec((1, k), lambda i: (0, 0)),
            pl.BlockSpec((k, LANES), lambda i: (0, 0)),
            pl.BlockSpec((1, LANES), lambda i: (0, 0)),
        ],
        out_specs=[
            pl.BlockSpec((tm, LANES), lambda i: (i, 0)),
            pl.BlockSpec((tm, LANES), lambda i: (i, 0)),
            pl.BlockSpec((8, LANES), lambda i: (0, 0)),
        ],
        out_shape=[
            jax.ShapeDtypeStruct((m, LANES), I32),
            jax.ShapeDtypeStruct((m, LANES), F32),
            jax.ShapeDtypeStruct((8, LANES), F32),
        ],
        scratch_shapes=[pltpu.VMEM((8, LANES), F32)],
        compiler_params=_params(("arbitrary",)),
        name="moe_router",
    )(x, gain.reshape(1, k).astype(F32), wr, br)


def _row_copy(src, dst, src_row, dst_row, sem):
    return pltpu.make_async_copy(src.at[pl.ds(src_row, 1)], dst.at[pl.ds(dst_row, 1)], sem)


def _scatter_body(d_ref, x_ref, g_ref, xs_in_ref, xs_ref, buf_ref, sem):
    del xs_in_ref
    tm = buf_ref.shape[0]
    buf_ref[...] = _rms(x_ref[...], g_ref[...])

    def issue(t, carry):
        for kk in range(2):
            _row_copy(buf_ref, xs_ref, t, d_ref[0, 0, 2 * t + kk], sem).start()
        return carry

    lax.fori_loop(0, tm, issue, 0, unroll=8)
    for kk in range(2):
        pltpu.make_async_copy(buf_ref, xs_ref.at[pl.ds(0, tm)], sem).wait()


def _moe_scatter(x, gain, dest_tiles, rows, tm):
    m, k = x.shape
    return pl.pallas_call(
        _scatter_body,
        grid=(m // tm,),
        in_specs=[
            pl.BlockSpec((1, 1, 2 * tm), lambda i: (i, 0, 0), memory_space=pltpu.SMEM),
            pl.BlockSpec((tm, k), lambda i: (i, 0)),
            pl.BlockSpec((1, k), lambda i: (0, 0)),
            pl.BlockSpec(memory_space=pl.ANY),
        ],
        out_specs=pl.BlockSpec(memory_space=pl.ANY),
        out_shape=jax.ShapeDtypeStruct((rows, k), F32),
        scratch_shapes=[pltpu.VMEM((tm, k), F32), pltpu.SemaphoreType.DMA(())],
        input_output_aliases={3: 0},
        compiler_params=_params(("arbitrary",)),
        name="moe_scatter",
    )(dest_tiles, x, gain.reshape(1, k).astype(F32), jnp.zeros((rows, k), F32))


def _ffn_group_body(te_ref, nu_ref, x_ref, wg_ref, wu_ref, wd_ref, o_ref, *, precise):
    del te_ref
    used = pl.program_id(0) < nu_ref[0]

    @pl.when(used)
    def _():
        xh, xl = _split(x_ref[...]) if precise else (x_ref[...].astype(BF16), None)
        o_ref[...] = _swiglu(xh, xl, wg_ref[...], wu_ref[...], wd_ref[...])

    @pl.when(jnp.logical_not(used))
    def _():
        o_ref[...] = jnp.zeros_like(o_ref)


def _ffn_grouped(xs, tile_expert, n_used, w_gu, w_down, tm):
    precise = _is_precise(w_gu)
    rows, k = xs.shape
    ff = w_down.shape[1]
    nt = rows // tm

    def row_map(i, te, nu):
        return (jnp.minimum(i, nu[0] - 1), 0)

    grid_spec = pltpu.PrefetchScalarGridSpec(
        num_scalar_prefetch=2,
        grid=(nt,),
        in_specs=[
            pl.BlockSpec((tm, k), row_map),
            _wspec(w_gu, (None, k, ff), lambda i, te, nu: (te[i], 0, 0)),
            _wspec(w_gu, (None, k, ff), lambda i, te, nu: (te[i], 0, 1)),
            _wspec(w_down, (None, ff, k), lambda i, te, nu: (te[i], 0, 0)),
        ],
        out_specs=pl.BlockSpec((tm, k), lambda i, te, nu: (i, 0)),
    )
    return pl.pallas_call(
        functools.partial(_ffn_group_body, precise=precise),
        grid_spec=grid_spec,
        out_shape=jax.ShapeDtypeStruct((rows, k), F32),
        compiler_params=_params(("arbitrary",)),
        name="moe_experts",
    )(tile_expert, n_used, xs, _warr(w_gu), _warr(w_gu), _warr(w_down))


def _combine_body(d_ref, x_ref, gate_ref, ys_ref, o_ref, buf_ref, sem):
    tm = x_ref.shape[0]

    def issue(t, carry):
        for kk in range(2):
            _row_copy(ys_ref, buf_ref.at[kk], d_ref[0, 0, 2 * t + kk], t, sem).start()
        return carry

    lax.fori_loop(0, tm, issue, 0, unroll=8)
    for kk in range(2):
        pltpu.make_async_copy(ys_ref.at[pl.ds(0, tm)], buf_ref.at[kk], sem).wait()
    g = gate_ref[...]
    o_ref[...] = x_ref[...] + g[:, 0:1] * buf_ref[0] + g[:, 1:2] * buf_ref[1]


def _moe_combine(x, gates, dest_tiles, ys, tm):
    m, k = x.shape
    return pl.pallas_call(
        _combine_body,
        grid=(m // tm,),
        in_specs=[
            pl.BlockSpec((1, 1, 2 * tm), lambda i: (i, 0, 0), memory_space=pltpu.SMEM),
            pl.BlockSpec((tm, k), lambda i: (i, 0)),
            pl.BlockSpec((tm, LANES), lambda i: (i, 0)),
            pl.BlockSpec(memory_space=pl.ANY),
        ],
        out_specs=pl.BlockSpec((tm, k), lambda i: (i, 0)),
        out_shape=jax.ShapeDtypeStruct((m, k), F32),
        scratch_shapes=[pltpu.VMEM((2, tm, k), F32), pltpu.SemaphoreType.DMA(())],
        compiler_params=_params(("arbitrary",)),
        name="moe_combine",
    )(dest_tiles, x, gates, ys)


SEG_TOK = 256
SEG_ROWS = 640
EXPERT_TILE = 512


def _router_seg_body(x_ref, g_ref, wr_ref, br_ref, idx_ref, gate_ref, tab_ref, cnt_ref, carry_ref):
    i = pl.program_id(0)

    @pl.when(i == 0)
    def _():
        carry_ref[...] = jnp.zeros_like(carry_ref)

    tm = x_ref.shape[0]
    xn = _rms(x_ref[...], g_ref[...])
    logits = _dot3(*_split(xn), wr_ref[...]) + br_ref[...]
    lane = lax.broadcasted_iota(I32, logits.shape, 1)
    m1 = jnp.max(logits, axis=-1, keepdims=True)
    i1 = jnp.min(jnp.where(logits == m1, lane, LANES), axis=-1, keepdims=True)
    rest = jnp.where(lane == i1, NEG, logits)
    m2 = jnp.max(rest, axis=-1, keepdims=True)
    i2 = jnp.min(jnp.where(rest == m2, lane, LANES), axis=-1, keepdims=True)
    e2 = jnp.exp(m2 - m1)
    g1 = 1.0 / (1.0 + e2)
    g2 = e2 * g1
    oh1 = lane == i1
    oh2 = lane == i2
    oh = jnp.where(oh1 | oh2, 1.0, 0.0)
    r = lax.broadcasted_iota(I32, (tm, tm), 0)
    c = lax.broadcasted_iota(I32, (tm, tm), 1)
    below = jnp.where(c < r, 1.0, 0.0).astype(BF16)
    rank = jnp.dot(below, oh.astype(BF16), preferred_element_type=F32)
    count = jnp.sum(oh, axis=0, keepdims=True)
    padded = jnp.floor((count + 7.0) * 0.125) * 8.0
    er = lax.broadcasted_iota(I32, (LANES, LANES), 0)
    ec = lax.broadcasted_iota(I32, (LANES, LANES), 1)
    start = jnp.dot(jnp.broadcast_to(padded, (8, LANES)), jnp.where(er < ec, 1.0, 0.0),
                    precision=HI, preferred_element_type=F32)[0:1]
    slot = rank + start
    s1 = jnp.sum(jnp.where(oh1, slot, 0.0), axis=-1, keepdims=True).astype(I32)
    s2 = jnp.sum(jnp.where(oh2, slot, 0.0), axis=-1, keepdims=True).astype(I32)
    idx_ref[...] = jnp.where(lane == 0, i1, jnp.where(lane == 1, i2, jnp.where(lane == 2, s1, s2)))
    gate_ref[...] = jnp.where(lane == 0, g1, g2)
    sub = lax.broadcasted_iota(I32, (8, LANES), 0)
    rows = jnp.where(sub == 0, padded * 0.125, jnp.where(sub == 1, start, carry_ref[...]))
    tab_ref[...] = jnp.where(sub < 3, rows, 0.0).astype(I32)
    carry_ref[...] = carry_ref[...] + padded
    cnt_ref[...] = carry_ref[...]


def _router_seg(x, gain, w_router, b_router):
    m, k = x.shape
    tm = SEG_TOK
    wr = jnp.zeros((k, LANES), F32).at[:, :N_EXPERTS].set(w_router)
    br = jnp.full((1, LANES), NEG, F32).at[0, :N_EXPERTS].set(b_router)
    return pl.pallas_call(
        _router_seg_body,
        grid=(m // tm,),
        in_specs=[
            pl.BlockSpec((tm, k), lambda i: (i, 0)),
            pl.BlockSpec((1, k), lambda i: (0, 0)),
            pl.BlockSpec((k, LANES), lambda i: (0, 0)),
            pl.BlockSpec((1, LANES), lambda i: (0, 0)),
        ],
        out_specs=[
            pl.BlockSpec((tm, LANES), lambda i: (i, 0)),
            pl.BlockSpec((tm, LANES), lambda i: (i, 0)),
            pl.BlockSpec((None, 8, LANES), lambda i: (i, 0, 0)),
            pl.BlockSpec((8, LANES), lambda i: (0, 0)),
        ],
        out_shape=[
            jax.ShapeDtypeStruct((m, LANES), I32),
            jax.ShapeDtypeStruct((m, LANES), F32),
            jax.ShapeDtypeStruct((m // tm, 8, LANES), I32),
            jax.ShapeDtypeStruct((8, LANES), F32),
        ],
        scratch_shapes=[pltpu.VMEM((8, LANES), F32)],
        compiler_params=_params(("arbitrary",)),
        name="moe_router_seg",
    )(x, gain.reshape(1, k).astype(F32), wr, br)


def _chunk_copy(src, dst, src_row, dst_row, sem):
    rows = lambda r: pl.ds(r if isinstance(r, int) else pl.multiple_of(r, 8), 8)
    return pltpu.make_async_copy(src.at[rows(src_row)], dst.at[rows(dst_row)], sem)


def _segment_start(tab_ref, base_ref, local_ref, hbm_ref, sem, to_hbm):
    for e in range(N_EXPERTS):
        local0 = tab_ref[1, e]
        hbm0 = base_ref[e] + tab_ref[2, e]

        def issue(j, carry, local0=local0, hbm0=hbm0):
            if to_hbm:
                _chunk_copy(local_ref, hbm_ref, local0 + 8 * j, hbm0 + 8 * j, sem).start()
            else:
                _chunk_copy(hbm_ref, local_ref, hbm0 + 8 * j, local0 + 8 * j, sem).start()
            return carry

        lax.fori_loop(0, tab_ref[0, e], issue, 0)


def _segment_wait(tab_ref, local_ref, hbm_ref, sem, to_hbm):
    total = 0
    for e in range(N_EXPERTS):
        total = total + tab_ref[0, e]

    def drain(j, carry):
        if to_hbm:
            _chunk_copy(local_ref, hbm_ref, 0, 0, sem).wait()
        else:
            _chunk_copy(hbm_ref, local_ref, 0, 0, sem).wait()
        return carry

    lax.fori_loop(0, total, drain, 0)


def _slot_onehot(idx_ref, which, width):
    lane = lax.broadcasted_iota(I32, (idx_ref.shape[0], width), 1)
    return jnp.where(lane == idx_ref[:, 2 + which:3 + which], 1.0, 0.0)


def _scatter_seg_body(base_ref, tab_ref, tab_prev_ref, idx_ref, x_ref, g_ref, xs_ref, buf_ref, zero_ref, sem,
                      fill_sem):
    i = pl.program_id(0)
    slot = i % 2
    xn = _rms(x_ref[...], g_ref[...]).astype(BF16)
    width = buf_ref.shape[1]
    place = (_slot_onehot(idx_ref, 0, width) + _slot_onehot(idx_ref, 1, width)).astype(BF16)
    buf_ref[slot] = lax.dot_general(place, xn, (((0,), (0,)), ((), ())), preferred_element_type=F32)
    _segment_start(tab_ref, base_ref, buf_ref.at[slot], xs_ref, sem.at[slot], to_hbm=True)

    @pl.when(i > 0)
    def _():
        _segment_wait(tab_prev_ref, buf_ref.at[1 - slot], xs_ref, sem.at[1 - slot], to_hbm=True)

    @pl.when(i == pl.num_programs(0) - 1)
    def _():
        _segment_wait(tab_ref, buf_ref.at[slot], xs_ref, sem.at[slot], to_hbm=True)
        zero_ref[...] = jnp.zeros_like(zero_ref)
        tile_rows = zero_ref.shape[0]
        chunks = 0
        for e in range(N_EXPERTS):
            row0 = base_ref[N_EXPERTS + e]

            def fill(j, carry, row0=row0):
                _chunk_copy(zero_ref, xs_ref, 0, row0 + 8 * j, fill_sem).start()
                return carry

            lax.fori_loop(0, base_ref[2 * N_EXPERTS + e], fill, 0)
            chunks = chunks + base_ref[2 * N_EXPERTS + e]

        def tile_copy(j):
            row = pl.multiple_of(base_ref[3 * N_EXPERTS] + tile_rows * j, 8)
            return pltpu.make_async_copy(zero_ref, xs_ref.at[pl.ds(row, tile_rows)], fill_sem)

        def fill_tile(j, carry):
            tile_copy(j).start()
            return carry

        lax.fori_loop(0, base_ref[3 * N_EXPERTS + 1], fill_tile, 0)

        def drain(j, carry):
            _chunk_copy(zero_ref, xs_ref, 0, 0, fill_sem).wait()
            return carry

        lax.fori_loop(0, chunks, drain, 0)

        def drain_tile(j, carry):
            tile_copy(0).wait()
            return carry

        lax.fori_loop(0, base_ref[3 * N_EXPERTS + 1], drain_tile, 0)


def _tab_spec(shift, nt):
    return pl.BlockSpec((None, 8, LANES), lambda i, b: (jnp.clip(i + shift, 0, nt - 1), 0, 0),
                        memory_space=pltpu.SMEM)


def _moe_scatter_seg(x, gain, idx, tab, plan, rows):
    m, k = x.shape
    tm = SEG_TOK
    nt = m // tm
    grid_spec = pltpu.PrefetchScalarGridSpec(
        num_scalar_prefetch=1,
        grid=(nt,),
        in_specs=[
            _tab_spec(0, nt), _tab_spec(-1, nt),
            pl.BlockSpec((tm, LANES), lambda i, b: (i, 0)),
            pl.BlockSpec((tm, k), lambda i, b: (i, 0)),
            pl.BlockSpec((1, k), lambda i, b: (0, 0)),
        ],
        out_specs=pl.BlockSpec(memory_space=pl.ANY),
        scratch_shapes=[pltpu.VMEM((2, SEG_ROWS, k), F32), pltpu.VMEM((EXPERT_TILE, k), F32),
                        pltpu.SemaphoreType.DMA((2,)), pltpu.SemaphoreType.DMA(())],
    )
    return pl.pallas_call(
        _scatter_seg_body,
        grid_spec=grid_spec,
        out_shape=jax.ShapeDtypeStruct((rows, k), F32),
        compiler_params=_params(("arbitrary",)),
        name="moe_scatter_seg",
    )(plan, tab, tab, idx, x, gain.reshape(1, k).astype(F32))


def _combine_seg_body(base_ref, tab_ref, tab_next_ref, idx_ref, x_ref, gate_ref, ys_ref, o_ref, buf_ref, sem):
    i = pl.program_id(0)
    slot = i % 2

    @pl.when(i == 0)
    def _():
        buf_ref[...] = jnp.zeros_like(buf_ref)
        _segment_start(tab_ref, base_ref, buf_ref.at[0], ys_ref, sem.at[0], to_hbm=False)

    @pl.when(i < pl.num_programs(0) - 1)
    def _():
        _segment_start(tab_next_ref, base_ref, buf_ref.at[1 - slot], ys_ref, sem.at[1 - slot], to_hbm=False)

    _segment_wait(tab_ref, buf_ref.at[slot], ys_ref, sem.at[slot], to_hbm=False)
    width = buf_ref.shape[1]
    y = buf_ref[slot].astype(BF16)
    g = gate_ref[...]
    pick = lambda which: jnp.dot(_slot_onehot(idx_ref, which, width).astype(BF16), y, preferred_element_type=F32)
    o_ref[...] = x_ref[...] + g[:, 0:1] * pick(0) + g[:, 1:2] * pick(1)


def _moe_combine_seg(x, gates, idx, tab, base, ys):
    m, k = x.shape
    tm = SEG_TOK
    nt = m // tm
    grid_spec = pltpu.PrefetchScalarGridSpec(
        num_scalar_prefetch=1,
        grid=(nt,),
        in_specs=[
            _tab_spec(0, nt), _tab_spec(1, nt),
            pl.BlockSpec((tm, LANES), lambda i, b: (i, 0)),
            pl.BlockSpec((tm, k), lambda i, b: (i, 0)),
            pl.BlockSpec((tm, LANES), lambda i, b: (i, 0)),
            pl.BlockSpec(memory_space=pl.ANY),
        ],
        out_specs=pl.BlockSpec((tm, k), lambda i, b: (i, 0)),
        scratch_shapes=[pltpu.VMEM((2, SEG_ROWS, k), F32), pltpu.SemaphoreType.DMA((2,))],
    )
    return pl.pallas_call(
        _combine_seg_body,
        grid_spec=grid_spec,
        out_shape=jax.ShapeDtypeStruct((m, k), F32),
        compiler_params=_params(("arbitrary",)),
        name="moe_combine_seg",
    )(base, tab, tab, idx, x, gates, ys)


def _moe_seg(x, gain, w_router, b_router, w_gu, w_down):
    m, k = x.shape
    idx, gates, tab, totals = _router_seg(x, gain, w_router, b_router)
    totals = totals[0, :N_EXPERTS].astype(I32)
    region = ((totals + EXPERT_TILE - 1) // EXPERT_TILE) * EXPERT_TILE
    ends = jnp.cumsum(region)
    base = (ends - region).astype(I32)
    max_rows = 2 * m + N_EXPERTS * (7 * (m // SEG_TOK) + EXPERT_TILE - 1)
    rows = (max_rows // EXPERT_TILE) * EXPERT_TILE
    nt = rows // EXPERT_TILE
    n_used = (ends[-1] // EXPERT_TILE).astype(I32)
    tile_start = jnp.minimum(jnp.arange(nt, dtype=I32), n_used - 1) * EXPERT_TILE
    tile_expert = jnp.sum(tile_start[:, None] >= ends[None, :], axis=1).astype(I32)
    plan = jnp.concatenate([base, base + totals, (region - totals) // 8,
                            jnp.stack([ends[-1], nt * EXPERT_TILE - ends[-1]]) // jnp.array([1, EXPERT_TILE])]).astype(I32)
    xs = _moe_scatter_seg(x, gain, idx, tab, plan, rows)
    ys = _ffn_grouped(xs, tile_expert, n_used.reshape(1), w_gu, w_down, EXPERT_TILE)
    return _moe_combine_seg(x, gates, idx, tab, base, ys)


def _moe(x, gain, w_router, b_router, w_gu, w_down):
    m, k = x.shape
    if m % SEG_TOK == 0 and m >= 8 * SEG_TOK and not _is_precise(w_gu):
        return _moe_seg(x, gain, w_router, b_router, w_gu, w_down)
    tm_tok = min(256, m)
    tm_grp = 512 if m >= 4096 else 128
    idx, gates, counts = _router(x, gain, w_router, b_router)
    counts = counts[0, :N_EXPERTS].astype(I32)
    padded = ((counts + tm_grp - 1) // tm_grp) * tm_grp
    ends = jnp.cumsum(padded)
    starts = ends - padded
    dest = starts[idx[:, 0:2]] + idx[:, 2:4]
    dest_tiles = dest.reshape(m // tm_tok, 1, 2 * tm_tok)
    rows = ((2 * m + N_EXPERTS * (tm_grp - 1)) // tm_grp) * tm_grp
    nt = rows // tm_grp
    n_used = (ends[-1] // tm_grp).astype(I32)
    tile_start = jnp.minimum(jnp.arange(nt, dtype=I32), n_used - 1) * tm_grp
    tile_expert = jnp.sum(tile_start[:, None] >= ends[None, :], axis=1).astype(I32)
    xs = _moe_scatter(x, gain, dest_tiles, rows, tm_tok)
    ys = _ffn_grouped(xs, tile_expert, n_used.reshape(1), w_gu, w_down, tm_grp)
    return _moe_combine(x, gates, dest_tiles, ys, tm_tok)


def _head_cols(x, h, width=GDN_DK):
    return x[:, h * width:(h + 1) * width]


def _gdn_prep_body(x_ref, halo_ref, ab_ref, cw_ref, alog_ref, dtb_ref,
                   q_ref, qg_ref, k_ref, kb_ref, kbg_ref, kd_ref, vb_ref,
                   gc_ref, gct_ref, egl_ref, ext_ref):
    i = pl.program_id(1)
    tt = x_ref.shape[0]
    ext_ref[0:8, :] = jnp.where(i == 0, 0.0, halo_ref[...])
    ext_ref[8:, :] = x_ref[...]
    ext = ext_ref[...]
    acc = pltpu.roll(ext, CONV_TAPS - 1, axis=0)[8:] * cw_ref[0:1, :]
    for j in range(1, CONV_TAPS - 1):
        acc = acc + pltpu.roll(ext, CONV_TAPS - 1 - j, axis=0)[8:] * cw_ref[j:j + 1, :]
    acc = acc + ext[8:] * cw_ref[CONV_TAPS - 1:CONV_TAPS, :]
    y = _silu(acc)

    ab = ab_ref[...]
    g = -jnp.exp(alog_ref[...]) * jax.nn.softplus(ab + dtb_ref[...])
    beta = jax.nn.sigmoid(pltpu.roll(ab, LANES - GDN_HEADS, axis=1))
    r = lax.broadcasted_iota(I32, (tt, tt), 0)
    c = lax.broadcasted_iota(I32, (tt, tt), 1)
    same = (r // CHUNK) == (c // CHUNK)
    tril = jnp.where(same & (c <= r), 1.0, 0.0)
    gc = jnp.dot(tril, g, precision=HI, preferred_element_type=F32)
    gl = jnp.dot(jnp.where(same, 1.0, 0.0), g, precision=HI, preferred_element_type=F32)
    e_gc = jnp.exp(gc)
    e_gd = jnp.exp(gl - gc)
    gc_ref[...] = gc
    gct = gc.T
    sub = lax.broadcasted_iota(I32, (8, LANES), 0)
    lan = lax.broadcasted_iota(I32, (8, LANES), 1)
    for cc in range(tt // CHUNK):
        gct_ref[cc] = gct[0:8, cc * CHUNK:(cc + 1) * CHUNK]
        row = jnp.broadcast_to(gl[cc * CHUNK:cc * CHUNK + 1, :], (8, LANES))
        val = jnp.sum(jnp.where(sub == lan, row, 0.0), axis=-1, keepdims=True)
        egl_ref[cc] = jnp.broadcast_to(jnp.exp(val), (8, LANES))

    nq = GDN_HEADS * GDN_DK
    for h in range(GDN_HEADS):
        qh = _head_cols(y, h)
        kh = _head_cols(y, GDN_HEADS + h)
        vh = _head_cols(y, 2 * GDN_HEADS + h)
        qn = qh * lax.rsqrt(jnp.sum(qh * qh, axis=-1, keepdims=True) + EPS) * (GDN_DK ** -0.5)
        kn = kh * lax.rsqrt(jnp.sum(kh * kh, axis=-1, keepdims=True) + EPS)
        b_h = beta[:, h:h + 1]
        eg_h = e_gc[:, h:h + 1]
        sl = slice(h * GDN_DK, (h + 1) * GDN_DK)
        kb = kn * b_h
        q_ref[:, sl] = qn.astype(BF16)
        qg_ref[:, sl] = (qn * eg_h).astype(BF16)
        k_ref[:, sl] = kn.astype(BF16)
        kb_ref[:, sl] = kb.astype(BF16)
        kbg_ref[:, sl] = (kb * eg_h).astype(BF16)
        kd_ref[:, sl] = (kn * e_gd[:, h:h + 1]).astype(BF16)
        vb_ref[:, sl] = (vh * b_h).astype(BF16)
    del nq


def _gdn_prep(proj, ab, conv_w, a_log, dt_bias, tt=256):
    b, t, _ = proj.shape
    nc = t // CHUNK
    cpt = tt // CHUNK
    row = lambda v: jnp.zeros((1, LANES), F32).at[0, :GDN_HEADS].set(v)
    tok = pl.BlockSpec((None, tt, D_MODEL), lambda bi, i: (bi, i, 0))
    out_shapes = [jax.ShapeDtypeStruct((b, t, D_MODEL), BF16)] * 7 + [
        jax.ShapeDtypeStruct((b, t, LANES), F32),
        jax.ShapeDtypeStruct((b, nc, 8, CHUNK), F32),
        jax.ShapeDtypeStruct((b, nc, 8, LANES), F32),
    ]
    out_specs = [tok] * 7 + [
        pl.BlockSpec((None, tt, LANES), lambda bi, i: (bi, i, 0)),
        pl.BlockSpec((None, cpt, 8, CHUNK), lambda bi, i: (bi, i, 0, 0)),
        pl.BlockSpec((None, cpt, 8, LANES), lambda bi, i: (bi, i, 0, 0)),
    ]
    return pl.pallas_call(
        _gdn_prep_body,
        grid=(b, t // tt),
        in_specs=[
            pl.BlockSpec((None, tt, QKV_W), lambda bi, i: (bi, i, 0)),
            pl.BlockSpec((None, 8, QKV_W), lambda bi, i: (bi, jnp.maximum(i * (tt // 8) - 1, 0), 0)),
            pl.BlockSpec((None, tt, LANES), lambda bi, i: (bi, i, 0)),
            pl.BlockSpec((CONV_TAPS, QKV_W), lambda bi, i: (0, 0)),
            pl.BlockSpec((1, LANES), lambda bi, i: (0, 0)),
            pl.BlockSpec((1, LANES), lambda bi, i: (0, 0)),
        ],
        out_specs=out_specs,
        out_shape=out_shapes,
        scratch_shapes=[pltpu.VMEM((tt + 8, QKV_W), F32)],
        compiler_params=_params(("parallel", "arbitrary")),
        name="gdn_prep",
    )(proj, proj, ab, conv_w.astype(F32), row(a_log), row(dt_bias))


def _decay(gc_col, gc_row, keep):
    return jnp.where(keep, jnp.exp(jnp.where(keep, gc_col - gc_row, 0.0)), 0.0)


def _gdn_l_body(kb_ref, k_ref, gc_ref, gct_ref, l_ref):
    r = lax.broadcasted_iota(I32, (CHUNK, CHUNK), 0)
    c = lax.broadcasted_iota(I32, (CHUNK, CHUNK), 1)
    strict = c < r
    for cc in range(l_ref.shape[0]):
        rows = slice(cc * CHUNK, (cc + 1) * CHUNK)
        for h in range(GDN_HEADS):
            cols = slice(h * GDN_DK, (h + 1) * GDN_DK)
            kk = lax.dot_general(kb_ref[rows, cols], k_ref[rows, cols], (((1,), (1,)), ((), ())),
                                 preferred_element_type=F32)
            dec = _decay(gc_ref[rows, h:h + 1], gct_ref[cc, h:h + 1, :], strict)
            l_ref[cc, :, h * CHUNK:(h + 1) * CHUNK] = kk * dec


def _gdn_l(kb, k, gc, gct, cpt=4):
    b, t, _ = k.shape
    nc = t // CHUNK
    tt = cpt * CHUNK
    tok = pl.BlockSpec((None, tt, D_MODEL), lambda bi, i: (bi, i, 0))
    return pl.pallas_call(
        _gdn_l_body,
        grid=(b, nc // cpt),
        in_specs=[tok, tok,
                  pl.BlockSpec((None, tt, LANES), lambda bi, i: (bi, i, 0)),
                  pl.BlockSpec((None, cpt, 8, CHUNK), lambda bi, i: (bi, i, 0, 0))],
        out_specs=pl.BlockSpec((None, cpt, CHUNK, GDN_HEADS * CHUNK), lambda bi, i: (bi, i, 0, 0)),
        out_shape=jax.ShapeDtypeStruct((b, nc, CHUNK, GDN_HEADS * CHUNK), F32),
        compiler_params=_params(("parallel", "parallel")),
        name="gdn_l",
    )(kb, k, gc, gct)


def _tri_inv_body(l_ref, t_ref):
    sub = lax.broadcasted_iota(I32, (8, LANES), 0)
    for i in range(CHUNK):
        nq = i // 8 + 1
        acc = [jnp.where(sub == i % 8, 1.0, 0.0) if q == i // 8 else jnp.zeros((8, LANES), F32)
               for q in range(nq)]
        for j in range(i):
            lij = l_ref[i, j:j + 1, :]
            for q in range(j // 8 + 1):
                acc[q] = acc[q] - lij * t_ref[j, 8 * q:8 * q + 8, :]
        for q in range(CHUNK // 8):
            t_ref[i, 8 * q:8 * q + 8, :] = acc[q] if q < nq else jnp.zeros((8, LANES), F32)


def _tri_inv(lt):
    _, nh, _, p = lt.shape
    spec = pl.BlockSpec((CHUNK, None, CHUNK, LANES), lambda h, i: (0, h, 0, i))
    return pl.pallas_call(
        _tri_inv_body,
        grid=(nh, p // LANES),
        in_specs=[spec],
        out_specs=spec,
        out_shape=jax.ShapeDtypeStruct(lt.shape, F32),
        compiler_params=_params(("parallel", "parallel")),
        name="gdn_tri_inv",
    )(lt)


def _gdn_wy_body(t_ref, vb_ref, kbg_ref, q_ref, k_ref, qg_ref, gc_ref, gct_ref, u_ref, wq_ref, attn_ref):
    r = lax.broadcasted_iota(I32, (CHUNK, CHUNK), 0)
    c = lax.broadcasted_iota(I32, (CHUNK, CHUNK), 1)
    tril = c <= r
    for cc in range(t_ref.shape[0]):
        rows = slice(cc * CHUNK, (cc + 1) * CHUNK)
        for h in range(GDN_HEADS):
            cols = slice(h * GDN_DK, (h + 1) * GDN_DK)
            hc = slice(h * CHUNK, (h + 1) * CHUNK)
            rhs = jnp.concatenate([vb_ref[rows, cols], kbg_ref[rows, cols]], axis=1)
            sol = jnp.dot(t_ref[cc, :, hc].astype(BF16), rhs, preferred_element_type=F32)
            u_ref[rows, cols] = sol[:, :GDN_DK]
            wq_ref[cc, 0:CHUNK, cols] = sol[:, GDN_DK:].astype(BF16)
            wq_ref[cc, CHUNK:2 * CHUNK, cols] = qg_ref[rows, cols]
            qk = lax.dot_general(q_ref[rows, cols], k_ref[rows, cols], (((1,), (1,)), ((), ())),
                                 preferred_element_type=F32)
            dec = _decay(gc_ref[rows, h:h + 1], gct_ref[cc, h:h + 1, :], tril)
            attn_ref[cc, :, hc] = (qk * dec).astype(BF16)


def _gdn_wy(tinv, vb, kbg, q, k, qg, gc, gct, cpt=4):
    b, t, _ = k.shape
    nc = t // CHUNK
    tt = cpt * CHUNK
    tok = pl.BlockSpec((None, tt, D_MODEL), lambda bi, i: (bi, i, 0))
    sq = pl.BlockSpec((None, cpt, CHUNK, GDN_HEADS * CHUNK), lambda bi, i: (bi, i, 0, 0))
    return pl.pallas_call(
        _gdn_wy_body,
        grid=(b, nc // cpt),
        in_specs=[sq, tok, tok, tok, tok, tok,
                  pl.BlockSpec((None, tt, LANES), lambda bi, i: (bi, i, 0)),
                  pl.BlockSpec((None, cpt, 8, CHUNK), lambda bi, i: (bi, i, 0, 0))],
        out_specs=[tok,
                   pl.BlockSpec((None, cpt, 2 * CHUNK, D_MODEL), lambda bi, i: (bi, i, 0, 0)),
                   sq],
        out_shape=[jax.ShapeDtypeStruct((b, t, D_MODEL), F32),
                   jax.ShapeDtypeStruct((b, nc, 2 * CHUNK, D_MODEL), BF16),
                   jax.ShapeDtypeStruct((b, nc, CHUNK, GDN_HEADS * CHUNK), BF16)],
        compiler_params=_params(("parallel", "parallel")),
        name="gdn_wy",
    )(tinv, vb, kbg, q, k, qg, gc, gct)


def _gdn_chunk_body(wq_ref, u_ref, attn_ref, kd_ref, egl_ref, o_ref, s_out_ref, sa_ref, sb_ref):
    c = pl.program_id(0)

    @pl.when(c == 0)
    def _():
        sa_ref[...] = jnp.zeros_like(sa_ref)

    def step(src_ref, dst_ref):
        heads = [(b, h) for b in range(src_ref.shape[0]) for h in range(GDN_HEADS)]
        cols = lambda h: slice(h * GDN_DK, (h + 1) * GDN_DK)
        ws = [jnp.dot(wq_ref[b, :, cols(h)], src_ref[b, h].astype(BF16), preferred_element_type=F32)
              for b, h in heads]
        v_nb = [(u_ref[b, :, cols(h)] - w_[:CHUNK]).astype(BF16) for (b, h), w_ in zip(heads, ws)]
        for (b, h), w_, v_ in zip(heads, ws, v_nb):
            o_ref[b, :, cols(h)] = w_[CHUNK:] + jnp.dot(attn_ref[b, :, h * CHUNK:(h + 1) * CHUNK], v_,
                                                        preferred_element_type=F32)
        for (b, h), v_ in zip(heads, v_nb):
            dst_ref[b, h] = src_ref[b, h] * egl_ref[b, h:h + 1, :] + lax.dot_general(
                kd_ref[b, :, cols(h)], v_, (((0,), (0,)), ((), ())), preferred_element_type=F32)

        @pl.when(c == pl.num_programs(0) - 1)
        def _():
            s_out_ref[...] = dst_ref[...]

    @pl.when(c % 2 == 0)
    def _():
        step(sa_ref, sb_ref)

    @pl.when(c % 2 == 1)
    def _():
        step(sb_ref, sa_ref)


def _gdn_chunks(wq, u, attn, kd, egl):
    b, t, _ = u.shape
    nc = t // CHUNK
    tok = pl.BlockSpec((b, CHUNK, D_MODEL), lambda c: (0, c, 0))
    return pl.pallas_call(
        _gdn_chunk_body,
        grid=(nc,),
        in_specs=[
            pl.BlockSpec((b, None, 2 * CHUNK, D_MODEL), lambda c: (0, c, 0, 0)),
            tok,
            pl.BlockSpec((b, None, CHUNK, GDN_HEADS * CHUNK), lambda c: (0, c, 0, 0)),
            tok,
            pl.BlockSpec((b, None, 8, LANES), lambda c: (0, c, 0, 0)),
        ],
        out_specs=[
            tok,
            pl.BlockSpec((b, GDN_HEADS, GDN_DK, GDN_DK), lambda c: (0, 0, 0, 0)),
        ],
        out_shape=[
            jax.ShapeDtypeStruct((b, t, D_MODEL), F32),
            jax.ShapeDtypeStruct((b, GDN_HEADS, GDN_DK, GDN_DK), F32),
        ],
        scratch_shapes=[pltpu.VMEM((b, GDN_HEADS, GDN_DK, GDN_DK), F32)] * 2,
        compiler_params=_params(("arbitrary",)),
        name="gdn_chunks",
    )(wq, u, attn, kd, egl)


def _gdn_out_body(o_ref, z_ref, on_ref, w_ref, r_ref, out_ref, *, precise):
    parts = []
    for h in range(GDN_HEADS):
        oh = _head_cols(o_ref[...], h)
        zh = _head_cols(z_ref[...], h)
        parts.append(_rms(oh, on_ref[...]) * _silu(zh))
    if precise:
        gh, gl = _split(jnp.concatenate(parts, axis=1))
        out_ref[...] = r_ref[...] + _dot3(gh, gl, w_ref[...])
    else:
        gated = jnp.concatenate([a.astype(BF16) for a in parts], axis=1)
        out_ref[...] = r_ref[...] + jnp.dot(gated, w_ref[...], preferred_element_type=F32)


def _gdn_out(o, proj, o_norm, w_out, res, tm=512):
    m, k = o.shape
    tm = min(tm, m)
    return pl.pallas_call(
        functools.partial(_gdn_out_body, precise=_is_precise(w_out)),
        grid=(m // tm,),
        in_specs=[
            pl.BlockSpec((tm, k), lambda i: (i, 0)),
            pl.BlockSpec((tm, k), lambda i: (i, QKV_W // D_MODEL)),
            pl.BlockSpec((1, GDN_DK), lambda i: (0, 0)),
            _wspec(w_out, (k, k), lambda i: (0, 0)),
            pl.BlockSpec((tm, k), lambda i: (i, 0)),
        ],
        out_specs=pl.BlockSpec((tm, k), lambda i: (i, 0)),
        out_shape=jax.ShapeDtypeStruct((m, k), F32),
        compiler_params=_params(("parallel",)),
        name="gdn_out",
    )(o, proj, o_norm.reshape(1, GDN_DK).astype(F32), _warr(w_out), res)


def _gdn_prompt_layer(h, l, p, w):
    b, t, _ = h.shape
    hf = h.reshape(b * t, D_MODEL)
    proj, ab = _mm(hf, w['a_w_in'][l], gain=p['a_norm'][l], w_aux=w['a_w_ab'][l], tm=512, tn=PROJ_W)
    proj3 = proj.reshape(b, t, PROJ_W)
    q, qg, k, kb, kbg, kd, vb, gc, gct, egl = _gdn_prep(
        proj3, ab.reshape(b, t, LANES), p['a_conv_w'][l], p['a_A_log'][l], p['a_dt_bias'][l])
    lmat = _gdn_l(kb, k, gc, gct)
    nprob = b * (t // CHUNK)
    ppad = -nprob % LANES
    lt = jnp.transpose(lmat.reshape(nprob, CHUNK, GDN_HEADS * CHUNK), (1, 2, 0))
    lt = jnp.pad(lt, ((0, 0), (0, 0), (0, ppad))).reshape(CHUNK, GDN_HEADS, CHUNK, nprob + ppad)
    tt = _tri_inv(lt).reshape(CHUNK, GDN_HEADS * CHUNK, nprob + ppad)[:, :, :nprob]
    tinv = jnp.transpose(tt, (2, 0, 1)).reshape(b, t // CHUNK, CHUNK, GDN_HEADS * CHUNK)
    u, wq, attn = _gdn_wy(tinv, vb, kbg, q, k, qg, gc, gct)
    o, s_new = _gdn_chunks(wq, u, attn, kd, egl)
    h_new = _gdn_out(o.reshape(b * t, D_MODEL), proj, p['a_o_norm'][l], w['a_w_out'][l], hf)
    conv_new = proj3[:, t - (CONV_TAPS - 1):, :QKV_W]
    return h_new.reshape(b, t, D_MODEL), s_new, conv_new


def _gdn_prep1_body(x_ref, st_ref, ab_ref, cw_ref, alog_ref, dtb_ref,
                    q_ref, k_ref, v_ref, eg_ref, beta_ref):
    acc = st_ref[0] * cw_ref[0:1, :]
    for j in range(1, CONV_TAPS - 1):
        acc = acc + st_ref[j] * cw_ref[j:j + 1, :]
    acc = acc + x_ref[...] * cw_ref[CONV_TAPS - 1:CONV_TAPS, :]
    y = _silu(acc)
    ab = ab_ref[...]
    eg_ref[...] = jnp.exp(-jnp.exp(alog_ref[...]) * jax.nn.softplus(ab + dtb_ref[...]))
    beta_ref[...] = jax.nn.sigmoid(pltpu.roll(ab, LANES - GDN_HEADS, axis=1))
    for h in range(GDN_HEADS):
        qh = _head_cols(y, h)
        kh = _head_cols(y, GDN_HEADS + h)
        sl = slice(h * GDN_DK, (h + 1) * GDN_DK)
        q_ref[:, sl] = qh * lax.rsqrt(jnp.sum(qh * qh, axis=-1, keepdims=True) + EPS) * (GDN_DK ** -0.5)
        k_ref[:, sl] = kh * lax.rsqrt(jnp.sum(kh * kh, axis=-1, keepdims=True) + EPS)
    v_ref[...] = y[:, 2 * GDN_HEADS * GDN_DK:]


def _gdn_prep1(proj, ab, conv_state, conv_w, a_log, dt_bias):
    n = proj.shape[0]
    row = lambda v: jnp.zeros((1, LANES), F32).at[0, :GDN_HEADS].set(v)
    tok = pl.BlockSpec((n, D_MODEL), lambda i: (0, 0))
    small = pl.BlockSpec((n, LANES), lambda i: (0, 0))
    return pl.pallas_call(
        _gdn_prep1_body,
        grid=(1,),
        in_specs=[
            pl.BlockSpec((n, QKV_W), lambda i: (0, 0)),
            pl.BlockSpec((CONV_TAPS - 1, n, QKV_W), lambda i: (0, 0, 0)),
            pl.BlockSpec((n, LANES), lambda i: (0, 0)),
            pl.BlockSpec((CONV_TAPS, QKV_W), lambda i: (0, 0)),
            pl.BlockSpec((1, LANES), lambda i: (0, 0)),
            pl.BlockSpec((1, LANES), lambda i: (0, 0)),
        ],
        out_specs=[tok, tok, tok, small, small],
        out_shape=[jax.ShapeDtypeStruct((n, D_MODEL), F32)] * 3 + [jax.ShapeDtypeStruct((n, LANES), F32)] * 2,
        compiler_params=_params(("arbitrary",)),
        name="gdn_prep1",
    )(proj, conv_state, ab, conv_w.astype(F32), row(a_log), row(dt_bias))


def _gdn_step_body(qt_ref, kt_ref, v_ref, eg_ref, beta_ref, s_ref, o_ref, s_out_ref, *, first):
    def compute():
        for b in range(s_ref.shape[0]):
            for h in range(GDN_HEADS):
                k_col = kt_ref[b, :, h:h + 1]
                q_col = qt_ref[b, :, h:h + 1]
                s = s_ref[b, h] * eg_ref[b, h:h + 1, :]
                kv_mem = jnp.sum(k_col * s, axis=0, keepdims=True)
                delta = (v_ref[b, h:h + 1, :] - kv_mem) * beta_ref[b, h:h + 1, :]
                s = s + k_col * delta
                s_out_ref[b, h] = s
                o_ref[b, h:h + 1, :] = jnp.sum(q_col * s, axis=0, keepdims=True)

    if first:
        pl.when(pl.program_id(1) == 0)(compute)

        @pl.when(pl.program_id(1) > 0)
        def _():
            s_out_ref[...] = s_ref[...]
    else:
        compute()


def _gdn_step(qt, kt, v, eg, beta, states, layer, carry, sb=4):
    nl, n = states.shape[:2]
    first = carry is None
    assert first == (layer == 0)
    blk = (None, sb, GDN_HEADS, GDN_DK, GDN_DK)
    if first:
        grid = (n // sb, nl)
        vec = lambda i, j: (i, 0, 0)
        st = pl.BlockSpec(blk, lambda i, j: (j, i, 0, 0, 0))
        sem = ("parallel", "arbitrary")
    else:
        grid = (n // sb,)
        vec = lambda i: (i, 0, 0)
        st = pl.BlockSpec(blk, lambda i: (layer, i, 0, 0, 0))
        sem = ("parallel",)
    col = pl.BlockSpec((sb, GDN_DK, GDN_HEADS), vec)
    row = pl.BlockSpec((sb, GDN_HEADS, GDN_DK), vec)
    return pl.pallas_call(
        functools.partial(_gdn_step_body, first=first),
        grid=grid,
        in_specs=[col, col, row, row, row, st],
        out_specs=[row, st],
        out_shape=[jax.ShapeDtypeStruct((n, GDN_HEADS, GDN_DK), F32),
                   jax.ShapeDtypeStruct(states.shape, F32)],
        input_output_aliases={} if first else {5: 1},
        compiler_params=_params(sem),
        name="gdn_step",
    )(qt, kt, v, eg, beta, states if first else carry)


def _gdn_sample_layer(h, l, p, w, conv_state, delta_states, carry):
    n = h.shape[0]
    proj, ab = _mm(h, w['a_w_in'][l], gain=p['a_norm'][l], w_aux=w['a_w_ab'][l])
    conv_t = jnp.transpose(conv_state, (1, 0, 2))
    q, k, v, eg, beta = _gdn_prep1(proj, ab, conv_t, p['a_conv_w'][l], p['a_A_log'][l], p['a_dt_bias'][l])
    heads = lambda a: a.reshape(n, GDN_HEADS, GDN_DK)
    cols = lambda a: jnp.transpose(heads(a), (0, 2, 1))
    lanes = lambda a: jnp.broadcast_to(a[:, :GDN_HEADS, None], (n, GDN_HEADS, GDN_DK))
    o, s_new = _gdn_step(cols(q), cols(k), heads(v), lanes(eg), lanes(beta), delta_states, l, carry)
    h_new = _gdn_out(o.reshape(n, D_MODEL), proj, p['a_o_norm'][l], w['a_w_out'][l], h)
    conv_new = jnp.concatenate([conv_state[:, 1:], proj[:, None, :QKV_W]], axis=1)
    return h_new, s_new, conv_new


def _rope_tables(pos):
    half = ROT // 2
    inv = ROPE_THETA ** (-jnp.arange(0, ROT, 2, dtype=F32) / ROT)
    ang = pos.astype(F32)[:, None] * inv[None, :]
    cos, sin = jnp.cos(ang), jnp.sin(ang)
    ones = jnp.ones((pos.shape[0], HEAD_DIM - ROT), F32)
    zeros = jnp.zeros_like(ones)
    z8 = jnp.zeros_like(sin)
    c = jnp.concatenate([cos, cos, ones], axis=1)
    sm = jnp.concatenate([-sin, z8, zeros], axis=1)
    sp = jnp.concatenate([z8, sin, zeros], axis=1)
    del half
    return tuple(jnp.tile(a, (1, LANES // HEAD_DIM)) for a in (c, sm, sp))


def _dot_split(a, sel):
    hi = a.astype(BF16)
    lo = (a - hi.astype(F32)).astype(BF16)
    return (jnp.dot(hi, sel, preferred_element_type=F32) + jnp.dot(lo, sel, preferred_element_type=F32))


def _headnorm_rope_body(x_ref, g_ref, seg_ref, segt_ref, c_ref, sm_ref, sp_ref, o_ref):
    x = x_ref[...]
    w = x.shape[1]
    ms = _dot_split(x * x, seg_ref[...]) * (1.0 / HEAD_DIM)
    scale = _dot_split(lax.rsqrt(ms + EPS), segt_ref[...])
    y = x * scale * g_ref[...]
    reps = w // LANES
    tile = lambda a: jnp.concatenate([a] * reps, axis=1) if reps > 1 else a
    half = ROT // 2
    out = (y * tile(c_ref[...]) + pltpu.roll(y, w - half, axis=1) * tile(sm_ref[...])
           + pltpu.roll(y, half, axis=1) * tile(sp_ref[...]))
    o_ref[...] = out.astype(o_ref.dtype)


def _headnorm_rope(x, col_block, width, gain, tables, rows_per_seq, out_dtype, tm=512):
    m = x.shape[0]
    tm = min(tm, m, rows_per_seq)
    nheads = width // HEAD_DIM
    seg = (jnp.arange(width)[:, None] // HEAD_DIM == jnp.arange(LANES)[None, :]).astype(BF16)
    per_seq = rows_per_seq // tm
    tab = pl.BlockSpec((tm, LANES), lambda i: (i % per_seq, 0))
    del nheads
    return pl.pallas_call(
        _headnorm_rope_body,
        grid=(m // tm,),
        in_specs=[
            pl.BlockSpec((tm, width), lambda i: (i, col_block)),
            pl.BlockSpec((1, width), lambda i: (0, 0)),
            pl.BlockSpec((width, LANES), lambda i: (0, 0)),
            pl.BlockSpec((LANES, width), lambda i: (0, 0)),
            tab, tab, tab,
        ],
        out_specs=pl.BlockSpec((tm, width), lambda i: (i, 0)),
        out_shape=jax.ShapeDtypeStruct((m, width), out_dtype),
        compiler_params=_params(("parallel",)),
        name="headnorm_rope",
    )(x, jnp.tile(gain.astype(F32), width // HEAD_DIM).reshape(1, width), seg, seg.T, *tables)


def _softmax_sink(s, sink):
    m = jnp.maximum(jnp.max(s, axis=-1, keepdims=True), sink)
    p = jnp.exp(s - m)
    return p, jnp.sum(p, axis=-1, keepdims=True) + jnp.exp(sink - m)


def _swa_prompt_body(sink_ref, q_ref, kp_ref, kc_ref, vp_ref, vc_ref, o_ref):
    i = pl.program_id(1)
    blk = q_ref.shape[0]
    r = lax.broadcasted_iota(I32, (blk, 2 * blk), 0)
    c = lax.broadcasted_iota(I32, (blk, 2 * blk), 1)
    valid = (c >= r) & (c <= r + WINDOW) & ((c >= blk) | (i > 0))
    for kvh in range(KV_HEADS):
        cols = slice(kvh * HEAD_DIM, (kvh + 1) * HEAD_DIM)
        kk = jnp.concatenate([kp_ref[:, cols], kc_ref[:, cols]], axis=0).astype(BF16)
        vv = jnp.concatenate([vp_ref[:, cols], vc_ref[:, cols]], axis=0).astype(BF16)
        for g in range(ATT_GROUP):
            h = kvh * ATT_GROUP + g
            hc = slice(h * HEAD_DIM, (h + 1) * HEAD_DIM)
            s = lax.dot_general(q_ref[:, hc], kk, (((1,), (1,)), ((), ())),
                                preferred_element_type=F32) * (HEAD_DIM ** -0.5)
            s = jnp.where(valid, s, -jnp.inf)
            p, den = _softmax_sink(s, sink_ref[h])
            o_ref[:, hc] = jnp.dot((p / den).astype(BF16), vv, preferred_element_type=F32).astype(o_ref.dtype)


def _swa_prompt(q, k, kv, sinks):
    b, t, _ = q.shape
    blk = WINDOW
    kvw = KV_HEADS * HEAD_DIM
    prev = lambda bi, i, s: (bi, jnp.maximum(i - 1, 0), 0)
    cur = lambda bi, i, s: (bi, i, 0)
    prev_v = lambda bi, i, s: (bi, jnp.maximum(i - 1, 0), 1)
    cur_v = lambda bi, i, s: (bi, i, 1)
    grid_spec = pltpu.PrefetchScalarGridSpec(
        num_scalar_prefetch=1,
        grid=(b, t // blk),
        in_specs=[
            pl.BlockSpec((None, blk, D_MODEL), cur),
            pl.BlockSpec((None, blk, kvw), prev),
            pl.BlockSpec((None, blk, kvw), cur),
            pl.BlockSpec((None, blk, kvw), prev_v),
            pl.BlockSpec((None, blk, kvw), cur_v),
        ],
        out_specs=pl.BlockSpec((None, blk, D_MODEL), cur),
    )
    return pl.pallas_call(
        _swa_prompt_body,
        grid_spec=grid_spec,
        out_shape=jax.ShapeDtypeStruct((b, t, D_MODEL), BF16),
        compiler_params=_params(("parallel", "parallel")),
        name="swa_prompt",
    )(sinks.astype(F32), q, k, k, kv, kv)


def _swa_sample_body(sink_ref, q_ref, kc_ref, kn_ref, vc_ref, vn_ref, o_ref):
    wc = kc_ref.shape[1]
    kvw = kc_ref.shape[2]
    nkeys = 2 * wc
    fill = jnp.zeros((nkeys - wc - 8, kvw), F32)
    key_idx = lax.broadcasted_iota(I32, (ATT_HEADS, nkeys), 1)
    sink = sink_ref[:, 0:1]
    seqs = range(q_ref.shape[0])
    scores = []
    for b in seqs:
        kk = jnp.concatenate([kc_ref[b], kn_ref[b], fill], axis=0)
        scores.append(lax.dot_general(q_ref[b], kk, (((1,), (1,)), ((), ())), precision=HI,
                                      preferred_element_type=F32))
    probs = []
    for s in scores:
        s = jnp.where(key_idx <= wc, s * (HEAD_DIM ** -0.5), -jnp.inf)
        p, den = _softmax_sink(s, sink)
        probs.append(p / den)
    outs = []
    for b, p in zip(seqs, probs):
        vv = jnp.concatenate([vc_ref[b], vn_ref[b], fill], axis=0)
        outs.append(jnp.dot(p, vv, precision=HI, preferred_element_type=F32))
    for b, o in zip(seqs, outs):
        for kvh in range(KV_HEADS):
            heads = slice(kvh * ATT_GROUP, (kvh + 1) * ATT_GROUP)
            o_ref[b, heads, :] = o[heads, kvh * HEAD_DIM:(kvh + 1) * HEAD_DIM]


def _swa_sample(q, k_cache, k_new, v_cache, v_new, sinks, sb=8):
    n = q.shape[0]
    wc = k_cache.shape[1]
    kvw = KV_HEADS * HEAD_DIM
    own = (jnp.arange(ATT_HEADS)[:, None] // ATT_GROUP == jnp.arange(KV_HEADS)[None, :])
    q_rows = (q[:, :, None, :] * own[None, :, :, None].astype(q.dtype)).reshape(n, ATT_HEADS, kvw)
    pad8 = lambda a: jnp.pad(a[:, None, :], ((0, 0), (0, 7), (0, 0)))
    blk = lambda shape: pl.BlockSpec(shape, lambda i: (i, 0, 0))
    return pl.pallas_call(
        _swa_sample_body,
        grid=(n // sb,),
        in_specs=[pl.BlockSpec((ATT_HEADS, LANES), lambda i: (0, 0)),
                  blk((sb, ATT_HEADS, kvw)), blk((sb, wc, kvw)), blk((sb, 8, kvw)),
                  blk((sb, wc, kvw)), blk((sb, 8, kvw))],
        out_specs=blk((sb, ATT_HEADS, HEAD_DIM)),
        out_shape=jax.ShapeDtypeStruct((n, ATT_HEADS, HEAD_DIM), F32),
        compiler_params=_params(("parallel",)),
        name="swa_sample",
    )(jnp.broadcast_to(sinks.astype(F32)[:, None], (ATT_HEADS, LANES)), q_rows, k_cache, pad8(k_new),
      v_cache, pad8(v_new))


def _channel_mixer(hf, l, p, w):
    if l % 2 == 0:
        return _ffn_dense(hf, p['ffn_norm'][l], w['dense_w_gu'][l // 2], w['dense_w_down'][l // 2])
    return _moe(hf, p['ffn_norm'][l], p['moe_router'][l // 2], p['moe_router_b'][l // 2],
                w['moe_w_gu'][l // 2], w['moe_w_down'][l // 2])


def _run_prompt(x, p, w):
    b, t, _ = x.shape
    n = b * t
    h = x
    convs, deltas = [], []
    for l in range(N_GDN):
        h, s_new, c_new = _gdn_prompt_layer(h, l, p, w)
        convs.append(c_new)
        deltas.append(s_new)
        h = _channel_mixer(h.reshape(n, D_MODEL), l, p, w).reshape(b, t, D_MODEL)
    hf = h.reshape(n, D_MODEL)
    tables = _rope_tables(jnp.arange(t, dtype=I32))
    kvw = KV_HEADS * HEAD_DIM
    kv = _mm(hf, w['w_kv'], gain=p['kv_norm'], tn=2 * kvw)
    k_sh = _headnorm_rope(kv, 0, kvw, p['k_norm'], tables, t, F32)
    k3 = k_sh.reshape(b, t, kvw)
    kv3 = kv.reshape(b, t, 2 * kvw)
    for j in range(2):
        l = N_GDN + j
        qp = _mm(hf, w['b_w_q'][j], gain=p['b_norm'][j])
        q = _headnorm_rope(qp, 0, D_MODEL, p['b_q_norm'][j], tables, t, BF16)
        o = _swa_prompt(q.reshape(b, t, D_MODEL), k3, kv3, p['b_sinks'][j])
        hf = _mm(o.reshape(n, D_MODEL), w['b_w_o'][j], res=hf)
        hf = _channel_mixer(hf, l, p, w)
    k_win = k3[:, t - WINDOW:].reshape(b, WINDOW, KV_HEADS, HEAD_DIM)
    v_win = kv3[:, t - WINDOW:, kvw:].reshape(b, WINDOW, KV_HEADS, HEAD_DIM)
    return hf.reshape(b, t, D_MODEL), jnp.stack(convs), jnp.stack(deltas), k_win, v_win


def _run_sample(x, conv_state, delta_state, k_cache, v_cache, p, w):
    n = x.shape[0]
    h = x.reshape(n, D_MODEL)
    convs, deltas = [], None
    for l in range(N_GDN):
        h, deltas, c_new = _gdn_sample_layer(h, l, p, w, conv_state[l], delta_state, deltas)
        convs.append(c_new)
        h = _channel_mixer(h, l, p, w)
    tables = _rope_tables(jnp.full((n,), PAST_LEN, I32))
    kvw = KV_HEADS * HEAD_DIM
    wc = k_cache.shape[1]
    kv = _mm(h, w['w_kv'], gain=p['kv_norm'], tn=2 * kvw)
    k_new = _headnorm_rope(kv, 0, kvw, p['k_norm'], tables, n, F32)
    v_new = kv[:, kvw:]
    kc = k_cache.reshape(n, wc, kvw)
    vc = v_cache.reshape(n, wc, kvw)
    for j in range(2):
        l = N_GDN + j
        qp = _mm(h, w['b_w_q'][j], gain=p['b_norm'][j])
        q = _headnorm_rope(qp, 0, D_MODEL, p['b_q_norm'][j], tables, n, F32)
        o = _swa_sample(q.reshape(n, ATT_HEADS, HEAD_DIM), kc, k_new, vc, v_new, p['b_sinks'][j])
        h = _mm(o.reshape(n, D_MODEL), w['b_w_o'][j], res=h)
        h = _channel_mixer(h, l, p, w)
    k_win = jnp.concatenate([kc, k_new[:, None, :]], axis=1)[:, -wc:].reshape(n, wc, KV_HEADS, HEAD_DIM)
    v_win = jnp.concatenate([vc, v_new[:, None, :]], axis=1)[:, -wc:].reshape(n, wc, KV_HEADS, HEAD_DIM)
    return h.reshape(n, 1, D_MODEL), jnp.stack(convs), deltas, k_win, v_win


def kernel(x_prompt, x_sample, state_conv, state_delta, cache_k_win, cache_v_win, a_norm, a_w_in, a_conv_w,
           a_A_log, a_dt_bias, a_o_norm, a_w_out, kv_norm, w_kv, k_norm, b_norm, b_w_q, b_q_norm, b_sinks,
           b_w_o, ffn_norm, dense_w_gu, dense_w_down, moe_router, moe_router_b, moe_w_gu, moe_w_down):
    p = dict(a_norm=a_norm, a_conv_w=a_conv_w, a_A_log=a_A_log, a_dt_bias=a_dt_bias, a_o_norm=a_o_norm,
             kv_norm=kv_norm, k_norm=k_norm, b_norm=b_norm, b_q_norm=b_q_norm, b_sinks=b_sinks,
             ffn_norm=ffn_norm, moe_router=moe_router, moe_router_b=moe_router_b)
    n_proj = PROJ_W
    gate_cols = jnp.pad(a_w_in[:, :, n_proj:], ((0, 0), (0, 0), (0, LANES - (a_w_in.shape[-1] - n_proj))))
    raw = dict(a_w_in=a_w_in[:, :, :n_proj], a_w_ab=gate_cols, a_w_out=a_w_out, w_kv=w_kv[None],
               b_w_q=b_w_q, b_w_o=b_w_o, dense_w_gu=dense_w_gu, dense_w_down=dense_w_down,
               moe_w_gu=moe_w_gu, moe_w_down=moe_w_down)
    w_fast = {name: [val[l].astype(BF16) for l in range(val.shape[0])] for name, val in raw.items()}
    w_precise = {name: [_Layer(val, l) for l in range(val.shape[0])] for name, val in raw.items()}
    w_precise['a_w_in'] = [_Layer(a_w_in, l, cols=n_proj) for l in range(a_w_in.shape[0])]
    w_fast['w_kv'], w_precise['w_kv'] = w_fast['w_kv'][0], w_kv
    y_p, conv_p, delta_p, kw_p, vw_p = _run_prompt(x_prompt, p, w_fast)
    y_s, conv_s, delta_s, kw_s, vw_s = _run_sample(
        x_sample, state_conv, state_delta, cache_k_win, cache_v_win, p, w_precise)
    return (y_p, y_s, conv_p, conv_s, delta_p, delta_s, kw_p, kw_s, vw_p, vw_s)
```

```python
import functools

import jax
import jax.numpy as jnp
from jax import lax
from jax.experimental import pallas as pl
from jax.experimental.pallas import tpu as pltpu

F32, BF16, I32 = jnp.float32, jnp.bfloat16, jnp.int32
HI = lax.Precision.HIGHEST

D_MODEL = 1024
N_GDN = 2
GDN_HEADS = 8
GDN_DK = 128
CONV_TAPS = 4
CHUNK = 64
QKV_W = 3 * GDN_HEADS * GDN_DK
PROJ_W = QKV_W + D_MODEL
ATT_HEADS = 16
KV_HEADS = 4
HEAD_DIM = 64
ATT_GROUP = ATT_HEADS // KV_HEADS
WINDOW = 128
ROT = HEAD_DIM // 4
ROPE_THETA = 500000.0
PAST_LEN = 8192
N_EXPERTS = 8
EPS = 1e-6

LANES = 128
VMEM_LIMIT = 56 * 1024 * 1024
NEG = -1e30


def _params(sem):
    return pltpu.CompilerParams(dimension_semantics=sem, vmem_limit_bytes=VMEM_LIMIT)


def _rms(x, g):
    return x * lax.rsqrt(jnp.mean(x * x, axis=-1, keepdims=True) + EPS) * g


def _silu(x):
    return x * jax.nn.sigmoid(x)


def _split(x):
    hi = x.astype(BF16)
    return hi, (x - hi.astype(F32)).astype(BF16)


def _dot3(xh, xl, w):
    wh, wl = _split(w)
    return (jnp.dot(xh, wh, preferred_element_type=F32)
            + (jnp.dot(xl, wh, preferred_element_type=F32) + jnp.dot(xh, wl, preferred_element_type=F32)))


class _Layer:
    def __init__(self, arr, layer, cols=None):
        self.arr, self.lead, self.dtype = arr, (layer,), arr.dtype
        self.shape = arr.shape[1:] if cols is None else arr.shape[1:-1] + (cols,)


def _warr(w):
    return w.arr if isinstance(w, _Layer) else w


def _wspec(w, block, index_fn, **kwargs):
    lead = w.lead if isinstance(w, _Layer) else ()
    return pl.BlockSpec((None,) * len(lead) + tuple(block), lambda *a: lead + tuple(index_fn(*a)), **kwargs)


def _is_precise(w):
    assert w.dtype in (BF16, F32)
    return w.dtype == F32


def _mm_body(*refs, has_norm, has_aux, has_res, precise):
    it = iter(refs)
    x_ref = next(it)
    g_ref = next(it) if has_norm else None
    w_ref = next(it)
    wa_ref = next(it) if has_aux else None
    r_ref = next(it) if has_res else None
    o_ref = next(it)
    oa_ref = next(it) if has_aux else None
    xn_ref = next(it)

    def product(w):
        if precise:
            return _dot3(xn_ref[0], xn_ref[1], w)
        return jnp.dot(xn_ref[0], w, preferred_element_type=F32)

    @pl.when(pl.program_id(1) == 0)
    def _():
        x = x_ref[...].astype(F32)
        if has_norm:
            x = _rms(x, g_ref[...])
        if precise:
            xn_ref[0], xn_ref[1] = _split(x)
        else:
            xn_ref[0] = x.astype(BF16)
        if has_aux:
            oa_ref[...] = product(wa_ref[...])

    acc = product(w_ref[...])
    if has_res:
        acc = acc + r_ref[...]
    o_ref[...] = acc.astype(o_ref.dtype)


def _mm(x, w, gain=None, res=None, w_aux=None, tm=1024, tn=1024, out_dtype=F32):
    precise = _is_precise(w)
    m, k = x.shape
    n = w.shape[1]
    tm = min(tm, m)
    tn = min(tn, n)
    assert m % tm == 0 and n % tn == 0
    ins = [x]
    specs = [pl.BlockSpec((tm, k), lambda i, j: (i, 0))]
    if gain is not None:
        ins.append(gain.reshape(1, k).astype(F32))
        specs.append(pl.BlockSpec((1, k), lambda i, j: (0, 0)))
    ins.append(_warr(w))
    resident = dict(pipeline_mode=pl.Buffered(1)) if tn == n else {}
    specs.append(_wspec(w, (k, tn), lambda i, j: (0, j), **resident))
    out_specs = [pl.BlockSpec((tm, tn), lambda i, j: (i, j))]
    out_shape = [jax.ShapeDtypeStruct((m, n), out_dtype)]
    if w_aux is not None:
        na = w_aux.shape[1]
        ins.append(_warr(w_aux))
        specs.append(_wspec(w_aux, (k, na), lambda i, j: (0, 0)))
        out_specs.append(pl.BlockSpec((tm, na), lambda i, j: (i, 0)))
        out_shape.append(jax.ShapeDtypeStruct((m, na), F32))
    if res is not None:
        ins.append(res)
        specs.append(pl.BlockSpec((tm, tn), lambda i, j: (i, j)))
    out = pl.pallas_call(
        functools.partial(_mm_body, has_norm=gain is not None, has_aux=w_aux is not None,
                          has_res=res is not None, precise=precise),
        grid=(m // tm, n // tn),
        in_specs=specs,
        out_specs=out_specs,
        out_shape=out_shape,
        scratch_shapes=[pltpu.VMEM((1 + precise, tm, k), BF16)],
        compiler_params=_params(("parallel", "arbitrary")),
        name="mm",
    )(*ins)
    return out if w_aux is not None else out[0]


def _swiglu(xh, xl, wg, wu, wd):
    if xl is None:
        gate = jnp.dot(xh, wg, preferred_element_type=F32)
        up = jnp.dot(xh, wu, preferred_element_type=F32)
        return jnp.dot((_silu(gate) * up).astype(BF16), wd, preferred_element_type=F32)
    gate = _dot3(xh, xl, wg)
    up = _dot3(xh, xl, wu)
    ah, al = _split(_silu(gate) * up)
    return _dot3(ah, al, wd)


def _ffn_dense_body(x_ref, g_ref, wg_ref, wu_ref, wd_ref, o_ref, xn_ref, acc_ref, *, precise):
    f = pl.program_id(1)

    if not precise:
        xn = _rms(x_ref[...], g_ref[...]).astype(BF16)
        o_ref[...] = x_ref[...] + _swiglu(xn, None, wg_ref[...], wu_ref[...], wd_ref[...])
        return

    @pl.when(f == 0)
    def _():
        xn_ref[0], xn_ref[1] = _split(_rms(x_ref[...], g_ref[...]))
        acc_ref[...] = x_ref[...]

    acc_ref[...] += _swiglu(xn_ref[0], xn_ref[1], wg_ref[...], wu_ref[...], wd_ref[...])

    @pl.when(f == pl.num_programs(1) - 1)
    def _():
        o_ref[...] = acc_ref[...]


def _ffn_dense(x, gain, w_gu, w_down, tm=512):
    precise = _is_precise(w_gu)
    m, k = x.shape
    ff = w_down.shape[0]
    tm = min(tm, m)
    nf = 1 if not precise else ff // (2 * LANES)
    tf = ff // nf
    assert tf % LANES == 0 and tf * nf == ff
    resident = {} if precise else dict(pipeline_mode=pl.Buffered(1))
    return pl.pallas_call(
        functools.partial(_ffn_dense_body, precise=precise),
        grid=(m // tm, nf),
        in_specs=[
            pl.BlockSpec((tm, k), lambda i, f: (i, 0)),
            pl.BlockSpec((1, k), lambda i, f: (0, 0)),
            _wspec(w_gu, (k, tf), lambda i, f: (0, f), **resident),
            _wspec(w_gu, (k, tf), lambda i, f: (0, nf + f), **resident),
            _wspec(w_down, (tf, k), lambda i, f: (f, 0), **resident),
        ],
        out_specs=pl.BlockSpec((tm, k), lambda i, f: (i, 0)),
        out_shape=jax.ShapeDtypeStruct((m, k), F32),
        scratch_shapes=[pltpu.VMEM((1 + precise, tm, k), BF16), pltpu.VMEM((tm, k), F32)],
        compiler_params=_params(("parallel", "arbitrary")),
        name="ffn_dense",
    )(x, gain.reshape(1, k).astype(F32), _warr(w_gu), _warr(w_gu), _warr(w_down))


def _router_body(x_ref, g_ref, wr_ref, br_ref, idx_ref, gate_ref, cnt_ref, carry_ref):
    i = pl.program_id(0)

    @pl.when(i == 0)
    def _():
        carry_ref[...] = jnp.zeros_like(carry_ref)

    tm = x_ref.shape[0]
    xn = _rms(x_ref[...], g_ref[...])
    logits = _dot3(*_split(xn), wr_ref[...]) + br_ref[...]
    lane = lax.broadcasted_iota(I32, logits.shape, 1)
    m1 = jnp.max(logits, axis=-1, keepdims=True)
    i1 = jnp.min(jnp.where(logits == m1, lane, LANES), axis=-1, keepdims=True)
    rest = jnp.where(lane == i1, NEG, logits)
    m2 = jnp.max(rest, axis=-1, keepdims=True)
    i2 = jnp.min(jnp.where(rest == m2, lane, LANES), axis=-1, keepdims=True)
    e2 = jnp.exp(m2 - m1)
    g1 = 1.0 / (1.0 + e2)
    g2 = e2 * g1
    oh1 = lane == i1
    oh2 = lane == i2
    oh = jnp.where(oh1 | oh2, 1.0, 0.0).astype(BF16)
    r = lax.broadcasted_iota(I32, (tm, tm), 0)
    c = lax.broadcasted_iota(I32, (tm, tm), 1)
    below = jnp.where(c < r, 1.0, 0.0).astype(BF16)
    before = jnp.dot(below, oh, preferred_element_type=F32) + carry_ref[0:1, :]
    rk1 = jnp.sum(jnp.where(oh1, before, 0.0), axis=-1, keepdims=True).astype(I32)
    rk2 = jnp.sum(jnp.where(oh2, before, 0.0), axis=-1, keepdims=True).astype(I32)
    idx_ref[...] = jnp.where(lane == 0, i1, jnp.where(lane == 1, i2, jnp.where(lane == 2, rk1, rk2)))
    gate_ref[...] = jnp.where(lane == 0, g1, g2)
    total = carry_ref[0:1, :] + jnp.sum(oh.astype(F32), axis=0, keepdims=True)
    carry_ref[...] = jnp.broadcast_to(total, carry_ref.shape)
    cnt_ref[...] = carry_ref[...]


def _router(x, gain, w_router, b_router, tm=512):
    m, k = x.shape
    tm = min(tm, m)
    wr = jnp.zeros((k, LANES), F32).at[:, :N_EXPERTS].set(w_router)
    br = jnp.full((1, LANES), NEG, F32).at[0, :N_EXPERTS].set(b_router)
    return pl.pallas_call(
        _router_body,
        grid=(m // tm,),
        in_specs=[
            pl.BlockSpec((tm, k), lambda i: (i, 0)),
            pl.BlockSpec((1, k), lambda i: (0, 0)),
            pl.BlockSpec((k, LANES), lambda i: (0, 0)),
            pl.BlockSpec((1, LANES), lambda i: (0, 0)),
        ],
        out_specs=[
            pl.BlockSpec((tm, LANES), lambda i: (i, 0)),
            pl.BlockSpec((tm, LANES), lambda i: (i, 0)),
            pl.BlockSpec((8, LANES), lambda i: (0, 0)),
        ],
        out_shape=[
            jax.ShapeDtypeStruct((m, LANES), I32),
            jax.ShapeDtypeStruct((m, LANES), F32),
            jax.ShapeDtypeStruct((8, LANES), F32),
        ],
        scratch_shapes=[pltpu.VMEM((8, LANES), F32)],
        compiler_params=_params(("arbitrary",)),
        name="moe_router",
    )(x, gain.reshape(1, k).astype(F32), wr, br)


def _row_copy(src, dst, src_row, dst_row, sem):
    return pltpu.make_async_copy(src.at[pl.ds(src_row, 1)], dst.at[pl.ds(dst_row, 1)], sem)


def _scatter_body(d_ref, x_ref, g_ref, xs_in_ref, xs_ref, buf_ref, sem):
    del xs_in_ref
    tm = buf_ref.shape[0]
    buf_ref[...] = _rms(x_ref[...], g_ref[...])

    def issue(t, carry):
        for kk in range(2):
            _row_copy(buf_ref, xs_ref, t, d_ref[0, 0, 2 * t + kk], sem).start()
        return carry

    lax.fori_loop(0, tm, issue, 0, unroll=8)
    for kk in range(2):
        pltpu.make_async_copy(buf_ref, xs_ref.at[pl.ds(0, tm)], sem).wait()


def _moe_scatter(x, gain, dest_tiles, rows, tm):
    m, k = x.shape
    return pl.pallas_call(
        _scatter_body,
        grid=(m // tm,),
        in_specs=[
            pl.BlockSpec((1, 1, 2 * tm), lambda i: (i, 0, 0), memory_space=pltpu.SMEM),
            pl.BlockSpec((tm, k), lambda i: (i, 0)),
            pl.BlockSpec((1, k), lambda i: (0, 0)),
            pl.BlockSpec(memory_space=pl.ANY),
        ],
        out_specs=pl.BlockSpec(memory_space=pl.ANY),
        out_shape=jax.ShapeDtypeStruct((rows, k), F32),
        scratch_shapes=[pltpu.VMEM((tm, k), F32), pltpu.SemaphoreType.DMA(())],
        input_output_aliases={3: 0},
        compiler_params=_params(("arbitrary",)),
        name="moe_scatter",
    )(dest_tiles, x, gain.reshape(1, k).astype(F32), jnp.zeros((rows, k), F32))


def _ffn_group_body(te_ref, nu_ref, x_ref, wg_ref, wu_ref, wd_ref, o_ref, *, precise):
    del te_ref
    used = pl.program_id(0) < nu_ref[0]

    @pl.when(used)
    def _():
        xh, xl = _split(x_ref[...]) if precise else (x_ref[...].astype(BF16), None)
        o_ref[...] = _swiglu(xh, xl, wg_ref[...], wu_ref[...], wd_ref[...])

    @pl.when(jnp.logical_not(used))
    def _():
        o_ref[...] = jnp.zeros_like(o_ref)


def _ffn_grouped(xs, tile_expert, n_used, w_gu, w_down, tm):
    precise = _is_precise(w_gu)
    rows, k = xs.shape
    ff = w_down.shape[1]
    nt = rows // tm

    def row_map(i, te, nu):
        return (jnp.minimum(i, nu[0] - 1), 0)

    grid_spec = pltpu.PrefetchScalarGridSpec(
        num_scalar_prefetch=2,
        grid=(nt,),
        in_specs=[
            pl.BlockSpec((tm, k), row_map),
            _wspec(w_gu, (None, k, ff), lambda i, te, nu: (te[i], 0, 0)),
            _wspec(w_gu, (None, k, ff), lambda i, te, nu: (te[i], 0, 1)),
            _wspec(w_down, (None, ff, k), lambda i, te, nu: (te[i], 0, 0)),
        ],
        out_specs=pl.BlockSpec((tm, k), lambda i, te, nu: (i, 0)),
    )
    return pl.pallas_call(
        functools.partial(_ffn_group_body, precise=precise),
        grid_spec=grid_spec,
        out_shape=jax.ShapeDtypeStruct((rows, k), F32),
        compiler_params=_params(("arbitrary",)),
        name="moe_experts",
    )(tile_expert, n_used, xs, _warr(w_gu), _warr(w_gu), _warr(w_down))


def _combine_body(d_ref, x_ref, gate_ref, ys_ref, o_ref, buf_ref, sem):
    tm = x_ref.shape[0]

    def issue(t, carry):
        for kk in range(2):
            _row_copy(ys_ref, buf_ref.at[kk], d_ref[0, 0, 2 * t + kk], t, sem).start()
        return carry

    lax.fori_loop(0, tm, issue, 0, unroll=8)
    for kk in range(2):
        pltpu.make_async_copy(ys_ref.at[pl.ds(0, tm)], buf_ref.at[kk], sem).wait()
    g = gate_ref[...]
    o_ref[...] = x_ref[...] + g[:, 0:1] * buf_ref[0] + g[:, 1:2] * buf_ref[1]


def _moe_combine(x, gates, dest_tiles, ys, tm):
    m, k = x.shape
    return pl.pallas_call(
        _combine_body,
        grid=(m // tm,),
        in_specs=[
            pl.BlockSpec((1, 1, 2 * tm), lambda i: (i, 0, 0), memory_space=pltpu.SMEM),
            pl.BlockSpec((tm, k), lambda i: (i, 0)),
            pl.BlockSpec((tm, LANES), lambda i: (i, 0)),
            pl.BlockSpec(memory_space=pl.ANY),
        ],
        out_specs=pl.BlockSpec((tm, k), lambda i: (i, 0)),
        out_shape=jax.ShapeDtypeStruct((m, k), F32),
        scratch_shapes=[pltpu.VMEM((2, tm, k), F32), pltpu.SemaphoreType.DMA(())],
        compiler_params=_params(("arbitrary",)),
        name="moe_combine",
    )(dest_tiles, x, gates, ys)


SEG_TOK = 256
SEG_ROWS = 640
EXPERT_TILE = 512


def _router_seg_body(x_ref, g_ref, wr_ref, br_ref, idx_ref, gate_ref, tab_ref, cnt_ref, carry_ref):
    i = pl.program_id(0)

    @pl.when(i == 0)
    def _():
        carry_ref[...] = jnp.zeros_like(carry_ref)

    tm = x_ref.shape[0]
    xn = _rms(x_ref[...], g_ref[...])
    logits = _dot3(*_split(xn), wr_ref[...]) + br_ref[...]
    lane = lax.broadcasted_iota(I32, logits.shape, 1)
    m1 = jnp.max(logits, axis=-1, keepdims=True)
    i1 = jnp.min(jnp.where(logits == m1, lane, LANES), axis=-1, keepdims=True)
    rest = jnp.where(lane == i1, NEG, logits)
    m2 = jnp.max(rest, axis=-1, keepdims=True)
    i2 = jnp.min(jnp.where(rest == m2, lane, LANES), axis=-1, keepdims=True)
    e2 = jnp.exp(m2 - m1)
    g1 = 1.0 / (1.0 + e2)
    g2 = e2 * g1
    oh1 = lane == i1
    oh2 = lane == i2
    oh = jnp.where(oh1 | oh2, 1.0, 0.0)
    r = lax.broadcasted_iota(I32, (tm, tm), 0)
    c = lax.broadcasted_iota(I32, (tm, tm), 1)
    below = jnp.where(c < r, 1.0, 0.0).astype(BF16)
    rank = jnp.dot(below, oh.astype(BF16), preferred_element_type=F32)
    count = jnp.sum(oh, axis=0, keepdims=True)
    padded = jnp.floor((count + 7.0) * 0.125) * 8.0
    er = lax.broadcasted_iota(I32, (LANES, LANES), 0)
    ec = lax.broadcasted_iota(I32, (LANES, LANES), 1)
    start = jnp.dot(jnp.broadcast_to(padded, (8, LANES)), jnp.where(er < ec, 1.0, 0.0),
                    precision=HI, preferred_element_type=F32)[0:1]
    slot = rank + start
    s1 = jnp.sum(jnp.where(oh1, slot, 0.0), axis=-1, keepdims=True).astype(I32)
    s2 = jnp.sum(jnp.where(oh2, slot, 0.0), axis=-1, keepdims=True).astype(I32)
    idx_ref[...] = jnp.where(lane == 0, i1, jnp.where(lane == 1, i2, jnp.where(lane == 2, s1, s2)))
    gate_ref[...] = jnp.where(lane == 0, g1, g2)
    sub = lax.broadcasted_iota(I32, (8, LANES), 0)
    rows = jnp.where(sub == 0, padded * 0.125, jnp.where(sub == 1, start, carry_ref[...]))
    tab_ref[...] = jnp.where(sub < 3, rows, 0.0).astype(I32)
    carry_ref[...] = carry_ref[...] + padded
    cnt_ref[...] = carry_ref[...]


def _router_seg(x, gain, w_router, b_router):
    m, k = x.shape
    tm = SEG_TOK
    wr = jnp.zeros((k, LANES), F32).at[:, :N_EXPERTS].set(w_router)
    br = jnp.full((1, LANES), NEG, F32).at[0, :N_EXPERTS].set(b_router)
    return pl.pallas_call(
        _router_seg_body,
        grid=(m // tm,),
        in_specs=[
            pl.BlockSpec((tm, k), lambda i: (i, 0)),
            pl.BlockSpec((1, k), lambda i: (0, 0)),
            pl.BlockSpec((k, LANES), lambda i: (0, 0)),
            pl.BlockSpec((1, LANES), lambda i: (0, 0)),
        ],
        out_specs=[
            pl.BlockSpec((tm, LANES), lambda i: (i, 0)),
            pl.BlockSpec((tm, LANES), lambda i: (i, 0)),
            pl.BlockSpec((None, 8, LANES), lambda i: (i, 0, 0)),
            pl.BlockSpec((8, LANES), lambda i: (0, 0)),
        ],
        out_shape=[
            jax.ShapeDtypeStruct((m, LANES), I32),
            jax.ShapeDtypeStruct((m, LANES), F32),
            jax.ShapeDtypeStruct((m // tm, 8, LANES), I32),
            jax.ShapeDtypeStruct((8, LANES), F32),
        ],
        scratch_shapes=[pltpu.VMEM((8, LANES), F32)],
        compiler_params=_params(("arbitrary",)),
        name="moe_router_seg",
    )(x, gain.reshape(1, k).astype(F32), wr, br)


def _chunk_copy(src, dst, src_row, dst_row, sem):
    rows = lambda r: pl.ds(r if isinstance(r, int) else pl.multiple_of(r, 8), 8)
    return pltpu.make_async_copy(src.at[rows(src_row)], dst.at[rows(dst_row)], sem)


def _segment_start(tab_ref, base_ref, local_ref, hbm_ref, sem, to_hbm):
    for e in range(N_EXPERTS):
        local0 = tab_ref[1, e]
        hbm0 = base_ref[e] + tab_ref[2, e]

        def issue(j, carry, local0=local0, hbm0=hbm0):
            if to_hbm:
                _chunk_copy(local_ref, hbm_ref, local0 + 8 * j, hbm0 + 8 * j, sem).start()
            else:
                _chunk_copy(hbm_ref, local_ref, hbm0 + 8 * j, local0 + 8 * j, sem).start()
            return carry

        lax.fori_loop(0, tab_ref[0, e], issue, 0)


def _segment_wait(tab_ref, local_ref, hbm_ref, sem, to_hbm):
    total = 0
    for e in range(N_EXPERTS):
        total = total + tab_ref[0, e]

    def drain(j, carry):
        if to_hbm:
            _chunk_copy(local_ref, hbm_ref, 0, 0, sem).wait()
        else:
            _chunk_copy(hbm_ref, local_ref, 0, 0, sem).wait()
        return carry

    lax.fori_loop(0, total, drain, 0)


def _slot_onehot(idx_ref, which, width):
    lane = lax.broadcasted_iota(I32, (idx_ref.shape[0], width), 1)
    return jnp.where(lane == idx_ref[:, 2 + which:3 + which], 1.0, 0.0)


def _scatter_seg_body(base_ref, tab_ref, tab_prev_ref, idx_ref, x_ref, g_ref, xs_ref, buf_ref, zero_ref, sem,
                      fill_sem):
    i = pl.program_id(0)
    slot = i % 2
    xn = _rms(x_ref[...], g_ref[...]).astype(BF16)
    width = buf_ref.shape[1]
    place = (_slot_onehot(idx_ref, 0, width) + _slot_onehot(idx_ref, 1, width)).astype(BF16)
    buf_ref[slot] = lax.dot_general(place, xn, (((0,), (0,)), ((), ())), preferred_element_type=F32)
    _segment_start(tab_ref, base_ref, buf_ref.at[slot], xs_ref, sem.at[slot], to_hbm=True)

    @pl.when(i > 0)
    def _():
        _segment_wait(tab_prev_ref, buf_ref.at[1 - slot], xs_ref, sem.at[1 - slot], to_hbm=True)

    @pl.when(i == pl.num_programs(0) - 1)
    def _():
        _segment_wait(tab_ref, buf_ref.at[slot], xs_ref, sem.at[slot], to_hbm=True)
        zero_ref[...] = jnp.zeros_like(zero_ref)
        tile_rows = zero_ref.shape[0]
        chunks = 0
        for e in range(N_EXPERTS):
            row0 = base_ref[N_EXPERTS + e]

            def fill(j, carry, row0=row0):
                _chunk_copy(zero_ref, xs_ref, 0, row0 + 8 * j, fill_sem).start()
                return carry

            lax.fori_loop(0, base_ref[2 * N_EXPERTS + e], fill, 0)
            chunks = chunks + base_ref[2 * N_EXPERTS + e]

        def tile_copy(j):
            row = pl.multiple_of(base_ref[3 * N_EXPERTS] + tile_rows * j, 8)
            return pltpu.make_async_copy(zero_ref, xs_ref.at[pl.ds(row, tile_rows)], fill_sem)

        def fill_tile(j, carry):
            tile_copy(j).start()
            return carry

        lax.fori_loop(0, base_ref[3 * N_EXPERTS + 1], fill_tile, 0)

        def drain(j, carry):
            _chunk_copy(zero_ref, xs_ref, 0, 0, fill_sem).wait()
            return carry

        lax.fori_loop(0, chunks, drain, 0)

        def drain_tile(j, carry):
            tile_copy(0).wait()
            return carry

        lax.fori_loop(0, base_ref[3 * N_EXPERTS + 1], drain_tile, 0)


def _tab_spec(shift, nt):
    return pl.BlockSpec((None, 8, LANES), lambda i, b: (jnp.clip(i + shift, 0, nt - 1), 0, 0),
                        memory_space=pltpu.SMEM)


def _moe_scatter_seg(x, gain, idx, tab, plan, rows):
    m, k = x.shape
    tm = SEG_TOK
    nt = m // tm
    grid_spec = pltpu.PrefetchScalarGridSpec(
        num_scalar_prefetch=1,
        grid=(nt,),
        in_specs=[
            _tab_spec(0, nt), _tab_spec(-1, nt),
            pl.BlockSpec((tm, LANES), lambda i, b: (i, 0)),
            pl.BlockSpec((tm, k), lambda i, b: (i, 0)),
            pl.BlockSpec((1, k), lambda i, b: (0, 0)),
        ],
        out_specs=pl.BlockSpec(memory_space=pl.ANY),
        scratch_shapes=[pltpu.VMEM((2, SEG_ROWS, k), F32), pltpu.VMEM((EXPERT_TILE, k), F32),
                        pltpu.SemaphoreType.DMA((2,)), pltpu.SemaphoreType.DMA(())],
    )
    return pl.pallas_call(
        _scatter_seg_body,
        grid_spec=grid_spec,
        out_shape=jax.ShapeDtypeStruct((rows, k), F32),
        compiler_params=_params(("arbitrary",)),
        name="moe_scatter_seg",
    )(plan, tab, tab, idx, x, gain.reshape(1, k).astype(F32))


def _combine_seg_body(base_ref, tab_ref, tab_next_ref, idx_ref, x_ref, gate_ref, ys_ref, o_ref, buf_ref, sem):
    i = pl.program_id(0)
    slot = i % 2

    @pl.when(i == 0)
    def _():
        buf_ref[...] = jnp.zeros_like(buf_ref)
        _segment_start(tab_ref, base_ref, buf_ref.at[0], ys_ref, sem.at[0], to_hbm=False)

    @pl.when(i < pl.num_programs(0) - 1)
    def _():
        _segment_start(tab_next_ref, base_ref, buf_ref.at[1 - slot], ys_ref, sem.at[1 - slot], to_hbm=False)

    _segment_wait(tab_ref, buf_ref.at[slot], ys_ref, sem.at[slot], to_hbm=False)
    width = buf_ref.shape[1]
    y = buf_ref[slot].astype(BF16)
    g = gate_ref[...]
    pick = lambda which: jnp.dot(_slot_onehot(idx_ref, which, width).astype(BF16), y, preferred_element_type=F32)
    o_ref[...] = x_ref[...] + g[:, 0:1] * pick(0) + g[:, 1:2] * pick(1)


def _moe_combine_seg(x, gates, idx, tab, base, ys):
    m, k = x.shape
    tm = SEG_TOK
    nt = m // tm
    grid_spec = pltpu.PrefetchScalarGridSpec(
        num_scalar_prefetch=1,
        grid=(nt,),
        in_specs=[
            _tab_spec(0, nt), _tab_spec(1, nt),
            pl.BlockSpec((tm, LANES), lambda i, b: (i, 0)),
            pl.BlockSpec((tm, k), lambda i, b: (i, 0)),
            pl.BlockSpec((tm, LANES), lambda i, b: (i, 0)),
            pl.BlockSpec(memory_space=pl.ANY),
        ],
        out_specs=pl.BlockSpec((tm, k), lambda i, b: (i, 0)),
        scratch_shapes=[pltpu.VMEM((2, SEG_ROWS, k), F32), pltpu.SemaphoreType.DMA((2,))],
    )
    return pl.pallas_call(
        _combine_seg_body,
        grid_spec=grid_spec,
        out_shape=jax.ShapeDtypeStruct((m, k), F32),
        compiler_params=_params(("arbitrary",)),
        name="moe_combine_seg",
    )(base, tab, tab, idx, x, gates, ys)


def _moe_seg(x, gain, w_router, b_router, w_gu, w_down):
    m, k = x.shape
    idx, gates, tab, totals = _router_seg(x, gain, w_router, b_router)
    totals = totals[0, :N_EXPERTS].astype(I32)
    region = ((totals + EXPERT_TILE - 1) // EXPERT_TILE) * EXPERT_TILE
    ends = jnp.cumsum(region)
    base = (ends - region).astype(I32)
    max_rows = 2 * m + N_EXPERTS * (7 * (m // SEG_TOK) + EXPERT_TILE - 1)
    rows = (max_rows // EXPERT_TILE) * EXPERT_TILE
    nt = rows // EXPERT_TILE
    n_used = (ends[-1] // EXPERT_TILE).astype(I32)
    tile_start = jnp.minimum(jnp.arange(nt, dtype=I32), n_used - 1) * EXPERT_TILE
    tile_expert = jnp.sum(tile_start[:, None] >= ends[None, :], axis=1).astype(I32)
    plan = jnp.concatenate([base, base + totals, (region - totals) // 8,
                            jnp.stack([ends[-1], nt * EXPERT_TILE - ends[-1]]) // jnp.array([1, EXPERT_TILE])]).astype(I32)
    xs = _moe_scatter_seg(x, gain, idx, tab, plan, rows)
    ys = _ffn_grouped(xs, tile_expert, n_used.reshape(1), w_gu, w_down, EXPERT_TILE)
    return _moe_combine_seg(x, gates, idx, tab, base, ys)


def _moe(x, gain, w_router, b_router, w_gu, w_down):
    m, k = x.shape
    if m % SEG_TOK == 0 and m >= 8 * SEG_TOK and not _is_precise(w_gu):
        return _moe_seg(x, gain, w_router, b_router, w_gu, w_down)
    tm_tok = min(256, m)
    tm_grp = 512 if m >= 4096 else 128
    idx, gates, counts = _router(x, gain, w_router, b_router)
    counts = counts[0, :N_EXPERTS].astype(I32)
    padded = ((counts + tm_grp - 1) // tm_grp) * tm_grp
    ends = jnp.cumsum(padded)
    starts = ends - padded
    dest = starts[idx[:, 0:2]] + idx[:, 2:4]
    dest_tiles = dest.reshape(m // tm_tok, 1, 2 * tm_tok)
    rows = ((2 * m + N_EXPERTS * (tm_grp - 1)) // tm_grp) * tm_grp
    nt = rows // tm_grp
    n_used = (ends[-1] // tm_grp).astype(I32)
    tile_start = jnp.minimum(jnp.arange(nt, dtype=I32), n_used - 1) * tm_grp
    tile_expert = jnp.sum(tile_start[:, None] >= ends[None, :], axis=1).astype(I32)
    xs = _moe_scatter(x, gain, dest_tiles, rows, tm_tok)
    ys = _ffn_grouped(xs, tile_expert, n_used.reshape(1), w_gu, w_down, tm_grp)
    return _moe_combine(x, gates, dest_tiles, ys, tm_tok)


def _head_cols(x, h, width=GDN_DK):
    return x[:, h * width:(h + 1) * width]


def _gdn_prep_body(x_ref, halo_ref, ab_ref, cw_ref, alog_ref, dtb_ref,
                   q_ref, qg_ref, k_ref, kb_ref, kbg_ref, kd_ref, vb_ref,
                   gc_ref, gct_ref, egl_ref, ext_ref):
    i = pl.program_id(1)
    tt = x_ref.shape[0]
    ext_ref[0:8, :] = jnp.where(i == 0, 0.0, halo_ref[...])
    ext_ref[8:, :] = x_ref[...]
    ext = ext_ref[...]
    acc = pltpu.roll(ext, CONV_TAPS - 1, axis=0)[8:] * cw_ref[0:1, :]
    for j in range(1, CONV_TAPS - 1):
        acc = acc + pltpu.roll(ext, CONV_TAPS - 1 - j, axis=0)[8:] * cw_ref[j:j + 1, :]
    acc = acc + ext[8:] * cw_ref[CONV_TAPS - 1:CONV_TAPS, :]
    y = _silu(acc)

    ab = ab_ref[...]
    g = -jnp.exp(alog_ref[...]) * jax.nn.softplus(ab + dtb_ref[...])
    beta = jax.nn.sigmoid(pltpu.roll(ab, LANES - GDN_HEADS, axis=1))
    r = lax.broadcasted_iota(I32, (tt, tt), 0)
    c = lax.broadcasted_iota(I32, (tt, tt), 1)
    same = (r // CHUNK) == (c // CHUNK)
    tril = jnp.where(same & (c <= r), 1.0, 0.0)
    gc = jnp.dot(tril, g, precision=HI, preferred_element_type=F32)
    gl = jnp.dot(jnp.where(same, 1.0, 0.0), g, precision=HI, preferred_element_type=F32)
    e_gc = jnp.exp(gc)
    e_gd = jnp.exp(gl - gc)
    gc_ref[...] = gc
    gct = gc.T
    sub = lax.broadcasted_iota(I32, (8, LANES), 0)
    lan = lax.broadcasted_iota(I32, (8, LANES), 1)
    for cc in range(tt // CHUNK):
        gct_ref[cc] = gct[0:8, cc * CHUNK:(cc + 1) * CHUNK]
        row = jnp.broadcast_to(gl[cc * CHUNK:cc * CHUNK + 1, :], (8, LANES))
        val = jnp.sum(jnp.where(sub == lan, row, 0.0), axis=-1, keepdims=True)
        egl_ref[cc] = jnp.broadcast_to(jnp.exp(val), (8, LANES))

    nq = GDN_HEADS * GDN_DK
    for h in range(GDN_HEADS):
        qh = _head_cols(y, h)
        kh = _head_cols(y, GDN_HEADS + h)
        vh = _head_cols(y, 2 * GDN_HEADS + h)
        qn = qh * lax.rsqrt(jnp.sum(qh * qh, axis=-1, keepdims=True) + EPS) * (GDN_DK ** -0.5)
        kn = kh * lax.rsqrt(jnp.sum(kh * kh, axis=-1, keepdims=True) + EPS)
        b_h = beta[:, h:h + 1]
        eg_h = e_gc[:, h:h + 1]
        sl = slice(h * GDN_DK, (h + 1) * GDN_DK)
        kb = kn * b_h
        q_ref[:, sl] = qn.astype(BF16)
        qg_ref[:, sl] = (qn * eg_h).astype(BF16)
        k_ref[:, sl] = kn.astype(BF16)
        kb_ref[:, sl] = kb.astype(BF16)
        kbg_ref[:, sl] = (kb * eg_h).astype(BF16)
        kd_ref[:, sl] = (kn * e_gd[:, h:h + 1]).astype(BF16)
        vb_ref[:, sl] = (vh * b_h).astype(BF16)
    del nq


def _gdn_prep(proj, ab, conv_w, a_log, dt_bias, tt=256):
    b, t, _ = proj.shape
    nc = t // CHUNK
    cpt = tt // CHUNK
    row = lambda v: jnp.zeros((1, LANES), F32).at[0, :GDN_HEADS].set(v)
    tok = pl.BlockSpec((None, tt, D_MODEL), lambda bi, i: (bi, i, 0))
    out_shapes = [jax.ShapeDtypeStruct((b, t, D_MODEL), BF16)] * 7 + [
        jax.ShapeDtypeStruct((b, t, LANES), F32),
        jax.ShapeDtypeStruct((b, nc, 8, CHUNK), F32),
        jax.ShapeDtypeStruct((b, nc, 8, LANES), F32),
    ]
    out_specs = [tok] * 7 + [
        pl.BlockSpec((None, tt, LANES), lambda bi, i: (bi, i, 0)),
        pl.BlockSpec((None, cpt, 8, CHUNK), lambda bi, i: (bi, i, 0, 0)),
        pl.BlockSpec((None, cpt, 8, LANES), lambda bi, i: (bi, i, 0, 0)),
    ]
    return pl.pallas_call(
        _gdn_prep_body,
        grid=(b, t // tt),
        in_specs=[
            pl.BlockSpec((None, tt, QKV_W), lambda bi, i: (bi, i, 0)),
            pl.BlockSpec((None, 8, QKV_W), lambda bi, i: (bi, jnp.maximum(i * (tt // 8) - 1, 0), 0)),
            pl.BlockSpec((None, tt, LANES), lambda bi, i: (bi, i, 0)),
            pl.BlockSpec((CONV_TAPS, QKV_W), lambda bi, i: (0, 0)),
            pl.BlockSpec((1, LANES), lambda bi, i: (0, 0)),
            pl.BlockSpec((1, LANES), lambda bi, i: (0, 0)),
        ],
        out_specs=out_specs,
        out_shape=out_shapes,
        scratch_shapes=[pltpu.VMEM((tt + 8, QKV_W), F32)],
        compiler_params=_params(("parallel", "arbitrary")),
        name="gdn_prep",
    )(proj, proj, ab, conv_w.astype(F32), row(a_log), row(dt_bias))


def _decay(gc_col, gc_row, keep):
    return jnp.where(keep, jnp.exp(jnp.where(keep, gc_col - gc_row, 0.0)), 0.0)


def _gdn_l_body(kb_ref, k_ref, gc_ref, gct_ref, l_ref):
    r = lax.broadcasted_iota(I32, (CHUNK, CHUNK), 0)
    c = lax.broadcasted_iota(I32, (CHUNK, CHUNK), 1)
    strict = c < r
    for cc in range(l_ref.shape[0]):
        rows = slice(cc * CHUNK, (cc + 1) * CHUNK)
        for h in range(GDN_HEADS):
            cols = slice(h * GDN_DK, (h + 1) * GDN_DK)
            kk = lax.dot_general(kb_ref[rows, cols], k_ref[rows, cols], (((1,), (1,)), ((), ())),
                                 preferred_element_type=F32)
            dec = _decay(gc_ref[rows, h:h + 1], gct_ref[cc, h:h + 1, :], strict)
            l_ref[cc, :, h * CHUNK:(h + 1) * CHUNK] = kk * dec


def _gdn_l(kb, k, gc, gct, cpt=4):
    b, t, _ = k.shape
    nc = t // CHUNK
    tt = cpt * CHUNK
    tok = pl.BlockSpec((None, tt, D_MODEL), lambda bi, i: (bi, i, 0))
    return pl.pallas_call(
        _gdn_l_body,
        grid=(b, nc // cpt),
        in_specs=[tok, tok,
                  pl.BlockSpec((None, tt, LANES), lambda bi, i: (bi, i, 0)),
                  pl.BlockSpec((None, cpt, 8, CHUNK), lambda bi, i: (bi, i, 0, 0))],
        out_specs=pl.BlockSpec((None, cpt, CHUNK, GDN_HEADS * CHUNK), lambda bi, i: (bi, i, 0, 0)),
        out_shape=jax.ShapeDtypeStruct((b, nc, CHUNK, GDN_HEADS * CHUNK), F32),
        compiler_params=_params(("parallel", "parallel")),
        name="gdn_l",
    )(kb, k, gc, gct)


def _tri_inv_body(l_ref, t_ref):
    sub = lax.broadcasted_iota(I32, (8, LANES), 0)
    for i in range(CHUNK):
        nq = i // 8 + 1
        acc = [jnp.where(sub == i % 8, 1.0, 0.0) if q == i // 8 else jnp.zeros((8, LANES), F32)
               for q in range(nq)]
        for j in range(i):
            lij = l_ref[i, j:j + 1, :]
            for q in range(j // 8 + 1):
                acc[q] = acc[q] - lij * t_ref[j, 8 * q:8 * q + 8, :]
        for q in range(CHUNK // 8):
            t_ref[i, 8 * q:8 * q + 8, :] = acc[q] if q < nq else jnp.zeros((8, LANES), F32)


def _tri_inv(lt):
    _, nh, _, p = lt.shape
    spec = pl.BlockSpec((CHUNK, None, CHUNK, LANES), lambda h, i: (0, h, 0, i))
    return pl.pallas_call(
        _tri_inv_body,
        grid=(nh, p // LANES),
        in_specs=[spec],
        out_specs=spec,
        out_shape=jax.ShapeDtypeStruct(lt.shape, F32),
        compiler_params=_params(("parallel", "parallel")),
        name="gdn_tri_inv",
    )(lt)


def _gdn_wy_body(t_ref, vb_ref, kbg_ref, q_ref, k_ref, qg_ref, gc_ref, gct_ref, u_ref, wq_ref, attn_ref):
    r = lax.broadcasted_iota(I32, (CHUNK, CHUNK), 0)
    c = lax.broadcasted_iota(I32, (CHUNK, CHUNK), 1)
    tril = c <= r
    for cc in range(t_ref.shape[0]):
        rows = slice(cc * CHUNK, (cc + 1) * CHUNK)
        for h in range(GDN_HEADS):
            cols = slice(h * GDN_DK, (h + 1) * GDN_DK)
            hc = slice(h * CHUNK, (h + 1) * CHUNK)
            rhs = jnp.concatenate([vb_ref[rows, cols], kbg_ref[rows, cols]], axis=1)
            sol = jnp.dot(t_ref[cc, :, hc].astype(BF16), rhs, preferred_element_type=F32)
            u_ref[rows, cols] = sol[:, :GDN_DK].astype(u_ref.dtype)
            wq_ref[cc, 0:CHUNK, cols] = sol[:, GDN_DK:].astype(BF16)
            wq_ref[cc, CHUNK:2 * CHUNK, cols] = qg_ref[rows, cols]
            qk = lax.dot_general(q_ref[rows, cols], k_ref[rows, cols], (((1,), (1,)), ((), ())),
                                 preferred_element_type=F32)
            dec = _decay(gc_ref[rows, h:h + 1], gct_ref[cc, h:h + 1, :], tril)
            attn_ref[cc, :, hc] = (qk * dec).astype(BF16)


def _gdn_wy(tinv, vb, kbg, q, k, qg, gc, gct, cpt=4):
    b, t, _ = k.shape
    nc = t // CHUNK
    tt = cpt * CHUNK
    tok = pl.BlockSpec((None, tt, D_MODEL), lambda bi, i: (bi, i, 0))
    sq = pl.BlockSpec((None, cpt, CHUNK, GDN_HEADS * CHUNK), lambda bi, i: (bi, i, 0, 0))
    return pl.pallas_call(
        _gdn_wy_body,
        grid=(b, nc // cpt),
        in_specs=[sq, tok, tok, tok, tok, tok,
                  pl.BlockSpec((None, tt, LANES), lambda bi, i: (bi, i, 0)),
                  pl.BlockSpec((None, cpt, 8, CHUNK), lambda bi, i: (bi, i, 0, 0))],
        out_specs=[tok,
                   pl.BlockSpec((None, cpt, 2 * CHUNK, D_MODEL), lambda bi, i: (bi, i, 0, 0)),
                   sq],
        out_shape=[jax.ShapeDtypeStruct((b, t, D_MODEL), BF16),
                   jax.ShapeDtypeStruct((b, nc, 2 * CHUNK, D_MODEL), BF16),
                   jax.ShapeDtypeStruct((b, nc, CHUNK, GDN_HEADS * CHUNK), BF16)],
        compiler_params=_params(("parallel", "parallel")),
        name="gdn_wy",
    )(tinv, vb, kbg, q, k, qg, gc, gct)


def _gdn_chunk_body(wq_ref, u_ref, attn_ref, kd_ref, egl_ref, o_ref, s_out_ref, sa_ref, sb_ref):
    c = pl.program_id(0)

    @pl.when(c == 0)
    def _():
        sa_ref[...] = jnp.zeros_like(sa_ref)

    def step(src_ref, dst_ref):
        heads = [(b, h) for b in range(src_ref.shape[0]) for h in range(GDN_HEADS)]
        cols = lambda h: slice(h * GDN_DK, (h + 1) * GDN_DK)
        ws = [jnp.dot(wq_ref[b, :, cols(h)], src_ref[b, h].astype(BF16), preferred_element_type=F32)
              for b, h in heads]
        v_nb = [(u_ref[b, :, cols(h)] - w_[:CHUNK]).astype(BF16) for (b, h), w_ in zip(heads, ws)]
        for (b, h), w_, v_ in zip(heads, ws, v_nb):
            o_ref[b, :, cols(h)] = (w_[CHUNK:] + jnp.dot(attn_ref[b, :, h * CHUNK:(h + 1) * CHUNK], v_,
                                                         preferred_element_type=F32)).astype(o_ref.dtype)
        for (b, h), v_ in zip(heads, v_nb):
            dst_ref[b, h] = src_ref[b, h] * egl_ref[b, h:h + 1, :] + lax.dot_general(
                kd_ref[b, :, cols(h)], v_, (((0,), (0,)), ((), ())), preferred_element_type=F32)

        @pl.when(c == pl.num_programs(0) - 1)
        def _():
            s_out_ref[...] = dst_ref[...]

    @pl.when(c % 2 == 0)
    def _():
        step(sa_ref, sb_ref)

    @pl.when(c % 2 == 1)
    def _():
        step(sb_ref, sa_ref)


def _gdn_chunks(wq, u, attn, kd, egl):
    b, t, _ = u.shape
    nc = t // CHUNK
    tok = pl.BlockSpec((b, CHUNK, D_MODEL), lambda c: (0, c, 0))
    return pl.pallas_call(
        _gdn_chunk_body,
        grid=(nc,),
        in_specs=[
            pl.BlockSpec((b, None, 2 * CHUNK, D_MODEL), lambda c: (0, c, 0, 0)),
            tok,
            pl.BlockSpec((b, None, CHUNK, GDN_HEADS * CHUNK), lambda c: (0, c, 0, 0)),
            tok,
            pl.BlockSpec((b, None, 8, LANES), lambda c: (0, c, 0, 0)),
        ],
        out_specs=[
            tok,
            pl.BlockSpec((b, GDN_HEADS, GDN_DK, GDN_DK), lambda c: (0, 0, 0, 0)),
        ],
        out_shape=[
            jax.ShapeDtypeStruct((b, t, D_MODEL), BF16),
            jax.ShapeDtypeStruct((b, GDN_HEADS, GDN_DK, GDN_DK), F32),
        ],
        scratch_shapes=[pltpu.VMEM((b, GDN_HEADS, GDN_DK, GDN_DK), F32)] * 2,
        compiler_params=_params(("arbitrary",)),
        name="gdn_chunks",
    )(wq, u, attn, kd, egl)


def _gdn_out_body(o_ref, z_ref, on_ref, w_ref, r_ref, out_ref, *, precise):
    parts = []
    for h in range(GDN_HEADS):
        oh = _head_cols(o_ref[...], h).astype(F32)
        zh = _head_cols(z_ref[...], h)
        parts.append(_rms(oh, on_ref[...]) * _silu(zh))
    if precise:
        gh, gl = _split(jnp.concatenate(parts, axis=1))
        out_ref[...] = r_ref[...] + _dot3(gh, gl, w_ref[...])
    else:
        gated = jnp.concatenate([a.astype(BF16) for a in parts], axis=1)
        out_ref[...] = r_ref[...] + jnp.dot(gated, w_ref[...], preferred_element_type=F32)


def _gdn_out(o, proj, o_norm, w_out, res, tm=512):
    m, k = o.shape
    tm = min(tm, m)
    return pl.pallas_call(
        functools.partial(_gdn_out_body, precise=_is_precise(w_out)),
        grid=(m // tm,),
        in_specs=[
            pl.BlockSpec((tm, k), lambda i: (i, 0)),
            pl.BlockSpec((tm, k), lambda i: (i, QKV_W // D_MODEL)),
            pl.BlockSpec((1, GDN_DK), lambda i: (0, 0)),
            _wspec(w_out, (k, k), lambda i: (0, 0)),
            pl.BlockSpec((tm, k), lambda i: (i, 0)),
        ],
        out_specs=pl.BlockSpec((tm, k), lambda i: (i, 0)),
        out_shape=jax.ShapeDtypeStruct((m, k), F32),
        compiler_params=_params(("parallel",)),
        name="gdn_out",
    )(o, proj, o_norm.reshape(1, GDN_DK).astype(F32), _warr(w_out), res)


def _gdn_prompt_layer(h, l, p, w):
    b, t, _ = h.shape
    hf = h.reshape(b * t, D_MODEL)
    proj, ab = _mm(hf, w['a_w_in'][l], gain=p['a_norm'][l], w_aux=w['a_w_ab'][l], tm=512, tn=PROJ_W)
    proj3 = proj.reshape(b, t, PROJ_W)
    q, qg, k, kb, kbg, kd, vb, gc, gct, egl = _gdn_prep(
        proj3, ab.reshape(b, t, LANES), p['a_conv_w'][l], p['a_A_log'][l], p['a_dt_bias'][l])
    lmat = _gdn_l(kb, k, gc, gct)
    nprob = b * (t // CHUNK)
    ppad = -nprob % LANES
    lt = jnp.transpose(lmat.reshape(nprob, CHUNK, GDN_HEADS * CHUNK), (1, 2, 0))
    lt = jnp.pad(lt, ((0, 0), (0, 0), (0, ppad))).reshape(CHUNK, GDN_HEADS, CHUNK, nprob + ppad)
    tt = _tri_inv(lt).reshape(CHUNK, GDN_HEADS * CHUNK, nprob + ppad)[:, :, :nprob]
    tinv = jnp.transpose(tt, (2, 0, 1)).reshape(b, t // CHUNK, CHUNK, GDN_HEADS * CHUNK)
    u, wq, attn = _gdn_wy(tinv, vb, kbg, q, k, qg, gc, gct)
    o, s_new = _gdn_chunks(wq, u, attn, kd, egl)
    h_new = _gdn_out(o.reshape(b * t, D_MODEL), proj, p['a_o_norm'][l], w['a_w_out'][l], hf)
    conv_new = proj3[:, t - (CONV_TAPS - 1):, :QKV_W]
    return h_new.reshape(b, t, D_MODEL), s_new, conv_new


def _gdn_prep1_body(x_ref, st_ref, ab_ref, cw_ref, alog_ref, dtb_ref,
                    q_ref, k_ref, v_ref, eg_ref, beta_ref):
    acc = st_ref[0] * cw_ref[0:1, :]
    for j in range(1, CONV_TAPS - 1):
        acc = acc + st_ref[j] * cw_ref[j:j + 1, :]
    acc = acc + x_ref[...] * cw_ref[CONV_TAPS - 1:CONV_TAPS, :]
    y = _silu(acc)
    ab = ab_ref[...]
    eg_ref[...] = jnp.exp(-jnp.exp(alog_ref[...]) * jax.nn.softplus(ab + dtb_ref[...]))
    beta_ref[...] = jax.nn.sigmoid(pltpu.roll(ab, LANES - GDN_HEADS, axis=1))
    for h in range(GDN_HEADS):
        qh = _head_cols(y, h)
        kh = _head_cols(y, GDN_HEADS + h)
        sl = slice(h * GDN_DK, (h + 1) * GDN_DK)
        q_ref[:, sl] = qh * lax.rsqrt(jnp.sum(qh * qh, axis=-1, keepdims=True) + EPS) * (GDN_DK ** -0.5)
        k_ref[:, sl] = kh * lax.rsqrt(jnp.sum(kh * kh, axis=-1, keepdims=True) + EPS)
    v_ref[...] = y[:, 2 * GDN_HEADS * GDN_DK:]


def _gdn_prep1(proj, ab, conv_state, conv_w, a_log, dt_bias):
    n = proj.shape[0]
    row = lambda v: jnp.zeros((1, LANES), F32).at[0, :GDN_HEADS].set(v)
    tok = pl.BlockSpec((n, D_MODEL), lambda i: (0, 0))
    small = pl.BlockSpec((n, LANES), lambda i: (0, 0))
    return pl.pallas_call(
        _gdn_prep1_body,
        grid=(1,),
        in_specs=[
            pl.BlockSpec((n, QKV_W), lambda i: (0, 0)),
            pl.BlockSpec((CONV_TAPS - 1, n, QKV_W), lambda i: (0, 0, 0)),
            pl.BlockSpec((n, LANES), lambda i: (0, 0)),
            pl.BlockSpec((CONV_TAPS, QKV_W), lambda i: (0, 0)),
            pl.BlockSpec((1, LANES), lambda i: (0, 0)),
            pl.BlockSpec((1, LANES), lambda i: (0, 0)),
        ],
        out_specs=[tok, tok, tok, small, small],
        out_shape=[jax.ShapeDtypeStruct((n, D_MODEL), F32)] * 3 + [jax.ShapeDtypeStruct((n, LANES), F32)] * 2,
        compiler_params=_params(("arbitrary",)),
        name="gdn_prep1",
    )(proj, conv_state, ab, conv_w.astype(F32), row(a_log), row(dt_bias))


def _gdn_step_body(qt_ref, kt_ref, v_ref, eg_ref, beta_ref, s_ref, o_ref, s_out_ref, *, first):
    def compute():
        for b in range(s_ref.shape[0]):
            for h in range(GDN_HEADS):
                k_col = kt_ref[b, :, h:h + 1]
                q_col = qt_ref[b, :, h:h + 1]
                s = s_ref[b, h] * eg_ref[b, h:h + 1, :]
                kv_mem = jnp.sum(k_col * s, axis=0, keepdims=True)
                delta = (v_ref[b, h:h + 1, :] - kv_mem) * beta_ref[b, h:h + 1, :]
                s = s + k_col * delta
                s_out_ref[b, h] = s
                o_ref[b, h:h + 1, :] = jnp.sum(q_col * s, axis=0, keepdims=True)

    if first:
        pl.when(pl.program_id(1) == 0)(compute)

        @pl.when(pl.program_id(1) > 0)
        def _():
            s_out_ref[...] = s_ref[...]
    else:
        compute()


def _gdn_step(qt, kt, v, eg, beta, states, layer, carry, sb=4):
    nl, n = states.shape[:2]
    first = carry is None
    assert first == (layer == 0)
    blk = (None, sb, GDN_HEADS, GDN_DK, GDN_DK)
    if first:
        grid = (n // sb, nl)
        vec = lambda i, j: (i, 0, 0)
        st = pl.BlockSpec(blk, lambda i, j: (j, i, 0, 0, 0))
        sem = ("parallel", "arbitrary")
    else:
        grid = (n // sb,)
        vec = lambda i: (i, 0, 0)
        st = pl.BlockSpec(blk, lambda i: (layer, i, 0, 0, 0))
        sem = ("parallel",)
    col = pl.BlockSpec((sb, GDN_DK, GDN_HEADS), vec)
    row = pl.BlockSpec((sb, GDN_HEADS, GDN_DK), vec)
    return pl.pallas_call(
        functools.partial(_gdn_step_body, first=first),
        grid=grid,
        in_specs=[col, col, row, row, row, st],
        out_specs=[row, st],
        out_shape=[jax.ShapeDtypeStruct((n, GDN_HEADS, GDN_DK), F32),
                   jax.ShapeDtypeStruct(states.shape, F32)],
        input_output_aliases={} if first else {5: 1},
        compiler_params=_params(sem),
        name="gdn_step",
    )(qt, kt, v, eg, beta, states if first else carry)


def _gdn_sample_layer(h, l, p, w, conv_state, delta_states, carry):
    n = h.shape[0]
    proj, ab = _mm(h, w['a_w_in'][l], gain=p['a_norm'][l], w_aux=w['a_w_ab'][l])
    conv_t = jnp.transpose(conv_state, (1, 0, 2))
    q, k, v, eg, beta = _gdn_prep1(proj, ab, conv_t, p['a_conv_w'][l], p['a_A_log'][l], p['a_dt_bias'][l])
    heads = lambda a: a.reshape(n, GDN_HEADS, GDN_DK)
    cols = lambda a: jnp.transpose(heads(a), (0, 2, 1))
    lanes = lambda a: jnp.broadcast_to(a[:, :GDN_HEADS, None], (n, GDN_HEADS, GDN_DK))
    o, s_new = _gdn_step(cols(q), cols(k), heads(v), lanes(eg), lanes(beta), delta_states, l, carry)
    h_new = _gdn_out(o.reshape(n, D_MODEL), proj, p['a_o_norm'][l], w['a_w_out'][l], h)
    conv_new = jnp.concatenate([conv_state[:, 1:], proj[:, None, :QKV_W]], axis=1)
    return h_new, s_new, conv_new


def _rope_tables(pos):
    half = ROT // 2
    inv = ROPE_THETA ** (-jnp.arange(0, ROT, 2, dtype=F32) / ROT)
    ang = pos.astype(F32)[:, None] * inv[None, :]
    cos, sin = jnp.cos(ang), jnp.sin(ang)
    ones = jnp.ones((pos.shape[0], HEAD_DIM - ROT), F32)
    zeros = jnp.zeros_like(ones)
    z8 = jnp.zeros_like(sin)
    c = jnp.concatenate([cos, cos, ones], axis=1)
    sm = jnp.concatenate([-sin, z8, zeros], axis=1)
    sp = jnp.concatenate([z8, sin, zeros], axis=1)
    del half
    return tuple(jnp.tile(a, (1, LANES // HEAD_DIM)) for a in (c, sm, sp))


def _dot_split(a, sel):
    hi = a.astype(BF16)
    lo = (a - hi.astype(F32)).astype(BF16)
    return (jnp.dot(hi, sel, preferred_element_type=F32) + jnp.dot(lo, sel, preferred_element_type=F32))


def _headnorm_rope_body(x_ref, g_ref, seg_ref, segt_ref, c_ref, sm_ref, sp_ref, o_ref):
    x = x_ref[...]
    w = x.shape[1]
    ms = _dot_split(x * x, seg_ref[...]) * (1.0 / HEAD_DIM)
    scale = _dot_split(lax.rsqrt(ms + EPS), segt_ref[...])
    y = x * scale * g_ref[...]
    reps = w // LANES
    tile = lambda a: jnp.concatenate([a] * reps, axis=1) if reps > 1 else a
    half = ROT // 2
    out = (y * tile(c_ref[...]) + pltpu.roll(y, w - half, axis=1) * tile(sm_ref[...])
           + pltpu.roll(y, half, axis=1) * tile(sp_ref[...]))
    o_ref[...] = out.astype(o_ref.dtype)


def _headnorm_rope(x, col_block, width, gain, tables, rows_per_seq, out_dtype, tm=512):
    m = x.shape[0]
    tm = min(tm, m, rows_per_seq)
    nheads = width // HEAD_DIM
    seg = (jnp.arange(width)[:, None] // HEAD_DIM == jnp.arange(LANES)[None, :]).astype(BF16)
    per_seq = rows_per_seq // tm
    tab = pl.BlockSpec((tm, LANES), lambda i: (i % per_seq, 0))
    del nheads
    return pl.pallas_call(
        _headnorm_rope_body,
        grid=(m // tm,),
        in_specs=[
            pl.BlockSpec((tm, width), lambda i: (i, col_block)),
            pl.BlockSpec((1, width), lambda i: (0, 0)),
            pl.BlockSpec((width, LANES), lambda i: (0, 0)),
            pl.BlockSpec((LANES, width), lambda i: (0, 0)),
            tab, tab, tab,
        ],
        out_specs=pl.BlockSpec((tm, width), lambda i: (i, 0)),
        out_shape=jax.ShapeDtypeStruct((m, width), out_dtype),
        compiler_params=_params(("parallel",)),
        name="headnorm_rope",
    )(x, jnp.tile(gain.astype(F32), width // HEAD_DIM).reshape(1, width), seg, seg.T, *tables)


def _softmax_sink(s, sink):
    m = jnp.maximum(jnp.max(s, axis=-1, keepdims=True), sink)
    p = jnp.exp(s - m)
    return p, jnp.sum(p, axis=-1, keepdims=True) + jnp.exp(sink - m)


def _swa_prompt_body(sink_ref, q_ref, kp_ref, kc_ref, vp_ref, vc_ref, o_ref):
    i = pl.program_id(1)
    blk = q_ref.shape[0]
    r = lax.broadcasted_iota(I32, (blk, 2 * blk), 0)
    c = lax.broadcasted_iota(I32, (blk, 2 * blk), 1)
    valid = (c >= r) & (c <= r + WINDOW) & ((c >= blk) | (i > 0))
    for kvh in range(KV_HEADS):
        cols = slice(kvh * HEAD_DIM, (kvh + 1) * HEAD_DIM)
        kk = jnp.concatenate([kp_ref[:, cols], kc_ref[:, cols]], axis=0).astype(BF16)
        vv = jnp.concatenate([vp_ref[:, cols], vc_ref[:, cols]], axis=0).astype(BF16)
        for g in range(ATT_GROUP):
            h = kvh * ATT_GROUP + g
            hc = slice(h * HEAD_DIM, (h + 1) * HEAD_DIM)
            s = lax.dot_general(q_ref[:, hc], kk, (((1,), (1,)), ((), ())),
                                preferred_element_type=F32) * (HEAD_DIM ** -0.5)
            s = jnp.where(valid, s, -jnp.inf)
            p, den = _softmax_sink(s, sink_ref[h])
            o_ref[:, hc] = jnp.dot((p / den).astype(BF16), vv, preferred_element_type=F32).astype(o_ref.dtype)


def _swa_prompt(q, k, kv, sinks):
    b, t, _ = q.shape
    blk = WINDOW
    kvw = KV_HEADS * HEAD_DIM
    prev = lambda bi, i, s: (bi, jnp.maximum(i - 1, 0), 0)
    cur = lambda bi, i, s: (bi, i, 0)
    prev_v = lambda bi, i, s: (bi, jnp.maximum(i - 1, 0), 1)
    cur_v = lambda bi, i, s: (bi, i, 1)
    grid_spec = pltpu.PrefetchScalarGridSpec(
        num_scalar_prefetch=1,
        grid=(b, t // blk),
        in_specs=[
            pl.BlockSpec((None, blk, D_MODEL), cur),
            pl.BlockSpec((None, blk, kvw), prev),
            pl.BlockSpec((None, blk, kvw), cur),
            pl.BlockSpec((None, blk, kvw), prev_v),
            pl.BlockSpec((None, blk, kvw), cur_v),
        ],
        out_specs=pl.BlockSpec((None, blk, D_MODEL), cur),
    )
    return pl.pallas_call(
        _swa_prompt_body,
        grid_spec=grid_spec,
        out_shape=jax.ShapeDtypeStruct((b, t, D_MODEL), BF16),
        compiler_params=_params(("parallel", "parallel")),
        name="swa_prompt",
    )(sinks.astype(F32), q, k, k, kv, kv)


def _swa_sample_body(sink_ref, q_ref, kc_ref, kn_ref, vc_ref, vn_ref, o_ref):
    wc = kc_ref.shape[1]
    kvw = kc_ref.shape[2]
    nkeys = 2 * wc
    fill = jnp.zeros((nkeys - wc - 8, kvw), F32)
    key_idx = lax.broadcasted_iota(I32, (ATT_HEADS, nkeys), 1)
    sink = sink_ref[:, 0:1]
    seqs = range(q_ref.shape[0])
    scores = []
    for b in seqs:
        kk = jnp.concatenate([kc_ref[b], kn_ref[b], fill], axis=0)
        scores.append(lax.dot_general(q_ref[b], kk, (((1,), (1,)), ((), ())), precision=HI,
                                      preferred_element_type=F32))
    probs = []
    for s in scores:
        s = jnp.where(key_idx <= wc, s * (HEAD_DIM ** -0.5), -jnp.inf)
        p, den = _softmax_sink(s, sink)
        probs.append(p / den)
    outs = []
    for b, p in zip(seqs, probs):
        vv = jnp.concatenate([vc_ref[b], vn_ref[b], fill], axis=0)
        outs.append(jnp.dot(p, vv, precision=HI, preferred_element_type=F32))
    for b, o in zip(seqs, outs):
        for kvh in range(KV_HEADS):
            heads = slice(kvh * ATT_GROUP, (kvh + 1) * ATT_GROUP)
            o_ref[b, heads, :] = o[heads, kvh * HEAD_DIM:(kvh + 1) * HEAD_DIM]


def _swa_sample(q, k_cache, k_new, v_cache, v_new, sinks, sb=8):
    n = q.shape[0]
    wc = k_cache.shape[1]
    kvw = KV_HEADS * HEAD_DIM
    own = (jnp.arange(ATT_HEADS)[:, None] // ATT_GROUP == jnp.arange(KV_HEADS)[None, :])
    q_rows = (q[:, :, None, :] * own[None, :, :, None].astype(q.dtype)).reshape(n, ATT_HEADS, kvw)
    pad8 = lambda a: jnp.pad(a[:, None, :], ((0, 0), (0, 7), (0, 0)))
    blk = lambda shape: pl.BlockSpec(shape, lambda i: (i, 0, 0))
    return pl.pallas_call(
        _swa_sample_body,
        grid=(n // sb,),
        in_specs=[pl.BlockSpec((ATT_HEADS, LANES), lambda i: (0, 0)),
                  blk((sb, ATT_HEADS, kvw)), blk((sb, wc, kvw)), blk((sb, 8, kvw)),
                  blk((sb, wc, kvw)), blk((sb, 8, kvw))],
        out_specs=blk((sb, ATT_HEADS, HEAD_DIM)),
        out_shape=jax.ShapeDtypeStruct((n, ATT_HEADS, HEAD_DIM), F32),
        compiler_params=_params(("parallel",)),
        name="swa_sample",
    )(jnp.broadcast_to(sinks.astype(F32)[:, None], (ATT_HEADS, LANES)), q_rows, k_cache, pad8(k_new),
      v_cache, pad8(v_new))


def _channel_mixer(hf, l, p, w):
    if l % 2 == 0:
        return _ffn_dense(hf, p['ffn_norm'][l], w['dense_w_gu'][l // 2], w['dense_w_down'][l // 2])
    return _moe(hf, p['ffn_norm'][l], p['moe_router'][l // 2], p['moe_router_b'][l // 2],
                w['moe_w_gu'][l // 2], w['moe_w_down'][l // 2])


def _run_prompt(x, p, w):
    b, t, _ = x.shape
    n = b * t
    h = x
    convs, deltas = [], []
    for l in range(N_GDN):
        h, s_new, c_new = _gdn_prompt_layer(h, l, p, w)
        convs.append(c_new)
        deltas.append(s_new)
        h = _channel_mixer(h.reshape(n, D_MODEL), l, p, w).reshape(b, t, D_MODEL)
    hf = h.reshape(n, D_MODEL)
    tables = _rope_tables(jnp.arange(t, dtype=I32))
    kvw = KV_HEADS * HEAD_DIM
    kv = _mm(hf, w['w_kv'], gain=p['kv_norm'], tn=2 * kvw)
    k_sh = _headnorm_rope(kv, 0, kvw, p['k_norm'], tables, t, F32)
    k3 = k_sh.reshape(b, t, kvw)
    kv3 = kv.reshape(b, t, 2 * kvw)
    for j in range(2):
        l = N_GDN + j
        qp = _mm(hf, w['b_w_q'][j], gain=p['b_norm'][j])
        q = _headnorm_rope(qp, 0, D_MODEL, p['b_q_norm'][j], tables, t, BF16)
        o = _swa_prompt(q.reshape(b, t, D_MODEL), k3, kv3, p['b_sinks'][j])
        hf = _mm(o.reshape(n, D_MODEL), w['b_w_o'][j], res=hf)
        hf = _channel_mixer(hf, l, p, w)
    k_win = k3[:, t - WINDOW:].reshape(b, WINDOW, KV_HEADS, HEAD_DIM)
    v_win = kv3[:, t - WINDOW:, kvw:].reshape(b, WINDOW, KV_HEADS, HEAD_DIM)
    return hf.reshape(b, t, D_MODEL), jnp.stack(convs), jnp.stack(deltas), k_win, v_win


def _run_sample(x, conv_state, delta_state, k_cache, v_cache, p, w):
    n = x.shape[0]
    h = x.reshape(n, D_MODEL)
    convs, deltas = [], None
    for l in range(N_GDN):
        h, deltas, c_new = _gdn_sample_layer(h, l, p, w, conv_state[l], delta_state, deltas)
        convs.append(c_new)
        h = _channel_mixer(h, l, p, w)
    tables = _rope_tables(jnp.full((n,), PAST_LEN, I32))
    kvw = KV_HEADS * HEAD_DIM
    wc = k_cache.shape[1]
    kv = _mm(h, w['w_kv'], gain=p['kv_norm'], tn=2 * kvw)
    k_new = _headnorm_rope(kv, 0, kvw, p['k_norm'], tables, n, F32)
    v_new = kv[:, kvw:]
    kc = k_cache.reshape(n, wc, kvw)
    vc = v_cache.reshape(n, wc, kvw)
    for j in range(2):
        l = N_GDN + j
        qp = _mm(h, w['b_w_q'][j], gain=p['b_norm'][j])
        q = _headnorm_rope(qp, 0, D_MODEL, p['b_q_norm'][j], tables, n, F32)
        o = _swa_sample(q.reshape(n, ATT_HEADS, HEAD_DIM), kc, k_new, vc, v_new, p['b_sinks'][j])
        h = _mm(o.reshape(n, D_MODEL), w['b_w_o'][j], res=h)
        h = _channel_mixer(h, l, p, w)
    k_win = jnp.concatenate([kc, k_new[:, None, :]], axis=1)[:, -wc:].reshape(n, wc, KV_HEADS, HEAD_DIM)
    v_win = jnp.concatenate([vc, v_new[:, None, :]], axis=1)[:, -wc:].reshape(n, wc, KV_HEADS, HEAD_DIM)
    return h.reshape(n, 1, D_MODEL), jnp.stack(convs), deltas, k_win, v_win


def kernel(x_prompt, x_sample, state_conv, state_delta, cache_k_win, cache_v_win, a_norm, a_w_in, a_conv_w,
           a_A_log, a_dt_bias, a_o_norm, a_w_out, kv_norm, w_kv, k_norm, b_norm, b_w_q, b_q_norm, b_sinks,
           b_w_o, ffn_norm, dense_w_gu, dense_w_down, moe_router, moe_router_b, moe_w_gu, moe_w_down):
    p = dict(a_norm=a_norm, a_conv_w=a_conv_w, a_A_log=a_A_log, a_dt_bias=a_dt_bias, a_o_norm=a_o_norm,
             kv_norm=kv_norm, k_norm=k_norm, b_norm=b_norm, b_q_norm=b_q_norm, b_sinks=b_sinks,
             ffn_norm=ffn_norm, moe_router=moe_router, moe_router_b=moe_router_b)
    n_proj = PROJ_W
    gate_cols = jnp.pad(a_w_in[:, :, n_proj:], ((0, 0), (0, 0), (0, LANES - (a_w_in.shape[-1] - n_proj))))
    raw = dict(a_w_in=a_w_in[:, :, :n_proj], a_w_ab=gate_cols, a_w_out=a_w_out, w_kv=w_kv[None],
               b_w_q=b_w_q, b_w_o=b_w_o, dense_w_gu=dense_w_gu, dense_w_down=dense_w_down,
               moe_w_gu=moe_w_gu, moe_w_down=moe_w_down)
    w_fast = {name: [val[l].astype(BF16) for l in range(val.shape[0])] for name, val in raw.items()}
    w_precise = {name: [_Layer(val, l) for l in range(val.shape[0])] for name, val in raw.items()}
    w_precise['a_w_in'] = [_Layer(a_w_in, l, cols=n_proj) for l in range(a_w_in.shape[0])]
    w_fast['w_kv'], w_precise['w_kv'] = w_fast['w_kv'][0], w_kv
    y_p, conv_p, delta_p, kw_p, vw_p = _run_prompt(x_prompt, p, w_fast)
    y_s, conv_s, delta_s, kw_s, vw_s = _run_sample(
        x_sample, state_conv, state_delta, cache_k_win, cache_v_win, p, w_precise)
    return (y_p, y_s, conv_p, conv_s, delta_p, delta_s, kw_p, kw_s, vw_p, vw_s)
```

```python
import functools

import jax
import jax.numpy as jnp
from jax import lax
from jax.experimental import pallas as pl
from jax.experimental.pallas import tpu as pltpu

F32, BF16, I32 = jnp.float32, jnp.bfloat16, jnp.int32
HI = lax.Precision.HIGHEST

D_MODEL = 1024
N_GDN = 2
GDN_HEADS = 8
GDN_DK = 128
CONV_TAPS = 4
CHUNK = 64
QKV_W = 3 * GDN_HEADS * GDN_DK
PROJ_W = QKV_W + D_MODEL
ATT_HEADS = 16
KV_HEADS = 4
HEAD_DIM = 64
ATT_GROUP = ATT_HEADS // KV_HEADS
WINDOW = 128
ROT = HEAD_DIM // 4
ROPE_THETA = 500000.0
PAST_LEN = 8192
N_EXPERTS = 8
EPS = 1e-6

LANES = 128
VMEM_LIMIT = 56 * 1024 * 1024
NEG = -1e30


def _params(sem):
    return pltpu.CompilerParams(dimension_semantics=sem, vmem_limit_bytes=VMEM_LIMIT)


def _rms(x, g):
    return x * lax.rsqrt(jnp.mean(x * x, axis=-1, keepdims=True) + EPS) * g


def _silu(x):
    return x * jax.nn.sigmoid(x)


def _split(x):
    hi = x.astype(BF16)
    return hi, (x - hi.astype(F32)).astype(BF16)


def _dot3(xh, xl, w):
    wh, wl = _split(w)
    return (jnp.dot(xh, wh, preferred_element_type=F32)
            + (jnp.dot(xl, wh, preferred_element_type=F32) + jnp.dot(xh, wl, preferred_element_type=F32)))


class _Layer:
    def __init__(self, arr, layer, cols=None):
        self.arr, self.lead, self.dtype = arr, (layer,), arr.dtype
        self.shape = arr.shape[1:] if cols is None else arr.shape[1:-1] + (cols,)


def _warr(w):
    return w.arr if isinstance(w, _Layer) else w


def _wspec(w, block, index_fn, **kwargs):
    lead = w.lead if isinstance(w, _Layer) else ()
    return pl.BlockSpec((None,) * len(lead) + tuple(block), lambda *a: lead + tuple(index_fn(*a)), **kwargs)


def _is_precise(w):
    assert w.dtype in (BF16, F32)
    return w.dtype == F32


def _mm_body(*refs, has_norm, has_aux, has_res, precise):
    it = iter(refs)
    x_ref = next(it)
    g_ref = next(it) if has_norm else None
    w_ref = next(it)
    wa_ref = next(it) if has_aux else None
    r_ref = next(it) if has_res else None
    o_ref = next(it)
    oa_ref = next(it) if has_aux else None
    xn_ref = next(it)

    def product(w):
        if precise:
            return _dot3(xn_ref[0], xn_ref[1], w)
        return jnp.dot(xn_ref[0], w, preferred_element_type=F32)

    @pl.when(pl.program_id(1) == 0)
    def _():
        x = x_ref[...].astype(F32)
        if has_norm:
            x = _rms(x, g_ref[...])
        if precise:
            xn_ref[0], xn_ref[1] = _split(x)
        else:
            xn_ref[0] = x.astype(BF16)
        if has_aux:
            oa_ref[...] = product(wa_ref[...])

    acc = product(w_ref[...])
    if has_res:
        acc = acc + r_ref[...]
    o_ref[...] = acc.astype(o_ref.dtype)


def _mm(x, w, gain=None, res=None, w_aux=None, tm=1024, tn=1024, out_dtype=F32):
    precise = _is_precise(w)
    m, k = x.shape
    n = w.shape[1]
    tm = min(tm, m)
    tn = min(tn, n)
    assert m % tm == 0 and n % tn == 0
    ins = [x]
    specs = [pl.BlockSpec((tm, k), lambda i, j: (i, 0))]
    if gain is not None:
        ins.append(gain.reshape(1, k).astype(F32))
        specs.append(pl.BlockSpec((1, k), lambda i, j: (0, 0)))
    ins.append(_warr(w))
    resident = dict(pipeline_mode=pl.Buffered(1)) if tn == n else {}
    specs.append(_wspec(w, (k, tn), lambda i, j: (0, j), **resident))
    out_specs = [pl.BlockSpec((tm, tn), lambda i, j: (i, j))]
    out_shape = [jax.ShapeDtypeStruct((m, n), out_dtype)]
    if w_aux is not None:
        na = w_aux.shape[1]
        ins.append(_warr(w_aux))
        specs.append(_wspec(w_aux, (k, na), lambda i, j: (0, 0)))
        out_specs.append(pl.BlockSpec((tm, na), lambda i, j: (i, 0)))
        out_shape.append(jax.ShapeDtypeStruct((m, na), F32))
    if res is not None:
        ins.append(res)
        specs.append(pl.BlockSpec((tm, tn), lambda i, j: (i, j)))
    out = pl.pallas_call(
        functools.partial(_mm_body, has_norm=gain is not None, has_aux=w_aux is not None,
                          has_res=res is not None, precise=precise),
        grid=(m // tm, n // tn),
        in_specs=specs,
        out_specs=out_specs,
        out_shape=out_shape,
        scratch_shapes=[pltpu.VMEM((1 + precise, tm, k), BF16)],
        compiler_params=_params(("parallel", "arbitrary")),
        name="mm",
    )(*ins)
    return out if w_aux is not None else out[0]


def _swiglu(xh, xl, wg, wu, wd):
    if xl is None:
        gate = jnp.dot(xh, wg, preferred_element_type=F32)
        up = jnp.dot(xh, wu, preferred_element_type=F32)
        return jnp.dot((_silu(gate) * up).astype(BF16), wd, preferred_element_type=F32)
    gate = _dot3(xh, xl, wg)
    up = _dot3(xh, xl, wu)
    ah, al = _split(_silu(gate) * up)
    return _dot3(ah, al, wd)


def _ffn_dense_body(x_ref, g_ref, wg_ref, wu_ref, wd_ref, o_ref, xn_ref, acc_ref, *, precise):
    f = pl.program_id(1)

    if not precise:
        xn = _rms(x_ref[...], g_ref[...]).astype(BF16)
        o_ref[...] = x_ref[...] + _swiglu(xn, None, wg_ref[...], wu_ref[...], wd_ref[...])
        return

    @pl.when(f == 0)
    def _():
        xn_ref[0], xn_ref[1] = _split(_rms(x_ref[...], g_ref[...]))
        acc_ref[...] = x_ref[...]

    acc_ref[...] += _swiglu(xn_ref[0], xn_ref[1], wg_ref[...], wu_ref[...], wd_ref[...])

    @pl.when(f == pl.num_programs(1) - 1)
    def _():
        o_ref[...] = acc_ref[...]


def _ffn_dense(x, gain, w_gu, w_down, tm=512):
    precise = _is_precise(w_gu)
    m, k = x.shape
    ff = w_down.shape[0]
    tm = min(tm, m)
    nf = 1 if not precise else ff // (2 * LANES)
    tf = ff // nf
    assert tf % LANES == 0 and tf * nf == ff
    resident = {} if precise else dict(pipeline_mode=pl.Buffered(1))
    return pl.pallas_call(
        functools.partial(_ffn_dense_body, precise=precise),
        grid=(m // tm, nf),
        in_specs=[
            pl.BlockSpec((tm, k), lambda i, f: (i, 0)),
            pl.BlockSpec((1, k), lambda i, f: (0, 0)),
            _wspec(w_gu, (k, tf), lambda i, f: (0, f), **resident),
            _wspec(w_gu, (k, tf), lambda i, f: (0, nf + f), **resident),
            _wspec(w_down, (tf, k), lambda i, f: (f, 0), **resident),
        ],
        out_specs=pl.BlockSpec((tm, k), lambda i, f: (i, 0)),
        out_shape=jax.ShapeDtypeStruct((m, k), F32),
        scratch_shapes=[pltpu.VMEM((1 + precise, tm, k), BF16), pltpu.VMEM((tm, k), F32)],
        compiler_params=_params(("parallel", "arbitrary")),
        name="ffn_dense",
    )(x, gain.reshape(1, k).astype(F32), _warr(w_gu), _warr(w_gu), _warr(w_down))


def _router_body(x_ref, g_ref, wr_ref, br_ref, idx_ref, gate_ref, cnt_ref, carry_ref):
    i = pl.program_id(0)

    @pl.when(i == 0)
    def _():
        carry_ref[...] = jnp.zeros_like(carry_ref)

    tm = x_ref.shape[0]
    xn = _rms(x_ref[...], g_ref[...])
    logits = _dot3(*_split(xn), wr_ref[...]) + br_ref[...]
    lane = lax.broadcasted_iota(I32, logits.shape, 1)
    m1 = jnp.max(logits, axis=-1, keepdims=True)
    i1 = jnp.min(jnp.where(logits == m1, lane, LANES), axis=-1, keepdims=True)
    rest = jnp.where(lane == i1, NEG, logits)
    m2 = jnp.max(rest, axis=-1, keepdims=True)
    i2 = jnp.min(jnp.where(rest == m2, lane, LANES), axis=-1, keepdims=True)
    e2 = jnp.exp(m2 - m1)
    g1 = 1.0 / (1.0 + e2)
    g2 = e2 * g1
    oh1 = lane == i1
    oh2 = lane == i2
    oh = jnp.where(oh1 | oh2, 1.0, 0.0).astype(BF16)
    r = lax.broadcasted_iota(I32, (tm, tm), 0)
    c = lax.broadcasted_iota(I32, (tm, tm), 1)
    below = jnp.where(c < r, 1.0, 0.0).astype(BF16)
    before = jnp.dot(below, oh, preferred_element_type=F32) + carry_ref[0:1, :]
    rk1 = jnp.sum(jnp.where(oh1, before, 0.0), axis=-1, keepdims=True).astype(I32)
    rk2 = jnp.sum(jnp.where(oh2, before, 0.0), axis=-1, keepdims=True).astype(I32)
    idx_ref[...] = jnp.where(lane == 0, i1, jnp.where(lane == 1, i2, jnp.where(lane == 2, rk1, rk2)))
    gate_ref[...] = jnp.where(lane == 0, g1, g2)
    total = carry_ref[0:1, :] + jnp.sum(oh.astype(F32), axis=0, keepdims=True)
    carry_ref[...] = jnp.broadcast_to(total, carry_ref.shape)
    cnt_ref[...] = carry_ref[...]


def _router(x, gain, w_router, b_router, tm=512):
    m, k = x.shape
    tm = min(tm, m)
    wr = jnp.zeros((k, LANES), F32).at[:, :N_EXPERTS].set(w_router)
    br = jnp.full((1, LANES), NEG, F32).at[0, :N_EXPERTS].set(b_router)
    return pl.pallas_call(
        _router_body,
        grid=(m // tm,),
        in_specs=[
            pl.BlockSpec((tm, k), lambda i: (i, 0)),
            pl.BlockSpec((1, k), lambda i: (0, 0)),
            pl.BlockSpec((k, LANES), lambda i: (0, 0)),
            pl.BlockSpec((1, LANES), lambda i: (0, 0)),
        ],
        out_specs=[
            pl.BlockSpec((tm, LANES), lambda i: (i, 0)),
            pl.BlockSpec((tm, LANES), lambda i: (i, 0)),
            pl.BlockSpec((8, LANES), lambda i: (0, 0)),
        ],
        out_shape=[
            jax.ShapeDtypeStruct((m, LANES), I32),
            jax.ShapeDtypeStruct((m, LANES), F32),
            jax.ShapeDtypeStruct((8, LANES), F32),
        ],
        scratch_shapes=[pltpu.VMEM((8, LANES), F32)],
        compiler_params=_params(("arbitrary",)),
        name="moe_router",
    )(x, gain.reshape(1, k).astype(F32), wr, br)


def _row_copy(src, dst, src_row, dst_row, sem):
    return pltpu.make_async_copy(src.at[pl.ds(src_row, 1)], dst.at[pl.ds(dst_row, 1)], sem)


def _scatter_body(d_ref, x_ref, g_ref, xs_in_ref, xs_ref, buf_ref, sem):
    del xs_in_ref
    tm = buf_ref.shape[0]
    buf_ref[...] = _rms(x_ref[...], g_ref[...])

    def issue(t, carry):
        for kk in range(2):
            _row_copy(buf_ref, xs_ref, t, d_ref[0, 0, 2 * t + kk], sem).start()
        return carry

    lax.fori_loop(0, tm, issue, 0, unroll=8)
    for kk in range(2):
        pltpu.make_async_copy(buf_ref, xs_ref.at[pl.ds(0, tm)], sem).wait()


def _moe_scatter(x, gain, dest_tiles, rows, tm):
    m, k = x.shape
    return pl.pallas_call(
        _scatter_body,
        grid=(m // tm,),
        in_specs=[
            pl.BlockSpec((1, 1, 2 * tm), lambda i: (i, 0, 0), memory_space=pltpu.SMEM),
            pl.BlockSpec((tm, k), lambda i: (i, 0)),
            pl.BlockSpec((1, k), lambda i: (0, 0)),
            pl.BlockSpec(memory_space=pl.ANY),
        ],
        out_specs=pl.BlockSpec(memory_space=pl.ANY),
        out_shape=jax.ShapeDtypeStruct((rows, k), F32),
        scratch_shapes=[pltpu.VMEM((tm, k), F32), pltpu.SemaphoreType.DMA(())],
        input_output_aliases={3: 0},
        compiler_params=_params(("arbitrary",)),
        name="moe_scatter",
    )(dest_tiles, x, gain.reshape(1, k).astype(F32), jnp.zeros((rows, k), F32))


def _ffn_group_body(te_ref, nu_ref, x_ref, wg_ref, wu_ref, wd_ref, o_ref, *, precise):
    del te_ref
    used = pl.program_id(0) < nu_ref[0]

    @pl.when(used)
    def _():
        xh, xl = _split(x_ref[...]) if precise else (x_ref[...].astype(BF16), None)
        o_ref[...] = _swiglu(xh, xl, wg_ref[...], wu_ref[...], wd_ref[...])

    @pl.when(jnp.logical_not(used))
    def _():
        o_ref[...] = jnp.zeros_like(o_ref)


def _ffn_grouped(xs, tile_expert, n_used, w_gu, w_down, tm):
    precise = _is_precise(w_gu)
    rows, k = xs.shape
    ff = w_down.shape[1]
    nt = rows // tm

    def row_map(i, te, nu):
        return (jnp.minimum(i, nu[0] - 1), 0)

    grid_spec = pltpu.PrefetchScalarGridSpec(
        num_scalar_prefetch=2,
        grid=(nt,),
        in_specs=[
            pl.BlockSpec((tm, k), row_map),
            _wspec(w_gu, (None, k, ff), lambda i, te, nu: (te[i], 0, 0)),
            _wspec(w_gu, (None, k, ff), lambda i, te, nu: (te[i], 0, 1)),
            _wspec(w_down, (None, ff, k), lambda i, te, nu: (te[i], 0, 0)),
        ],
        out_specs=pl.BlockSpec((tm, k), lambda i, te, nu: (i, 0)),
    )
    return pl.pallas_call(
        functools.partial(_ffn_group_body, precise=precise),
        grid_spec=grid_spec,
        out_shape=jax.ShapeDtypeStruct((rows, k), F32),
        compiler_params=_params(("arbitrary",)),
        name="moe_experts",
    )(tile_expert, n_used, xs, _warr(w_gu), _warr(w_gu), _warr(w_down))


def _combine_body(d_ref, x_ref, gate_ref, ys_ref, o_ref, buf_ref, sem):
    tm = x_ref.shape[0]

    def issue(t, carry):
        for kk in range(2):
            _row_copy(ys_ref, buf_ref.at[kk], d_ref[0, 0, 2 * t + kk], t, sem).start()
        return carry

    lax.fori_loop(0, tm, issue, 0, unroll=8)
    for kk in range(2):
        pltpu.make_async_copy(ys_ref.at[pl.ds(0, tm)], buf_ref.at[kk], sem).wait()
    g = gate_ref[...]
    o_ref[...] = x_ref[...] + g[:, 0:1] * buf_ref[0] + g[:, 1:2] * buf_ref[1]


def _moe_combine(x, gates, dest_tiles, ys, tm):
    m, k = x.shape
    return pl.pallas_call(
        _combine_body,
        grid=(m // tm,),
        in_specs=[
            pl.BlockSpec((1, 1, 2 * tm), lambda i: (i, 0, 0), memory_space=pltpu.SMEM),
            pl.BlockSpec((tm, k), lambda i: (i, 0)),
            pl.BlockSpec((tm, LANES), lambda i: (i, 0)),
            pl.BlockSpec(memory_space=pl.ANY),
        ],
        out_specs=pl.BlockSpec((tm, k), lambda i: (i, 0)),
        out_shape=jax.ShapeDtypeStruct((m, k), F32),
        scratch_shapes=[pltpu.VMEM((2, tm, k), F32), pltpu.SemaphoreType.DMA(())],
        compiler_params=_params(("arbitrary",)),
        name="moe_combine",
    )(dest_tiles, x, gates, ys)


SEG_TOK = 256
SEG_ROWS = 640
EXPERT_TILE = 512


def _router_seg_body(x_ref, g_ref, wr_ref, br_ref, idx_ref, gate_ref, tab_ref, cnt_ref, carry_ref):
    i = pl.program_id(0)

    @pl.when(i == 0)
    def _():
        carry_ref[...] = jnp.zeros_like(carry_ref)

    tm = x_ref.shape[0]
    xn = _rms(x_ref[...], g_ref[...])
    logits = _dot3(*_split(xn), wr_ref[...]) + br_ref[...]
    lane = lax.broadcasted_iota(I32, logits.shape, 1)
    m1 = jnp.max(logits, axis=-1, keepdims=True)
    i1 = jnp.min(jnp.where(logits == m1, lane, LANES), axis=-1, keepdims=True)
    rest = jnp.where(lane == i1, NEG, logits)
    m2 = jnp.max(rest, axis=-1, keepdims=True)
    i2 = jnp.min(jnp.where(rest == m2, lane, LANES), axis=-1, keepdims=True)
    e2 = jnp.exp(m2 - m1)
    g1 = 1.0 / (1.0 + e2)
    g2 = e2 * g1
    oh1 = lane == i1
    oh2 = lane == i2
    oh = jnp.where(oh1 | oh2, 1.0, 0.0)
    r = lax.broadcasted_iota(I32, (tm, tm), 0)
    c = lax.broadcasted_iota(I32, (tm, tm), 1)
    below = jnp.where(c < r, 1.0, 0.0).astype(BF16)
    rank = jnp.dot(below, oh.astype(BF16), preferred_element_type=F32)
    count = jnp.sum(oh, axis=0, keepdims=True)
    padded = jnp.floor((count + 7.0) * 0.125) * 8.0
    er = lax.broadcasted_iota(I32, (LANES, LANES), 0)
    ec = lax.broadcasted_iota(I32, (LANES, LANES), 1)
    start = jnp.dot(jnp.broadcast_to(padded, (8, LANES)), jnp.where(er < ec, 1.0, 0.0),
                    precision=HI, preferred_element_type=F32)[0:1]
    slot = rank + start
    s1 = jnp.sum(jnp.where(oh1, slot, 0.0), axis=-1, keepdims=True).astype(I32)
    s2 = jnp.sum(jnp.where(oh2, slot, 0.0), axis=-1, keepdims=True).astype(I32)
    idx_ref[...] = jnp.where(lane == 0, i1, jnp.where(lane == 1, i2, jnp.where(lane == 2, s1, s2)))
    gate_ref[...] = jnp.where(lane == 0, g1, g2)
    sub = lax.broadcasted_iota(I32, (8, LANES), 0)
    rows = jnp.where(sub == 0, padded * 0.125, jnp.where(sub == 1, start, carry_ref[...]))
    tab_ref[...] = jnp.where(sub < 3, rows, 0.0).astype(I32)
    carry_ref[...] = carry_ref[...] + padded
    cnt_ref[...] = carry_ref[...]


def _router_seg(x, gain, w_router, b_router):
    m, k = x.shape
    tm = SEG_TOK
    wr = jnp.zeros((k, LANES), F32).at[:, :N_EXPERTS].set(w_router)
    br = jnp.full((1, LANES), NEG, F32).at[0, :N_EXPERTS].set(b_router)
    return pl.pallas_call(
        _router_seg_body,
        grid=(m // tm,),
        in_specs=[
            pl.BlockSpec((tm, k), lambda i: (i, 0)),
            pl.BlockSpec((1, k), lambda i: (0, 0)),
            pl.BlockSpec((k, LANES), lambda i: (0, 0)),
            pl.BlockSpec((1, LANES), lambda i: (0, 0)),
        ],
        out_specs=[
            pl.BlockSpec((tm, LANES), lambda i: (i, 0)),
            pl.BlockSpec((tm, LANES), lambda i: (i, 0)),
            pl.BlockSpec((None, 8, LANES), lambda i: (i, 0, 0)),
            pl.BlockSpec((8, LANES), lambda i: (0, 0)),
        ],
        out_shape=[
            jax.ShapeDtypeStruct((m, LANES), I32),
            jax.ShapeDtypeStruct((m, LANES), F32),
            jax.ShapeDtypeStruct((m // tm, 8, LANES), I32),
            jax.ShapeDtypeStruct((8, LANES), F32),
        ],
        scratch_shapes=[pltpu.VMEM((8, LANES), F32)],
        compiler_params=_params(("arbitrary",)),
        name="moe_router_seg",
    )(x, gain.reshape(1, k).astype(F32), wr, br)


def _chunk_copy(src, dst, src_row, dst_row, sem):
    rows = lambda r: pl.ds(r if isinstance(r, int) else pl.multiple_of(r, 8), 8)
    return pltpu.make_async_copy(src.at[rows(src_row)], dst.at[rows(dst_row)], sem)


def _segment_start(tab_ref, base_ref, local_ref, hbm_ref, sem, to_hbm):
    for e in range(N_EXPERTS):
        local0 = tab_ref[1, e]
        hbm0 = base_ref[e] + tab_ref[2, e]

        def issue(j, carry, local0=local0, hbm0=hbm0):
            if to_hbm:
                _chunk_copy(local_ref, hbm_ref, local0 + 8 * j, hbm0 + 8 * j, sem).start()
            else:
                _chunk_copy(hbm_ref, local_ref, hbm0 + 8 * j, local0 + 8 * j, sem).start()
            return carry

        lax.fori_loop(0, tab_ref[0, e], issue, 0)


def _segment_wait(tab_ref, local_ref, hbm_ref, sem, to_hbm):
    total = 0
    for e in range(N_EXPERTS):
        total = total + tab_ref[0, e]

    def drain(j, carry):
        if to_hbm:
            _chunk_copy(local_ref, hbm_ref, 0, 0, sem).wait()
        else:
            _chunk_copy(hbm_ref, local_ref, 0, 0, sem).wait()
        return carry

    lax.fori_loop(0, total, drain, 0)


def _slot_onehot(idx_ref, which, width):
    lane = lax.broadcasted_iota(I32, (idx_ref.shape[0], width), 1)
    return jnp.where(lane == idx_ref[:, 2 + which:3 + which], 1.0, 0.0)


def _scatter_seg_body(base_ref, tab_ref, tab_prev_ref, idx_ref, x_ref, g_ref, xs_ref, buf_ref, zero_ref, sem,
                      fill_sem):
    i = pl.program_id(0)
    slot = i % 2
    xn = _rms(x_ref[...], g_ref[...]).astype(BF16)
    width = buf_ref.shape[1]
    place = (_slot_onehot(idx_ref, 0, width) + _slot_onehot(idx_ref, 1, width)).astype(BF16)
    buf_ref[slot] = lax.dot_general(place, xn, (((0,), (0,)), ((), ())), preferred_element_type=F32)
    _segment_start(tab_ref, base_ref, buf_ref.at[slot], xs_ref, sem.at[slot], to_hbm=True)

    @pl.when(i > 0)
    def _():
        _segment_wait(tab_prev_ref, buf_ref.at[1 - slot], xs_ref, sem.at[1 - slot], to_hbm=True)

    @pl.when(i == pl.num_programs(0) - 1)
    def _():
        _segment_wait(tab_ref, buf_ref.at[slot], xs_ref, sem.at[slot], to_hbm=True)
        zero_ref[...] = jnp.zeros_like(zero_ref)
        tile_rows = zero_ref.shape[0]
        chunks = 0
        for e in range(N_EXPERTS):
            row0 = base_ref[N_EXPERTS + e]

            def fill(j, carry, row0=row0):
                _chunk_copy(zero_ref, xs_ref, 0, row0 + 8 * j, fill_sem).start()
                return carry

            lax.fori_loop(0, base_ref[2 * N_EXPERTS + e], fill, 0)
            chunks = chunks + base_ref[2 * N_EXPERTS + e]

        def tile_copy(j):
            row = pl.multiple_of(base_ref[3 * N_EXPERTS] + tile_rows * j, 8)
            return pltpu.make_async_copy(zero_ref, xs_ref.at[pl.ds(row, tile_rows)], fill_sem)

        def fill_tile(j, carry):
            tile_copy(j).start()
            return carry

        lax.fori_loop(0, base_ref[3 * N_EXPERTS + 1], fill_tile, 0)

        def drain(j, carry):
            _chunk_copy(zero_ref, xs_ref, 0, 0, fill_sem).wait()
            return carry

        lax.fori_loop(0, chunks, drain, 0)

        def drain_tile(j, carry):
            tile_copy(0).wait()
            return carry

        lax.fori_loop(0, base_ref[3 * N_EXPERTS + 1], drain_tile, 0)


def _tab_spec(shift, nt):
    return pl.BlockSpec((None, 8, LANES), lambda i, b: (jnp.clip(i + shift, 0, nt - 1), 0, 0),
                        memory_space=pltpu.SMEM)


def _moe_scatter_seg(x, gain, idx, tab, plan, rows):
    m, k = x.shape
    tm = SEG_TOK
    nt = m // tm
    grid_spec = pltpu.PrefetchScalarGridSpec(
        num_scalar_prefetch=1,
        grid=(nt,),
        in_specs=[
            _tab_spec(0, nt), _tab_spec(-1, nt),
            pl.BlockSpec((tm, LANES), lambda i, b: (i, 0)),
            pl.BlockSpec((tm, k), lambda i, b: (i, 0)),
            pl.BlockSpec((1, k), lambda i, b: (0, 0)),
        ],
        out_specs=pl.BlockSpec(memory_space=pl.ANY),
        scratch_shapes=[pltpu.VMEM((2, SEG_ROWS, k), F32), pltpu.VMEM((EXPERT_TILE, k), F32),
                        pltpu.SemaphoreType.DMA((2,)), pltpu.SemaphoreType.DMA(())],
    )
    return pl.pallas_call(
        _scatter_seg_body,
        grid_spec=grid_spec,
        out_shape=jax.ShapeDtypeStruct((rows, k), F32),
        compiler_params=_params(("arbitrary",)),
        name="moe_scatter_seg",
    )(plan, tab, tab, idx, x, gain.reshape(1, k).astype(F32))


def _combine_seg_body(base_ref, tab_ref, tab_next_ref, idx_ref, x_ref, gate_ref, ys_ref, o_ref, buf_ref, sem):
    i = pl.program_id(0)
    slot = i % 2

    @pl.when(i == 0)
    def _():
        buf_ref[...] = jnp.zeros_like(buf_ref)
        _segment_start(tab_ref, base_ref, buf_ref.at[0], ys_ref, sem.at[0], to_hbm=False)

    @pl.when(i < pl.num_programs(0) - 1)
    def _():
        _segment_start(tab_next_ref, base_ref, buf_ref.at[1 - slot], ys_ref, sem.at[1 - slot], to_hbm=False)

    _segment_wait(tab_ref, buf_ref.at[slot], ys_ref, sem.at[slot], to_hbm=False)
    width = buf_ref.shape[1]
    y = buf_ref[slot].astype(BF16)
    g = gate_ref[...]
    pick = lambda which: jnp.dot(_slot_onehot(idx_ref, which, width).astype(BF16), y, preferred_element_type=F32)
    o_ref[...] = x_ref[...] + g[:, 0:1] * pick(0) + g[:, 1:2] * pick(1)


def _moe_combine_seg(x, gates, idx, tab, base, ys):
    m, k = x.shape
    tm = SEG_TOK
    nt = m // tm
    grid_spec = pltpu.PrefetchScalarGridSpec(
        num_scalar_prefetch=1,
        grid=(nt,),
        in_specs=[
            _tab_spec(0, nt), _tab_spec(1, nt),
            pl.BlockSpec((tm, LANES), lambda i, b: (i, 0)),
            pl.BlockSpec((tm, k), lambda i, b: (i, 0)),
            pl.BlockSpec((tm, LANES), lambda i, b: (i, 0)),
            pl.BlockSpec(memory_space=pl.ANY),
        ],
        out_specs=pl.BlockSpec((tm, k), lambda i, b: (i, 0)),
        scratch_shapes=[pltpu.VMEM((2, SEG_ROWS, k), F32), pltpu.SemaphoreType.DMA((2,))],
    )
    return pl.pallas_call(
        _combine_seg_body,
        grid_spec=grid_spec,
        out_shape=jax.ShapeDtypeStruct((m, k), F32),
        compiler_params=_params(("arbitrary",)),
        name="moe_combine_seg",
    )(base, tab, tab, idx, x, gates, ys)


def _moe_seg(x, gain, w_router, b_router, w_gu, w_down):
    m, k = x.shape
    idx, gates, tab, totals = _router_seg(x, gain, w_router, b_router)
    totals = totals[0, :N_EXPERTS].astype(I32)
    region = ((totals + EXPERT_TILE - 1) // EXPERT_TILE) * EXPERT_TILE
    ends = jnp.cumsum(region)
    base = (ends - region).astype(I32)
    max_rows = 2 * m + N_EXPERTS * (7 * (m // SEG_TOK) + EXPERT_TILE - 1)
    rows = (max_rows // EXPERT_TILE) * EXPERT_TILE
    nt = rows // EXPERT_TILE
    n_used = (ends[-1] // EXPERT_TILE).astype(I32)
    tile_start = jnp.minimum(jnp.arange(nt, dtype=I32), n_used - 1) * EXPERT_TILE
    tile_expert = jnp.sum(tile_start[:, None] >= ends[None, :], axis=1).astype(I32)
    plan = jnp.concatenate([base, base + totals, (region - totals) // 8,
                            jnp.stack([ends[-1], nt * EXPERT_TILE - ends[-1]]) // jnp.array([1, EXPERT_TILE])]).astype(I32)
    xs = _moe_scatter_seg(x, gain, idx, tab, plan, rows)
    ys = _ffn_grouped(xs, tile_expert, n_used.reshape(1), w_gu, w_down, EXPERT_TILE)
    return _moe_combine_seg(x, gates, idx, tab, base, ys)


def _moe(x, gain, w_router, b_router, w_gu, w_down):
    m, k = x.shape
    if m % SEG_TOK == 0 and m >= 8 * SEG_TOK and not _is_precise(w_gu):
        return _moe_seg(x, gain, w_router, b_router, w_gu, w_down)
    tm_tok = min(256, m)
    tm_grp = 512 if m >= 4096 else 128
    idx, gates, counts = _router(x, gain, w_router, b_router)
    counts = counts[0, :N_EXPERTS].astype(I32)
    padded = ((counts + tm_grp - 1) // tm_grp) * tm_grp
    ends = jnp.cumsum(padded)
    starts = ends - padded
    dest = starts[idx[:, 0:2]] + idx[:, 2:4]
    dest_tiles = dest.reshape(m // tm_tok, 1, 2 * tm_tok)
    rows = ((2 * m + N_EXPERTS * (tm_grp - 1)) // tm_grp) * tm_grp
    nt = rows // tm_grp
    n_used = (ends[-1] // tm_grp).astype(I32)
    tile_start = jnp.minimum(jnp.arange(nt, dtype=I32), n_used - 1) * tm_grp
    tile_expert = jnp.sum(tile_start[:, None] >= ends[None, :], axis=1).astype(I32)
    xs = _moe_scatter(x, gain, dest_tiles, rows, tm_tok)
    ys = _ffn_grouped(xs, tile_expert, n_used.reshape(1), w_gu, w_down, tm_grp)
    return _moe_combine(x, gates, dest_tiles, ys, tm_tok)


def _head_cols(x, h, width=GDN_DK):
    return x[:, h * width:(h + 1) * width]


def _gdn_prep_body(x_ref, halo_ref, ab_ref, cw_ref, alog_ref, dtb_ref,
                   q_ref, qg_ref, k_ref, kb_ref, kbg_ref, kd_ref, vb_ref,
                   gc_ref, gct_ref, egl_ref, ext_ref):
    i = pl.program_id(1)
    tt = x_ref.shape[0]
    ext_ref[0:8, :] = jnp.where(i == 0, 0.0, halo_ref[...])
    ext_ref[8:, :] = x_ref[...]
    ext = ext_ref[...]
    acc = pltpu.roll(ext, CONV_TAPS - 1, axis=0)[8:] * cw_ref[0:1, :]
    for j in range(1, CONV_TAPS - 1):
        acc = acc + pltpu.roll(ext, CONV_TAPS - 1 - j, axis=0)[8:] * cw_ref[j:j + 1, :]
    acc = acc + ext[8:] * cw_ref[CONV_TAPS - 1:CONV_TAPS, :]
    y = _silu(acc)

    ab = ab_ref[...]
    g = -jnp.exp(alog_ref[...]) * jax.nn.softplus(ab + dtb_ref[...])
    beta = jax.nn.sigmoid(pltpu.roll(ab, LANES - GDN_HEADS, axis=1))
    r = lax.broadcasted_iota(I32, (tt, tt), 0)
    c = lax.broadcasted_iota(I32, (tt, tt), 1)
    same = (r // CHUNK) == (c // CHUNK)
    tril = jnp.where(same & (c <= r), 1.0, 0.0)
    gc = jnp.dot(tril, g, precision=HI, preferred_element_type=F32)
    gl = jnp.dot(jnp.where(same, 1.0, 0.0), g, precision=HI, preferred_element_type=F32)
    e_gc = jnp.exp(gc)
    e_gd = jnp.exp(gl - gc)
    gc_ref[...] = gc
    gct = gc.T
    sub = lax.broadcasted_iota(I32, (8, LANES), 0)
    lan = lax.broadcasted_iota(I32, (8, LANES), 1)
    for cc in range(tt // CHUNK):
        gct_ref[cc] = gct[0:8, cc * CHUNK:(cc + 1) * CHUNK]
        row = jnp.broadcast_to(gl[cc * CHUNK:cc * CHUNK + 1, :], (8, LANES))
        val = jnp.sum(jnp.where(sub == lan, row, 0.0), axis=-1, keepdims=True)
        egl_ref[cc] = jnp.broadcast_to(jnp.exp(val), (8, LANES))

    nq = GDN_HEADS * GDN_DK
    for h in range(GDN_HEADS):
        qh = _head_cols(y, h)
        kh = _head_cols(y, GDN_HEADS + h)
        vh = _head_cols(y, 2 * GDN_HEADS + h)
        qn = qh * lax.rsqrt(jnp.sum(qh * qh, axis=-1, keepdims=True) + EPS) * (GDN_DK ** -0.5)
        kn = kh * lax.rsqrt(jnp.sum(kh * kh, axis=-1, keepdims=True) + EPS)
        b_h = beta[:, h:h + 1]
        eg_h = e_gc[:, h:h + 1]
        sl = slice(h * GDN_DK, (h + 1) * GDN_DK)
        kb = kn * b_h
        q_ref[:, sl] = qn.astype(BF16)
        qg_ref[:, sl] = (qn * eg_h).astype(BF16)
        k_ref[:, sl] = kn.astype(BF16)
        kb_ref[:, sl] = kb.astype(BF16)
        kbg_ref[:, sl] = (kb * eg_h).astype(BF16)
        kd_ref[:, sl] = (kn * e_gd[:, h:h + 1]).astype(BF16)
        vb_ref[:, sl] = (vh * b_h).astype(BF16)
    del nq


def _gdn_prep(proj, ab, conv_w, a_log, dt_bias, tt=256):
    b, t, _ = proj.shape
    nc = t // CHUNK
    cpt = tt // CHUNK
    row = lambda v: jnp.zeros((1, LANES), F32).at[0, :GDN_HEADS].set(v)
    tok = pl.BlockSpec((None, tt, D_MODEL), lambda bi, i: (bi, i, 0))
    out_shapes = [jax.ShapeDtypeStruct((b, t, D_MODEL), BF16)] * 7 + [
        jax.ShapeDtypeStruct((b, t, LANES), F32),
        jax.ShapeDtypeStruct((b, nc, 8, CHUNK), F32),
        jax.ShapeDtypeStruct((b, nc, 8, LANES), F32),
    ]
    out_specs = [tok] * 7 + [
        pl.BlockSpec((None, tt, LANES), lambda bi, i: (bi, i, 0)),
        pl.BlockSpec((None, cpt, 8, CHUNK), lambda bi, i: (bi, i, 0, 0)),
        pl.BlockSpec((None, cpt, 8, LANES), lambda bi, i: (bi, i, 0, 0)),
    ]
    return pl.pallas_call(
        _gdn_prep_body,
        grid=(b, t // tt),
        in_specs=[
            pl.BlockSpec((None, tt, QKV_W), lambda bi, i: (bi, i, 0)),
            pl.BlockSpec((None, 8, QKV_W), lambda bi, i: (bi, jnp.maximum(i * (tt // 8) - 1, 0), 0)),
            pl.BlockSpec((None, tt, LANES), lambda bi, i: (bi, i, 0)),
            pl.BlockSpec((CONV_TAPS, QKV_W), lambda bi, i: (0, 0)),
            pl.BlockSpec((1, LANES), lambda bi, i: (0, 0)),
            pl.BlockSpec((1, LANES), lambda bi, i: (0, 0)),
        ],
        out_specs=out_specs,
        out_shape=out_shapes,
        scratch_shapes=[pltpu.VMEM((tt + 8, QKV_W), F32)],
        compiler_params=_params(("parallel", "arbitrary")),
        name="gdn_prep",
    )(proj, proj, ab, conv_w.astype(F32), row(a_log), row(dt_bias))


def _decay(gc_col, gc_row, keep):
    return jnp.where(keep, jnp.exp(jnp.where(keep, gc_col - gc_row, 0.0)), 0.0)


def _gdn_l_body(kb_ref, k_ref, gc_ref, gct_ref, l_ref):
    r = lax.broadcasted_iota(I32, (CHUNK, CHUNK), 0)
    c = lax.broadcasted_iota(I32, (CHUNK, CHUNK), 1)
    strict = c < r
    for cc in range(l_ref.shape[0]):
        rows = slice(cc * CHUNK, (cc + 1) * CHUNK)
        for h in range(GDN_HEADS):
            cols = slice(h * GDN_DK, (h + 1) * GDN_DK)
            kk = lax.dot_general(kb_ref[rows, cols], k_ref[rows, cols], (((1,), (1,)), ((), ())),
                                 preferred_element_type=F32)
            dec = _decay(gc_ref[rows, h:h + 1], gct_ref[cc, h:h + 1, :], strict)
            l_ref[cc, :, h * CHUNK:(h + 1) * CHUNK] = kk * dec


def _gdn_l(kb, k, gc, gct, cpt=4):
    b, t, _ = k.shape
    nc = t // CHUNK
    tt = cpt * CHUNK
    tok = pl.BlockSpec((None, tt, D_MODEL), lambda bi, i: (bi, i, 0))
    return pl.pallas_call(
        _gdn_l_body,
        grid=(b, nc // cpt),
        in_specs=[tok, tok,
                  pl.BlockSpec((None, tt, LANES), lambda bi, i: (bi, i, 0)),
                  pl.BlockSpec((None, cpt, 8, CHUNK), lambda bi, i: (bi, i, 0, 0))],
        out_specs=pl.BlockSpec((None, cpt, CHUNK, GDN_HEADS * CHUNK), lambda bi, i: (bi, i, 0, 0)),
        out_shape=jax.ShapeDtypeStruct((b, nc, CHUNK, GDN_HEADS * CHUNK), F32),
        compiler_params=_params(("parallel", "parallel")),
        name="gdn_l",
    )(kb, k, gc, gct)


def _tri_inv_body(l_ref, t_ref):
    sub = lax.broadcasted_iota(I32, (8, LANES), 0)
    for i in range(CHUNK):
        nq = i // 8 + 1
        acc = [jnp.where(sub == i % 8, 1.0, 0.0) if q == i // 8 else jnp.zeros((8, LANES), F32)
               for q in range(nq)]
        for j in range(i):
            lij = l_ref[i, j:j + 1, :]
            for q in range(j // 8 + 1):
                acc[q] = acc[q] - lij * t_ref[j, 8 * q:8 * q + 8, :]
        for q in range(CHUNK // 8):
            t_ref[i, 8 * q:8 * q + 8, :] = acc[q] if q < nq else jnp.zeros((8, LANES), F32)


def _tri_inv(lt):
    _, nh, _, p = lt.shape
    spec = pl.BlockSpec((CHUNK, None, CHUNK, LANES), lambda h, i: (0, h, 0, i))
    return pl.pallas_call(
        _tri_inv_body,
        grid=(nh, p // LANES),
        in_specs=[spec],
        out_specs=spec,
        out_shape=jax.ShapeDtypeStruct(lt.shape, F32),
        compiler_params=_params(("parallel", "parallel")),
        name="gdn_tri_inv",
    )(lt)


def _gdn_wy_body(t_ref, vb_ref, kbg_ref, q_ref, k_ref, qg_ref, gc_ref, gct_ref, u_ref, wq_ref, attn_ref):
    r = lax.broadcasted_iota(I32, (CHUNK, CHUNK), 0)
    c = lax.broadcasted_iota(I32, (CHUNK, CHUNK), 1)
    tril = c <= r
    for cc in range(t_ref.shape[0]):
        rows = slice(cc * CHUNK, (cc + 1) * CHUNK)
        for h in range(GDN_HEADS):
            cols = slice(h * GDN_DK, (h + 1) * GDN_DK)
            hc = slice(h * CHUNK, (h + 1) * CHUNK)
            rhs = jnp.concatenate([vb_ref[rows, cols], kbg_ref[rows, cols]], axis=1)
            sol = jnp.dot(t_ref[cc, :, hc].astype(BF16), rhs, preferred_element_type=F32)
            u_ref[rows, cols] = sol[:, :GDN_DK].astype(u_ref.dtype)
            wq_ref[cc, 0:CHUNK, cols] = sol[:, GDN_DK:].astype(BF16)
            wq_ref[cc, CHUNK:2 * CHUNK, cols] = qg_ref[rows, cols]
            qk = lax.dot_general(q_ref[rows, cols], k_ref[rows, cols], (((1,), (1,)), ((), ())),
                                 preferred_element_type=F32)
            dec = _decay(gc_ref[rows, h:h + 1], gct_ref[cc, h:h + 1, :], tril)
            attn_ref[cc, :, hc] = (qk * dec).astype(BF16)


def _gdn_wy(tinv, vb, kbg, q, k, qg, gc, gct, cpt=4):
    b, t, _ = k.shape
    nc = t // CHUNK
    tt = cpt * CHUNK
    tok = pl.BlockSpec((None, tt, D_MODEL), lambda bi, i: (bi, i, 0))
    sq = pl.BlockSpec((None, cpt, CHUNK, GDN_HEADS * CHUNK), lambda bi, i: (bi, i, 0, 0))
    return pl.pallas_call(
        _gdn_wy_body,
        grid=(b, nc // cpt),
        in_specs=[sq, tok, tok, tok, tok, tok,
                  pl.BlockSpec((None, tt, LANES), lambda bi, i: (bi, i, 0)),
                  pl.BlockSpec((None, cpt, 8, CHUNK), lambda bi, i: (bi, i, 0, 0))],
        out_specs=[tok,
                   pl.BlockSpec((None, cpt, 2 * CHUNK, D_MODEL), lambda bi, i: (bi, i, 0, 0)),
                   sq],
        out_shape=[jax.ShapeDtypeStruct((b, t, D_MODEL), BF16),
                   jax.ShapeDtypeStruct((b, nc, 2 * CHUNK, D_MODEL), BF16),
                   jax.ShapeDtypeStruct((b, nc, CHUNK, GDN_HEADS * CHUNK), BF16)],
        compiler_params=_params(("parallel", "parallel")),
        name="gdn_wy",
    )(tinv, vb, kbg, q, k, qg, gc, gct)


def _gdn_chunk_body(wq_ref, u_ref, attn_ref, kd_ref, egl_ref, o_ref, s_out_ref, sa_ref, sb_ref):
    c = pl.program_id(0)

    @pl.when(c == 0)
    def _():
        sa_ref[...] = jnp.zeros_like(sa_ref)

    def step(src_ref, dst_ref):
        heads = [(b, h) for b in range(src_ref.shape[0]) for h in range(GDN_HEADS)]
        cols = lambda h: slice(h * GDN_DK, (h + 1) * GDN_DK)
        ws = [jnp.dot(wq_ref[b, :, cols(h)], src_ref[b, h].astype(BF16), preferred_element_type=F32)
              for b, h in heads]
        v_nb = [(u_ref[b, :, cols(h)] - w_[:CHUNK]).astype(BF16) for (b, h), w_ in zip(heads, ws)]
        for (b, h), w_, v_ in zip(heads, ws, v_nb):
            o_ref[b, :, cols(h)] = (w_[CHUNK:] + jnp.dot(attn_ref[b, :, h * CHUNK:(h + 1) * CHUNK], v_,
                                                         preferred_element_type=F32)).astype(o_ref.dtype)
        for (b, h), v_ in zip(heads, v_nb):
            dst_ref[b, h] = src_ref[b, h] * egl_ref[b, h:h + 1, :] + lax.dot_general(
                kd_ref[b, :, cols(h)], v_, (((0,), (0,)), ((), ())), preferred_element_type=F32)

        @pl.when(c == pl.num_programs(0) - 1)
        def _():
            s_out_ref[...] = dst_ref[...]

    @pl.when(c % 2 == 0)
    def _():
        step(sa_ref, sb_ref)

    @pl.when(c % 2 == 1)
    def _():
        step(sb_ref, sa_ref)


def _gdn_chunks(wq, u, attn, kd, egl):
    b, t, _ = u.shape
    nc = t // CHUNK
    tok = pl.BlockSpec((b, CHUNK, D_MODEL), lambda c: (0, c, 0))
    return pl.pallas_call(
        _gdn_chunk_body,
        grid=(nc,),
        in_specs=[
            pl.BlockSpec((b, None, 2 * CHUNK, D_MODEL), lambda c: (0, c, 0, 0)),
            tok,
            pl.BlockSpec((b, None, CHUNK, GDN_HEADS * CHUNK), lambda c: (0, c, 0, 0)),
            tok,
            pl.BlockSpec((b, None, 8, LANES), lambda c: (0, c, 0, 0)),
        ],
        out_specs=[
            tok,
            pl.BlockSpec((b, GDN_HEADS, GDN_DK, GDN_DK), lambda c: (0, 0, 0, 0)),
        ],
        out_shape=[
            jax.ShapeDtypeStruct((b, t, D_MODEL), BF16),
            jax.ShapeDtypeStruct((b, GDN_HEADS, GDN_DK, GDN_DK), F32),
        ],
        scratch_shapes=[pltpu.VMEM((b, GDN_HEADS, GDN_DK, GDN_DK), F32)] * 2,
        compiler_params=_params(("arbitrary",)),
        name="gdn_chunks",
    )(wq, u, attn, kd, egl)


def _gdn_out_body(o_ref, z_ref, on_ref, w_ref, r_ref, out_ref, *, precise):
    parts = []
    for h in range(GDN_HEADS):
        oh = _head_cols(o_ref[...], h).astype(F32)
        zh = _head_cols(z_ref[...], h)
        parts.append(_rms(oh, on_ref[...]) * _silu(zh))
    if precise:
        gh, gl = _split(jnp.concatenate(parts, axis=1))
        out_ref[...] = r_ref[...] + _dot3(gh, gl, w_ref[...])
    else:
        gated = jnp.concatenate([a.astype(BF16) for a in parts], axis=1)
        out_ref[...] = r_ref[...] + jnp.dot(gated, w_ref[...], preferred_element_type=F32)


def _gdn_out(o, proj, o_norm, w_out, res, tm=512):
    m, k = o.shape
    tm = min(tm, m)
    return pl.pallas_call(
        functools.partial(_gdn_out_body, precise=_is_precise(w_out)),
        grid=(m // tm,),
        in_specs=[
            pl.BlockSpec((tm, k), lambda i: (i, 0)),
            pl.BlockSpec((tm, k), lambda i: (i, QKV_W // D_MODEL)),
            pl.BlockSpec((1, GDN_DK), lambda i: (0, 0)),
            _wspec(w_out, (k, k), lambda i: (0, 0)),
            pl.BlockSpec((tm, k), lambda i: (i, 0)),
        ],
        out_specs=pl.BlockSpec((tm, k), lambda i: (i, 0)),
        out_shape=jax.ShapeDtypeStruct((m, k), F32),
        compiler_params=_params(("parallel",)),
        name="gdn_out",
    )(o, proj, o_norm.reshape(1, GDN_DK).astype(F32), _warr(w_out), res)


def _gdn_prompt_layer(h, l, p, w):
    b, t, _ = h.shape
    hf = h.reshape(b * t, D_MODEL)
    proj, ab = _mm(hf, w['a_w_in'][l], gain=p['a_norm'][l], w_aux=w['a_w_ab'][l], tm=512, tn=PROJ_W)
    proj3 = proj.reshape(b, t, PROJ_W)
    q, qg, k, kb, kbg, kd, vb, gc, gct, egl = _gdn_prep(
        proj3, ab.reshape(b, t, LANES), p['a_conv_w'][l], p['a_A_log'][l], p['a_dt_bias'][l])
    lmat = _gdn_l(kb, k, gc, gct)
    nprob = b * (t // CHUNK)
    ppad = -nprob % LANES
    lt = jnp.transpose(lmat.reshape(nprob, CHUNK, GDN_HEADS * CHUNK), (1, 2, 0))
    lt = jnp.pad(lt, ((0, 0), (0, 0), (0, ppad))).reshape(CHUNK, GDN_HEADS, CHUNK, nprob + ppad)
    tt = _tri_inv(lt).reshape(CHUNK, GDN_HEADS * CHUNK, nprob + ppad)[:, :, :nprob]
    tinv = jnp.transpose(tt, (2, 0, 1)).reshape(b, t // CHUNK, CHUNK, GDN_HEADS * CHUNK)
    u, wq, attn = _gdn_wy(tinv, vb, kbg, q, k, qg, gc, gct)
    o, s_new = _gdn_chunks(wq, u, attn, kd, egl)
    h_new = _gdn_out(o.reshape(b * t, D_MODEL), proj, p['a_o_norm'][l], w['a_w_out'][l], hf)
    conv_new = proj3[:, t - (CONV_TAPS - 1):, :QKV_W]
    return h_new.reshape(b, t, D_MODEL), s_new, conv_new


def _gdn_prep1_body(x_ref, st_ref, ab_ref, cw_ref, alog_ref, dtb_ref,
                    q_ref, k_ref, v_ref, eg_ref, beta_ref):
    acc = st_ref[0] * cw_ref[0:1, :]
    for j in range(1, CONV_TAPS - 1):
        acc = acc + st_ref[j] * cw_ref[j:j + 1, :]
    acc = acc + x_ref[...] * cw_ref[CONV_TAPS - 1:CONV_TAPS, :]
    y = _silu(acc)
    ab = ab_ref[...]
    eg_ref[...] = jnp.exp(-jnp.exp(alog_ref[...]) * jax.nn.softplus(ab + dtb_ref[...]))
    beta_ref[...] = jax.nn.sigmoid(pltpu.roll(ab, LANES - GDN_HEADS, axis=1))
    for h in range(GDN_HEADS):
        qh = _head_cols(y, h)
        kh = _head_cols(y, GDN_HEADS + h)
        sl = slice(h * GDN_DK, (h + 1) * GDN_DK)
        q_ref[:, sl] = qh * lax.rsqrt(jnp.sum(qh * qh, axis=-1, keepdims=True) + EPS) * (GDN_DK ** -0.5)
        k_ref[:, sl] = kh * lax.rsqrt(jnp.sum(kh * kh, axis=-1, keepdims=True) + EPS)
    v_ref[...] = y[:, 2 * GDN_HEADS * GDN_DK:]


def _gdn_prep1(proj, ab, conv_state, conv_w, a_log, dt_bias):
    n = proj.shape[0]
    row = lambda v: jnp.zeros((1, LANES), F32).at[0, :GDN_HEADS].set(v)
    tok = pl.BlockSpec((n, D_MODEL), lambda i: (0, 0))
    small = pl.BlockSpec((n, LANES), lambda i: (0, 0))
    return pl.pallas_call(
        _gdn_prep1_body,
        grid=(1,),
        in_specs=[
            pl.BlockSpec((n, QKV_W), lambda i: (0, 0)),
            pl.BlockSpec((CONV_TAPS - 1, n, QKV_W), lambda i: (0, 0, 0)),
            pl.BlockSpec((n, LANES), lambda i: (0, 0)),
            pl.BlockSpec((CONV_TAPS, QKV_W), lambda i: (0, 0)),
            pl.BlockSpec((1, LANES), lambda i: (0, 0)),
            pl.BlockSpec((1, LANES), lambda i: (0, 0)),
        ],
        out_specs=[tok, tok, tok, small, small],
        out_shape=[jax.ShapeDtypeStruct((n, D_MODEL), F32)] * 3 + [jax.ShapeDtypeStruct((n, LANES), F32)] * 2,
        compiler_params=_params(("arbitrary",)),
        name="gdn_prep1",
    )(proj, conv_state, ab, conv_w.astype(F32), row(a_log), row(dt_bias))


def _gdn_step_body(qt_ref, kt_ref, v_ref, eg_ref, beta_ref, s_ref, o_ref, s_out_ref, *, first):
    def compute():
        for b in range(s_ref.shape[0]):
            for h in range(GDN_HEADS):
                k_col = kt_ref[b, :, h:h + 1]
                q_col = qt_ref[b, :, h:h + 1]
                s = s_ref[b, h] * eg_ref[b, h:h + 1, :]
                kv_mem = jnp.sum(k_col * s, axis=0, keepdims=True)
                delta = (v_ref[b, h:h + 1, :] - kv_mem) * beta_ref[b, h:h + 1, :]
                s = s + k_col * delta
                s_out_ref[b, h] = s
                o_ref[b, h:h + 1, :] = jnp.sum(q_col * s, axis=0, keepdims=True)

    if first:
        pl.when(pl.program_id(1) == 0)(compute)

        @pl.when(pl.program_id(1) > 0)
        def _():
            s_out_ref[...] = s_ref[...]
    else:
        compute()


def _gdn_step(qt, kt, v, eg, beta, states, layer, carry, sb=4):
    nl, n = states.shape[:2]
    first = carry is None
    assert first == (layer == 0)
    blk = (None, sb, GDN_HEADS, GDN_DK, GDN_DK)
    if first:
        grid = (n // sb, nl)
        vec = lambda i, j: (i, 0, 0)
        st = pl.BlockSpec(blk, lambda i, j: (j, i, 0, 0, 0))
        sem = ("parallel", "arbitrary")
    else:
        grid = (n // sb,)
        vec = lambda i: (i, 0, 0)
        st = pl.BlockSpec(blk, lambda i: (layer, i, 0, 0, 0))
        sem = ("parallel",)
    col = pl.BlockSpec((sb, GDN_DK, GDN_HEADS), vec)
    row = pl.BlockSpec((sb, GDN_HEADS, GDN_DK), vec)
    return pl.pallas_call(
        functools.partial(_gdn_step_body, first=first),
        grid=grid,
        in_specs=[col, col, row, row, row, st],
        out_specs=[row, st],
        out_shape=[jax.ShapeDtypeStruct((n, GDN_HEADS, GDN_DK), F32),
                   jax.ShapeDtypeStruct(states.shape, F32)],
        input_output_aliases={} if first else {5: 1},
        compiler_params=_params(sem),
        name="gdn_step",
    )(qt, kt, v, eg, beta, states if first else carry)


def _gdn_sample_layer(h, l, p, w, conv_state, delta_states, carry):
    n = h.shape[0]
    proj, ab = _mm(h, w['a_w_in'][l], gain=p['a_norm'][l], w_aux=w['a_w_ab'][l])
    conv_t = jnp.transpose(conv_state, (1, 0, 2))
    q, k, v, eg, beta = _gdn_prep1(proj, ab, conv_t, p['a_conv_w'][l], p['a_A_log'][l], p['a_dt_bias'][l])
    heads = lambda a: a.reshape(n, GDN_HEADS, GDN_DK)
    cols = lambda a: jnp.transpose(heads(a), (0, 2, 1))
    lanes = lambda a: jnp.broadcast_to(a[:, :GDN_HEADS, None], (n, GDN_HEADS, GDN_DK))
    o, s_new = _gdn_step(cols(q), cols(k), heads(v), lanes(eg), lanes(beta), delta_states, l, carry)
    h_new = _gdn_out(o.reshape(n, D_MODEL), proj, p['a_o_norm'][l], w['a_w_out'][l], h)
    conv_new = jnp.concatenate([conv_state[:, 1:], proj[:, None, :QKV_W]], axis=1)
    return h_new, s_new, conv_new


def _rope_tables(pos):
    half = ROT // 2
    inv = ROPE_THETA ** (-jnp.arange(0, ROT, 2, dtype=F32) / ROT)
    ang = pos.astype(F32)[:, None] * inv[None, :]
    cos, sin = jnp.cos(ang), jnp.sin(ang)
    ones = jnp.ones((pos.shape[0], HEAD_DIM - ROT), F32)
    zeros = jnp.zeros_like(ones)
    z8 = jnp.zeros_like(sin)
    c = jnp.concatenate([cos, cos, ones], axis=1)
    sm = jnp.concatenate([-sin, z8, zeros], axis=1)
    sp = jnp.concatenate([z8, sin, zeros], axis=1)
    del half
    return tuple(jnp.tile(a, (1, LANES // HEAD_DIM)) for a in (c, sm, sp))


def _dot_split(a, sel):
    hi = a.astype(BF16)
    lo = (a - hi.astype(F32)).astype(BF16)
    return (jnp.dot(hi, sel, preferred_element_type=F32) + jnp.dot(lo, sel, preferred_element_type=F32))


def _headnorm_rope_body(x_ref, g_ref, seg_ref, segt_ref, c_ref, sm_ref, sp_ref, o_ref):
    x = x_ref[...].astype(F32)
    w = x.shape[1]
    ms = _dot_split(x * x, seg_ref[...]) * (1.0 / HEAD_DIM)
    scale = _dot_split(lax.rsqrt(ms + EPS), segt_ref[...])
    y = x * scale * g_ref[...]
    reps = w // LANES
    tile = lambda a: jnp.concatenate([a] * reps, axis=1) if reps > 1 else a
    half = ROT // 2
    out = (y * tile(c_ref[...]) + pltpu.roll(y, w - half, axis=1) * tile(sm_ref[...])
           + pltpu.roll(y, half, axis=1) * tile(sp_ref[...]))
    o_ref[...] = out.astype(o_ref.dtype)


def _headnorm_rope(x, col_block, width, gain, tables, rows_per_seq, out_dtype, tm=512):
    m = x.shape[0]
    tm = min(tm, m, rows_per_seq)
    nheads = width // HEAD_DIM
    seg = (jnp.arange(width)[:, None] // HEAD_DIM == jnp.arange(LANES)[None, :]).astype(BF16)
    per_seq = rows_per_seq // tm
    tab = pl.BlockSpec((tm, LANES), lambda i: (i % per_seq, 0))
    del nheads
    return pl.pallas_call(
        _headnorm_rope_body,
        grid=(m // tm,),
        in_specs=[
            pl.BlockSpec((tm, width), lambda i: (i, col_block)),
            pl.BlockSpec((1, width), lambda i: (0, 0)),
            pl.BlockSpec((width, LANES), lambda i: (0, 0)),
            pl.BlockSpec((LANES, width), lambda i: (0, 0)),
            tab, tab, tab,
        ],
        out_specs=pl.BlockSpec((tm, width), lambda i: (i, 0)),
        out_shape=jax.ShapeDtypeStruct((m, width), out_dtype),
        compiler_params=_params(("parallel",)),
        name="headnorm_rope",
    )(x, jnp.tile(gain.astype(F32), width // HEAD_DIM).reshape(1, width), seg, seg.T, *tables)


def _softmax_sink(s, sink):
    m = jnp.maximum(jnp.max(s, axis=-1, keepdims=True), sink)
    p = jnp.exp(s - m)
    return p, jnp.sum(p, axis=-1, keepdims=True) + jnp.exp(sink - m)


def _swa_prompt_body(sink_ref, q_ref, kp_ref, kc_ref, vp_ref, vc_ref, o_ref):
    i = pl.program_id(1)
    blk = q_ref.shape[0]
    r = lax.broadcasted_iota(I32, (blk, 2 * blk), 0)
    c = lax.broadcasted_iota(I32, (blk, 2 * blk), 1)
    valid = (c >= r) & (c <= r + WINDOW) & ((c >= blk) | (i > 0))
    for kvh in range(KV_HEADS):
        cols = slice(kvh * HEAD_DIM, (kvh + 1) * HEAD_DIM)
        kk = jnp.concatenate([kp_ref[:, cols], kc_ref[:, cols]], axis=0).astype(BF16)
        vv = jnp.concatenate([vp_ref[:, cols], vc_ref[:, cols]], axis=0).astype(BF16)
        for g in range(ATT_GROUP):
            h = kvh * ATT_GROUP + g
            hc = slice(h * HEAD_DIM, (h + 1) * HEAD_DIM)
            s = lax.dot_general(q_ref[:, hc], kk, (((1,), (1,)), ((), ())),
                                preferred_element_type=F32) * (HEAD_DIM ** -0.5)
            s = jnp.where(valid, s, -jnp.inf)
            p, den = _softmax_sink(s, sink_ref[h])
            o_ref[:, hc] = jnp.dot((p / den).astype(BF16), vv, preferred_element_type=F32).astype(o_ref.dtype)


def _swa_prompt(q, k, kv, sinks):
    b, t, _ = q.shape
    blk = WINDOW
    kvw = KV_HEADS * HEAD_DIM
    prev = lambda bi, i, s: (bi, jnp.maximum(i - 1, 0), 0)
    cur = lambda bi, i, s: (bi, i, 0)
    prev_v = lambda bi, i, s: (bi, jnp.maximum(i - 1, 0), 1)
    cur_v = lambda bi, i, s: (bi, i, 1)
    grid_spec = pltpu.PrefetchScalarGridSpec(
        num_scalar_prefetch=1,
        grid=(b, t // blk),
        in_specs=[
            pl.BlockSpec((None, blk, D_MODEL), cur),
            pl.BlockSpec((None, blk, kvw), prev),
            pl.BlockSpec((None, blk, kvw), cur),
            pl.BlockSpec((None, blk, kvw), prev_v),
            pl.BlockSpec((None, blk, kvw), cur_v),
        ],
        out_specs=pl.BlockSpec((None, blk, D_MODEL), cur),
    )
    return pl.pallas_call(
        _swa_prompt_body,
        grid_spec=grid_spec,
        out_shape=jax.ShapeDtypeStruct((b, t, D_MODEL), BF16),
        compiler_params=_params(("parallel", "parallel")),
        name="swa_prompt",
    )(sinks.astype(F32), q, k, k, kv, kv)


def _swa_sample_body(sink_ref, q_ref, kc_ref, kn_ref, vc_ref, vn_ref, o_ref):
    wc = kc_ref.shape[1]
    kvw = kc_ref.shape[2]
    nkeys = 2 * wc
    fill = jnp.zeros((nkeys - wc - 8, kvw), F32)
    key_idx = lax.broadcasted_iota(I32, (ATT_HEADS, nkeys), 1)
    sink = sink_ref[:, 0:1]
    seqs = range(q_ref.shape[0])
    scores = []
    for b in seqs:
        kk = jnp.concatenate([kc_ref[b], kn_ref[b], fill], axis=0)
        scores.append(lax.dot_general(q_ref[b], kk, (((1,), (1,)), ((), ())), precision=HI,
                                      preferred_element_type=F32))
    probs = []
    for s in scores:
        s = jnp.where(key_idx <= wc, s * (HEAD_DIM ** -0.5), -jnp.inf)
        p, den = _softmax_sink(s, sink)
        probs.append(p / den)
    outs = []
    for b, p in zip(seqs, probs):
        vv = jnp.concatenate([vc_ref[b], vn_ref[b], fill], axis=0)
        outs.append(jnp.dot(p, vv, precision=HI, preferred_element_type=F32))
    for b, o in zip(seqs, outs):
        for kvh in range(KV_HEADS):
            heads = slice(kvh * ATT_GROUP, (kvh + 1) * ATT_GROUP)
            o_ref[b, heads, :] = o[heads, kvh * HEAD_DIM:(kvh + 1) * HEAD_DIM]


def _swa_sample(q, k_cache, k_new, v_cache, v_new, sinks, sb=8):
    n = q.shape[0]
    wc = k_cache.shape[1]
    kvw = KV_HEADS * HEAD_DIM
    own = (jnp.arange(ATT_HEADS)[:, None] // ATT_GROUP == jnp.arange(KV_HEADS)[None, :])
    q_rows = (q[:, :, None, :] * own[None, :, :, None].astype(q.dtype)).reshape(n, ATT_HEADS, kvw)
    pad8 = lambda a: jnp.pad(a[:, None, :], ((0, 0), (0, 7), (0, 0)))
    blk = lambda shape: pl.BlockSpec(shape, lambda i: (i, 0, 0))
    return pl.pallas_call(
        _swa_sample_body,
        grid=(n // sb,),
        in_specs=[pl.BlockSpec((ATT_HEADS, LANES), lambda i: (0, 0)),
                  blk((sb, ATT_HEADS, kvw)), blk((sb, wc, kvw)), blk((sb, 8, kvw)),
                  blk((sb, wc, kvw)), blk((sb, 8, kvw))],
        out_specs=blk((sb, ATT_HEADS, HEAD_DIM)),
        out_shape=jax.ShapeDtypeStruct((n, ATT_HEADS, HEAD_DIM), F32),
        compiler_params=_params(("parallel",)),
        name="swa_sample",
    )(jnp.broadcast_to(sinks.astype(F32)[:, None], (ATT_HEADS, LANES)), q_rows, k_cache, pad8(k_new),
      v_cache, pad8(v_new))


def _channel_mixer(hf, l, p, w):
    if l % 2 == 0:
        return _ffn_dense(hf, p['ffn_norm'][l], w['dense_w_gu'][l // 2], w['dense_w_down'][l // 2])
    return _moe(hf, p['ffn_norm'][l], p['moe_router'][l // 2], p['moe_router_b'][l // 2],
                w['moe_w_gu'][l // 2], w['moe_w_down'][l // 2])


def _run_prompt(x, p, w):
    b, t, _ = x.shape
    n = b * t
    h = x
    convs, deltas = [], []
    for l in range(N_GDN):
        h, s_new, c_new = _gdn_prompt_layer(h, l, p, w)
        convs.append(c_new)
        deltas.append(s_new)
        h = _channel_mixer(h.reshape(n, D_MODEL), l, p, w).reshape(b, t, D_MODEL)
    hf = h.reshape(n, D_MODEL)
    tables = _rope_tables(jnp.arange(t, dtype=I32))
    kvw = KV_HEADS * HEAD_DIM
    kv = _mm(hf, w['w_kv'], gain=p['kv_norm'], tn=2 * kvw)
    k_sh = _headnorm_rope(kv, 0, kvw, p['k_norm'], tables, t, F32)
    k3 = k_sh.reshape(b, t, kvw)
    kv3 = kv.reshape(b, t, 2 * kvw)
    for j in range(2):
        l = N_GDN + j
        qp = _mm(hf, w['b_w_q'][j], gain=p['b_norm'][j], out_dtype=BF16)
        q = _headnorm_rope(qp, 0, D_MODEL, p['b_q_norm'][j], tables, t, BF16)
        o = _swa_prompt(q.reshape(b, t, D_MODEL), k3, kv3, p['b_sinks'][j])
        hf = _mm(o.reshape(n, D_MODEL), w['b_w_o'][j], res=hf)
        hf = _channel_mixer(hf, l, p, w)
    k_win = k3[:, t - WINDOW:].reshape(b, WINDOW, KV_HEADS, HEAD_DIM)
    v_win = kv3[:, t - WINDOW:, kvw:].reshape(b, WINDOW, KV_HEADS, HEAD_DIM)
    return hf.reshape(b, t, D_MODEL), jnp.stack(convs), jnp.stack(deltas), k_win, v_win


def _run_sample(x, conv_state, delta_state, k_cache, v_cache, p, w):
    n = x.shape[0]
    h = x.reshape(n, D_MODEL)
    convs, deltas = [], None
    for l in range(N_GDN):
        h, deltas, c_new = _gdn_sample_layer(h, l, p, w, conv_state[l], delta_state, deltas)
        convs.append(c_new)
        h = _channel_mixer(h, l, p, w)
    tables = _rope_tables(jnp.full((n,), PAST_LEN, I32))
    kvw = KV_HEADS * HEAD_DIM
    wc = k_cache.shape[1]
    kv = _mm(h, w['w_kv'], gain=p['kv_norm'], tn=2 * kvw)
    k_new = _headnorm_rope(kv, 0, kvw, p['k_norm'], tables, n, F32)
    v_new = kv[:, kvw:]
    kc = k_cache.reshape(n, wc, kvw)
    vc = v_cache.reshape(n, wc, kvw)
    for j in range(2):
        l = N_GDN + j
        qp = _mm(h, w['b_w_q'][j], gain=p['b_norm'][j])
        q = _headnorm_rope(qp, 0, D_MODEL, p['b_q_norm'][j], tables, n, F32)
        o = _swa_sample(q.reshape(n, ATT_HEADS, HEAD_DIM), kc, k_new, vc, v_new, p['b_sinks'][j])
        h = _mm(o.reshape(n, D_MODEL), w['b_w_o'][j], res=h)
        h = _channel_mixer(h, l, p, w)
    k_win = jnp.concatenate([kc, k_new[:, None, :]], axis=1)[:, -wc:].reshape(n, wc, KV_HEADS, HEAD_DIM)
    v_win = jnp.concatenate([vc, v_new[:, None, :]], axis=1)[:, -wc:].reshape(n, wc, KV_HEADS, HEAD_DIM)
    return h.reshape(n, 1, D_MODEL), jnp.stack(convs), deltas, k_win, v_win


def kernel(x_prompt, x_sample, state_conv, state_delta, cache_k_win, cache_v_win, a_norm, a_w_in, a_conv_w,
           a_A_log, a_dt_bias, a_o_norm, a_w_out, kv_norm, w_kv, k_norm, b_norm, b_w_q, b_q_norm, b_sinks,
           b_w_o, ffn_norm, dense_w_gu, dense_w_down, moe_router, moe_router_b, moe_w_gu, moe_w_down):
    p = dict(a_norm=a_norm, a_conv_w=a_conv_w, a_A_log=a_A_log, a_dt_bias=a_dt_bias, a_o_norm=a_o_norm,
             kv_norm=kv_norm, k_norm=k_norm, b_norm=b_norm, b_q_norm=b_q_norm, b_sinks=b_sinks,
             ffn_norm=ffn_norm, moe_router=moe_router, moe_router_b=moe_router_b)
    n_proj = PROJ_W
    gate_cols = jnp.pad(a_w_in[:, :, n_proj:], ((0, 0), (0, 0), (0, LANES - (a_w_in.shape[-1] - n_proj))))
    raw = dict(a_w_in=a_w_in[:, :, :n_proj], a_w_ab=gate_cols, a_w_out=a_w_out, w_kv=w_kv[None],
               b_w_q=b_w_q, b_w_o=b_w_o, dense_w_gu=dense_w_gu, dense_w_down=dense_w_down,
               moe_w_gu=moe_w_gu, moe_w_down=moe_w_down)
    w_fast = {name: [val[l].astype(BF16) for l in range(val.shape[0])] for name, val in raw.items()}
    w_precise = {name: [_Layer(val, l) for l in range(val.shape[0])] for name, val in raw.items()}
    w_precise['a_w_in'] = [_Layer(a_w_in, l, cols=n_proj) for l in range(a_w_in.shape[0])]
    w_fast['w_kv'], w_precise['w_kv'] = w_fast['w_kv'][0], w_kv
    y_p, conv_p, delta_p, kw_p, vw_p = _run_prompt(x_prompt, p, w_fast)
    y_s, conv_s, delta_s, kw_s, vw_s = _run_sample(
        x_sample, state_conv, state_delta, cache_k_win, cache_v_win, p, w_precise)
    return (y_p, y_s, conv_p, conv_s, delta_p, delta_s, kw_p, kw_s, vw_p, vw_s)
```
